```python
import jax, jax.numpy as jnp
from jax import lax
import numpy as np

D_MODEL = 1024
BATCH = 2
SEQ = 8192
DEPTH = 1

CHUNK = 64
EPS = 1e-6
MLA_HEADS = 8
Q_RANK = 256
KV_RANK = 128
QK_NOPE = 64
QK_ROPE = 32
V_HEAD = 64
MLA_WIDTH = MLA_HEADS * V_HEAD
ROPE_THETA = 10000.0
Q_BLOCK = 128
SGU_GROUPS = 8
SGU_GROUP_DIM = 64
SGU_WIDTH = SGU_GROUPS * SGU_GROUP_DIM
SGU_LEN = 128
N_BRANCH = 2
N_GROUPS = 4
EXPERTS_PER_GROUP = 8
N_EXPERTS = N_GROUPS * EXPERTS_PER_GROUP
TOP_K = 2
D_EXPERT = 256
MOE_BLOCK = 128
SPLIT_POINTS = [Q_RANK, Q_RANK + KV_RANK, Q_RANK + KV_RANK + QK_ROPE,
                Q_RANK + KV_RANK + QK_ROPE + 2 * SGU_WIDTH]
IN_COLS = Q_RANK + KV_RANK + QK_ROPE + 2 * SGU_WIDTH + N_BRANCH * D_MODEL

kernel_name = "hybrid_mla_gmlp_hiermoe_block"


def _rmsnorm(x, g):
    xf = x.astype(jnp.float32)
    y = xf * lax.rsqrt(jnp.mean(xf * xf, axis=-1, keepdims=True) + EPS)
    return (y * g.astype(jnp.float32)).astype(x.dtype)


def _layernorm(x, g, b):
    xf = x.astype(jnp.float32)
    mu = jnp.mean(xf, axis=-1, keepdims=True)
    var = jnp.mean(jnp.square(xf - mu), axis=-1, keepdims=True)
    y = (xf - mu) * lax.rsqrt(var + EPS)
    return (y * g.astype(jnp.float32) + b.astype(jnp.float32)).astype(x.dtype)


def _rope(x, positions):
    r = x.shape[-1]
    freqs = ROPE_THETA ** (-jnp.arange(0, r, 2, dtype=jnp.float32) / r)
    ang = positions.astype(jnp.float32)[..., None] * freqs
    cos = jnp.cos(ang)[:, :, None, :]
    sin = jnp.sin(ang)[:, :, None, :]
    xf = x.astype(jnp.float32)
    x1, x2 = xf[..., : r // 2], xf[..., r // 2:]
    out = jnp.concatenate([x1 * cos - x2 * sin, x1 * sin + x2 * cos], axis=-1)
    return out.astype(x.dtype)


def _mla(c_q, c_kv, k_r, positions, g_cq, w_uq, g_ckv, w_uk, w_uv):
    b, s, _ = c_q.shape
    c_q = _rmsnorm(c_q, g_cq)
    q = (c_q @ w_uq).reshape(b, s, MLA_HEADS, QK_NOPE + QK_ROPE)
    q_nope = q[..., :QK_NOPE]
    q_rope = _rope(q[..., QK_NOPE:], positions)
    c_kv = _rmsnorm(c_kv, g_ckv)
    k_nope = (c_kv @ w_uk).reshape(b, s, MLA_HEADS, QK_NOPE)
    v = (c_kv @ w_uv).reshape(b, s, MLA_HEADS, V_HEAD)
    k_rope = _rope(k_r[:, :, None, :], positions)[:, :, 0, :]
    scale = (QK_NOPE + QK_ROPE) ** -0.5
    n_blk = s // Q_BLOCK
    qn_b = q_nope.reshape(b, n_blk, Q_BLOCK, MLA_HEADS, QK_NOPE).transpose(1, 0, 2, 3, 4)
    qr_b = q_rope.reshape(b, n_blk, Q_BLOCK, MLA_HEADS, QK_ROPE).transpose(1, 0, 2, 3, 4)
    key_chunk = jnp.arange(s) // CHUNK
    neg = jnp.finfo(jnp.float32).min

    def q_block(args):
        qn_i, qr_i, i = args
        sc = (jnp.einsum('bqhd,bkhd->bhqk', qn_i, k_nope)
              + jnp.einsum('bqhr,bkr->bhqk', qr_i, k_rope)).astype(jnp.float32) * scale
        q_chunk = (i * Q_BLOCK + jnp.arange(Q_BLOCK)) // CHUNK
        mask = key_chunk[None, :] <= q_chunk[:, None]
        sc = jnp.where(mask[None, None], sc, neg)
        p = jax.nn.softmax(sc, axis=-1).astype(v.dtype)
        return jnp.einsum('bhqk,bkhd->bqhd', p, v)

    o = lax.map(q_block, (qn_b, qr_b, jnp.arange(n_blk)))
    return o.transpose(1, 0, 2, 3, 4).reshape(b, s, MLA_WIDTH)


def _sgu(u, v, sgu_gain, sgu_bias, w_spatial, b_spatial):
    b, s, _ = v.shape
    v = _layernorm(v, sgu_gain, sgu_bias)
    v = v.reshape(b, s // SGU_LEN, SGU_LEN, SGU_GROUPS, SGU_GROUP_DIM)
    idx = jnp.arange(SGU_LEN) // CHUNK
    mask = idx[:, None] >= idx[None, :]
    w = jnp.where(mask[None], w_spatial, jnp.zeros_like(w_spatial))
    sv = jnp.einsum('gij,bnjgc->bnigc', w, v) + b_spatial.T[None, None, :, :, None]
    return u * sv.reshape(b, s, SGU_WIDTH)


def _hier_moe(h, w_rg, b_rg, w_re, b_re, w_eg, w_eu, w_ed):
    b, s, d = h.shape
    n = b * s
    xf = h.reshape(n, d)
    p_group = jax.nn.softmax((xf @ w_rg + b_rg).astype(jnp.float32), axis=-1)
    p_top, g_idx = lax.top_k(p_group, 1)
    le = (jnp.einsum('nd,gde->nge', xf, w_re) + b_re[None]).astype(jnp.float32)
    le_sel = jnp.take_along_axis(le, g_idx[:, :, None], axis=1)[:, 0, :]
    top_l, e_loc = lax.top_k(le_sel, TOP_K)
    gate = (jax.nn.softmax(top_l, axis=-1) * p_top).astype(h.dtype)
    e_glob = g_idx * EXPERTS_PER_GROUP + e_loc
    a = n * TOP_K
    e_flat = e_glob.reshape(a)
    tok_flat = jnp.repeat(jnp.arange(n, dtype=jnp.int32), TOP_K)
    w_flat = gate.reshape(a)
    order = jnp.argsort(e_flat)
    e_sorted = e_flat[order]
    tok_sorted = tok_flat[order]
    w_sorted = w_flat[order]
    counts = jnp.zeros((N_EXPERTS,), jnp.int32).at[e_flat].add(1)
    starts = jnp.cumsum(counts) - counts
    pcounts = (counts + MOE_BLOCK - 1) // MOE_BLOCK * MOE_BLOCK
    pends = jnp.cumsum(pcounts)
    pstarts = pends - pcounts
    rank = jnp.arange(a, dtype=jnp.int32) - starts[e_sorted]
    dest = pstarts[e_sorted] + rank
    p_rows = a + N_EXPERTS * MOE_BLOCK
    buf_tok = jnp.zeros((p_rows,), jnp.int32).at[dest].set(tok_sorted)
    buf_w = jnp.zeros((p_rows,), h.dtype).at[dest].set(w_sorted)
    n_blk = p_rows // MOE_BLOCK
    blk_e = jnp.clip(jnp.searchsorted(pends, jnp.arange(n_blk, dtype=jnp.int32) * MOE_BLOCK,
                                      side='right'), 0, N_EXPERTS - 1)
    xs = xf[buf_tok].reshape(n_blk, MOE_BLOCK, d)

    def expert_block(args):
        xb, e = args
        hid = jax.nn.silu(xb @ w_eg[e]) * (xb @ w_eu[e])
        return hid @ w_ed[e]

    ys = lax.map(expert_block, (xs, blk_e)).reshape(p_rows, d)
    out = jnp.zeros((n, d), h.dtype).at[buf_tok].add(ys * buf_w[:, None])
    return out.reshape(b, s, d)


def setup_inputs(seed: int = 0) -> dict:
    key = jax.random.key(seed)
    ks = jax.random.split(key, 26)
    f32 = jnp.float32

    def nrm(k, shape, fan_in):
        return jax.random.normal(k, shape, f32) * (fan_in ** -0.5)

    def gain(k, shape):
        return 1.0 + 0.02 * jax.random.normal(k, shape, f32)

    L = DEPTH
    x = jax.random.normal(ks[0], (BATCH, SEQ, D_MODEL), f32)
    offset = jax.random.randint(ks[1], (BATCH, 1), 0, 1024, dtype=jnp.int32)
    positions = offset + jnp.arange(SEQ, dtype=jnp.int32)[None, :]
    return {
        "x": x,
        "positions": positions,
        "g_attn_norm": gain(ks[2], (L, D_MODEL)),
        "w_in": nrm(ks[3], (L, D_MODEL, IN_COLS), D_MODEL),
        "g_cq": gain(ks[4], (L, Q_RANK)),
        "w_uq": nrm(ks[5], (L, Q_RANK, MLA_HEADS * (QK_NOPE + QK_ROPE)), Q_RANK),
        "g_ckv": gain(ks[6], (L, KV_RANK)),
        "w_uk": nrm(ks[7], (L, KV_RANK, MLA_HEADS * QK_NOPE), KV_RANK),
        "w_uv": nrm(ks[8], (L, KV_RANK, MLA_WIDTH), KV_RANK),
        "w_o_attn": nrm(ks[9], (L, MLA_WIDTH, D_MODEL), MLA_WIDTH),
        "sgu_gain": gain(ks[10], (L, SGU_WIDTH)),
        "sgu_bias": 0.02 * jax.random.normal(ks[11], (L, SGU_WIDTH), f32),
        "w_spatial": nrm(ks[12], (L, SGU_GROUPS, SGU_LEN, SGU_LEN), SGU_LEN),
        "b_spatial": gain(ks[13], (L, SGU_GROUPS, SGU_LEN)),
        "w_o_sgu": nrm(ks[14], (L, SGU_WIDTH, D_MODEL), SGU_WIDTH),
        "w_out": nrm(ks[15], (L, D_MODEL, D_MODEL), D_MODEL),
        "g_ffn_norm": gain(ks[16], (L, D_MODEL)),
        "w_router_group": nrm(ks[17], (L, D_MODEL, N_GROUPS), D_MODEL),
        "b_router_group": 0.01 * jax.random.normal(ks[18], (L, N_GROUPS), f32),
        "w_router_expert": nrm(ks[19], (L, N_GROUPS, D_MODEL, EXPERTS_PER_GROUP), D_MODEL),
        "b_router_expert": 0.01 * jax.random.normal(ks[20], (L, N_GROUPS, EXPERTS_PER_GROUP), f32),
        "w_exp_gate": nrm(ks[21], (L, N_EXPERTS, D_MODEL, D_EXPERT), D_MODEL),
        "w_exp_up": nrm(ks[22], (L, N_EXPERTS, D_MODEL, D_EXPERT), D_MODEL),
        "w_exp_down": nrm(ks[23], (L, N_EXPERTS, D_EXPERT, D_MODEL), D_EXPERT),
        "g_final": gain(ks[24], (D_MODEL,)),
    }


def reference(x, positions, g_attn_norm, w_in, g_cq, w_uq, g_ckv, w_uk, w_uv, w_o_attn,
              sgu_gain, sgu_bias, w_spatial, b_spatial, w_o_sgu, w_out, g_ffn_norm,
              w_router_group, b_router_group, w_router_expert, b_router_expert,
              w_exp_gate, w_exp_up, w_exp_down, g_final):
    for l in range(DEPTH):
        h = _rmsnorm(x, g_attn_norm[l])
        z = h @ w_in[l]
        c_q, c_kv, k_r, uv, gate_logits = jnp.split(z, SPLIT_POINTS, axis=-1)
        attn = _mla(c_q, c_kv, k_r, positions, g_cq[l], w_uq[l], g_ckv[l], w_uk[l], w_uv[l])
        attn = attn @ w_o_attn[l]
        uv = jax.nn.gelu(uv)
        u, v = uv[..., :SGU_WIDTH], uv[..., SGU_WIDTH:]
        sgu = _sgu(u, v, sgu_gain[l], sgu_bias[l], w_spatial[l], b_spatial[l]) @ w_o_sgu[l]
        gates = jax.nn.sigmoid(gate_logits.astype(jnp.float32)).astype(x.dtype)
        g_a, g_b = gates[..., :D_MODEL], gates[..., D_MODEL:]
        x = x + (g_a * attn + g_b * sgu) @ w_out[l]
        x = x + _hier_moe(_rmsnorm(x, g_ffn_norm[l]), w_router_group[l], b_router_group[l],
                          w_router_expert[l], b_router_expert[l],
                          w_exp_gate[l], w_exp_up[l], w_exp_down[l])
    return _rmsnorm(x, g_final)
```

```python
import functools

import jax
import jax.numpy as jnp
from jax import lax
from jax.experimental import pallas as pl
from jax.experimental.pallas import tpu as pltpu

D_MODEL = 1024
BATCH = 2
SEQ = 8192
N_TOK = BATCH * SEQ
CHUNK = 64
EPS = 1e-6
MLA_HEADS = 8
Q_RANK = 256
KV_RANK = 128
QK_NOPE = 64
QK_ROPE = 32
V_HEAD = 64
MLA_WIDTH = MLA_HEADS * V_HEAD
ROPE_THETA = 10000.0
SGU_GROUPS = 8
SGU_GROUP_DIM = 64
SGU_WIDTH = SGU_GROUPS * SGU_GROUP_DIM
SGU_LEN = 128
N_GROUPS = 4
EXPERTS_PER_GROUP = 8
N_EXPERTS = N_GROUPS * EXPERTS_PER_GROUP
TOP_K = 2
D_EXPERT = 256

LANES = 128
QK_PAD = 2 * LANES
SCALE = (QK_NOPE + QK_ROPE) ** -0.5
NEG = -1e30

TM = 256
TQ = 128
TK = 256
TK_SHIFT = TK.bit_length() - 1
CHUNK_SHIFT = CHUNK.bit_length() - 1
assert 1 << TK_SHIFT == TK and 1 << CHUNK_SHIFT == CHUNK
MOE_T = 256
MOE_BLOCKS = (N_TOK * TOP_K) // MOE_T + N_EXPERTS
MOE_ROWS = MOE_BLOCKS * MOE_T
DMA_BATCH = 256

C_Q = 0
C_KV = C_Q + Q_RANK
C_KR = C_KV + KV_RANK
C_KRS = C_KR + LANES
C_U = C_KRS + LANES
C_V = C_U + SGU_WIDTH
C_GA = C_V + SGU_WIDTH
C_GB = C_GA + D_MODEL
C_END = C_GB + D_MODEL

F32 = jnp.float32
BF16 = jnp.bfloat16


def _dot(a, b):
    return jnp.dot(a, b, preferred_element_type=F32)


def _rms(x, g):
    return x * lax.rsqrt(jnp.mean(x * x, axis=-1, keepdims=True) + EPS) * g


def _fold_kernel(a_ref, b_ref, o_ref):
    o_ref[...] = SCALE * jnp.dot(a_ref[...], b_ref[...], preferred_element_type=F32,
                                 precision=lax.Precision.HIGHEST)


def _fold(w_uq_nope, w_ukt):
    return pl.pallas_call(
        _fold_kernel,
        grid=(MLA_HEADS,),
        in_specs=[pl.BlockSpec((None, Q_RANK, QK_NOPE), lambda h: (h, 0, 0)),
                  pl.BlockSpec((None, QK_NOPE, KV_RANK), lambda h: (h, 0, 0))],
        out_specs=pl.BlockSpec((None, Q_RANK, KV_RANK), lambda h: (h, 0, 0)),
        out_shape=jax.ShapeDtypeStruct((MLA_HEADS, Q_RANK, KV_RANK), F32),
        name="fold",
    )(w_uq_nope, w_ukt)


def _inproj_kernel(x_ref, pos_ref, gattn_ref, win_ref, gcq_ref, gckv_ref, wqlat_ref, wqr_ref,
                   wqrs_ref, freq_ref, sgn_ref, sgain_ref, sbias_ref, wsp_ref, bsp_ref, wosgu_ref,
                   q_ref, kt_ref, v_ref, p_ref, ga_ref):
    hb = _rms(x_ref[...], gattn_ref[...]).astype(BF16)

    def proj(a, b):
        return _dot(hb, win_ref[:, a:b])

    ang = pos_ref[...] * freq_ref[...]
    cos_t = jnp.cos(ang)
    sin_t = jnp.sin(ang) * sgn_ref[...]

    cqn = _rms(proj(C_Q, C_KV), gcq_ref[...]).astype(BF16)
    qlat = _dot(cqn, wqlat_ref[...])
    qr = _dot(cqn, wqr_ref[...])
    qs = _dot(cqn, wqrs_ref[...])
    cos_q = cos_t * SCALE
    sin_q = sin_t * SCALE
    for h in range(MLA_HEADS):
        a, b = h * LANES, (h + 1) * LANES
        q_ref[:, h * QK_PAD:h * QK_PAD + LANES] = qlat[:, a:b].astype(BF16)
        q_ref[:, h * QK_PAD + LANES:(h + 1) * QK_PAD] = (qr[:, a:b] * cos_q + qs[:, a:b] * sin_q).astype(BF16)

    zk = proj(C_KV, C_U)
    ckvn = _rms(zk[:, :KV_RANK], gckv_ref[...])
    krope = zk[:, KV_RANK:KV_RANK + LANES] * cos_t + zk[:, KV_RANK + LANES:] * sin_t
    kt_ref[...] = jnp.concatenate([ckvn, krope], axis=1).T.astype(BF16)
    v_ref[...] = ckvn.astype(BF16)

    u = jax.nn.gelu(proj(C_U, C_V))
    v = jax.nn.gelu(proj(C_V, C_GA))
    mu = jnp.mean(v, axis=-1, keepdims=True)
    vc = v - mu
    var = jnp.mean(vc * vc, axis=-1, keepdims=True)
    vb = (vc * lax.rsqrt(var + EPS) * sgain_ref[...] + sbias_ref[...]).astype(BF16)
    row = lax.broadcasted_iota(jnp.int32, (SGU_LEN, SGU_LEN), 0)
    col = lax.broadcasted_iota(jnp.int32, (SGU_LEN, SGU_LEN), 1)
    causal = (row >> CHUNK_SHIFT) >= (col >> CHUNK_SHIFT)
    low_half = col < SGU_GROUP_DIM
    row_blocks = []
    for r in range(TM // SGU_LEN):
        pieces = []
        for p in range(SGU_WIDTH // LANES):
            blk = vb[r * SGU_LEN:(r + 1) * SGU_LEN, p * LANES:(p + 1) * LANES]
            w0 = jnp.where(causal, wsp_ref[2 * p], jnp.zeros_like(wsp_ref[2 * p]))
            w1 = jnp.where(causal, wsp_ref[2 * p + 1], jnp.zeros_like(wsp_ref[2 * p + 1]))
            sv = jnp.where(low_half, _dot(w0, blk), _dot(w1, blk)) + bsp_ref[p]
            pieces.append((u[r * SGU_LEN:(r + 1) * SGU_LEN, p * LANES:(p + 1) * LANES] * sv).astype(BF16))
        row_blocks.append(jnp.concatenate(pieces, axis=1))
    sgu = _dot(jnp.concatenate(row_blocks, axis=0), wosgu_ref[...])

    ga_ref[...] = jax.nn.sigmoid(proj(C_GA, C_GB)).astype(BF16)
    p_ref[...] = (jax.nn.sigmoid(proj(C_GB, C_END)) * sgu).astype(BF16)


def _inproj(x, pos, gattn, win, gcq, gckv, wqlat, wqr, wqrs, freq, sgn, sgain, sbias, wsp, bsp, wosgu):
    nt = N_TOK // TM
    per_b = SEQ // TM

    def const(shape):
        return pl.BlockSpec(shape, lambda i: (0,) * len(shape))

    return pl.pallas_call(
        _inproj_kernel,
        grid=(nt,),
        in_specs=[pl.BlockSpec((TM, D_MODEL), lambda i: (i, 0)),
                  pl.BlockSpec((TM, 1), lambda i: (i, 0)),
                  const((1, D_MODEL)), const((D_MODEL, C_END)), const((1, Q_RANK)), const((1, KV_RANK)),
                  const((Q_RANK, MLA_HEADS * LANES)), const((Q_RANK, MLA_HEADS * LANES)),
                  const((Q_RANK, MLA_HEADS * LANES)), const((1, LANES)), const((1, LANES)),
                  const((1, SGU_WIDTH)), const((1, SGU_WIDTH)),
                  const((SGU_GROUPS, SGU_LEN, SGU_LEN)), const((SGU_GROUPS // 2, SGU_LEN, LANES)),
                  const((SGU_WIDTH, D_MODEL))],
        out_specs=[pl.BlockSpec((TM, MLA_HEADS * QK_PAD), lambda i: (i, 0)),
                   pl.BlockSpec((None, None, QK_PAD, TM), lambda i: (i // per_b, i % per_b, 0, 0)),
                   pl.BlockSpec((None, None, TM, KV_RANK), lambda i: (i // per_b, i % per_b, 0, 0)),
                   pl.BlockSpec((TM, D_MODEL), lambda i: (i, 0)),
                   pl.BlockSpec((TM, D_MODEL), lambda i: (i, 0))],
        out_shape=[jax.ShapeDtypeStruct((N_TOK, MLA_HEADS * QK_PAD), BF16),
                   jax.ShapeDtypeStruct((BATCH, per_b, QK_PAD, TM), BF16),
                   jax.ShapeDtypeStruct((BATCH, per_b, TM, KV_RANK), BF16),
                   jax.ShapeDtypeStruct((N_TOK, D_MODEL), BF16),
                   jax.ShapeDtypeStruct((N_TOK, D_MODEL), BF16)],
        compiler_params=pltpu.CompilerParams(dimension_semantics=("arbitrary",),
                                             vmem_limit_bytes=56 * 1024 * 1024),
        name="inproj",
    )(x, pos, gattn, win, gcq, gckv, wqlat, wqr, wqrs, freq, sgn, sgain, sbias, wsp, bsp, wosgu)


def _attn_kernel(q_ref, kt_ref, v_ref, wuv_ref, o_ref, m_ref, l_ref, acc_ref):
    qi = pl.program_id(1)
    m_ref[...] = jnp.full(m_ref.shape, NEG, F32)
    l_ref[...] = jnp.zeros(l_ref.shape, F32)
    acc_ref[...] = jnp.zeros(acc_ref.shape, F32)

    def step(j, visible):
        kt = kt_ref[j]
        vv = v_ref[j]
        for h in range(MLA_HEADS):
            s = _dot(q_ref[:, h * QK_PAD:(h + 1) * QK_PAD], kt)
            if visible is not None:
                s = jnp.where(visible, s, NEG)
            m_prev = m_ref[h]
            m_new = jnp.maximum(m_prev, jnp.max(s, axis=-1, keepdims=True))
            alpha = jnp.exp(m_prev - m_new)
            p = jnp.exp(s - jnp.concatenate([m_new] * (TK // LANES), axis=1))
            l_ref[h] = alpha * l_ref[h] + jnp.sum(p, axis=-1, keepdims=True)
            acc_ref[h] = alpha * acc_ref[h] + _dot(p.astype(BF16), vv)
            m_ref[h] = m_new

    diag = (qi * TQ) >> TK_SHIFT

    def body(j, carry):
        step(j, None)
        return carry

    lax.fori_loop(0, diag, body, 0)

    q_chunk = ((qi * TQ - diag * TK) >> CHUNK_SHIFT) + (
        lax.broadcasted_iota(jnp.int32, (TQ, TK), 0) >> CHUNK_SHIFT)
    k_chunk = lax.broadcasted_iota(jnp.int32, (TQ, TK), 1) >> CHUNK_SHIFT
    step(diag, k_chunk <= q_chunk)

    out = jnp.zeros((TQ, MLA_WIDTH), F32)
    for h in range(MLA_HEADS):
        out = out + _dot((acc_ref[h] / l_ref[h]).astype(BF16), wuv_ref[h])
    o_ref[...] = out.astype(BF16)


def _attention(q, kt, v, wuv):
    nk = SEQ // TK
    return pl.pallas_call(
        _attn_kernel,
        grid=(BATCH, SEQ // TQ),
        in_specs=[pl.BlockSpec((None, TQ, MLA_HEADS * QK_PAD), lambda b, i: (b, i, 0)),
                  pl.BlockSpec((None, nk, QK_PAD, TK), lambda b, i: (b, 0, 0, 0)),
                  pl.BlockSpec((None, nk, TK, KV_RANK), lambda b, i: (b, 0, 0, 0)),
                  pl.BlockSpec((MLA_HEADS, KV_RANK, MLA_WIDTH), lambda b, i: (0, 0, 0))],
        out_specs=pl.BlockSpec((None, TQ, MLA_WIDTH), lambda b, i: (b, i, 0)),
        out_shape=jax.ShapeDtypeStruct((BATCH, SEQ, MLA_WIDTH), BF16),
        scratch_shapes=[pltpu.VMEM((MLA_HEADS, TQ, LANES), F32),
                        pltpu.VMEM((MLA_HEADS, TQ, LANES), F32),
                        pltpu.VMEM((MLA_HEADS, TQ, KV_RANK), F32)],
        compiler_params=pltpu.CompilerParams(dimension_semantics=("arbitrary", "arbitrary"),
                                             vmem_limit_bytes=40 * 1024 * 1024),
        name="attention",
    )(q, kt, v, wuv)


def _mix_kernel(attn_ref, ga_ref, p_ref, x_ref, woa_ref, wout_ref, gffn_ref, wr_ref, br_ref,
                x1_ref, h2_ref, mi_ref, mf_ref, cnt_ref, carry_ref):
    i = pl.program_id(0)

    @pl.when(i == 0)
    def _():
        carry_ref[...] = jnp.zeros(carry_ref.shape, F32)

    a = _dot(attn_ref[...], woa_ref[...])
    mix = (ga_ref[...].astype(F32) * a + p_ref[...].astype(F32)).astype(BF16)
    x1 = x_ref[...] + _dot(mix, wout_ref[...])
    x1_ref[...] = x1
    h2 = _rms(x1, gffn_ref[...])
    h2_ref[...] = h2

    hi = h2.astype(BF16)
    lo = (h2 - hi.astype(F32)).astype(BF16)
    r1 = _dot(hi, wr_ref[...])
    r2 = _dot(lo, wr_ref[:, :LANES])
    logits = r1[:, :LANES] + r1[:, LANES:] + r2 + br_ref[...]

    lane_i = lax.broadcasted_iota(jnp.int32, (TM, LANES), 1)
    lane = lane_i.astype(F32)
    lane_group = (lane_i >> 3).astype(F32)
    ninf = -jnp.inf
    is_group = (lane_i >= N_EXPERTS) & (lane_i < N_EXPERTS + N_GROUPS)
    lg = jnp.where(is_group, logits, ninf)
    gmax = jnp.max(lg, axis=-1, keepdims=True)
    gsum = jnp.sum(jnp.exp(lg - gmax), axis=-1, keepdims=True)
    p_top = 1.0 / gsum
    g_idx = jnp.min(jnp.where(lg == gmax, lane - N_EXPERTS, float(N_GROUPS)), axis=-1, keepdims=True)
    le = jnp.where((lane_i < N_EXPERTS) & (lane_group == g_idx), logits, ninf)
    t1 = jnp.max(le, axis=-1, keepdims=True)
    e1 = jnp.min(jnp.where(le == t1, lane, float(LANES)), axis=-1, keepdims=True)
    le2 = jnp.where(lane == e1, ninf, le)
    t2 = jnp.max(le2, axis=-1, keepdims=True)
    e2 = jnp.min(jnp.where(le2 == t2, lane, float(LANES)), axis=-1, keepdims=True)
    ex = jnp.exp(t2 - t1)
    w1 = p_top / (1.0 + ex)
    w2 = p_top * ex / (1.0 + ex)

    sel1 = lane == e1
    sel2 = lane == e2
    onehot = jnp.where(sel1 | sel2, 1.0, 0.0)
    rr = lax.broadcasted_iota(jnp.int32, (TM, TM), 0)
    cc = lax.broadcasted_iota(jnp.int32, (TM, TM), 1)
    ltri = jnp.where(cc < rr, 1.0, 0.0).astype(BF16)
    before = carry_ref[...] + _dot(ltri, onehot.astype(BF16))
    rank1 = jnp.sum(jnp.where(sel1, before, 0.0), axis=-1, keepdims=True).astype(jnp.int32)
    rank2 = jnp.sum(jnp.where(sel2, before, 0.0), axis=-1, keepdims=True).astype(jnp.int32)
    carry = carry_ref[...] + jnp.sum(onehot, axis=0, keepdims=True)
    carry_ref[...] = carry
    cnt_ref[...] = carry

    mi_ref[...] = jnp.where(lane_i == 0, e1.astype(jnp.int32),
                            jnp.where(lane_i == 1, e2.astype(jnp.int32),
                                      jnp.where(lane_i == 2, rank1, rank2)))
    mf_ref[...] = jnp.where(lane_i == 0, w1, w2)


def _mix(attn, ga, p, x, woa, wout, gffn, wr, br):
    nt = N_TOK // TM

    def const(shape):
        return pl.BlockSpec(shape, lambda i: (0,) * len(shape))

    def rows(width):
        return pl.BlockSpec((TM, width), lambda i: (i, 0))

    return pl.pallas_call(
        _mix_kernel,
        grid=(nt,),
        in_specs=[rows(MLA_WIDTH), rows(D_MODEL), rows(D_MODEL), rows(D_MODEL),
                  const((MLA_WIDTH, D_MODEL)), const((D_MODEL, D_MODEL)), const((1, D_MODEL)),
                  const((D_MODEL, 2 * LANES)), const((1, LANES))],
        out_specs=[rows(D_MODEL), rows(D_MODEL), rows(LANES), rows(LANES), const((1, LANES))],
        out_shape=[jax.ShapeDtypeStruct((N_TOK, D_MODEL), F32),
                   jax.ShapeDtypeStruct((N_TOK, D_MODEL), F32),
                   jax.ShapeDtypeStruct((N_TOK, LANES), jnp.int32),
                   jax.ShapeDtypeStruct((N_TOK, LANES), F32),
                   jax.ShapeDtypeStruct((1, LANES), F32)],
        scratch_shapes=[pltpu.VMEM((1, LANES), F32)],
        compiler_params=pltpu.CompilerParams(dimension_semantics=("arbitrary",),
                                             vmem_limit_bytes=40 * 1024 * 1024),
        name="mix",
    )(attn, ga, p, x, woa, wout, gffn, wr, br)


def _row_copy(src_hbm, dst, s, d, sem):
    return pltpu.make_async_copy(src_hbm.at[pl.ds(s, 1)], dst.at[pl.ds(d, 1)], sem)


def _dispatch_kernel(dest_ref, h2_hbm, xs_in_hbm, xs_hbm, sem):
    del xs_in_hbm

    def batch(c, carry):
        base = c * DMA_BATCH

        def issue(t, carry2):
            n = base + t
            _row_copy(h2_hbm, xs_hbm, n, dest_ref[2 * n], sem).start()
            _row_copy(h2_hbm, xs_hbm, n, dest_ref[2 * n + 1], sem).start()
            return carry2

        lax.fori_loop(0, DMA_BATCH, issue, 0)

        def drain(t, carry2):
            _row_copy(h2_hbm, xs_hbm, 0, 0, sem).wait()
            _row_copy(h2_hbm, xs_hbm, 0, 0, sem).wait()
            return carry2

        lax.fori_loop(0, DMA_BATCH, drain, 0)
        return carry

    lax.fori_loop(0, N_TOK // DMA_BATCH, batch, 0)


def _dispatch(dest, h2, xs0):
    return pl.pallas_call(
        _dispatch_kernel,
        grid_spec=pltpu.PrefetchScalarGridSpec(
            num_scalar_prefetch=1,
            grid=(1,),
            in_specs=[pl.BlockSpec(memory_space=pl.ANY), pl.BlockSpec(memory_space=pl.ANY)],
            out_specs=pl.BlockSpec(memory_space=pl.ANY),
            scratch_shapes=[pltpu.SemaphoreType.DMA(())]),
        out_shape=jax.ShapeDtypeStruct((MOE_ROWS, D_MODEL), F32),
        input_output_aliases={2: 0},
        name="dispatch",
    )(dest, h2, xs0)


def _expert_kernel(blk_e_ref, nvb_ref, xs_ref, wg_ref, wu_ref, wd_ref, ys_ref):
    del blk_e_ref
    live = pl.program_id(0) < nvb_ref[0]

    @pl.when(live)
    def _():
        xb = xs_ref[...].astype(BF16)
        g = _dot(xb, wg_ref[...].astype(BF16))
        u = _dot(xb, wu_ref[...].astype(BF16))
        hid = (jax.nn.silu(g) * u).astype(BF16)
        ys_ref[...] = _dot(hid, wd_ref[...].astype(BF16))

    @pl.when(jnp.logical_not(live))
    def _():
        ys_ref[...] = jnp.zeros(ys_ref.shape, ys_ref.dtype)


def _experts(blk_e, nvb, xs, wg, wu, wd):
    def row_block(i, be, nv):
        return (jnp.minimum(i, nv[0] - 1), 0)

    def weight(i, be, nv):
        return (be[jnp.minimum(i, nv[0] - 1)], 0, 0)

    return pl.pallas_call(
        _expert_kernel,
        grid_spec=pltpu.PrefetchScalarGridSpec(
            num_scalar_prefetch=2,
            grid=(MOE_BLOCKS,),
            in_specs=[pl.BlockSpec((MOE_T, D_MODEL), row_block),
                      pl.BlockSpec((None, D_MODEL, D_EXPERT), weight),
                      pl.BlockSpec((None, D_MODEL, D_EXPERT), weight),
                      pl.BlockSpec((None, D_EXPERT, D_MODEL), weight)],
            out_specs=pl.BlockSpec((MOE_T, D_MODEL), lambda i, be, nv: (i, 0))),
        out_shape=jax.ShapeDtypeStruct((MOE_ROWS, D_MODEL), F32),
        compiler_params=pltpu.CompilerParams(dimension_semantics=("arbitrary",),
                                             vmem_limit_bytes=40 * 1024 * 1024),
        name="experts",
    )(blk_e, nvb, xs, wg, wu, wd)


def _combine_kernel(dest_ref, x1_ref, mf_ref, gfin_ref, ys_hbm, o_ref, ybuf, sem):
    base = pl.program_id(0) * TM

    def issue(t, carry):
        n = base + t
        _row_copy(ys_hbm, ybuf.at[0], dest_ref[2 * n], t, sem).start()
        _row_copy(ys_hbm, ybuf.at[1], dest_ref[2 * n + 1], t, sem).start()
        return carry

    lax.fori_loop(0, TM, issue, 0)

    def drain(t, carry):
        _row_copy(ys_hbm, ybuf.at[0], 0, 0, sem).wait()
        _row_copy(ys_hbm, ybuf.at[1], 0, 0, sem).wait()
        return carry

    lax.fori_loop(0, TM, drain, 0)

    mf = mf_ref[...]
    x2 = x1_ref[...] + mf[:, 0:1] * ybuf[0] + mf[:, 1:2] * ybuf[1]
    o_ref[...] = _rms(x2, gfin_ref[...])


def _combine(dest, x1, mf, gfin, ys):
    return pl.pallas_call(
        _combine_kernel,
        grid_spec=pltpu.PrefetchScalarGridSpec(
            num_scalar_prefetch=1,
            grid=(N_TOK // TM,),
            in_specs=[pl.BlockSpec((TM, D_MODEL), lambda i, d: (i, 0)),
                      pl.BlockSpec((TM, LANES), lambda i, d: (i, 0)),
                      pl.BlockSpec((1, D_MODEL), lambda i, d: (0, 0)),
                      pl.BlockSpec(memory_space=pl.ANY)],
            out_specs=pl.BlockSpec((TM, D_MODEL), lambda i, d: (i, 0)),
            scratch_shapes=[pltpu.VMEM((TOP_K, TM, D_MODEL), F32), pltpu.SemaphoreType.DMA(())]),
        out_shape=jax.ShapeDtypeStruct((N_TOK, D_MODEL), F32),
        compiler_params=pltpu.CompilerParams(dimension_semantics=("arbitrary",)),
        name="combine",
    )(dest, x1, mf, gfin, ys)


def kernel(x, positions, g_attn_norm, w_in, g_cq, w_uq, g_ckv, w_uk, w_uv, w_o_attn, sgu_gain, sgu_bias, w_spatial, b_spatial, w_o_sgu, w_out, g_ffn_norm, w_router_group, b_router_group, w_router_expert, b_router_expert, w_exp_gate, w_exp_up, w_exp_down, g_final):
    assert x.shape == (BATCH, SEQ, D_MODEL) and w_in.shape[0] == 1
    half = QK_ROPE // 2
    swap = jnp.concatenate([jnp.arange(half, QK_ROPE), jnp.arange(0, half)])

    def pad_cols(w, width):
        return jnp.pad(w, ((0, 0), (0, width - w.shape[1])))

    wi = w_in[0]
    c0 = Q_RANK + KV_RANK
    kr = wi[:, c0:c0 + QK_ROPE]
    c1 = c0 + QK_ROPE
    win = jnp.concatenate([
        wi[:, :c0], pad_cols(kr, LANES), pad_cols(kr[:, swap], LANES), wi[:, c1:]], axis=1).astype(BF16)

    wq = w_uq[0].reshape(Q_RANK, MLA_HEADS, QK_NOPE + QK_ROPE)
    wq_nope = wq[:, :, :QK_NOPE].transpose(1, 0, 2)
    wq_rope = wq[:, :, QK_NOPE:]
    w_ukt = w_uk[0].reshape(KV_RANK, MLA_HEADS, QK_NOPE).transpose(1, 2, 0)
    wqlat = _fold(wq_nope, w_ukt).transpose(1, 0, 2).reshape(Q_RANK, MLA_HEADS * LANES).astype(BF16)

    def rope_cols(w):
        return jnp.pad(w, ((0, 0), (0, 0), (0, LANES - QK_ROPE))).reshape(Q_RANK, MLA_HEADS * LANES).astype(BF16)

    wqr = rope_cols(wq_rope)
    wqrs = rope_cols(wq_rope[:, :, swap])

    freqs = ROPE_THETA ** (-jnp.arange(0, QK_ROPE, 2, dtype=F32) / QK_ROPE)
    freq = pad_cols(jnp.concatenate([freqs, freqs])[None, :], LANES)
    sgn = pad_cols(jnp.concatenate([-jnp.ones((half,), F32), jnp.ones((half,), F32)])[None, :], LANES)

    head_of_col = jnp.arange(MLA_WIDTH) // V_HEAD
    wuv = jnp.where(head_of_col[None, None, :] == jnp.arange(MLA_HEADS)[:, None, None],
                    w_uv[0][None], 0.0).astype(BF16)

    wsp = w_spatial[0].astype(BF16)
    bs = b_spatial[0]
    bsp = jnp.repeat(bs.reshape(SGU_GROUPS // 2, 2, SGU_LEN).transpose(0, 2, 1), SGU_GROUP_DIM, axis=2)

    wr32 = jnp.concatenate([w_router_expert[0].transpose(1, 0, 2).reshape(D_MODEL, N_EXPERTS),
                            w_router_group[0]], axis=1)
    wr32 = pad_cols(wr32, LANES)
    wr_hi = wr32.astype(BF16)
    wr_lo = (wr32 - wr_hi.astype(F32)).astype(BF16)
    wr = jnp.concatenate([wr_hi, wr_lo], axis=1)
    br = pad_cols(jnp.concatenate([b_router_expert[0].reshape(-1), b_router_group[0]])[None, :], LANES)

    xf = x.reshape(N_TOK, D_MODEL)
    pos = positions.astype(F32).reshape(N_TOK, 1)
    q, kt, v, p, ga = _inproj(
        xf, pos, g_attn_norm, win, g_cq, g_ckv, wqlat, wqr, wqrs, freq, sgn,
        sgu_gain, sgu_bias, wsp, bsp, w_o_sgu[0].astype(BF16))
    attn = _attention(q.reshape(BATCH, SEQ, MLA_HEADS * QK_PAD), kt, v, wuv)
    x1, h2, mi, mf, cnt = _mix(attn.reshape(N_TOK, MLA_WIDTH), ga, p, xf, w_o_attn[0].astype(BF16),
                               w_out[0].astype(BF16), g_ffn_norm, wr, br)

    counts = cnt[0, :N_EXPERTS].astype(jnp.int32)
    pcounts = (counts + MOE_T - 1) // MOE_T * MOE_T
    pends = jnp.cumsum(pcounts)
    pstarts = pends - pcounts
    dest = (pstarts[mi[:, 0:2]] + mi[:, 2:4]).reshape(N_TOK * TOP_K)
    nvb = (pends[-1:] // MOE_T).astype(jnp.int32)
    blk_start = jnp.arange(MOE_BLOCKS, dtype=jnp.int32) * MOE_T
    blk_e = jnp.minimum(jnp.sum((pends[None, :] <= blk_start[:, None]).astype(jnp.int32), axis=1),
                        N_EXPERTS - 1)

    xs = _dispatch(dest, h2, jnp.zeros((MOE_ROWS, D_MODEL), F32))
    ys = _experts(blk_e, nvb, xs, w_exp_gate[0], w_exp_up[0], w_exp_down[0])
    out = _combine(dest, x1, mf, g_final.reshape(1, D_MODEL), ys)
    return out.reshape(BATCH, SEQ, D_MODEL)
```

```python
import functools

import jax
import jax.numpy as jnp
from jax import lax
from jax.experimental import pallas as pl
from jax.experimental.pallas import tpu as pltpu

D_MODEL = 1024
BATCH = 2
SEQ = 8192
N_TOK = BATCH * SEQ
CHUNK = 64
EPS = 1e-6
MLA_HEADS = 8
Q_RANK = 256
KV_RANK = 128
QK_NOPE = 64
QK_ROPE = 32
V_HEAD = 64
MLA_WIDTH = MLA_HEADS * V_HEAD
ROPE_THETA = 10000.0
SGU_GROUPS = 8
SGU_GROUP_DIM = 64
SGU_WIDTH = SGU_GROUPS * SGU_GROUP_DIM
SGU_LEN = 128
N_GROUPS = 4
EXPERTS_PER_GROUP = 8
N_EXPERTS = N_GROUPS * EXPERTS_PER_GROUP
TOP_K = 2
D_EXPERT = 256

LANES = 128
QK_PAD = 2 * LANES
SCALE = (QK_NOPE + QK_ROPE) ** -0.5
LOG2E = 1.4426950408889634
Q_SCALE = SCALE * LOG2E
NEG = -1e30

TM = 256
TQ = 128
TK = 512
TK_SHIFT = TK.bit_length() - 1
CHUNK_SHIFT = CHUNK.bit_length() - 1
assert 1 << TK_SHIFT == TK and 1 << CHUNK_SHIFT == CHUNK
MOE_T = 256
MOE_BLOCKS = (N_TOK * TOP_K) // MOE_T + N_EXPERTS
MOE_ROWS = MOE_BLOCKS * MOE_T

C_Q = 0
C_KV = C_Q + Q_RANK
C_KR = C_KV + KV_RANK
C_KRS = C_KR + LANES
C_U = C_KRS + LANES
C_V = C_U + SGU_WIDTH
C_GA = C_V + SGU_WIDTH
C_GB = C_GA + D_MODEL
C_END = C_GB + D_MODEL

F32 = jnp.float32
BF16 = jnp.bfloat16


def _dot(a, b):
    return jnp.dot(a, b, preferred_element_type=F32)


def _rms(x, g):
    return x * lax.rsqrt(jnp.mean(x * x, axis=-1, keepdims=True) + EPS) * g


def _fold_kernel(a_ref, b_ref, o_ref):
    o_ref[...] = Q_SCALE * jnp.dot(a_ref[...], b_ref[...], preferred_element_type=F32,
                                 precision=lax.Precision.HIGHEST)


def _fold(w_uq_nope, w_ukt):
    return pl.pallas_call(
        _fold_kernel,
        grid=(MLA_HEADS,),
        in_specs=[pl.BlockSpec((None, Q_RANK, QK_NOPE), lambda h: (h, 0, 0)),
                  pl.BlockSpec((None, QK_NOPE, KV_RANK), lambda h: (h, 0, 0))],
        out_specs=pl.BlockSpec((None, Q_RANK, KV_RANK), lambda h: (h, 0, 0)),
        out_shape=jax.ShapeDtypeStruct((MLA_HEADS, Q_RANK, KV_RANK), F32),
        name="fold",
    )(w_uq_nope, w_ukt)


def _inproj_kernel(x_ref, pos_ref, gattn_ref, win_ref, gcq_ref, gckv_ref, wqlat_ref, wqr_ref,
                   wqrs_ref, freq_ref, sgn_ref, sgain_ref, sbias_ref, wsp_ref, bsp_ref, wosgu_ref,
                   q_ref, kt_ref, v_ref, p_ref, ga_ref):
    hb = _rms(x_ref[...], gattn_ref[...]).astype(BF16)

    def proj(a, b):
        return _dot(hb, win_ref[:, a:b])

    ang = pos_ref[...] * freq_ref[...]
    cos_t = jnp.cos(ang)
    sin_t = jnp.sin(ang) * sgn_ref[...]

    cqn = _rms(proj(C_Q, C_KV), gcq_ref[...]).astype(BF16)
    qlat = _dot(cqn, wqlat_ref[...])
    qr = _dot(cqn, wqr_ref[...])
    qs = _dot(cqn, wqrs_ref[...])
    cos_q = cos_t * Q_SCALE
    sin_q = sin_t * Q_SCALE
    for r in range(TM // TQ):
        t0, t1 = r * TQ, (r + 1) * TQ
        for h in range(MLA_HEADS):
            a, b = h * LANES, (h + 1) * LANES
            q_ref[r, h * TQ:(h + 1) * TQ, :LANES] = qlat[t0:t1, a:b].astype(BF16)
            q_ref[r, h * TQ:(h + 1) * TQ, LANES:] = (
                qr[t0:t1, a:b] * cos_q[t0:t1] + qs[t0:t1, a:b] * sin_q[t0:t1]).astype(BF16)

    zk = proj(C_KV, C_U)
    ckvn = _rms(zk[:, :KV_RANK], gckv_ref[...])
    krope = zk[:, KV_RANK:KV_RANK + LANES] * cos_t + zk[:, KV_RANK + LANES:] * sin_t
    kt_ref[...] = jnp.concatenate([ckvn, krope], axis=1).T.astype(BF16)
    v_ref[...] = ckvn.astype(BF16)

    u = jax.nn.gelu(proj(C_U, C_V))
    v = jax.nn.gelu(proj(C_V, C_GA))
    mu = jnp.mean(v, axis=-1, keepdims=True)
    vc = v - mu
    var = jnp.mean(vc * vc, axis=-1, keepdims=True)
    vb = (vc * lax.rsqrt(var + EPS) * sgain_ref[...] + sbias_ref[...]).astype(BF16)
    row = lax.broadcasted_iota(jnp.int32, (SGU_LEN, SGU_LEN), 0)
    col = lax.broadcasted_iota(jnp.int32, (SGU_LEN, SGU_LEN), 1)
    causal = (row >> CHUNK_SHIFT) >= (col >> CHUNK_SHIFT)
    low_half = col < SGU_GROUP_DIM
    row_blocks = []
    for r in range(TM // SGU_LEN):
        pieces = []
        for p in range(SGU_WIDTH // LANES):
            blk = vb[r * SGU_LEN:(r + 1) * SGU_LEN, p * LANES:(p + 1) * LANES]
            w0 = jnp.where(causal, wsp_ref[2 * p], jnp.zeros_like(wsp_ref[2 * p]))
            w1 = jnp.where(causal, wsp_ref[2 * p + 1], jnp.zeros_like(wsp_ref[2 * p + 1]))
            sv = jnp.where(low_half, _dot(w0, blk), _dot(w1, blk)) + bsp_ref[p]
            pieces.append((u[r * SGU_LEN:(r + 1) * SGU_LEN, p * LANES:(p + 1) * LANES] * sv).astype(BF16))
        row_blocks.append(jnp.concatenate(pieces, axis=1))
    sgu = _dot(jnp.concatenate(row_blocks, axis=0), wosgu_ref[...])

    ga_ref[...] = jax.nn.sigmoid(proj(C_GA, C_GB)).astype(BF16)
    p_ref[...] = (jax.nn.sigmoid(proj(C_GB, C_END)) * sgu).astype(BF16)


def _inproj(x, pos, gattn, win, gcq, gckv, wqlat, wqr, wqrs, freq, sgn, sgain, sbias, wsp, bsp, wosgu):
    nt = N_TOK // TM
    per_b = SEQ // TM
    per_k = TK // TM

    def const(shape):
        return pl.BlockSpec(shape, lambda i: (0,) * len(shape))

    return pl.pallas_call(
        _inproj_kernel,
        grid=(nt,),
        in_specs=[pl.BlockSpec((TM, D_MODEL), lambda i: (i, 0)),
                  pl.BlockSpec((TM, 1), lambda i: (i, 0)),
                  const((1, D_MODEL)), const((D_MODEL, C_END)), const((1, Q_RANK)), const((1, KV_RANK)),
                  const((Q_RANK, MLA_HEADS * LANES)), const((Q_RANK, MLA_HEADS * LANES)),
                  const((Q_RANK, MLA_HEADS * LANES)), const((1, LANES)), const((1, LANES)),
                  const((1, SGU_WIDTH)), const((1, SGU_WIDTH)),
                  const((SGU_GROUPS, SGU_LEN, SGU_LEN)), const((SGU_GROUPS // 2, SGU_LEN, LANES)),
                  const((SGU_WIDTH, D_MODEL))],
        out_specs=[pl.BlockSpec((TM // TQ, MLA_HEADS * TQ, QK_PAD), lambda i: (i, 0, 0)),
                   pl.BlockSpec((None, None, QK_PAD, TM),
                                lambda i: (i // per_b, (i % per_b) // per_k, 0, i % per_k)),
                   pl.BlockSpec((None, None, TM, KV_RANK),
                                lambda i: (i // per_b, (i % per_b) // per_k, i % per_k, 0)),
                   pl.BlockSpec((TM, D_MODEL), lambda i: (i, 0)),
                   pl.BlockSpec((TM, D_MODEL), lambda i: (i, 0))],
        out_shape=[jax.ShapeDtypeStruct((N_TOK // TQ, MLA_HEADS * TQ, QK_PAD), BF16),
                   jax.ShapeDtypeStruct((BATCH, SEQ // TK, QK_PAD, TK), BF16),
                   jax.ShapeDtypeStruct((BATCH, SEQ // TK, TK, KV_RANK), BF16),
                   jax.ShapeDtypeStruct((N_TOK, D_MODEL), BF16),
                   jax.ShapeDtypeStruct((N_TOK, D_MODEL), BF16)],
        compiler_params=pltpu.CompilerParams(dimension_semantics=("arbitrary",),
                                             vmem_limit_bytes=56 * 1024 * 1024),
        name="inproj",
    )(x, pos, gattn, win, gcq, gckv, wqlat, wqr, wqrs, freq, sgn, sgain, sbias, wsp, bsp, wosgu)


def _attn_kernel(q_ref, kt_ref, v_ref, wuv_ref, o_ref, m_ref, l_ref, acc_ref):
    qi = pl.program_id(1)
    m_ref[...] = jnp.full(m_ref.shape, NEG, F32)
    l_ref[...] = jnp.zeros(l_ref.shape, F32)
    acc_ref[...] = jnp.zeros(acc_ref.shape, F32)

    def step(j, visible):
        s = _dot(q_ref[...], kt_ref[j])
        if visible is not None:
            s = jnp.where(visible, s, NEG)
        m_prev = m_ref[...]
        m_new = jnp.maximum(m_prev, jnp.max(s, axis=-1, keepdims=True))
        alpha = jnp.exp2(m_prev - m_new)
        p = jnp.exp2(s - jnp.concatenate([m_new] * (TK // LANES), axis=1))
        l_ref[...] = alpha * l_ref[...] + jnp.sum(p, axis=-1, keepdims=True)
        acc_ref[...] = alpha * acc_ref[...] + _dot(p.astype(BF16), v_ref[j])
        m_ref[...] = m_new

    diag = (qi * TQ) >> TK_SHIFT

    def body(j, carry):
        step(j, None)
        return carry

    lax.fori_loop(0, diag, body, 0)

    row = lax.broadcasted_iota(jnp.int32, (MLA_HEADS * TQ, TK), 0) & (TQ - 1)
    q_chunk = ((qi * TQ - diag * TK) >> CHUNK_SHIFT) + (row >> CHUNK_SHIFT)
    k_chunk = lax.broadcasted_iota(jnp.int32, (MLA_HEADS * TQ, TK), 1) >> CHUNK_SHIFT
    step(diag, k_chunk <= q_chunk)

    o_lat = (acc_ref[...] / l_ref[...]).astype(BF16)
    out = jnp.zeros((TQ, MLA_WIDTH), F32)
    for h in range(MLA_HEADS):
        out = out + _dot(o_lat[h * TQ:(h + 1) * TQ], wuv_ref[h])
    o_ref[...] = out.astype(BF16)


def _attention(q, kt, v, wuv):
    nk = SEQ // TK
    return pl.pallas_call(
        _attn_kernel,
        grid=(BATCH, SEQ // TQ),
        in_specs=[pl.BlockSpec((None, MLA_HEADS * TQ, QK_PAD), lambda b, i: (b * (SEQ // TQ) + i, 0, 0)),
                  pl.BlockSpec((None, nk, QK_PAD, TK), lambda b, i: (b, 0, 0, 0)),
                  pl.BlockSpec((None, nk, TK, KV_RANK), lambda b, i: (b, 0, 0, 0)),
                  pl.BlockSpec((MLA_HEADS, KV_RANK, MLA_WIDTH), lambda b, i: (0, 0, 0))],
        out_specs=pl.BlockSpec((None, TQ, MLA_WIDTH), lambda b, i: (b, i, 0)),
        out_shape=jax.ShapeDtypeStruct((BATCH, SEQ, MLA_WIDTH), BF16),
        scratch_shapes=[pltpu.VMEM((MLA_HEADS * TQ, LANES), F32),
                        pltpu.VMEM((MLA_HEADS * TQ, LANES), F32),
                        pltpu.VMEM((MLA_HEADS * TQ, KV_RANK), F32)],
        compiler_params=pltpu.CompilerParams(dimension_semantics=("arbitrary", "arbitrary"),
                                             vmem_limit_bytes=40 * 1024 * 1024),
        name="attention",
    )(q, kt, v, wuv)


def _mix_kernel(attn_ref, ga_ref, p_ref, x_ref, woa_ref, wout_ref, gffn_ref, wr_ref, br_ref,
                x1_ref, h2_ref, mi_ref, mf_ref, cnt_ref, carry_ref):
    i = pl.program_id(0)

    @pl.when(i == 0)
    def _():
        carry_ref[...] = jnp.zeros(carry_ref.shape, F32)

    a = _dot(attn_ref[...], woa_ref[...])
    mix = (ga_ref[...].astype(F32) * a + p_ref[...].astype(F32)).astype(BF16)
    x1 = x_ref[...] + _dot(mix, wout_ref[...])
    x1_ref[...] = x1
    h2 = _rms(x1, gffn_ref[...])
    h2_ref[...] = h2

    hi = h2.astype(BF16)
    lo = (h2 - hi.astype(F32)).astype(BF16)
    r1 = _dot(hi, wr_ref[...])
    r2 = _dot(lo, wr_ref[:, :LANES])
    logits = r1[:, :LANES] + r1[:, LANES:] + r2 + br_ref[...]

    lane_i = lax.broadcasted_iota(jnp.int32, (TM, LANES), 1)
    lane = lane_i.astype(F32)
    lane_group = (lane_i >> 3).astype(F32)
    ninf = -jnp.inf
    is_group = (lane_i >= N_EXPERTS) & (lane_i < N_EXPERTS + N_GROUPS)
    lg = jnp.where(is_group, logits, ninf)
    gmax = jnp.max(lg, axis=-1, keepdims=True)
    gsum = jnp.sum(jnp.exp(lg - gmax), axis=-1, keepdims=True)
    p_top = 1.0 / gsum
    g_idx = jnp.min(jnp.where(lg == gmax, lane - N_EXPERTS, float(N_GROUPS)), axis=-1, keepdims=True)
    le = jnp.where((lane_i < N_EXPERTS) & (lane_group == g_idx), logits, ninf)
    t1 = jnp.max(le, axis=-1, keepdims=True)
    e1 = jnp.min(jnp.where(le == t1, lane, float(LANES)), axis=-1, keepdims=True)
    le2 = jnp.where(lane == e1, ninf, le)
    t2 = jnp.max(le2, axis=-1, keepdims=True)
    e2 = jnp.min(jnp.where(le2 == t2, lane, float(LANES)), axis=-1, keepdims=True)
    ex = jnp.exp(t2 - t1)
    w1 = p_top / (1.0 + ex)
    w2 = p_top * ex / (1.0 + ex)

    sel1 = lane == e1
    sel2 = lane == e2
    onehot = jnp.where(sel1 | sel2, 1.0, 0.0)
    rr = lax.broadcasted_iota(jnp.int32, (TM, TM), 0)
    cc = lax.broadcasted_iota(jnp.int32, (TM, TM), 1)
    ltri = jnp.where(cc < rr, 1.0, 0.0).astype(BF16)
    before = carry_ref[...] + _dot(ltri, onehot.astype(BF16))
    rank1 = jnp.sum(jnp.where(sel1, before, 0.0), axis=-1, keepdims=True).astype(jnp.int32)
    rank2 = jnp.sum(jnp.where(sel2, before, 0.0), axis=-1, keepdims=True).astype(jnp.int32)
    carry = carry_ref[...] + jnp.sum(onehot, axis=0, keepdims=True)
    carry_ref[...] = carry
    cnt_ref[...] = carry

    mi_ref[...] = jnp.where(lane_i == 0, e1.astype(jnp.int32),
                            jnp.where(lane_i == 1, e2.astype(jnp.int32),
                                      jnp.where(lane_i == 2, rank1, rank2)))
    mf_ref[...] = jnp.where(lane_i == 0, w1, w2)


def _mix(attn, ga, p, x, woa, wout, gffn, wr, br):
    nt = N_TOK // TM

    def const(shape):
        return pl.BlockSpec(shape, lambda i: (0,) * len(shape))

    def rows(width):
        return pl.BlockSpec((TM, width), lambda i: (i, 0))

    return pl.pallas_call(
        _mix_kernel,
        grid=(nt,),
        in_specs=[rows(MLA_WIDTH), rows(D_MODEL), rows(D_MODEL), rows(D_MODEL),
                  const((MLA_WIDTH, D_MODEL)), const((D_MODEL, D_MODEL)), const((1, D_MODEL)),
                  const((D_MODEL, 2 * LANES)), const((1, LANES))],
        out_specs=[rows(D_MODEL), rows(D_MODEL), rows(LANES), rows(LANES), const((1, LANES))],
        out_shape=[jax.ShapeDtypeStruct((N_TOK, D_MODEL), F32),
                   jax.ShapeDtypeStruct((N_TOK, D_MODEL), F32),
                   jax.ShapeDtypeStruct((N_TOK, LANES), jnp.int32),
                   jax.ShapeDtypeStruct((N_TOK, LANES), F32),
                   jax.ShapeDtypeStruct((1, LANES), F32)],
        scratch_shapes=[pltpu.VMEM((1, LANES), F32)],
        compiler_params=pltpu.CompilerParams(dimension_semantics=("arbitrary",),
                                             vmem_limit_bytes=40 * 1024 * 1024),
        name="mix",
    )(attn, ga, p, x, woa, wout, gffn, wr, br)


def _row_copy(src_hbm, dst, s, d, sem):
    return pltpu.make_async_copy(src_hbm.at[pl.ds(s, 1)], dst.at[pl.ds(d, 1)], sem)


def _dispatch_kernel(dest_ref, h2_ref, xs_in_hbm, xs_hbm, sem):
    del xs_in_hbm
    base = pl.program_id(0) * TM

    def issue(t, carry):
        n = base + t
        _row_copy(h2_ref, xs_hbm, t, dest_ref[2 * n], sem).start()
        _row_copy(h2_ref, xs_hbm, t, dest_ref[2 * n + 1], sem).start()
        return carry

    lax.fori_loop(0, TM, issue, 0)

    def drain(t, carry):
        _row_copy(h2_ref, xs_hbm, 0, 0, sem).wait()
        _row_copy(h2_ref, xs_hbm, 0, 0, sem).wait()
        return carry

    lax.fori_loop(0, TM, drain, 0)


def _dispatch(dest, h2, xs0):
    return pl.pallas_call(
        _dispatch_kernel,
        grid_spec=pltpu.PrefetchScalarGridSpec(
            num_scalar_prefetch=1,
            grid=(N_TOK // TM,),
            in_specs=[pl.BlockSpec((TM, D_MODEL), lambda i, d: (i, 0)),
                      pl.BlockSpec(memory_space=pl.ANY)],
            out_specs=pl.BlockSpec(memory_space=pl.ANY),
            scratch_shapes=[pltpu.SemaphoreType.DMA(())]),
        out_shape=jax.ShapeDtypeStruct((MOE_ROWS, D_MODEL), F32),
        input_output_aliases={2: 0},
        compiler_params=pltpu.CompilerParams(dimension_semantics=("arbitrary",)),
        name="dispatch",
    )(dest, h2, xs0)


def _expert_kernel(blk_e_ref, nvb_ref, xs_ref, wg_ref, wu_ref, wd_ref, ys_ref):
    del blk_e_ref
    live = pl.program_id(0) < nvb_ref[0]

    @pl.when(live)
    def _():
        xb = xs_ref[...].astype(BF16)
        g = _dot(xb, wg_ref[...].astype(BF16))
        u = _dot(xb, wu_ref[...].astype(BF16))
        hid = (jax.nn.silu(g) * u).astype(BF16)
        ys_ref[...] = _dot(hid, wd_ref[...].astype(BF16))

    @pl.when(jnp.logical_not(live))
    def _():
        ys_ref[...] = jnp.zeros(ys_ref.shape, ys_ref.dtype)


def _experts(blk_e, nvb, xs, wg, wu, wd):
    def row_block(i, be, nv):
        return (jnp.minimum(i, nv[0] - 1), 0)

    def weight(i, be, nv):
        return (be[jnp.minimum(i, nv[0] - 1)], 0, 0)

    return pl.pallas_call(
        _expert_kernel,
        grid_spec=pltpu.PrefetchScalarGridSpec(
            num_scalar_prefetch=2,
            grid=(MOE_BLOCKS,),
            in_specs=[pl.BlockSpec((MOE_T, D_MODEL), row_block),
                      pl.BlockSpec((None, D_MODEL, D_EXPERT), weight),
                      pl.BlockSpec((None, D_MODEL, D_EXPERT), weight),
                      pl.BlockSpec((None, D_EXPERT, D_MODEL), weight)],
            out_specs=pl.BlockSpec((MOE_T, D_MODEL), lambda i, be, nv: (i, 0))),
        out_shape=jax.ShapeDtypeStruct((MOE_ROWS, D_MODEL), F32),
        compiler_params=pltpu.CompilerParams(dimension_semantics=("arbitrary",),
                                             vmem_limit_bytes=40 * 1024 * 1024),
        name="experts",
    )(blk_e, nvb, xs, wg, wu, wd)


def _combine_kernel(dest_ref, x1_ref, mf_ref, gfin_ref, ys_hbm, o_ref, ybuf, sem):
    base = pl.program_id(0) * TM

    def issue(t, carry):
        n = base + t
        _row_copy(ys_hbm, ybuf.at[0], dest_ref[2 * n], t, sem).start()
        _row_copy(ys_hbm, ybuf.at[1], dest_ref[2 * n + 1], t, sem).start()
        return carry

    lax.fori_loop(0, TM, issue, 0)

    def drain(t, carry):
        _row_copy(ys_hbm, ybuf.at[0], 0, 0, sem).wait()
        _row_copy(ys_hbm, ybuf.at[1], 0, 0, sem).wait()
        return carry

    lax.fori_loop(0, TM, drain, 0)

    mf = mf_ref[...]
    x2 = x1_ref[...] + mf[:, 0:1] * ybuf[0] + mf[:, 1:2] * ybuf[1]
    o_ref[...] = _rms(x2, gfin_ref[...])


def _combine(dest, x1, mf, gfin, ys):
    return pl.pallas_call(
        _combine_kernel,
        grid_spec=pltpu.PrefetchScalarGridSpec(
            num_scalar_prefetch=1,
            grid=(N_TOK // TM,),
            in_specs=[pl.BlockSpec((TM, D_MODEL), lambda i, d: (i, 0)),
                      pl.BlockSpec((TM, LANES), lambda i, d: (i, 0)),
                      pl.BlockSpec((1, D_MODEL), lambda i, d: (0, 0)),
                      pl.BlockSpec(memory_space=pl.ANY)],
            out_specs=pl.BlockSpec((TM, D_MODEL), lambda i, d: (i, 0)),
            scratch_shapes=[pltpu.VMEM((TOP_K, TM, D_MODEL), F32), pltpu.SemaphoreType.DMA(())]),
        out_shape=jax.ShapeDtypeStruct((N_TOK, D_MODEL), F32),
        compiler_params=pltpu.CompilerParams(dimension_semantics=("arbitrary",)),
        name="combine",
    )(dest, x1, mf, gfin, ys)


def kernel(x, positions, g_attn_norm, w_in, g_cq, w_uq, g_ckv, w_uk, w_uv, w_o_attn, sgu_gain, sgu_bias, w_spatial, b_spatial, w_o_sgu, w_out, g_ffn_norm, w_router_group, b_router_group, w_router_expert, b_router_expert, w_exp_gate, w_exp_up, w_exp_down, g_final):
    assert x.shape == (BATCH, SEQ, D_MODEL) and w_in.shape[0] == 1
    half = QK_ROPE // 2
    swap = jnp.concatenate([jnp.arange(half, QK_ROPE), jnp.arange(0, half)])

    def pad_cols(w, width):
        return jnp.pad(w, ((0, 0), (0, width - w.shape[1])))

    wi = w_in[0]
    c0 = Q_RANK + KV_RANK
    kr = wi[:, c0:c0 + QK_ROPE]
    c1 = c0 + QK_ROPE
    win = jnp.concatenate([
        wi[:, :c0], pad_cols(kr, LANES), pad_cols(kr[:, swap], LANES), wi[:, c1:]], axis=1).astype(BF16)

    wq = w_uq[0].reshape(Q_RANK, MLA_HEADS, QK_NOPE + QK_ROPE)
    wq_nope = wq[:, :, :QK_NOPE].transpose(1, 0, 2)
    wq_rope = wq[:, :, QK_NOPE:]
    w_ukt = w_uk[0].reshape(KV_RANK, MLA_HEADS, QK_NOPE).transpose(1, 2, 0)
    wqlat = _fold(wq_nope, w_ukt).transpose(1, 0, 2).reshape(Q_RANK, MLA_HEADS * LANES).astype(BF16)

    def rope_cols(w):
        return jnp.pad(w, ((0, 0), (0, 0), (0, LANES - QK_ROPE))).reshape(Q_RANK, MLA_HEADS * LANES).astype(BF16)

    wqr = rope_cols(wq_rope)
    wqrs = rope_cols(wq_rope[:, :, swap])

    freqs = ROPE_THETA ** (-jnp.arange(0, QK_ROPE, 2, dtype=F32) / QK_ROPE)
    freq = pad_cols(jnp.concatenate([freqs, freqs])[None, :], LANES)
    sgn = pad_cols(jnp.concatenate([-jnp.ones((half,), F32), jnp.ones((half,), F32)])[None, :], LANES)

    head_of_col = jnp.arange(MLA_WIDTH) // V_HEAD
    wuv = jnp.where(head_of_col[None, None, :] == jnp.arange(MLA_HEADS)[:, None, None],
                    w_uv[0][None], 0.0).astype(BF16)

    wsp = w_spatial[0].astype(BF16)
    bs = b_spatial[0]
    bsp = jnp.repeat(bs.reshape(SGU_GROUPS // 2, 2, SGU_LEN).transpose(0, 2, 1), SGU_GROUP_DIM, axis=2)

    wr32 = jnp.concatenate([w_router_expert[0].transpose(1, 0, 2).reshape(D_MODEL, N_EXPERTS),
                            w_router_group[0]], axis=1)
    wr32 = pad_cols(wr32, LANES)
    wr_hi = wr32.astype(BF16)
    wr_lo = (wr32 - wr_hi.astype(F32)).astype(BF16)
    wr = jnp.concatenate([wr_hi, wr_lo], axis=1)
    br = pad_cols(jnp.concatenate([b_router_expert[0].reshape(-1), b_router_group[0]])[None, :], LANES)

    xf = x.reshape(N_TOK, D_MODEL)
    pos = positions.astype(F32).reshape(N_TOK, 1)
    q, kt, v, p, ga = _inproj(
        xf, pos, g_attn_norm, win, g_cq, g_ckv, wqlat, wqr, wqrs, freq, sgn,
        sgu_gain, sgu_bias, wsp, bsp, w_o_sgu[0].astype(BF16))
    attn = _attention(q, kt, v, wuv)
    x1, h2, mi, mf, cnt = _mix(attn.reshape(N_TOK, MLA_WIDTH), ga, p, xf, w_o_attn[0].astype(BF16),
                               w_out[0].astype(BF16), g_ffn_norm, wr, br)

    counts = cnt[0, :N_EXPERTS].astype(jnp.int32)
    pcounts = (counts + MOE_T - 1) // MOE_T * MOE_T
    pends = jnp.cumsum(pcounts)
    pstarts = pends - pcounts
    dest = (pstarts[mi[:, 0:2]] + mi[:, 2:4]).reshape(N_TOK * TOP_K)
    nvb = (pends[-1:] // MOE_T).astype(jnp.int32)
    blk_start = jnp.arange(MOE_BLOCKS, dtype=jnp.int32) * MOE_T
    blk_e = jnp.minimum(jnp.sum((pends[None, :] <= blk_start[:, None]).astype(jnp.int32), axis=1),
                        N_EXPERTS - 1)

    xs = _dispatch(dest, h2, jnp.zeros((MOE_ROWS, D_MODEL), F32))
    ys = _experts(blk_e, nvb, xs, w_exp_gate[0], w_exp_up[0], w_exp_down[0])
    out = _combine(dest, x1, mf, g_final.reshape(1, D_MODEL), ys)
    return out.reshape(BATCH, SEQ, D_MODEL)
```

```python
import functools

import jax
import jax.numpy as jnp
from jax import lax
from jax.experimental import pallas as pl
from jax.experimental.pallas import tpu as pltpu

D_MODEL = 1024
BATCH = 2
SEQ = 8192
N_TOK = BATCH * SEQ
CHUNK = 64
EPS = 1e-6
MLA_HEADS = 8
Q_RANK = 256
KV_RANK = 128
QK_NOPE = 64
QK_ROPE = 32
V_HEAD = 64
MLA_WIDTH = MLA_HEADS * V_HEAD
ROPE_THETA = 10000.0
SGU_GROUPS = 8
SGU_GROUP_DIM = 64
SGU_WIDTH = SGU_GROUPS * SGU_GROUP_DIM
SGU_LEN = 128
N_GROUPS = 4
EXPERTS_PER_GROUP = 8
N_EXPERTS = N_GROUPS * EXPERTS_PER_GROUP
TOP_K = 2
D_EXPERT = 256

LANES = 128
QK_PAD = 2 * LANES
V_PAD = 2 * LANES
SCALE = (QK_NOPE + QK_ROPE) ** -0.5
LOG2E = 1.4426950408889634
Q_SCALE = SCALE * LOG2E
NEG = -1e30

TM = 256
TQ = 128
TK = 512
TK_SHIFT = TK.bit_length() - 1
CHUNK_SHIFT = CHUNK.bit_length() - 1
assert 1 << TK_SHIFT == TK and 1 << CHUNK_SHIFT == CHUNK
MOE_T = 256
CHUNK_ROWS = 8
TILE_ROWS = -(-(TOP_K * TM + N_EXPERTS * (CHUNK_ROWS - 1) + CHUNK_ROWS) // 256) * 256
MOE_ROWS_MAX = (N_TOK * TOP_K + (N_TOK // TM) * N_EXPERTS * (CHUNK_ROWS - 1)
                + N_EXPERTS * (MOE_T - CHUNK_ROWS))
MOE_BLOCKS = -(-MOE_ROWS_MAX // MOE_T)
MOE_ROWS = MOE_BLOCKS * MOE_T

C_Q = 0
C_KV = C_Q + Q_RANK
C_KR = C_KV + KV_RANK
C_KRS = C_KR + LANES
C_U = C_KRS + LANES
C_V = C_U + SGU_WIDTH
C_GA = C_V + SGU_WIDTH
C_GB = C_GA + D_MODEL
C_END = C_GB + D_MODEL

F32 = jnp.float32
BF16 = jnp.bfloat16


def _dot(a, b):
    return jnp.dot(a, b, preferred_element_type=F32)


def _rms(x, g):
    return x * lax.rsqrt(jnp.mean(x * x, axis=-1, keepdims=True) + EPS) * g


def _fold_kernel(a_ref, b_ref, o_ref):
    o_ref[...] = Q_SCALE * jnp.dot(a_ref[...], b_ref[...], preferred_element_type=F32,
                                 precision=lax.Precision.HIGHEST)


def _fold(w_uq_nope, w_ukt):
    return pl.pallas_call(
        _fold_kernel,
        grid=(MLA_HEADS,),
        in_specs=[pl.BlockSpec((None, Q_RANK, QK_NOPE), lambda h: (h, 0, 0)),
                  pl.BlockSpec((None, QK_NOPE, KV_RANK), lambda h: (h, 0, 0))],
        out_specs=pl.BlockSpec((None, Q_RANK, KV_RANK), lambda h: (h, 0, 0)),
        out_shape=jax.ShapeDtypeStruct((MLA_HEADS, Q_RANK, KV_RANK), F32),
        name="fold",
    )(w_uq_nope, w_ukt)


def _inproj_kernel(x_ref, pos_ref, gattn_ref, win_ref, gcq_ref, gckv_ref, wqlat_ref, wqr_ref,
                   wqrs_ref, freq_ref, sgn_ref, sgain_ref, sbias_ref, wsp_ref, bsp_ref, wosgu_ref,
                   q_ref, kt_ref, v_ref, p_ref, ga_ref):
    hb = _rms(x_ref[...], gattn_ref[...]).astype(BF16)

    def proj(a, b):
        return _dot(hb, win_ref[:, a:b])

    ang = pos_ref[...] * freq_ref[...]
    cos_t = jnp.cos(ang)
    sin_t = jnp.sin(ang) * sgn_ref[...]

    cqn = _rms(proj(C_Q, C_KV), gcq_ref[...]).astype(BF16)
    qlat = _dot(cqn, wqlat_ref[...])
    qr = _dot(cqn, wqr_ref[...])
    qs = _dot(cqn, wqrs_ref[...])
    cos_q = cos_t * Q_SCALE
    sin_q = sin_t * Q_SCALE
    for r in range(TM // TQ):
        t0, t1 = r * TQ, (r + 1) * TQ
        for h in range(MLA_HEADS):
            a, b = h * LANES, (h + 1) * LANES
            q_ref[r, h * TQ:(h + 1) * TQ, :LANES] = qlat[t0:t1, a:b].astype(BF16)
            q_ref[r, h * TQ:(h + 1) * TQ, LANES:] = (
                qr[t0:t1, a:b] * cos_q[t0:t1] + qs[t0:t1, a:b] * sin_q[t0:t1]).astype(BF16)

    zk = proj(C_KV, C_U)
    ckvn = _rms(zk[:, :KV_RANK], gckv_ref[...])
    krope = zk[:, KV_RANK:KV_RANK + LANES] * cos_t + zk[:, KV_RANK + LANES:] * sin_t
    kt_ref[...] = jnp.concatenate([ckvn, krope], axis=1).T.astype(BF16)
    v_ref[...] = jnp.concatenate([ckvn, jnp.ones_like(ckvn)], axis=1).astype(BF16)

    u = jax.nn.gelu(proj(C_U, C_V))
    v = jax.nn.gelu(proj(C_V, C_GA))
    mu = jnp.mean(v, axis=-1, keepdims=True)
    vc = v - mu
    var = jnp.mean(vc * vc, axis=-1, keepdims=True)
    vb = (vc * lax.rsqrt(var + EPS) * sgain_ref[...] + sbias_ref[...]).astype(BF16)
    row = lax.broadcasted_iota(jnp.int32, (SGU_LEN, SGU_LEN), 0)
    col = lax.broadcasted_iota(jnp.int32, (SGU_LEN, SGU_LEN), 1)
    causal = (row >> CHUNK_SHIFT) >= (col >> CHUNK_SHIFT)
    low_half = col < SGU_GROUP_DIM
    row_blocks = []
    for r in range(TM // SGU_LEN):
        pieces = []
        for p in range(SGU_WIDTH // LANES):
            blk = vb[r * SGU_LEN:(r + 1) * SGU_LEN, p * LANES:(p + 1) * LANES]
            w0 = jnp.where(causal, wsp_ref[2 * p], jnp.zeros_like(wsp_ref[2 * p]))
            w1 = jnp.where(causal, wsp_ref[2 * p + 1], jnp.zeros_like(wsp_ref[2 * p + 1]))
            sv = jnp.where(low_half, _dot(w0, blk), _dot(w1, blk)) + bsp_ref[p]
            pieces.append((u[r * SGU_LEN:(r + 1) * SGU_LEN, p * LANES:(p + 1) * LANES] * sv).astype(BF16))
        row_blocks.append(jnp.concatenate(pieces, axis=1))
    sgu = _dot(jnp.concatenate(row_blocks, axis=0), wosgu_ref[...])

    ga_ref[...] = jax.nn.sigmoid(proj(C_GA, C_GB)).astype(BF16)
    p_ref[...] = (jax.nn.sigmoid(proj(C_GB, C_END)) * sgu).astype(BF16)


def _inproj(x, pos, gattn, win, gcq, gckv, wqlat, wqr, wqrs, freq, sgn, sgain, sbias, wsp, bsp, wosgu):
    nt = N_TOK // TM
    per_b = SEQ // TM
    per_k = TK // TM

    def const(shape):
        return pl.BlockSpec(shape, lambda i: (0,) * len(shape))

    return pl.pallas_call(
        _inproj_kernel,
        grid=(nt,),
        in_specs=[pl.BlockSpec((TM, D_MODEL), lambda i: (i, 0)),
                  pl.BlockSpec((TM, 1), lambda i: (i, 0)),
                  const((1, D_MODEL)), const((D_MODEL, C_END)), const((1, Q_RANK)), const((1, KV_RANK)),
                  const((Q_RANK, MLA_HEADS * LANES)), const((Q_RANK, MLA_HEADS * LANES)),
                  const((Q_RANK, MLA_HEADS * LANES)), const((1, LANES)), const((1, LANES)),
                  const((1, SGU_WIDTH)), const((1, SGU_WIDTH)),
                  const((SGU_GROUPS, SGU_LEN, SGU_LEN)), const((SGU_GROUPS // 2, SGU_LEN, LANES)),
                  const((SGU_WIDTH, D_MODEL))],
        out_specs=[pl.BlockSpec((TM // TQ, MLA_HEADS * TQ, QK_PAD), lambda i: (i, 0, 0)),
                   pl.BlockSpec((None, None, QK_PAD, TM),
                                lambda i: (i // per_b, (i % per_b) // per_k, 0, i % per_k)),
                   pl.BlockSpec((None, None, TM, V_PAD),
                                lambda i: (i // per_b, (i % per_b) // per_k, i % per_k, 0)),
                   pl.BlockSpec((TM, D_MODEL), lambda i: (i, 0)),
                   pl.BlockSpec((TM, D_MODEL), lambda i: (i, 0))],
        out_shape=[jax.ShapeDtypeStruct((N_TOK // TQ, MLA_HEADS * TQ, QK_PAD), BF16),
                   jax.ShapeDtypeStruct((BATCH, SEQ // TK, QK_PAD, TK), BF16),
                   jax.ShapeDtypeStruct((BATCH, SEQ // TK, TK, V_PAD), BF16),
                   jax.ShapeDtypeStruct((N_TOK, D_MODEL), BF16),
                   jax.ShapeDtypeStruct((N_TOK, D_MODEL), BF16)],
        compiler_params=pltpu.CompilerParams(dimension_semantics=("arbitrary",),
                                             vmem_limit_bytes=56 * 1024 * 1024),
        name="inproj",
    )(x, pos, gattn, win, gcq, gckv, wqlat, wqr, wqrs, freq, sgn, sgain, sbias, wsp, bsp, wosgu)


def _attn_kernel(q_ref, kt_ref, v_ref, wuv_ref, o_ref, m_ref, acc_ref, s_ref, p_ref, a_ref):
    qi = pl.program_id(1)
    m_ref[...] = jnp.full(m_ref.shape, NEG, F32)
    acc_ref[...] = jnp.zeros(acc_ref.shape, F32)

    def scores(j, slot):
        s_ref[slot] = _dot(q_ref[...], kt_ref[j])

    def update(j, slot, masked):
        if masked:
            q_chunk = ((qi * TQ - j * TK) >> CHUNK_SHIFT) + (
                lax.broadcasted_iota(jnp.int32, (TQ, TK), 0) >> CHUNK_SHIFT)
            k_chunk = lax.broadcasted_iota(jnp.int32, (TQ, TK), 1) >> CHUNK_SHIFT
            visible = k_chunk <= q_chunk
        for h in range(MLA_HEADS):
            rows = slice(h * TQ, (h + 1) * TQ)
            s = s_ref[slot, rows, :]
            if masked:
                s = jnp.where(visible, s, NEG)
            m_prev = m_ref[rows, :]
            m_new = jnp.maximum(m_prev, jnp.max(s, axis=-1, keepdims=True))
            m_ref[rows, :] = m_new
            alpha = jnp.exp2(m_prev - m_new)
            p_ref[rows, :] = jnp.exp2(s - jnp.concatenate([m_new] * (TK // LANES), axis=1)).astype(BF16)
            a_ref[rows, :] = alpha
        alpha = a_ref[...]
        acc_ref[...] = jnp.concatenate([alpha, alpha], axis=1) * acc_ref[...] + _dot(p_ref[...], v_ref[j])

    diag = (qi * TQ) >> TK_SHIFT
    pairs = diag >> 1
    scores(0, 0)

    def body(t, carry):
        j = 2 * t
        scores(j + 1, 1)
        update(j, 0, False)
        scores(j + 2, 0)
        update(j + 1, 1, False)
        return carry

    lax.fori_loop(0, pairs, body, 0)

    @pl.when((diag & 1) == 1)
    def _():
        scores(diag, 1)
        update(diag - 1, 0, False)
        update(diag, 1, True)

    @pl.when((diag & 1) == 0)
    def _():
        update(diag, 0, True)

    o_lat = (acc_ref[:, :KV_RANK] / acc_ref[:, KV_RANK:]).astype(BF16)
    o_cat = jnp.concatenate([o_lat[h * TQ:(h + 1) * TQ] for h in range(MLA_HEADS)], axis=1)
    o_ref[...] = _dot(o_cat, wuv_ref[...]).astype(BF16)


def _attention(q, kt, v, wuv):
    nk = SEQ // TK
    return pl.pallas_call(
        _attn_kernel,
        grid=(BATCH, SEQ // TQ),
        in_specs=[pl.BlockSpec((None, MLA_HEADS * TQ, QK_PAD), lambda b, i: (b * (SEQ // TQ) + i, 0, 0)),
                  pl.BlockSpec((None, nk, QK_PAD, TK), lambda b, i: (b, 0, 0, 0)),
                  pl.BlockSpec((None, nk, TK, V_PAD), lambda b, i: (b, 0, 0, 0)),
                  pl.BlockSpec((MLA_HEADS * KV_RANK, MLA_WIDTH), lambda b, i: (0, 0))],
        out_specs=pl.BlockSpec((None, TQ, MLA_WIDTH), lambda b, i: (b, i, 0)),
        out_shape=jax.ShapeDtypeStruct((BATCH, SEQ, MLA_WIDTH), BF16),
        scratch_shapes=[pltpu.VMEM((MLA_HEADS * TQ, LANES), F32),
                        pltpu.VMEM((MLA_HEADS * TQ, V_PAD), F32),
                        pltpu.VMEM((2, MLA_HEADS * TQ, TK), F32),
                        pltpu.VMEM((MLA_HEADS * TQ, TK), BF16),
                        pltpu.VMEM((MLA_HEADS * TQ, LANES), F32)],
        compiler_params=pltpu.CompilerParams(dimension_semantics=("arbitrary", "arbitrary"),
                                             vmem_limit_bytes=40 * 1024 * 1024),
        name="attention",
    )(q, kt, v, wuv)


def _mix_kernel(attn_ref, ga_ref, p_ref, x_ref, woa_ref, wout_ref, gffn_ref, wr_ref, br_ref,
                x1_ref, h2_ref, mf_ref, slot_col_ref, slot_row_ref, c8_ref, loff8_ref):
    a = _dot(attn_ref[...], woa_ref[...])
    mix = (ga_ref[...].astype(F32) * a + p_ref[...].astype(F32)).astype(BF16)
    x1 = x_ref[...] + _dot(mix, wout_ref[...])
    x1_ref[...] = x1
    h2 = _rms(x1, gffn_ref[...])
    h2_ref[...] = h2.astype(BF16)

    hi = h2.astype(BF16)
    lo = (h2 - hi.astype(F32)).astype(BF16)
    r1 = _dot(hi, wr_ref[...])
    r2 = _dot(lo, wr_ref[:, :LANES])
    logits = r1[:, :LANES] + r1[:, LANES:] + r2 + br_ref[...]

    lane_i = lax.broadcasted_iota(jnp.int32, (TM, LANES), 1)
    lane = lane_i.astype(F32)
    lane_group = (lane_i >> 3).astype(F32)
    ninf = -jnp.inf
    is_group = (lane_i >= N_EXPERTS) & (lane_i < N_EXPERTS + N_GROUPS)
    lg = jnp.where(is_group, logits, ninf)
    gmax = jnp.max(lg, axis=-1, keepdims=True)
    gsum = jnp.sum(jnp.exp(lg - gmax), axis=-1, keepdims=True)
    p_top = 1.0 / gsum
    g_idx = jnp.min(jnp.where(lg == gmax, lane - N_EXPERTS, float(N_GROUPS)), axis=-1, keepdims=True)
    le = jnp.where((lane_i < N_EXPERTS) & (lane_group == g_idx), logits, ninf)
    t1 = jnp.max(le, axis=-1, keepdims=True)
    e1 = jnp.min(jnp.where(le == t1, lane, float(LANES)), axis=-1, keepdims=True)
    le2 = jnp.where(lane == e1, ninf, le)
    t2 = jnp.max(le2, axis=-1, keepdims=True)
    e2 = jnp.min(jnp.where(le2 == t2, lane, float(LANES)), axis=-1, keepdims=True)
    ex = jnp.exp(t2 - t1)
    w1 = p_top / (1.0 + ex)
    w2 = p_top * ex / (1.0 + ex)

    sel1 = lane == e1
    sel2 = lane == e2
    onehot = jnp.where(sel1 | sel2, 1.0, 0.0)
    rr = lax.broadcasted_iota(jnp.int32, (TM, TM), 0)
    cc = lax.broadcasted_iota(jnp.int32, (TM, TM), 1)
    ltri = jnp.where(cc < rr, 1.0, 0.0).astype(BF16)
    rank = _dot(ltri, onehot.astype(BF16))
    cnt = jnp.sum(onehot, axis=0, keepdims=True)
    c8 = jnp.floor((cnt + (CHUNK_ROWS - 1)) * (1.0 / CHUNK_ROWS))
    ur = lax.broadcasted_iota(jnp.int32, (LANES, LANES), 0)
    uc = lax.broadcasted_iota(jnp.int32, (LANES, LANES), 1)
    upper = jnp.where(ur < uc, 1.0, 0.0).astype(BF16)
    c8_rows = jnp.broadcast_to(c8, (8, LANES))
    loff8_rows = _dot(c8_rows.astype(BF16), upper)
    slot_all = CHUNK_ROWS * loff8_rows[0:1] + rank
    slot1 = jnp.sum(jnp.where(sel1, slot_all, 0.0), axis=-1, keepdims=True)
    slot2 = jnp.sum(jnp.where(sel2, slot_all, 0.0), axis=-1, keepdims=True)

    slots = jnp.where(lane_i == 0, slot1, jnp.where(lane_i == 1, slot2, 0.0))
    slot_col_ref[...] = slots.astype(jnp.int32)
    slot_row_ref[...] = slots.T[:8].astype(jnp.int32)
    c8_ref[...] = c8_rows.astype(jnp.int32)
    loff8_ref[...] = loff8_rows.astype(jnp.int32)
    mf_ref[...] = jnp.where(lane_i == 0, w1, w2)


def _mix(attn, ga, p, x, woa, wout, gffn, wr, br):
    nt = N_TOK // TM

    def const(shape):
        return pl.BlockSpec(shape, lambda i: (0,) * len(shape))

    def rows(width):
        return pl.BlockSpec((TM, width), lambda i: (i, 0))

    return pl.pallas_call(
        _mix_kernel,
        grid=(nt,),
        in_specs=[rows(MLA_WIDTH), rows(D_MODEL), rows(D_MODEL), rows(D_MODEL),
                  const((MLA_WIDTH, D_MODEL)), const((D_MODEL, D_MODEL)), const((1, D_MODEL)),
                  const((D_MODEL, 2 * LANES)), const((1, LANES))],
        out_specs=[rows(D_MODEL), rows(D_MODEL), rows(LANES), rows(LANES),
                   pl.BlockSpec((8, TM), lambda i: (i, 0)),
                   pl.BlockSpec((8, LANES), lambda i: (i, 0)),
                   pl.BlockSpec((8, LANES), lambda i: (i, 0))],
        out_shape=[jax.ShapeDtypeStruct((N_TOK, D_MODEL), F32),
                   jax.ShapeDtypeStruct((N_TOK, D_MODEL), BF16),
                   jax.ShapeDtypeStruct((N_TOK, LANES), F32),
                   jax.ShapeDtypeStruct((N_TOK, LANES), jnp.int32),
                   jax.ShapeDtypeStruct((nt * 8, TM), jnp.int32),
                   jax.ShapeDtypeStruct((nt * 8, LANES), jnp.int32),
                   jax.ShapeDtypeStruct((nt * 8, LANES), jnp.int32)],
        compiler_params=pltpu.CompilerParams(dimension_semantics=("arbitrary",),
                                             vmem_limit_bytes=40 * 1024 * 1024),
        name="mix",
    )(attn, ga, p, x, woa, wout, gffn, wr, br)


def _chunk_copy(src, dst, s8, d8, sem):
    return pltpu.make_async_copy(src.at[pl.ds(pl.multiple_of(s8 * CHUNK_ROWS, CHUNK_ROWS), CHUNK_ROWS)],
                                 dst.at[pl.ds(pl.multiple_of(d8 * CHUNK_ROWS, CHUNK_ROWS), CHUNK_ROWS)], sem)


def _for_each_chunk(t, c8_ref, loff8_ref, gbase8_ref, fn):
    def per_expert(e, carry):
        k = t * N_EXPERTS + e
        lo = loff8_ref[k]
        gb = gbase8_ref[k]

        def per_chunk(j, carry2):
            fn(lo + j, gb + j)
            return carry2

        lax.fori_loop(0, c8_ref[k], per_chunk, 0)
        return carry

    lax.fori_loop(0, N_EXPERTS, per_expert, 0)


def _dispatch_kernel(c8_ref, loff8_ref, gbase8_ref, ntile_ref, zstart8_ref, zcnt8_ref, nvb_ref,
                     slot_ref, h2_ref, xs_hbm, sbuf, sem):
    t = pl.program_id(0)
    row = lax.broadcasted_iota(jnp.int32, (TILE_ROWS, TM), 0)
    slots = slot_ref[...]
    perm = jnp.where((row == slots[0:1, :]) | (row == slots[1:2, :]), 1.0, 0.0).astype(BF16)
    sbuf[...] = _dot(perm, h2_ref[...])

    _for_each_chunk(t, c8_ref, loff8_ref, gbase8_ref,
                    lambda lc, gc: _chunk_copy(sbuf, xs_hbm, lc, gc, sem).start())

    def wait_one(j, carry):
        _chunk_copy(sbuf, xs_hbm, 0, 0, sem).wait()
        return carry

    lax.fori_loop(0, ntile_ref[t], wait_one, 0)

    @pl.when(t == pl.num_programs(0) - 1)
    def _():
        zero_chunk = TILE_ROWS // CHUNK_ROWS - 1

        def per_expert(e, total):
            n = zcnt8_ref[e]

            def per_chunk(j, carry):
                _chunk_copy(sbuf, xs_hbm, zero_chunk, zstart8_ref[e] + j, sem).start()
                return carry

            lax.fori_loop(0, n, per_chunk, 0)
            return total + n

        total = lax.fori_loop(0, N_EXPERTS, per_expert, 0)
        lax.fori_loop(0, total, wait_one, 0)

        def block_copy(b):
            return pltpu.make_async_copy(
                sbuf.at[pl.ds(0, MOE_T)], xs_hbm.at[pl.ds(pl.multiple_of(b * MOE_T, MOE_T), MOE_T)], sem)

        sbuf[pl.ds(0, MOE_T), :] = jnp.zeros((MOE_T, D_MODEL), F32)

        def fill(b, carry):
            block_copy(b).start()
            return carry

        def fill_wait(b, carry):
            block_copy(b).wait()
            return carry

        lax.fori_loop(nvb_ref[0], MOE_BLOCKS, fill, 0)
        lax.fori_loop(nvb_ref[0], MOE_BLOCKS, fill_wait, 0)


def _dispatch(tables, slot_rows, h2):
    return pl.pallas_call(
        _dispatch_kernel,
        grid_spec=pltpu.PrefetchScalarGridSpec(
            num_scalar_prefetch=len(tables),
            grid=(N_TOK // TM,),
            in_specs=[pl.BlockSpec((8, TM), lambda i, *_: (i, 0)),
                      pl.BlockSpec((TM, D_MODEL), lambda i, *_: (i, 0))],
            out_specs=pl.BlockSpec(memory_space=pl.ANY),
            scratch_shapes=[pltpu.VMEM((TILE_ROWS, D_MODEL), F32), pltpu.SemaphoreType.DMA(())]),
        out_shape=jax.ShapeDtypeStruct((MOE_ROWS, D_MODEL), F32),
        compiler_params=pltpu.CompilerParams(dimension_semantics=("arbitrary",)),
        name="dispatch",
    )(*tables, slot_rows, h2)


def _expert_kernel(blk_e_ref, nvb_ref, xs_ref, wg_ref, wu_ref, wd_ref, ys_ref, wgb, wub, wdb):
    i = pl.program_id(0)
    live = i < nvb_ref[0]

    @pl.when(live & ((i == 0) | (blk_e_ref[i] != blk_e_ref[jnp.maximum(i - 1, 0)])))
    def _():
        wgb[...] = wg_ref[...].astype(BF16)
        wub[...] = wu_ref[...].astype(BF16)
        wdb[...] = wd_ref[...].astype(BF16)

    @pl.when(live)
    def _():
        xb = xs_ref[...].astype(BF16)
        g = _dot(xb, wgb[...])
        u = _dot(xb, wub[...])
        hid = (jax.nn.silu(g) * u).astype(BF16)
        ys_ref[...] = _dot(hid, wdb[...])

    @pl.when(jnp.logical_not(live))
    def _():
        ys_ref[...] = jnp.zeros(ys_ref.shape, ys_ref.dtype)


def _experts(blk_e, nvb, xs, wg, wu, wd):
    def row_block(i, be, nv):
        return (jnp.minimum(i, nv[0] - 1), 0)

    def weight(i, be, nv):
        return (be[jnp.minimum(i, nv[0] - 1)], 0, 0)

    return pl.pallas_call(
        _expert_kernel,
        grid_spec=pltpu.PrefetchScalarGridSpec(
            num_scalar_prefetch=2,
            grid=(MOE_BLOCKS,),
            in_specs=[pl.BlockSpec((MOE_T, D_MODEL), row_block),
                      pl.BlockSpec((None, D_MODEL, D_EXPERT), weight),
                      pl.BlockSpec((None, D_MODEL, D_EXPERT), weight),
                      pl.BlockSpec((None, D_EXPERT, D_MODEL), weight)],
            out_specs=pl.BlockSpec((MOE_T, D_MODEL), lambda i, be, nv: (i, 0)),
            scratch_shapes=[pltpu.VMEM((D_MODEL, D_EXPERT), BF16), pltpu.VMEM((D_MODEL, D_EXPERT), BF16),
                            pltpu.VMEM((D_EXPERT, D_MODEL), BF16)]),
        out_shape=jax.ShapeDtypeStruct((MOE_ROWS, D_MODEL), F32),
        compiler_params=pltpu.CompilerParams(dimension_semantics=("arbitrary",),
                                             vmem_limit_bytes=40 * 1024 * 1024),
        name="experts",
    )(blk_e, nvb, xs, wg, wu, wd)


def _combine_kernel(c8_ref, loff8_ref, gbase8_ref, ntile_ref,
                    slot_ref, x1_ref, mf_ref, gfin_ref, ys_hbm, o_ref, ybuf, sem):
    t = pl.program_id(0)

    @pl.when(t == 0)
    def _():
        ybuf[...] = jnp.zeros(ybuf.shape, F32)

    _for_each_chunk(t, c8_ref, loff8_ref, gbase8_ref,
                    lambda lc, gc: _chunk_copy(ys_hbm, ybuf, gc, lc, sem).start())

    def wait_one(j, carry):
        _chunk_copy(ys_hbm, ybuf, 0, 0, sem).wait()
        return carry

    lax.fori_loop(0, ntile_ref[t], wait_one, 0)

    yb = ybuf[...].astype(BF16)
    col = lax.broadcasted_iota(jnp.int32, (TM, TILE_ROWS), 1)
    slots = slot_ref[...]
    y1 = _dot(jnp.where(col == slots[:, 0:1], 1.0, 0.0).astype(BF16), yb)
    y2 = _dot(jnp.where(col == slots[:, 1:2], 1.0, 0.0).astype(BF16), yb)
    mf = mf_ref[...]
    x2 = x1_ref[...] + mf[:, 0:1] * y1 + mf[:, 1:2] * y2
    o_ref[...] = _rms(x2, gfin_ref[...])


def _combine(tables, slot_cols, x1, mf, gfin, ys):
    return pl.pallas_call(
        _combine_kernel,
        grid_spec=pltpu.PrefetchScalarGridSpec(
            num_scalar_prefetch=len(tables),
            grid=(N_TOK // TM,),
            in_specs=[pl.BlockSpec((TM, LANES), lambda i, *_: (i, 0)),
                      pl.BlockSpec((TM, D_MODEL), lambda i, *_: (i, 0)),
                      pl.BlockSpec((TM, LANES), lambda i, *_: (i, 0)),
                      pl.BlockSpec((1, D_MODEL), lambda i, *_: (0, 0)),
                      pl.BlockSpec(memory_space=pl.ANY)],
            out_specs=pl.BlockSpec((TM, D_MODEL), lambda i, *_: (i, 0)),
            scratch_shapes=[pltpu.VMEM((TILE_ROWS, D_MODEL), F32), pltpu.SemaphoreType.DMA(())]),
        out_shape=jax.ShapeDtypeStruct((N_TOK, D_MODEL), F32),
        compiler_params=pltpu.CompilerParams(dimension_semantics=("arbitrary",)),
        name="combine",
    )(*tables, slot_cols, x1, mf, gfin, ys)


def kernel(x, positions, g_attn_norm, w_in, g_cq, w_uq, g_ckv, w_uk, w_uv, w_o_attn, sgu_gain, sgu_bias, w_spatial, b_spatial, w_o_sgu, w_out, g_ffn_norm, w_router_group, b_router_group, w_router_expert, b_router_expert, w_exp_gate, w_exp_up, w_exp_down, g_final):
    assert x.shape == (BATCH, SEQ, D_MODEL) and w_in.shape[0] == 1
    half = QK_ROPE // 2
    swap = jnp.concatenate([jnp.arange(half, QK_ROPE), jnp.arange(0, half)])

    def pad_cols(w, width):
        return jnp.pad(w, ((0, 0), (0, width - w.shape[1])))

    wi = w_in[0]
    c0 = Q_RANK + KV_RANK
    kr = wi[:, c0:c0 + QK_ROPE]
    c1 = c0 + QK_ROPE
    win = jnp.concatenate([
        wi[:, :c0], pad_cols(kr, LANES), pad_cols(kr[:, swap], LANES), wi[:, c1:]], axis=1).astype(BF16)

    wq = w_uq[0].reshape(Q_RANK, MLA_HEADS, QK_NOPE + QK_ROPE)
    wq_nope = wq[:, :, :QK_NOPE].transpose(1, 0, 2)
    wq_rope = wq[:, :, QK_NOPE:]
    w_ukt = w_uk[0].reshape(KV_RANK, MLA_HEADS, QK_NOPE).transpose(1, 2, 0)
    wqlat = _fold(wq_nope, w_ukt).transpose(1, 0, 2).reshape(Q_RANK, MLA_HEADS * LANES).astype(BF16)

    def rope_cols(w):
        return jnp.pad(w, ((0, 0), (0, 0), (0, LANES - QK_ROPE))).reshape(Q_RANK, MLA_HEADS * LANES).astype(BF16)

    wqr = rope_cols(wq_rope)
    wqrs = rope_cols(wq_rope[:, :, swap])

    freqs = ROPE_THETA ** (-jnp.arange(0, QK_ROPE, 2, dtype=F32) / QK_ROPE)
    freq = pad_cols(jnp.concatenate([freqs, freqs])[None, :], LANES)
    sgn = pad_cols(jnp.concatenate([-jnp.ones((half,), F32), jnp.ones((half,), F32)])[None, :], LANES)

    head_of_col = jnp.arange(MLA_WIDTH) // V_HEAD
    wuv = jnp.where(head_of_col[None, None, :] == jnp.arange(MLA_HEADS)[:, None, None],
                    w_uv[0][None], 0.0).astype(BF16)
    wuv = wuv.reshape(MLA_HEADS * KV_RANK, MLA_WIDTH)

    wsp = w_spatial[0].astype(BF16)
    bs = b_spatial[0]
    bsp = jnp.repeat(bs.reshape(SGU_GROUPS // 2, 2, SGU_LEN).transpose(0, 2, 1), SGU_GROUP_DIM, axis=2)

    wr32 = jnp.concatenate([w_router_expert[0].transpose(1, 0, 2).reshape(D_MODEL, N_EXPERTS),
                            w_router_group[0]], axis=1)
    wr32 = pad_cols(wr32, LANES)
    wr_hi = wr32.astype(BF16)
    wr_lo = (wr32 - wr_hi.astype(F32)).astype(BF16)
    wr = jnp.concatenate([wr_hi, wr_lo], axis=1)
    br = pad_cols(jnp.concatenate([b_router_expert[0].reshape(-1), b_router_group[0]])[None, :], LANES)

    xf = x.reshape(N_TOK, D_MODEL)
    pos = positions.astype(F32).reshape(N_TOK, 1)
    q, kt, v, p, ga = _inproj(
        xf, pos, g_attn_norm, win, g_cq, g_ckv, wqlat, wqr, wqrs, freq, sgn,
        sgu_gain, sgu_bias, wsp, bsp, w_o_sgu[0].astype(BF16))
    attn = _attention(q, kt, v, wuv)
    x1, h2, mf, slot_cols, slot_rows, c8_rows, loff8_rows = _mix(
        attn.reshape(N_TOK, MLA_WIDTH), ga, p, xf, w_o_attn[0].astype(BF16), w_out[0].astype(BF16),
        g_ffn_norm, wr, br)

    blk8 = MOE_T // CHUNK_ROWS
    c8 = c8_rows[::8, :N_EXPERTS]
    loff8 = loff8_rows[::8, :N_EXPERTS]
    tot8 = jnp.sum(c8, axis=0)
    pad8 = (tot8 + blk8 - 1) // blk8 * blk8
    gend8 = jnp.cumsum(pad8)
    gstart8 = gend8 - pad8
    gbase8 = gstart8[None, :] + jnp.cumsum(c8, axis=0) - c8
    nvb = (gend8[-1:] // blk8).astype(jnp.int32)
    blk_start8 = jnp.arange(MOE_BLOCKS, dtype=jnp.int32) * blk8
    blk_e = jnp.minimum(jnp.sum((gend8[None, :] <= blk_start8[:, None]).astype(jnp.int32), axis=1),
                        N_EXPERTS - 1)
    run_tables = (c8.reshape(-1), loff8.reshape(-1), gbase8.reshape(-1).astype(jnp.int32),
                  jnp.sum(c8, axis=1))

    xs = _dispatch(run_tables + ((gstart8 + tot8).astype(jnp.int32), (pad8 - tot8).astype(jnp.int32), nvb),
                   slot_rows, h2)
    ys = _experts(blk_e, nvb, xs, w_exp_gate[0], w_exp_up[0], w_exp_down[0])
    out = _combine(run_tables, slot_cols, x1, mf, g_final.reshape(1, D_MODEL), ys)
    return out.reshape(BATCH, SEQ, D_MODEL)
```

```python
import functools

import jax
import jax.numpy as jnp
from jax import lax
from jax.experimental import pallas as pl
from jax.experimental.pallas import tpu as pltpu

D_MODEL = 1024
BATCH = 2
SEQ = 8192
N_TOK = BATCH * SEQ
CHUNK = 64
EPS = 1e-6
MLA_HEADS = 8
Q_RANK = 256
KV_RANK = 128
QK_NOPE = 64
QK_ROPE = 32
V_HEAD = 64
MLA_WIDTH = MLA_HEADS * V_HEAD
ROPE_THETA = 10000.0
SGU_GROUPS = 8
SGU_GROUP_DIM = 64
SGU_WIDTH = SGU_GROUPS * SGU_GROUP_DIM
SGU_LEN = 128
N_GROUPS = 4
EXPERTS_PER_GROUP = 8
N_EXPERTS = N_GROUPS * EXPERTS_PER_GROUP
TOP_K = 2
D_EXPERT = 256

LANES = 128
QK_PAD = 2 * LANES
V_PAD = 2 * LANES
SCALE = (QK_NOPE + QK_ROPE) ** -0.5
LOG2E = 1.4426950408889634
Q_SCALE = SCALE * LOG2E
NEG = -1e30

TM = 256
TMI = 512
TQ = 128
TK = 512
TK_SHIFT = TK.bit_length() - 1
CHUNK_SHIFT = CHUNK.bit_length() - 1
assert 1 << TK_SHIFT == TK and 1 << CHUNK_SHIFT == CHUNK
MOE_T = 256
CHUNK_ROWS = 8
TILE_ROWS = -(-(TOP_K * TM + N_EXPERTS * (CHUNK_ROWS - 1) + CHUNK_ROWS) // 256) * 256
MOE_ROWS_MAX = (N_TOK * TOP_K + (N_TOK // TM) * N_EXPERTS * (CHUNK_ROWS - 1)
                + N_EXPERTS * (MOE_T - CHUNK_ROWS))
MOE_BLOCKS = -(-MOE_ROWS_MAX // MOE_T)
MOE_ROWS = MOE_BLOCKS * MOE_T

C_Q = 0
C_KV = C_Q + Q_RANK
C_KR = C_KV + KV_RANK
C_KRS = C_KR + LANES
C_U = C_KRS + LANES
C_V = C_U + SGU_WIDTH
C_GA = C_V + SGU_WIDTH
C_GB = C_GA + D_MODEL
C_END = C_GB + D_MODEL

F32 = jnp.float32
BF16 = jnp.bfloat16


def _dot(a, b):
    return jnp.dot(a, b, preferred_element_type=F32)


def _rms(x, g):
    return x * lax.rsqrt(jnp.mean(x * x, axis=-1, keepdims=True) + EPS) * g


def _fold_kernel(a_ref, b_ref, o_ref):
    o_ref[...] = Q_SCALE * jnp.dot(a_ref[...], b_ref[...], preferred_element_type=F32,
                                 precision=lax.Precision.HIGHEST)


def _fold(w_uq_nope, w_ukt):
    return pl.pallas_call(
        _fold_kernel,
        grid=(MLA_HEADS,),
        in_specs=[pl.BlockSpec((None, Q_RANK, QK_NOPE), lambda h: (h, 0, 0)),
                  pl.BlockSpec((None, QK_NOPE, KV_RANK), lambda h: (h, 0, 0))],
        out_specs=pl.BlockSpec((None, Q_RANK, KV_RANK), lambda h: (h, 0, 0)),
        out_shape=jax.ShapeDtypeStruct((MLA_HEADS, Q_RANK, KV_RANK), F32),
        name="fold",
    )(w_uq_nope, w_ukt)


def _rope_kernel(pos_ref, freq_ref, cos_ref, sin_ref):
    ang = pos_ref[...] * freq_ref[...]
    cos_ref[...] = jnp.cos(ang)
    sin_ref[...] = jnp.sin(ang)


def _rope_tables(pos8, freq8):
    rows = pos8.shape[0]
    blk = pl.BlockSpec((TM, LANES), lambda i: (i, 0))
    return pl.pallas_call(
        _rope_kernel,
        grid=(rows // TM,),
        in_specs=[blk, pl.BlockSpec((1, LANES), lambda i: (0, 0))],
        out_specs=[blk, blk],
        out_shape=[jax.ShapeDtypeStruct((rows, LANES), F32)] * 2,
        name="rope_tables",
    )(pos8, freq8)


def _inproj_kernel(x_ref, cos_ref, sin_ref, gattn_ref, win_ref, gcq_ref, gckv_ref, wqlat_ref, wqr_ref,
                   wqrs_ref, sgain_ref, sbias_ref, wsp_ref, bsp_ref, wosgu_ref,
                   q_ref, kt_ref, v_ref, p_ref, ga_ref):
    hb = _rms(x_ref[...], gattn_ref[...]).astype(BF16)

    def proj(a, b):
        return _dot(hb, win_ref[:, a:b])

    cos_t = cos_ref[...]
    sin_t = sin_ref[...]

    cqn = _rms(proj(C_Q, C_KV), gcq_ref[...]).astype(BF16)
    qlat = _dot(cqn, wqlat_ref[...])
    qr = _dot(cqn, wqr_ref[...])
    qs = _dot(cqn, wqrs_ref[...])
    cos_q = cos_t * Q_SCALE
    sin_q = sin_t * Q_SCALE
    for r in range(TMI // TQ):
        t0, t1 = r * TQ, (r + 1) * TQ
        for h in range(MLA_HEADS):
            a, b = h * LANES, (h + 1) * LANES
            q_ref[r, h * TQ:(h + 1) * TQ, :LANES] = qlat[t0:t1, a:b].astype(BF16)
            q_ref[r, h * TQ:(h + 1) * TQ, LANES:] = (
                qr[t0:t1, a:b] * cos_q[t0:t1] + qs[t0:t1, a:b] * sin_q[t0:t1]).astype(BF16)

    zk = proj(C_KV, C_U)
    ckvn = _rms(zk[:, :KV_RANK], gckv_ref[...])
    krope = zk[:, KV_RANK:KV_RANK + LANES] * cos_t + zk[:, KV_RANK + LANES:] * sin_t
    kt_ref[...] = jnp.concatenate([ckvn, krope], axis=1).T.astype(BF16)
    v_ref[...] = jnp.concatenate([ckvn, jnp.ones_like(ckvn)], axis=1).astype(BF16)

    u = jax.nn.gelu(proj(C_U, C_V))
    v = jax.nn.gelu(proj(C_V, C_GA))
    mu = jnp.mean(v, axis=-1, keepdims=True)
    vc = v - mu
    var = jnp.mean(vc * vc, axis=-1, keepdims=True)
    vb = (vc * lax.rsqrt(var + EPS) * sgain_ref[...] + sbias_ref[...]).astype(BF16)
    row = lax.broadcasted_iota(jnp.int32, (SGU_LEN, SGU_LEN), 0)
    col = lax.broadcasted_iota(jnp.int32, (SGU_LEN, SGU_LEN), 1)
    causal = (row >> CHUNK_SHIFT) >= (col >> CHUNK_SHIFT)
    low_half = col < SGU_GROUP_DIM
    row_blocks = []
    for r in range(TMI // SGU_LEN):
        pieces = []
        for p in range(SGU_WIDTH // LANES):
            blk = vb[r * SGU_LEN:(r + 1) * SGU_LEN, p * LANES:(p + 1) * LANES]
            w0 = jnp.where(causal, wsp_ref[2 * p], jnp.zeros_like(wsp_ref[2 * p]))
            w1 = jnp.where(causal, wsp_ref[2 * p + 1], jnp.zeros_like(wsp_ref[2 * p + 1]))
            sv = jnp.where(low_half, _dot(w0, blk), _dot(w1, blk)) + bsp_ref[p]
            pieces.append((u[r * SGU_LEN:(r + 1) * SGU_LEN, p * LANES:(p + 1) * LANES] * sv).astype(BF16))
        row_blocks.append(jnp.concatenate(pieces, axis=1))
    sgu = _dot(jnp.concatenate(row_blocks, axis=0), wosgu_ref[...])

    ga_ref[...] = jax.nn.sigmoid(proj(C_GA, C_GB)).astype(BF16)
    p_ref[...] = (jax.nn.sigmoid(proj(C_GB, C_END)) * sgu).astype(BF16)


def _inproj(x, cos_t, sin_t, gattn, win, gcq, gckv, wqlat, wqr, wqrs, sgain, sbias, wsp, bsp, wosgu):
    nt = N_TOK // TMI
    per_b = SEQ // TMI
    per_k = TK // TMI

    def const(shape):
        return pl.BlockSpec(shape, lambda i: (0,) * len(shape))

    return pl.pallas_call(
        _inproj_kernel,
        grid=(nt,),
        in_specs=[pl.BlockSpec((TMI, D_MODEL), lambda i: (i, 0)),
                  pl.BlockSpec((TMI, LANES), lambda i: (i, 0)), pl.BlockSpec((TMI, LANES), lambda i: (i, 0)),
                  const((1, D_MODEL)), const((D_MODEL, C_END)), const((1, Q_RANK)), const((1, KV_RANK)),
                  const((Q_RANK, MLA_HEADS * LANES)), const((Q_RANK, MLA_HEADS * LANES)),
                  const((Q_RANK, MLA_HEADS * LANES)),
                  const((1, SGU_WIDTH)), const((1, SGU_WIDTH)),
                  const((SGU_GROUPS, SGU_LEN, SGU_LEN)), const((SGU_GROUPS // 2, SGU_LEN, LANES)),
                  const((SGU_WIDTH, D_MODEL))],
        out_specs=[pl.BlockSpec((TMI // TQ, MLA_HEADS * TQ, QK_PAD), lambda i: (i, 0, 0)),
                   pl.BlockSpec((None, None, QK_PAD, TMI),
                                lambda i: (i // per_b, (i % per_b) // per_k, 0, i % per_k)),
                   pl.BlockSpec((None, None, TMI, V_PAD),
                                lambda i: (i // per_b, (i % per_b) // per_k, i % per_k, 0)),
                   pl.BlockSpec((TMI, D_MODEL), lambda i: (i, 0)),
                   pl.BlockSpec((TMI, D_MODEL), lambda i: (i, 0))],
        out_shape=[jax.ShapeDtypeStruct((N_TOK // TQ, MLA_HEADS * TQ, QK_PAD), BF16),
                   jax.ShapeDtypeStruct((BATCH, SEQ // TK, QK_PAD, TK), BF16),
                   jax.ShapeDtypeStruct((BATCH, SEQ // TK, TK, V_PAD), BF16),
                   jax.ShapeDtypeStruct((N_TOK, D_MODEL), BF16),
                   jax.ShapeDtypeStruct((N_TOK, D_MODEL), BF16)],
        compiler_params=pltpu.CompilerParams(dimension_semantics=("arbitrary",),
                                             vmem_limit_bytes=56 * 1024 * 1024),
        name="inproj",
    )(x, cos_t, sin_t, gattn, win, gcq, gckv, wqlat, wqr, wqrs, sgain, sbias, wsp, bsp, wosgu)


def _attn_kernel(q_ref, kt_ref, v_ref, wuv_ref, o_ref, m_ref, acc_ref, s_ref, p_ref, a_ref):
    qi = pl.program_id(1)
    m_ref[...] = jnp.full(m_ref.shape, NEG, F32)
    acc_ref[...] = jnp.zeros(acc_ref.shape, F32)

    def scores(j, slot):
        s_ref[slot] = _dot(q_ref[...], kt_ref[j])

    def update(j, slot, masked):
        if masked:
            q_chunk = ((qi * TQ - j * TK) >> CHUNK_SHIFT) + (
                lax.broadcasted_iota(jnp.int32, (TQ, TK), 0) >> CHUNK_SHIFT)
            k_chunk = lax.broadcasted_iota(jnp.int32, (TQ, TK), 1) >> CHUNK_SHIFT
            visible = k_chunk <= q_chunk
        for h in range(MLA_HEADS):
            rows = slice(h * TQ, (h + 1) * TQ)
            s = s_ref[slot, rows, :]
            if masked:
                s = jnp.where(visible, s, NEG)
            m_prev = m_ref[rows, :]
            m_new = jnp.maximum(m_prev, jnp.max(s, axis=-1, keepdims=True))
            m_ref[rows, :] = m_new
            alpha = jnp.exp2(m_prev - m_new)
            p_ref[rows, :] = jnp.exp2(s - jnp.concatenate([m_new] * (TK // LANES), axis=1)).astype(BF16)
            a_ref[rows, :] = alpha
        alpha = a_ref[...]
        acc_ref[...] = jnp.concatenate([alpha, alpha], axis=1) * acc_ref[...] + _dot(p_ref[...], v_ref[j])

    diag = (qi * TQ) >> TK_SHIFT
    pairs = diag >> 1
    scores(0, 0)

    def body(t, carry):
        j = 2 * t
        scores(j + 1, 1)
        update(j, 0, False)
        scores(j + 2, 0)
        update(j + 1, 1, False)
        return carry

    lax.fori_loop(0, pairs, body, 0)

    @pl.when((diag & 1) == 1)
    def _():
        scores(diag, 1)
        update(diag - 1, 0, False)
        update(diag, 1, True)

    @pl.when((diag & 1) == 0)
    def _():
        update(diag, 0, True)

    o_lat = (acc_ref[:, :KV_RANK] / acc_ref[:, KV_RANK:]).astype(BF16)
    o_cat = jnp.concatenate([o_lat[h * TQ:(h + 1) * TQ] for h in range(MLA_HEADS)], axis=1)
    o_ref[...] = _dot(o_cat, wuv_ref[...]).astype(BF16)


def _attention(q, kt, v, wuv):
    nk = SEQ // TK
    return pl.pallas_call(
        _attn_kernel,
        grid=(BATCH, SEQ // TQ),
        in_specs=[pl.BlockSpec((None, MLA_HEADS * TQ, QK_PAD), lambda b, i: (b * (SEQ // TQ) + i, 0, 0)),
                  pl.BlockSpec((None, nk, QK_PAD, TK), lambda b, i: (b, 0, 0, 0)),
                  pl.BlockSpec((None, nk, TK, V_PAD), lambda b, i: (b, 0, 0, 0)),
                  pl.BlockSpec((MLA_HEADS * KV_RANK, MLA_WIDTH), lambda b, i: (0, 0))],
        out_specs=pl.BlockSpec((None, TQ, MLA_WIDTH), lambda b, i: (b, i, 0)),
        out_shape=jax.ShapeDtypeStruct((BATCH, SEQ, MLA_WIDTH), BF16),
        scratch_shapes=[pltpu.VMEM((MLA_HEADS * TQ, LANES), F32),
                        pltpu.VMEM((MLA_HEADS * TQ, V_PAD), F32),
                        pltpu.VMEM((2, MLA_HEADS * TQ, TK), F32),
                        pltpu.VMEM((MLA_HEADS * TQ, TK), BF16),
                        pltpu.VMEM((MLA_HEADS * TQ, LANES), F32)],
        compiler_params=pltpu.CompilerParams(dimension_semantics=("arbitrary", "arbitrary"),
                                             vmem_limit_bytes=40 * 1024 * 1024),
        name="attention",
    )(q, kt, v, wuv)


def _mix_kernel(attn_ref, ga_ref, p_ref, x_ref, woa_ref, wout_ref, gffn_ref, wr_ref, br_ref,
                x1_ref, h2_ref, mf_ref, slot_col_ref, slot_row_ref, c8_ref, loff8_ref):
    a = _dot(attn_ref[...], woa_ref[...])
    mix = (ga_ref[...].astype(F32) * a + p_ref[...].astype(F32)).astype(BF16)
    x1 = x_ref[...] + _dot(mix, wout_ref[...])
    x1_ref[...] = x1
    h2 = _rms(x1, gffn_ref[...])
    h2_ref[...] = h2.astype(BF16)

    hi = h2.astype(BF16)
    lo = (h2 - hi.astype(F32)).astype(BF16)
    r1 = _dot(hi, wr_ref[...])
    r2 = _dot(lo, wr_ref[:, :LANES])
    logits = r1[:, :LANES] + r1[:, LANES:] + r2 + br_ref[...]

    lane_i = lax.broadcasted_iota(jnp.int32, (TM, LANES), 1)
    lane = lane_i.astype(F32)
    lane_group = (lane_i >> 3).astype(F32)
    ninf = -jnp.inf
    is_group = (lane_i >= N_EXPERTS) & (lane_i < N_EXPERTS + N_GROUPS)
    lg = jnp.where(is_group, logits, ninf)
    gmax = jnp.max(lg, axis=-1, keepdims=True)
    gsum = jnp.sum(jnp.exp(lg - gmax), axis=-1, keepdims=True)
    p_top = 1.0 / gsum
    g_idx = jnp.min(jnp.where(lg == gmax, lane - N_EXPERTS, float(N_GROUPS)), axis=-1, keepdims=True)
    le = jnp.where((lane_i < N_EXPERTS) & (lane_group == g_idx), logits, ninf)
    t1 = jnp.max(le, axis=-1, keepdims=True)
    e1 = jnp.min(jnp.where(le == t1, lane, float(LANES)), axis=-1, keepdims=True)
    le2 = jnp.where(lane == e1, ninf, le)
    t2 = jnp.max(le2, axis=-1, keepdims=True)
    e2 = jnp.min(jnp.where(le2 == t2, lane, float(LANES)), axis=-1, keepdims=True)
    ex = jnp.exp(t2 - t1)
    w1 = p_top / (1.0 + ex)
    w2 = p_top * ex / (1.0 + ex)

    sel1 = lane == e1
    sel2 = lane == e2
    onehot = jnp.where(sel1 | sel2, 1.0, 0.0)
    rr = lax.broadcasted_iota(jnp.int32, (TM, TM), 0)
    cc = lax.broadcasted_iota(jnp.int32, (TM, TM), 1)
    ltri = jnp.where(cc < rr, 1.0, 0.0).astype(BF16)
    rank = _dot(ltri, onehot.astype(BF16))
    cnt = jnp.sum(onehot, axis=0, keepdims=True)
    c8 = jnp.floor((cnt + (CHUNK_ROWS - 1)) * (1.0 / CHUNK_ROWS))
    ur = lax.broadcasted_iota(jnp.int32, (LANES, LANES), 0)
    uc = lax.broadcasted_iota(jnp.int32, (LANES, LANES), 1)
    upper = jnp.where(ur < uc, 1.0, 0.0).astype(BF16)
    c8_rows = jnp.broadcast_to(c8, (8, LANES))
    loff8_rows = _dot(c8_rows.astype(BF16), upper)
    slot_all = CHUNK_ROWS * loff8_rows[0:1] + rank
    slot1 = jnp.sum(jnp.where(sel1, slot_all, 0.0), axis=-1, keepdims=True)
    slot2 = jnp.sum(jnp.where(sel2, slot_all, 0.0), axis=-1, keepdims=True)

    slots = jnp.where(lane_i == 0, slot1, jnp.where(lane_i == 1, slot2, 0.0))
    slot_col_ref[...] = slots.astype(jnp.int32)
    slot_row_ref[...] = slots.T[:8].astype(jnp.int32)
    c8_ref[...] = c8_rows.astype(jnp.int32)
    loff8_ref[...] = loff8_rows.astype(jnp.int32)
    mf_ref[...] = jnp.where(lane_i == 0, w1, w2)


def _mix(attn, ga, p, x, woa, wout, gffn, wr, br):
    nt = N_TOK // TM

    def const(shape):
        return pl.BlockSpec(shape, lambda i: (0,) * len(shape))

    def rows(width):
        return pl.BlockSpec((TM, width), lambda i: (i, 0))

    return pl.pallas_call(
        _mix_kernel,
        grid=(nt,),
        in_specs=[rows(MLA_WIDTH), rows(D_MODEL), rows(D_MODEL), rows(D_MODEL),
                  const((MLA_WIDTH, D_MODEL)), const((D_MODEL, D_MODEL)), const((1, D_MODEL)),
                  const((D_MODEL, 2 * LANES)), const((1, LANES))],
        out_specs=[rows(D_MODEL), rows(D_MODEL), rows(LANES), rows(LANES),
                   pl.BlockSpec((8, TM), lambda i: (i, 0)),
                   pl.BlockSpec((8, LANES), lambda i: (i, 0)),
                   pl.BlockSpec((8, LANES), lambda i: (i, 0))],
        out_shape=[jax.ShapeDtypeStruct((N_TOK, D_MODEL), F32),
                   jax.ShapeDtypeStruct((N_TOK, D_MODEL), BF16),
                   jax.ShapeDtypeStruct((N_TOK, LANES), F32),
                   jax.ShapeDtypeStruct((N_TOK, LANES), jnp.int32),
                   jax.ShapeDtypeStruct((nt * 8, TM), jnp.int32),
                   jax.ShapeDtypeStruct((nt * 8, LANES), jnp.int32),
                   jax.ShapeDtypeStruct((nt * 8, LANES), jnp.int32)],
        compiler_params=pltpu.CompilerParams(dimension_semantics=("arbitrary",),
                                             vmem_limit_bytes=40 * 1024 * 1024),
        name="mix",
    )(attn, ga, p, x, woa, wout, gffn, wr, br)


def _chunk_copy(src, dst, s8, d8, sem):
    return pltpu.make_async_copy(src.at[pl.ds(pl.multiple_of(s8 * CHUNK_ROWS, CHUNK_ROWS), CHUNK_ROWS)],
                                 dst.at[pl.ds(pl.multiple_of(d8 * CHUNK_ROWS, CHUNK_ROWS), CHUNK_ROWS)], sem)


def _for_each_chunk(t, c8_ref, loff8_ref, gbase8_ref, fn):
    def per_expert(e, carry):
        k = t * N_EXPERTS + e
        lo = loff8_ref[k]
        gb = gbase8_ref[k]

        def per_chunk(j, carry2):
            fn(lo + j, gb + j)
            return carry2

        lax.fori_loop(0, c8_ref[k], per_chunk, 0)
        return carry

    lax.fori_loop(0, N_EXPERTS, per_expert, 0)


def _dispatch_kernel(c8_ref, loff8_ref, gbase8_ref, ntile_ref, zstart8_ref, zcnt8_ref, nvb_ref,
                     slot_ref, h2_ref, xs_hbm, sbuf, zbuf, sem, zsem):
    t = pl.program_id(0)
    last = pl.num_programs(0) - 1
    cur = t % 2

    def zero_chunk_copy(d8):
        return _chunk_copy(zbuf, xs_hbm, 0, d8, zsem)

    def zero_block_copy(b):
        return pltpu.make_async_copy(
            zbuf, xs_hbm.at[pl.ds(pl.multiple_of(b * MOE_T, MOE_T), MOE_T)], zsem)

    def for_each_zero(chunk_fn, block_fn):
        def per_expert(e, carry):
            def per_chunk(j, carry2):
                chunk_fn(zstart8_ref[e] + j)
                return carry2

            lax.fori_loop(0, zcnt8_ref[e], per_chunk, 0)
            return carry

        lax.fori_loop(0, N_EXPERTS, per_expert, 0)

        def per_block(b, carry):
            block_fn(b)
            return carry

        lax.fori_loop(nvb_ref[0], MOE_BLOCKS, per_block, 0)

    @pl.when(t == 0)
    def _():
        zbuf[...] = jnp.zeros(zbuf.shape, F32)
        for_each_zero(lambda d8: zero_chunk_copy(d8).start(), lambda b: zero_block_copy(b).start())

    row = lax.broadcasted_iota(jnp.int32, (TILE_ROWS, TM), 0)
    slots = slot_ref[...]
    perm = jnp.where((row == slots[0:1, :]) | (row == slots[1:2, :]), 1.0, 0.0).astype(BF16)
    sbuf[cur] = _dot(perm, h2_ref[...])
    _for_each_chunk(t, c8_ref, loff8_ref, gbase8_ref,
                    lambda lc, gc: _chunk_copy(sbuf.at[cur], xs_hbm, lc, gc, sem.at[cur]).start())

    def wait_tile(tile, slot):
        def wait_one(j, carry):
            _chunk_copy(sbuf.at[slot], xs_hbm, 0, 0, sem.at[slot]).wait()
            return carry

        lax.fori_loop(0, ntile_ref[tile], wait_one, 0)

    @pl.when(t > 0)
    def _():
        wait_tile(t - 1, 1 - cur)

    @pl.when(t == last)
    def _():
        wait_tile(t, cur)
        for_each_zero(lambda d8: zero_chunk_copy(d8).wait(), lambda b: zero_block_copy(b).wait())


def _dispatch(tables, slot_rows, h2):
    return pl.pallas_call(
        _dispatch_kernel,
        grid_spec=pltpu.PrefetchScalarGridSpec(
            num_scalar_prefetch=len(tables),
            grid=(N_TOK // TM,),
            in_specs=[pl.BlockSpec((8, TM), lambda i, *_: (i, 0)),
                      pl.BlockSpec((TM, D_MODEL), lambda i, *_: (i, 0))],
            out_specs=pl.BlockSpec(memory_space=pl.ANY),
            scratch_shapes=[pltpu.VMEM((2, TILE_ROWS, D_MODEL), F32), pltpu.VMEM((MOE_T, D_MODEL), F32),
                            pltpu.SemaphoreType.DMA((2,)), pltpu.SemaphoreType.DMA(())]),
        out_shape=jax.ShapeDtypeStruct((MOE_ROWS, D_MODEL), F32),
        compiler_params=pltpu.CompilerParams(dimension_semantics=("arbitrary",)),
        name="dispatch",
    )(*tables, slot_rows, h2)


def _expert_kernel(blk_e_ref, nvb_ref, xs_ref, wg_ref, wu_ref, wd_ref, ys_ref, wgb, wub, wdb):
    i = pl.program_id(0)
    live = i < nvb_ref[0]

    @pl.when(live & ((i == 0) | (blk_e_ref[i] != blk_e_ref[jnp.maximum(i - 1, 0)])))
    def _():
        wgb[...] = wg_ref[...].astype(BF16)
        wub[...] = wu_ref[...].astype(BF16)
        wdb[...] = wd_ref[...].astype(BF16)

    @pl.when(live)
    def _():
        xb = xs_ref[...].astype(BF16)
        g = _dot(xb, wgb[...])
        u = _dot(xb, wub[...])
        hid = (jax.nn.silu(g) * u).astype(BF16)
        ys_ref[...] = _dot(hid, wdb[...])

    @pl.when(jnp.logical_not(live))
    def _():
        ys_ref[...] = jnp.zeros(ys_ref.shape, ys_ref.dtype)


def _experts(blk_e, nvb, xs, wg, wu, wd):
    def row_block(i, be, nv):
        return (jnp.minimum(i, nv[0] - 1), 0)

    def weight(i, be, nv):
        return (be[jnp.minimum(i, nv[0] - 1)], 0, 0)

    return pl.pallas_call(
        _expert_kernel,
        grid_spec=pltpu.PrefetchScalarGridSpec(
            num_scalar_prefetch=2,
            grid=(MOE_BLOCKS,),
            in_specs=[pl.BlockSpec((MOE_T, D_MODEL), row_block),
                      pl.BlockSpec((None, D_MODEL, D_EXPERT), weight),
                      pl.BlockSpec((None, D_MODEL, D_EXPERT), weight),
                      pl.BlockSpec((None, D_EXPERT, D_MODEL), weight)],
            out_specs=pl.BlockSpec((MOE_T, D_MODEL), lambda i, be, nv: (i, 0)),
            scratch_shapes=[pltpu.VMEM((D_MODEL, D_EXPERT), BF16), pltpu.VMEM((D_MODEL, D_EXPERT), BF16),
                            pltpu.VMEM((D_EXPERT, D_MODEL), BF16)]),
        out_shape=jax.ShapeDtypeStruct((MOE_ROWS, D_MODEL), F32),
        compiler_params=pltpu.CompilerParams(dimension_semantics=("arbitrary",),
                                             vmem_limit_bytes=40 * 1024 * 1024),
        name="experts",
    )(blk_e, nvb, xs, wg, wu, wd)


def _combine_kernel(c8_ref, loff8_ref, gbase8_ref, ntile_ref,
                    slot_ref, x1_ref, mf_ref, gfin_ref, ys_hbm, o_ref, ybuf, sem):
    t = pl.program_id(0)
    cur = t % 2

    def fetch(tile, slot):
        _for_each_chunk(tile, c8_ref, loff8_ref, gbase8_ref,
                        lambda lc, gc: _chunk_copy(ys_hbm, ybuf.at[slot], gc, lc, sem.at[slot]).start())

    @pl.when(t == 0)
    def _():
        ybuf[...] = jnp.zeros(ybuf.shape, F32)
        fetch(0, 0)

    @pl.when(t + 1 < pl.num_programs(0))
    def _():
        fetch(t + 1, 1 - cur)

    def wait_one(j, carry):
        _chunk_copy(ys_hbm, ybuf.at[cur], 0, 0, sem.at[cur]).wait()
        return carry

    lax.fori_loop(0, ntile_ref[t], wait_one, 0)

    yb = ybuf[cur].astype(BF16)
    col = lax.broadcasted_iota(jnp.int32, (TM, TILE_ROWS), 1)
    slots = slot_ref[...]
    y1 = _dot(jnp.where(col == slots[:, 0:1], 1.0, 0.0).astype(BF16), yb)
    y2 = _dot(jnp.where(col == slots[:, 1:2], 1.0, 0.0).astype(BF16), yb)
    mf = mf_ref[...]
    x2 = x1_ref[...] + mf[:, 0:1] * y1 + mf[:, 1:2] * y2
    o_ref[...] = _rms(x2, gfin_ref[...])


def _combine(tables, slot_cols, x1, mf, gfin, ys):
    return pl.pallas_call(
        _combine_kernel,
        grid_spec=pltpu.PrefetchScalarGridSpec(
            num_scalar_prefetch=len(tables),
            grid=(N_TOK // TM,),
            in_specs=[pl.BlockSpec((TM, LANES), lambda i, *_: (i, 0)),
                      pl.BlockSpec((TM, D_MODEL), lambda i, *_: (i, 0)),
                      pl.BlockSpec((TM, LANES), lambda i, *_: (i, 0)),
                      pl.BlockSpec((1, D_MODEL), lambda i, *_: (0, 0)),
                      pl.BlockSpec(memory_space=pl.ANY)],
            out_specs=pl.BlockSpec((TM, D_MODEL), lambda i, *_: (i, 0)),
            scratch_shapes=[pltpu.VMEM((2, TILE_ROWS, D_MODEL), F32), pltpu.SemaphoreType.DMA((2,))]),
        out_shape=jax.ShapeDtypeStruct((N_TOK, D_MODEL), F32),
        compiler_params=pltpu.CompilerParams(dimension_semantics=("arbitrary",)),
        name="combine",
    )(*tables, slot_cols, x1, mf, gfin, ys)


def kernel(x, positions, g_attn_norm, w_in, g_cq, w_uq, g_ckv, w_uk, w_uv, w_o_attn, sgu_gain, sgu_bias, w_spatial, b_spatial, w_o_sgu, w_out, g_ffn_norm, w_router_group, b_router_group, w_router_expert, b_router_expert, w_exp_gate, w_exp_up, w_exp_down, g_final):
    assert x.shape == (BATCH, SEQ, D_MODEL) and w_in.shape[0] == 1
    half = QK_ROPE // 2
    swap = jnp.concatenate([jnp.arange(half, QK_ROPE), jnp.arange(0, half)])

    def pad_cols(w, width):
        return jnp.pad(w, ((0, 0), (0, width - w.shape[1])))

    wi = w_in[0]
    c0 = Q_RANK + KV_RANK
    kr = wi[:, c0:c0 + QK_ROPE]
    c1 = c0 + QK_ROPE
    win = jnp.concatenate([
        wi[:, :c0], pad_cols(kr, LANES), pad_cols(kr[:, swap], LANES), wi[:, c1:]], axis=1).astype(BF16)

    wq = w_uq[0].reshape(Q_RANK, MLA_HEADS, QK_NOPE + QK_ROPE)
    wq_nope = wq[:, :, :QK_NOPE].transpose(1, 0, 2)
    wq_rope = wq[:, :, QK_NOPE:]
    w_ukt = w_uk[0].reshape(KV_RANK, MLA_HEADS, QK_NOPE).transpose(1, 2, 0)
    wqlat = _fold(wq_nope, w_ukt).transpose(1, 0, 2).reshape(Q_RANK, MLA_HEADS * LANES).astype(BF16)

    def rope_cols(w):
        return jnp.pad(w, ((0, 0), (0, 0), (0, LANES - QK_ROPE))).reshape(Q_RANK, MLA_HEADS * LANES).astype(BF16)

    wqr = rope_cols(wq_rope)
    wqrs = rope_cols(wq_rope[:, :, swap])

    per_row = LANES // half
    freqs = ROPE_THETA ** (-jnp.arange(0, QK_ROPE, 2, dtype=F32) / QK_ROPE)
    pos8 = jnp.repeat(positions.astype(F32).reshape(N_TOK // per_row, per_row), half, axis=1)
    cos8, sin8 = _rope_tables(pos8, jnp.tile(freqs, per_row)[None, :])
    cos16 = cos8.reshape(N_TOK, half)
    sin16 = sin8.reshape(N_TOK, half)
    cos_t = pad_cols(jnp.concatenate([cos16, cos16], axis=1), LANES)
    sin_t = pad_cols(jnp.concatenate([-sin16, sin16], axis=1), LANES)

    head_of_col = jnp.arange(MLA_WIDTH) // V_HEAD
    wuv = jnp.where(head_of_col[None, None, :] == jnp.arange(MLA_HEADS)[:, None, None],
                    w_uv[0][None], 0.0).astype(BF16)
    wuv = wuv.reshape(MLA_HEADS * KV_RANK, MLA_WIDTH)

    wsp = w_spatial[0].astype(BF16)
    bs = b_spatial[0]
    bsp = jnp.repeat(bs.reshape(SGU_GROUPS // 2, 2, SGU_LEN).transpose(0, 2, 1), SGU_GROUP_DIM, axis=2)

    wr32 = jnp.concatenate([w_router_expert[0].transpose(1, 0, 2).reshape(D_MODEL, N_EXPERTS),
                            w_router_group[0]], axis=1)
    wr32 = pad_cols(wr32, LANES)
    wr_hi = wr32.astype(BF16)
    wr_lo = (wr32 - wr_hi.astype(F32)).astype(BF16)
    wr = jnp.concatenate([wr_hi, wr_lo], axis=1)
    br = pad_cols(jnp.concatenate([b_router_expert[0].reshape(-1), b_router_group[0]])[None, :], LANES)

    xf = x.reshape(N_TOK, D_MODEL)
    q, kt, v, p, ga = _inproj(
        xf, cos_t, sin_t, g_attn_norm, win, g_cq, g_ckv, wqlat, wqr, wqrs,
        sgu_gain, sgu_bias, wsp, bsp, w_o_sgu[0].astype(BF16))
    attn = _attention(q, kt, v, wuv)
    x1, h2, mf, slot_cols, slot_rows, c8_rows, loff8_rows = _mix(
        attn.reshape(N_TOK, MLA_WIDTH), ga, p, xf, w_o_attn[0].astype(BF16), w_out[0].astype(BF16),
        g_ffn_norm, wr, br)

    blk8 = MOE_T // CHUNK_ROWS
    c8 = c8_rows[::8, :N_EXPERTS]
    loff8 = loff8_rows[::8, :N_EXPERTS]
    tot8 = jnp.sum(c8, axis=0)
    pad8 = (tot8 + blk8 - 1) // blk8 * blk8
    gend8 = jnp.cumsum(pad8)
    gstart8 = gend8 - pad8
    gbase8 = gstart8[None, :] + jnp.cumsum(c8, axis=0) - c8
    nvb = (gend8[-1:] // blk8).astype(jnp.int32)
    blk_start8 = jnp.arange(MOE_BLOCKS, dtype=jnp.int32) * blk8
    blk_e = jnp.minimum(jnp.sum((gend8[None, :] <= blk_start8[:, None]).astype(jnp.int32), axis=1),
                        N_EXPERTS - 1)
    run_tables = (c8.reshape(-1), loff8.reshape(-1), gbase8.reshape(-1).astype(jnp.int32),
                  jnp.sum(c8, axis=1))

    xs = _dispatch(run_tables + ((gstart8 + tot8).astype(jnp.int32), (pad8 - tot8).astype(jnp.int32), nvb),
                   slot_rows, h2)
    ys = _experts(blk_e, nvb, xs, w_exp_gate[0], w_exp_up[0], w_exp_down[0])
    out = _combine(run_tables, slot_cols, x1, mf, g_final.reshape(1, D_MODEL), ys)
    return out.reshape(BATCH, SEQ, D_MODEL)
```

```python
import functools

import jax
import jax.numpy as jnp
from jax import lax
from jax.experimental import pallas as pl
from jax.experimental.pallas import tpu as pltpu

D_MODEL = 1024
BATCH = 2
SEQ = 8192
N_TOK = BATCH * SEQ
CHUNK = 64
EPS = 1e-6
MLA_HEADS = 8
Q_RANK = 256
KV_RANK = 128
QK_NOPE = 64
QK_ROPE = 32
V_HEAD = 64
MLA_WIDTH = MLA_HEADS * V_HEAD
ROPE_THETA = 10000.0
SGU_GROUPS = 8
SGU_GROUP_DIM = 64
SGU_WIDTH = SGU_GROUPS * SGU_GROUP_DIM
SGU_LEN = 128
N_GROUPS = 4
EXPERTS_PER_GROUP = 8
N_EXPERTS = N_GROUPS * EXPERTS_PER_GROUP
TOP_K = 2
D_EXPERT = 256

LANES = 128
QK_PAD = 2 * LANES
V_PAD = 2 * LANES
SCALE = (QK_NOPE + QK_ROPE) ** -0.5
LOG2E = 1.4426950408889634
Q_SCALE = SCALE * LOG2E
NEG = -1e30

TM = 256
TMI = 512
TMX = 512
TQ = 128
TK = 512
ATTN_UNROLL_SHIFT = 1
ATTN_UNROLL = 1 << ATTN_UNROLL_SHIFT
TK_SHIFT = TK.bit_length() - 1
CHUNK_SHIFT = CHUNK.bit_length() - 1
assert 1 << TK_SHIFT == TK and 1 << CHUNK_SHIFT == CHUNK
MOE_T = 256
CHUNK_ROWS = 8
TILE_ROWS = -(-(TOP_K * TM + N_EXPERTS * (CHUNK_ROWS - 1) + CHUNK_ROWS) // 256) * 256
TILE_CHUNKS = TILE_ROWS // CHUNK_ROWS
CHUNK_UNROLL_SHIFT = 2
CHUNK_UNROLL = 1 << CHUNK_UNROLL_SHIFT
MOE_ROWS_MAX = (N_TOK * TOP_K + (N_TOK // TM) * N_EXPERTS * (CHUNK_ROWS - 1)
                + N_EXPERTS * (MOE_T - CHUNK_ROWS))
MOE_BLOCKS = -(-MOE_ROWS_MAX // MOE_T)
MOE_ROWS = MOE_BLOCKS * MOE_T

C_Q = 0
C_KV = C_Q + Q_RANK
C_KR = C_KV + KV_RANK
C_KRS = C_KR + LANES
C_U = C_KRS + LANES
C_V = C_U + SGU_WIDTH
C_GA = C_V + SGU_WIDTH
C_GB = C_GA + D_MODEL
C_END = C_GB + D_MODEL

F32 = jnp.float32
BF16 = jnp.bfloat16


def _dot(a, b):
    return jnp.dot(a, b, preferred_element_type=F32)


def _rms(x, g):
    return x * lax.rsqrt(jnp.mean(x * x, axis=-1, keepdims=True) + EPS) * g


def _fold_kernel(a_ref, b_ref, o_ref):
    o_ref[...] = Q_SCALE * jnp.dot(a_ref[...], b_ref[...], preferred_element_type=F32,
                                 precision=lax.Precision.HIGHEST)


def _fold(w_uq_nope, w_ukt):
    return pl.pallas_call(
        _fold_kernel,
        grid=(MLA_HEADS,),
        in_specs=[pl.BlockSpec((None, Q_RANK, QK_NOPE), lambda h: (h, 0, 0)),
                  pl.BlockSpec((None, QK_NOPE, KV_RANK), lambda h: (h, 0, 0))],
        out_specs=pl.BlockSpec((None, Q_RANK, KV_RANK), lambda h: (h, 0, 0)),
        out_shape=jax.ShapeDtypeStruct((MLA_HEADS, Q_RANK, KV_RANK), F32),
        name="fold",
    )(w_uq_nope, w_ukt)


def _rope_kernel(pos_ref, freq_ref, cos_ref, sin_ref):
    ang = pos_ref[...] * freq_ref[...]
    cos_ref[...] = jnp.cos(ang)
    sin_ref[...] = jnp.sin(ang)


def _rope_tables(pos8, freq8):
    rows = pos8.shape[0]
    blk = pl.BlockSpec((TM, LANES), lambda i: (i, 0))
    return pl.pallas_call(
        _rope_kernel,
        grid=(rows // TM,),
        in_specs=[blk, pl.BlockSpec((1, LANES), lambda i: (0, 0))],
        out_specs=[blk, blk],
        out_shape=[jax.ShapeDtypeStruct((rows, LANES), F32)] * 2,
        name="rope_tables",
    )(pos8, freq8)


def _inproj_kernel(x_ref, cos_ref, sin_ref, gattn_ref, win_ref, gcq_ref, gckv_ref, wqlat_ref, wqr_ref,
                   wqrs_ref, sgain_ref, sbias_ref, wsp_ref, bsp_ref, wosgu_ref,
                   q_ref, kt_ref, v_ref, p_ref, ga_ref):
    hb = _rms(x_ref[...], gattn_ref[...]).astype(BF16)

    def proj(a, b):
        return _dot(hb, win_ref[:, a:b])

    cos_t = cos_ref[...]
    sin_t = sin_ref[...]

    cqn = _rms(proj(C_Q, C_KV), gcq_ref[...]).astype(BF16)
    qlat = _dot(cqn, wqlat_ref[...])
    qr = _dot(cqn, wqr_ref[...])
    qs = _dot(cqn, wqrs_ref[...])
    cos_q = cos_t * Q_SCALE
    sin_q = sin_t * Q_SCALE
    for r in range(TMI // TQ):
        t0, t1 = r * TQ, (r + 1) * TQ
        for h in range(MLA_HEADS):
            a, b = h * LANES, (h + 1) * LANES
            q_ref[r, h * TQ:(h + 1) * TQ, :LANES] = qlat[t0:t1, a:b].astype(BF16)
            q_ref[r, h * TQ:(h + 1) * TQ, LANES:] = (
                qr[t0:t1, a:b] * cos_q[t0:t1] + qs[t0:t1, a:b] * sin_q[t0:t1]).astype(BF16)

    zk = proj(C_KV, C_U)
    ckvn = _rms(zk[:, :KV_RANK], gckv_ref[...])
    krope = zk[:, KV_RANK:KV_RANK + LANES] * cos_t + zk[:, KV_RANK + LANES:] * sin_t
    kt_ref[...] = jnp.concatenate([ckvn, krope], axis=1).T.astype(BF16)
    v_ref[...] = jnp.concatenate([ckvn, jnp.ones_like(ckvn)], axis=1).astype(BF16)

    u = jax.nn.gelu(proj(C_U, C_V))
    v = jax.nn.gelu(proj(C_V, C_GA))
    mu = jnp.mean(v, axis=-1, keepdims=True)
    vc = v - mu
    var = jnp.mean(vc * vc, axis=-1, keepdims=True)
    vb = (vc * lax.rsqrt(var + EPS) * sgain_ref[...] + sbias_ref[...]).astype(BF16)
    row = lax.broadcasted_iota(jnp.int32, (SGU_LEN, SGU_LEN), 0)
    col = lax.broadcasted_iota(jnp.int32, (SGU_LEN, SGU_LEN), 1)
    causal = (row >> CHUNK_SHIFT) >= (col >> CHUNK_SHIFT)
    low_half = col < SGU_GROUP_DIM
    row_blocks = []
    for r in range(TMI // SGU_LEN):
        pieces = []
        for p in range(SGU_WIDTH // LANES):
            blk = vb[r * SGU_LEN:(r + 1) * SGU_LEN, p * LANES:(p + 1) * LANES]
            w0 = jnp.where(causal, wsp_ref[2 * p], jnp.zeros_like(wsp_ref[2 * p]))
            w1 = jnp.where(causal, wsp_ref[2 * p + 1], jnp.zeros_like(wsp_ref[2 * p + 1]))
            sv = jnp.where(low_half, _dot(w0, blk), _dot(w1, blk)) + bsp_ref[p]
            pieces.append((u[r * SGU_LEN:(r + 1) * SGU_LEN, p * LANES:(p + 1) * LANES] * sv).astype(BF16))
        row_blocks.append(jnp.concatenate(pieces, axis=1))
    sgu = _dot(jnp.concatenate(row_blocks, axis=0), wosgu_ref[...])

    ga_ref[...] = jax.nn.sigmoid(proj(C_GA, C_GB)).astype(BF16)
    p_ref[...] = (jax.nn.sigmoid(proj(C_GB, C_END)) * sgu).astype(BF16)


def _inproj(x, cos_t, sin_t, gattn, win, gcq, gckv, wqlat, wqr, wqrs, sgain, sbias, wsp, bsp, wosgu):
    nt = N_TOK // TMI
    per_b = SEQ // TMI
    per_k = TK // TMI

    def const(shape):
        return pl.BlockSpec(shape, lambda i: (0,) * len(shape))

    return pl.pallas_call(
        _inproj_kernel,
        grid=(nt,),
        in_specs=[pl.BlockSpec((TMI, D_MODEL), lambda i: (i, 0)),
                  pl.BlockSpec((TMI, LANES), lambda i: (i, 0)), pl.BlockSpec((TMI, LANES), lambda i: (i, 0)),
                  const((1, D_MODEL)), const((D_MODEL, C_END)), const((1, Q_RANK)), const((1, KV_RANK)),
                  const((Q_RANK, MLA_HEADS * LANES)), const((Q_RANK, MLA_HEADS * LANES)),
                  const((Q_RANK, MLA_HEADS * LANES)),
                  const((1, SGU_WIDTH)), const((1, SGU_WIDTH)),
                  const((SGU_GROUPS, SGU_LEN, SGU_LEN)), const((SGU_GROUPS // 2, SGU_LEN, LANES)),
                  const((SGU_WIDTH, D_MODEL))],
        out_specs=[pl.BlockSpec((TMI // TQ, MLA_HEADS * TQ, QK_PAD), lambda i: (i, 0, 0)),
                   pl.BlockSpec((None, None, QK_PAD, TMI),
                                lambda i: (i // per_b, (i % per_b) // per_k, 0, i % per_k)),
                   pl.BlockSpec((None, None, TMI, V_PAD),
                                lambda i: (i // per_b, (i % per_b) // per_k, i % per_k, 0)),
                   pl.BlockSpec((TMI, D_MODEL), lambda i: (i, 0)),
                   pl.BlockSpec((TMI, D_MODEL), lambda i: (i, 0))],
        out_shape=[jax.ShapeDtypeStruct((N_TOK // TQ, MLA_HEADS * TQ, QK_PAD), BF16),
                   jax.ShapeDtypeStruct((BATCH, SEQ // TK, QK_PAD, TK), BF16),
                   jax.ShapeDtypeStruct((BATCH, SEQ // TK, TK, V_PAD), BF16),
                   jax.ShapeDtypeStruct((N_TOK, D_MODEL), BF16),
                   jax.ShapeDtypeStruct((N_TOK, D_MODEL), BF16)],
        compiler_params=pltpu.CompilerParams(dimension_semantics=("arbitrary",),
                                             vmem_limit_bytes=56 * 1024 * 1024),
        name="inproj",
    )(x, cos_t, sin_t, gattn, win, gcq, gckv, wqlat, wqr, wqrs, sgain, sbias, wsp, bsp, wosgu)


def _attn_kernel(q_ref, kt_ref, v_ref, wuv_ref, o_ref, m_ref, acc_ref, s_ref, p_ref, a_ref):
    qi = pl.program_id(1)
    m_ref[...] = jnp.full(m_ref.shape, NEG, F32)
    acc_ref[...] = jnp.zeros(acc_ref.shape, F32)

    def scores(j, slot):
        s_ref[slot] = _dot(q_ref[...], kt_ref[j])

    def update(j, slot, masked):
        if masked:
            q_chunk = ((qi * TQ - j * TK) >> CHUNK_SHIFT) + (
                lax.broadcasted_iota(jnp.int32, (TQ, TK), 0) >> CHUNK_SHIFT)
            k_chunk = lax.broadcasted_iota(jnp.int32, (TQ, TK), 1) >> CHUNK_SHIFT
            visible = k_chunk <= q_chunk
        for h in range(MLA_HEADS):
            rows = slice(h * TQ, (h + 1) * TQ)
            s = s_ref[slot, rows, :]
            if masked:
                s = jnp.where(visible, s, NEG)
            m_prev = m_ref[rows, :]
            m_new = jnp.maximum(m_prev, jnp.max(s, axis=-1, keepdims=True))
            m_ref[rows, :] = m_new
            alpha = jnp.exp2(m_prev - m_new)
            p_ref[rows, :] = jnp.exp2(s - jnp.concatenate([m_new] * (TK // LANES), axis=1)).astype(BF16)
            a_ref[rows, :] = alpha
        alpha = a_ref[...]
        acc_ref[...] = jnp.concatenate([alpha, alpha], axis=1) * acc_ref[...] + _dot(p_ref[...], v_ref[j])

    diag = (qi * TQ) >> TK_SHIFT
    scores(0, 0)

    def run(first, count, last_masked):
        for u in range(count):
            last = u == count - 1
            if not (last and last_masked):
                scores(first + u + 1, (u + 1) % 2)
            update(first + u, u % 2, last and last_masked)

    def body(t, carry):
        run(ATTN_UNROLL * t, ATTN_UNROLL, False)
        return carry

    trips = diag >> ATTN_UNROLL_SHIFT
    lax.fori_loop(0, trips, body, 0)
    done = trips << ATTN_UNROLL_SHIFT
    for r in range(ATTN_UNROLL):
        @pl.when(diag - done == r)
        def _(r=r):
            run(done, r + 1, True)

    o_lat = (acc_ref[:, :KV_RANK] / acc_ref[:, KV_RANK:]).astype(BF16)
    o_cat = jnp.concatenate([o_lat[h * TQ:(h + 1) * TQ] for h in range(MLA_HEADS)], axis=1)
    o_ref[...] = _dot(o_cat, wuv_ref[...]).astype(BF16)


def _attention(q, kt, v, wuv):
    nk = SEQ // TK
    return pl.pallas_call(
        _attn_kernel,
        grid=(BATCH, SEQ // TQ),
        in_specs=[pl.BlockSpec((None, MLA_HEADS * TQ, QK_PAD), lambda b, i: (b * (SEQ // TQ) + i, 0, 0)),
                  pl.BlockSpec((None, nk, QK_PAD, TK), lambda b, i: (b, 0, 0, 0)),
                  pl.BlockSpec((None, nk, TK, V_PAD), lambda b, i: (b, 0, 0, 0)),
                  pl.BlockSpec((MLA_HEADS * KV_RANK, MLA_WIDTH), lambda b, i: (0, 0))],
        out_specs=pl.BlockSpec((None, TQ, MLA_WIDTH), lambda b, i: (b, i, 0)),
        out_shape=jax.ShapeDtypeStruct((BATCH, SEQ, MLA_WIDTH), BF16),
        scratch_shapes=[pltpu.VMEM((MLA_HEADS * TQ, LANES), F32),
                        pltpu.VMEM((MLA_HEADS * TQ, V_PAD), F32),
                        pltpu.VMEM((2, MLA_HEADS * TQ, TK), F32),
                        pltpu.VMEM((MLA_HEADS * TQ, TK), BF16),
                        pltpu.VMEM((MLA_HEADS * TQ, LANES), F32)],
        compiler_params=pltpu.CompilerParams(dimension_semantics=("arbitrary", "arbitrary"),
                                             vmem_limit_bytes=40 * 1024 * 1024),
        name="attention",
    )(q, kt, v, wuv)


def _mix_kernel(attn_ref, ga_ref, p_ref, x_ref, woa_ref, wout_ref, gffn_ref, wr_ref, br_ref,
                x1_ref, h2_ref, mf_ref, slot_col_ref, slot_row_ref, c8_ref, loff8_ref):
    a = _dot(attn_ref[...], woa_ref[...])
    mix = (ga_ref[...].astype(F32) * a + p_ref[...].astype(F32)).astype(BF16)
    x1 = x_ref[...] + _dot(mix, wout_ref[...])
    x1_ref[...] = x1
    h2 = _rms(x1, gffn_ref[...])
    h2_ref[...] = h2.astype(BF16)

    hi = h2.astype(BF16)
    lo = (h2 - hi.astype(F32)).astype(BF16)
    r1 = _dot(hi, wr_ref[...])
    r2 = _dot(lo, wr_ref[:, :LANES])
    logits_all = r1[:, :LANES] + r1[:, LANES:] + r2 + br_ref[...]

    for r in range(TMX // TM):
        mf, slots, c8_rows, loff8_rows = _route_tile(logits_all[r * TM:(r + 1) * TM])
        mf_ref[r * TM:(r + 1) * TM, :] = mf
        slot_col_ref[r * TM:(r + 1) * TM, :] = slots.astype(jnp.int32)
        slot_row_ref[8 * r:8 * (r + 1), :] = slots.T[:8].astype(jnp.int32)
        c8_ref[8 * r:8 * (r + 1), :] = c8_rows.astype(jnp.int32)
        loff8_ref[8 * r:8 * (r + 1), :] = loff8_rows.astype(jnp.int32)


def _route_tile(logits):
    lane_i = lax.broadcasted_iota(jnp.int32, (TM, LANES), 1)
    lane = lane_i.astype(F32)
    lane_group = (lane_i >> 3).astype(F32)
    ninf = -jnp.inf
    is_group = (lane_i >= N_EXPERTS) & (lane_i < N_EXPERTS + N_GROUPS)
    lg = jnp.where(is_group, logits, ninf)
    gmax = jnp.max(lg, axis=-1, keepdims=True)
    gsum = jnp.sum(jnp.exp(lg - gmax), axis=-1, keepdims=True)
    p_top = 1.0 / gsum
    g_idx = jnp.min(jnp.where(lg == gmax, lane - N_EXPERTS, float(N_GROUPS)), axis=-1, keepdims=True)
    le = jnp.where((lane_i < N_EXPERTS) & (lane_group == g_idx), logits, ninf)
    t1 = jnp.max(le, axis=-1, keepdims=True)
    e1 = jnp.min(jnp.where(le == t1, lane, float(LANES)), axis=-1, keepdims=True)
    le2 = jnp.where(lane == e1, ninf, le)
    t2 = jnp.max(le2, axis=-1, keepdims=True)
    e2 = jnp.min(jnp.where(le2 == t2, lane, float(LANES)), axis=-1, keepdims=True)
    ex = jnp.exp(t2 - t1)
    w1 = p_top / (1.0 + ex)
    w2 = p_top * ex / (1.0 + ex)

    sel1 = lane == e1
    sel2 = lane == e2
    onehot = jnp.where(sel1 | sel2, 1.0, 0.0)
    rr = lax.broadcasted_iota(jnp.int32, (TM, TM), 0)
    cc = lax.broadcasted_iota(jnp.int32, (TM, TM), 1)
    ltri = jnp.where(cc < rr, 1.0, 0.0).astype(BF16)
    rank = _dot(ltri, onehot.astype(BF16))
    cnt = jnp.sum(onehot, axis=0, keepdims=True)
    c8 = jnp.floor((cnt + (CHUNK_ROWS - 1)) * (1.0 / CHUNK_ROWS))
    ur = lax.broadcasted_iota(jnp.int32, (LANES, LANES), 0)
    uc = lax.broadcasted_iota(jnp.int32, (LANES, LANES), 1)
    upper = jnp.where(ur < uc, 1.0, 0.0).astype(BF16)
    c8_rows = jnp.broadcast_to(c8, (8, LANES))
    loff8_rows = _dot(c8_rows.astype(BF16), upper)
    slot_all = CHUNK_ROWS * loff8_rows[0:1] + rank
    slot1 = jnp.sum(jnp.where(sel1, slot_all, 0.0), axis=-1, keepdims=True)
    slot2 = jnp.sum(jnp.where(sel2, slot_all, 0.0), axis=-1, keepdims=True)

    slots = jnp.where(lane_i == 0, slot1, jnp.where(lane_i == 1, slot2, 0.0))
    return jnp.where(lane_i == 0, w1, w2), slots, c8_rows, loff8_rows


def _mix(attn, ga, p, x, woa, wout, gffn, wr, br):
    nt = N_TOK // TM
    sub = TMX // TM

    def const(shape):
        return pl.BlockSpec(shape, lambda i: (0,) * len(shape))

    def rows(width):
        return pl.BlockSpec((TMX, width), lambda i: (i, 0))

    return pl.pallas_call(
        _mix_kernel,
        grid=(N_TOK // TMX,),
        in_specs=[rows(MLA_WIDTH), rows(D_MODEL), rows(D_MODEL), rows(D_MODEL),
                  const((MLA_WIDTH, D_MODEL)), const((D_MODEL, D_MODEL)), const((1, D_MODEL)),
                  const((D_MODEL, 2 * LANES)), const((1, LANES))],
        out_specs=[rows(D_MODEL), rows(D_MODEL), rows(LANES), rows(LANES),
                   pl.BlockSpec((8 * sub, TM), lambda i: (i, 0)),
                   pl.BlockSpec((8 * sub, LANES), lambda i: (i, 0)),
                   pl.BlockSpec((8 * sub, LANES), lambda i: (i, 0))],
        out_shape=[jax.ShapeDtypeStruct((N_TOK, D_MODEL), F32),
                   jax.ShapeDtypeStruct((N_TOK, D_MODEL), BF16),
                   jax.ShapeDtypeStruct((N_TOK, LANES), F32),
                   jax.ShapeDtypeStruct((N_TOK, LANES), jnp.int32),
                   jax.ShapeDtypeStruct((nt * 8, TM), jnp.int32),
                   jax.ShapeDtypeStruct((nt * 8, LANES), jnp.int32),
                   jax.ShapeDtypeStruct((nt * 8, LANES), jnp.int32)],
        compiler_params=pltpu.CompilerParams(dimension_semantics=("arbitrary",),
                                             vmem_limit_bytes=40 * 1024 * 1024),
        name="mix",
    )(attn, ga, p, x, woa, wout, gffn, wr, br)


def _chunk_copy(src, dst, s8, d8, sem):
    return pltpu.make_async_copy(src.at[pl.ds(pl.multiple_of(s8 * CHUNK_ROWS, CHUNK_ROWS), CHUNK_ROWS)],
                                 dst.at[pl.ds(pl.multiple_of(d8 * CHUNK_ROWS, CHUNK_ROWS), CHUNK_ROWS)], sem)


def _for_each_chunk(t, ntile_ref, gchunk_ref, fn):
    n = ntile_ref[t]
    base = t * TILE_CHUNKS
    groups = n >> CHUNK_UNROLL_SHIFT

    def group(g, carry):
        for u in range(CHUNK_UNROLL):
            k = g * CHUNK_UNROLL + u
            fn(k, gchunk_ref[base + k])
        return carry

    lax.fori_loop(0, groups, group, 0)

    def single(k, carry):
        fn(k, gchunk_ref[base + k])
        return carry

    lax.fori_loop(groups << CHUNK_UNROLL_SHIFT, n, single, 0)


def _wait_chunks(n, src, dst, sem):
    groups = n >> CHUNK_UNROLL_SHIFT
    rows = CHUNK_UNROLL * CHUNK_ROWS

    def group(g, carry):
        pltpu.make_async_copy(src.at[pl.ds(0, rows)], dst.at[pl.ds(0, rows)], sem).wait()
        return carry

    lax.fori_loop(0, groups, group, 0)

    def single(k, carry):
        _chunk_copy(src, dst, 0, 0, sem).wait()
        return carry

    lax.fori_loop(groups << CHUNK_UNROLL_SHIFT, n, single, 0)


def _dispatch_kernel(ntile_ref, gchunk_ref, zstart8_ref, zcnt8_ref, nvb_ref,
                     slot_ref, h2_ref, xs_hbm, sbuf, zbuf, sem, zsem):
    t = pl.program_id(0)
    last = pl.num_programs(0) - 1
    cur = t % 2

    def zero_chunk_copy(d8):
        return _chunk_copy(zbuf, xs_hbm, 0, d8, zsem)

    def zero_block_copy(b):
        return pltpu.make_async_copy(
            zbuf, xs_hbm.at[pl.ds(pl.multiple_of(b * MOE_T, MOE_T), MOE_T)], zsem)

    def for_each_zero(chunk_fn, block_fn):
        def per_expert(e, carry):
            def per_chunk(j, carry2):
                chunk_fn(zstart8_ref[e] + j)
                return carry2

            lax.fori_loop(0, zcnt8_ref[e], per_chunk, 0)
            return carry

        lax.fori_loop(0, N_EXPERTS, per_expert, 0)

        def per_block(b, carry):
            block_fn(b)
            return carry

        lax.fori_loop(nvb_ref[0], MOE_BLOCKS, per_block, 0)

    @pl.when(t == 0)
    def _():
        zbuf[...] = jnp.zeros(zbuf.shape, F32)
        for_each_zero(lambda d8: zero_chunk_copy(d8).start(), lambda b: zero_block_copy(b).start())

    row = lax.broadcasted_iota(jnp.int32, (TILE_ROWS, TM), 0)
    slots = slot_ref[...]
    perm = jnp.where((row == slots[0:1, :]) | (row == slots[1:2, :]), 1.0, 0.0).astype(BF16)
    sbuf[cur] = _dot(perm, h2_ref[...])
    _for_each_chunk(t, ntile_ref, gchunk_ref,
                    lambda lc, gc: _chunk_copy(sbuf.at[cur], xs_hbm, lc, gc, sem.at[cur]).start())

    def wait_tile(tile, slot):
        _wait_chunks(ntile_ref[tile], sbuf.at[slot], xs_hbm, sem.at[slot])

    @pl.when(t > 0)
    def _():
        wait_tile(t - 1, 1 - cur)

    @pl.when(t == last)
    def _():
        wait_tile(t, cur)
        for_each_zero(lambda d8: zero_chunk_copy(d8).wait(), lambda b: zero_block_copy(b).wait())


def _dispatch(tables, slot_rows, h2):
    return pl.pallas_call(
        _dispatch_kernel,
        grid_spec=pltpu.PrefetchScalarGridSpec(
            num_scalar_prefetch=len(tables),
            grid=(N_TOK // TM,),
            in_specs=[pl.BlockSpec((8, TM), lambda i, *_: (i, 0)),
                      pl.BlockSpec((TM, D_MODEL), lambda i, *_: (i, 0))],
            out_specs=pl.BlockSpec(memory_space=pl.ANY),
            scratch_shapes=[pltpu.VMEM((2, TILE_ROWS, D_MODEL), F32), pltpu.VMEM((MOE_T, D_MODEL), F32),
                            pltpu.SemaphoreType.DMA((2,)), pltpu.SemaphoreType.DMA(())]),
        out_shape=jax.ShapeDtypeStruct((MOE_ROWS, D_MODEL), F32),
        compiler_params=pltpu.CompilerParams(dimension_semantics=("arbitrary",)),
        name="dispatch",
    )(*tables, slot_rows, h2)


def _expert_kernel(blk_e_ref, nvb_ref, xs_ref, wg_ref, wu_ref, wd_ref, ys_ref, wgb, wub, wdb):
    i = pl.program_id(0)
    live = i < nvb_ref[0]

    @pl.when(live & ((i == 0) | (blk_e_ref[i] != blk_e_ref[jnp.maximum(i - 1, 0)])))
    def _():
        wgb[...] = wg_ref[...].astype(BF16)
        wub[...] = wu_ref[...].astype(BF16)
        wdb[...] = wd_ref[...].astype(BF16)

    @pl.when(live)
    def _():
        xb = xs_ref[...].astype(BF16)
        g = _dot(xb, wgb[...])
        u = _dot(xb, wub[...])
        hid = (jax.nn.silu(g) * u).astype(BF16)
        ys_ref[...] = _dot(hid, wdb[...])

    @pl.when(jnp.logical_not(live))
    def _():
        ys_ref[...] = jnp.zeros(ys_ref.shape, ys_ref.dtype)


def _experts(blk_e, nvb, xs, wg, wu, wd):
    def row_block(i, be, nv):
        return (jnp.minimum(i, nv[0] - 1), 0)

    def weight(i, be, nv):
        return (be[jnp.minimum(i, nv[0] - 1)], 0, 0)

    return pl.pallas_call(
        _expert_kernel,
        grid_spec=pltpu.PrefetchScalarGridSpec(
            num_scalar_prefetch=2,
            grid=(MOE_BLOCKS,),
            in_specs=[pl.BlockSpec((MOE_T, D_MODEL), row_block),
                      pl.BlockSpec((None, D_MODEL, D_EXPERT), weight),
                      pl.BlockSpec((None, D_MODEL, D_EXPERT), weight),
                      pl.BlockSpec((None, D_EXPERT, D_MODEL), weight)],
            out_specs=pl.BlockSpec((MOE_T, D_MODEL), lambda i, be, nv: (i, 0)),
            scratch_shapes=[pltpu.VMEM((D_MODEL, D_EXPERT), BF16), pltpu.VMEM((D_MODEL, D_EXPERT), BF16),
                            pltpu.VMEM((D_EXPERT, D_MODEL), BF16)]),
        out_shape=jax.ShapeDtypeStruct((MOE_ROWS, D_MODEL), F32),
        compiler_params=pltpu.CompilerParams(dimension_semantics=("arbitrary",),
                                             vmem_limit_bytes=40 * 1024 * 1024),
        name="experts",
    )(blk_e, nvb, xs, wg, wu, wd)


def _combine_kernel(ntile_ref, gchunk_ref,
                    slot_ref, x1_ref, mf_ref, gfin_ref, ys_hbm, o_ref, ybuf, sem):
    t = pl.program_id(0)
    cur = t % 2

    def fetch(tile, slot):
        _for_each_chunk(tile, ntile_ref, gchunk_ref,
                        lambda lc, gc: _chunk_copy(ys_hbm, ybuf.at[slot], gc, lc, sem.at[slot]).start())

    @pl.when(t == 0)
    def _():
        ybuf[...] = jnp.zeros(ybuf.shape, F32)
        fetch(0, 0)

    @pl.when(t + 1 < pl.num_programs(0))
    def _():
        fetch(t + 1, 1 - cur)

    _wait_chunks(ntile_ref[t], ys_hbm, ybuf.at[cur], sem.at[cur])

    yb = ybuf[cur].astype(BF16)
    col = lax.broadcasted_iota(jnp.int32, (TM, TILE_ROWS), 1)
    slots = slot_ref[...]
    y1 = _dot(jnp.where(col == slots[:, 0:1], 1.0, 0.0).astype(BF16), yb)
    y2 = _dot(jnp.where(col == slots[:, 1:2], 1.0, 0.0).astype(BF16), yb)
    mf = mf_ref[...]
    x2 = x1_ref[...] + mf[:, 0:1] * y1 + mf[:, 1:2] * y2
    o_ref[...] = _rms(x2, gfin_ref[...])


def _combine(tables, slot_cols, x1, mf, gfin, ys):
    return pl.pallas_call(
        _combine_kernel,
        grid_spec=pltpu.PrefetchScalarGridSpec(
            num_scalar_prefetch=len(tables),
            grid=(N_TOK // TM,),
            in_specs=[pl.BlockSpec((TM, LANES), lambda i, *_: (i, 0)),
                      pl.BlockSpec((TM, D_MODEL), lambda i, *_: (i, 0)),
                      pl.BlockSpec((TM, LANES), lambda i, *_: (i, 0)),
                      pl.BlockSpec((1, D_MODEL), lambda i, *_: (0, 0)),
                      pl.BlockSpec(memory_space=pl.ANY)],
            out_specs=pl.BlockSpec((TM, D_MODEL), lambda i, *_: (i, 0)),
            scratch_shapes=[pltpu.VMEM((2, TILE_ROWS, D_MODEL), F32), pltpu.SemaphoreType.DMA((2,))]),
        out_shape=jax.ShapeDtypeStruct((N_TOK, D_MODEL), F32),
        compiler_params=pltpu.CompilerParams(dimension_semantics=("arbitrary",)),
        name="combine",
    )(*tables, slot_cols, x1, mf, gfin, ys)


def kernel(x, positions, g_attn_norm, w_in, g_cq, w_uq, g_ckv, w_uk, w_uv, w_o_attn, sgu_gain, sgu_bias, w_spatial, b_spatial, w_o_sgu, w_out, g_ffn_norm, w_router_group, b_router_group, w_router_expert, b_router_expert, w_exp_gate, w_exp_up, w_exp_down, g_final):
    assert x.shape == (BATCH, SEQ, D_MODEL) and w_in.shape[0] == 1
    half = QK_ROPE // 2
    swap = jnp.concatenate([jnp.arange(half, QK_ROPE), jnp.arange(0, half)])

    def pad_cols(w, width):
        return jnp.pad(w, ((0, 0), (0, width - w.shape[1])))

    wi = w_in[0]
    c0 = Q_RANK + KV_RANK
    kr = wi[:, c0:c0 + QK_ROPE]
    c1 = c0 + QK_ROPE
    win = jnp.concatenate([
        wi[:, :c0], pad_cols(kr, LANES), pad_cols(kr[:, swap], LANES), wi[:, c1:]], axis=1).astype(BF16)

    wq = w_uq[0].reshape(Q_RANK, MLA_HEADS, QK_NOPE + QK_ROPE)
    wq_nope = wq[:, :, :QK_NOPE].transpose(1, 0, 2)
    wq_rope = wq[:, :, QK_NOPE:]
    w_ukt = w_uk[0].reshape(KV_RANK, MLA_HEADS, QK_NOPE).transpose(1, 2, 0)
    wqlat = _fold(wq_nope, w_ukt).transpose(1, 0, 2).reshape(Q_RANK, MLA_HEADS * LANES).astype(BF16)

    def rope_cols(w):
        return jnp.pad(w, ((0, 0), (0, 0), (0, LANES - QK_ROPE))).reshape(Q_RANK, MLA_HEADS * LANES).astype(BF16)

    wqr = rope_cols(wq_rope)
    wqrs = rope_cols(wq_rope[:, :, swap])

    per_row = LANES // half
    freqs = ROPE_THETA ** (-jnp.arange(0, QK_ROPE, 2, dtype=F32) / QK_ROPE)
    pos8 = jnp.repeat(positions.astype(F32).reshape(N_TOK // per_row, per_row), half, axis=1)
    cos8, sin8 = _rope_tables(pos8, jnp.tile(freqs, per_row)[None, :])
    cos16 = cos8.reshape(N_TOK, half)
    sin16 = sin8.reshape(N_TOK, half)
    cos_t = pad_cols(jnp.concatenate([cos16, cos16], axis=1), LANES)
    sin_t = pad_cols(jnp.concatenate([-sin16, sin16], axis=1), LANES)

    head_of_col = jnp.arange(MLA_WIDTH) // V_HEAD
    wuv = jnp.where(head_of_col[None, None, :] == jnp.arange(MLA_HEADS)[:, None, None],
                    w_uv[0][None], 0.0).astype(BF16)
    wuv = wuv.reshape(MLA_HEADS * KV_RANK, MLA_WIDTH)

    wsp = w_spatial[0].astype(BF16)
    bs = b_spatial[0]
    bsp = jnp.repeat(bs.reshape(SGU_GROUPS // 2, 2, SGU_LEN).transpose(0, 2, 1), SGU_GROUP_DIM, axis=2)

    wr32 = jnp.concatenate([w_router_expert[0].transpose(1, 0, 2).reshape(D_MODEL, N_EXPERTS),
                            w_router_group[0]], axis=1)
    wr32 = pad_cols(wr32, LANES)
    wr_hi = wr32.astype(BF16)
    wr_lo = (wr32 - wr_hi.astype(F32)).astype(BF16)
    wr = jnp.concatenate([wr_hi, wr_lo], axis=1)
    br = pad_cols(jnp.concatenate([b_router_expert[0].reshape(-1), b_router_group[0]])[None, :], LANES)

    xf = x.reshape(N_TOK, D_MODEL)
    q, kt, v, p, ga = _inproj(
        xf, cos_t, sin_t, g_attn_norm, win, g_cq, g_ckv, wqlat, wqr, wqrs,
        sgu_gain, sgu_bias, wsp, bsp, w_o_sgu[0].astype(BF16))
    attn = _attention(q, kt, v, wuv)
    x1, h2, mf, slot_cols, slot_rows, c8_rows, loff8_rows = _mix(
        attn.reshape(N_TOK, MLA_WIDTH), ga, p, xf, w_o_attn[0].astype(BF16), w_out[0].astype(BF16),
        g_ffn_norm, wr, br)

    blk8 = MOE_T // CHUNK_ROWS
    c8 = c8_rows[::8, :N_EXPERTS]
    loff8 = loff8_rows[::8, :N_EXPERTS]
    tot8 = jnp.sum(c8, axis=0)
    pad8 = (tot8 + blk8 - 1) // blk8 * blk8
    gend8 = jnp.cumsum(pad8)
    gstart8 = gend8 - pad8
    gbase8 = gstart8[None, :] + jnp.cumsum(c8, axis=0) - c8
    nvb = (gend8[-1:] // blk8).astype(jnp.int32)
    blk_start8 = jnp.arange(MOE_BLOCKS, dtype=jnp.int32) * blk8
    blk_e = jnp.minimum(jnp.sum((gend8[None, :] <= blk_start8[:, None]).astype(jnp.int32), axis=1),
                        N_EXPERTS - 1)
    ntile = jnp.sum(c8, axis=1)
    k = jnp.arange(TILE_CHUNKS, dtype=jnp.int32)
    run_end = loff8 + c8
    e_of_chunk = jnp.minimum(
        jnp.sum((run_end[:, None, :] <= k[None, :, None]).astype(jnp.int32), axis=-1), N_EXPERTS - 1)
    gchunk = jnp.take_along_axis(gbase8 - loff8, e_of_chunk, axis=1) + k[None, :]
    run_tables = (ntile, gchunk.reshape(-1).astype(jnp.int32))

    xs = _dispatch(run_tables + ((gstart8 + tot8).astype(jnp.int32), (pad8 - tot8).astype(jnp.int32), nvb),
                   slot_rows, h2)
    ys = _experts(blk_e, nvb, xs, w_exp_gate[0], w_exp_up[0], w_exp_down[0])
    out = _combine(run_tables, slot_cols, x1, mf, g_final.reshape(1, D_MODEL), ys)
    return out.reshape(BATCH, SEQ, D_MODEL)
```

```python
import functools

import jax
import jax.numpy as jnp
from jax import lax
from jax.experimental import pallas as pl
from jax.experimental.pallas import tpu as pltpu

D_MODEL = 1024
BATCH = 2
SEQ = 8192
N_TOK = BATCH * SEQ
CHUNK = 64
EPS = 1e-6
MLA_HEADS = 8
Q_RANK = 256
KV_RANK = 128
QK_NOPE = 64
QK_ROPE = 32
V_HEAD = 64
MLA_WIDTH = MLA_HEADS * V_HEAD
ROPE_THETA = 10000.0
SGU_GROUPS = 8
SGU_GROUP_DIM = 64
SGU_WIDTH = SGU_GROUPS * SGU_GROUP_DIM
SGU_LEN = 128
N_GROUPS = 4
EXPERTS_PER_GROUP = 8
N_EXPERTS = N_GROUPS * EXPERTS_PER_GROUP
TOP_K = 2
D_EXPERT = 256

LANES = 128
QK_PAD = 2 * LANES
V_PAD = 2 * LANES
SCALE = (QK_NOPE + QK_ROPE) ** -0.5
LOG2E = 1.4426950408889634
Q_SCALE = SCALE * LOG2E
NEG = -1e30

TM = 256
TMI = 512
TMX = 512
TQ = 128
TK = 512
ATTN_UNROLL_SHIFT = 1
ATTN_UNROLL = 1 << ATTN_UNROLL_SHIFT
TK_SHIFT = TK.bit_length() - 1
CHUNK_SHIFT = CHUNK.bit_length() - 1
assert 1 << TK_SHIFT == TK and 1 << CHUNK_SHIFT == CHUNK
MOE_T = 512
CHUNK_ROWS = 8
TILE_ROWS = -(-(TOP_K * TM + N_EXPERTS * (CHUNK_ROWS - 1) + CHUNK_ROWS) // 256) * 256
TILE_CHUNKS = TILE_ROWS // CHUNK_ROWS
CHUNK_UNROLL_SHIFT = 2
CHUNK_UNROLL = 1 << CHUNK_UNROLL_SHIFT
MOE_ROWS_MAX = (N_TOK * TOP_K + (N_TOK // TM) * N_EXPERTS * (CHUNK_ROWS - 1)
                + N_EXPERTS * (MOE_T - CHUNK_ROWS))
MOE_BLOCKS = -(-MOE_ROWS_MAX // MOE_T)
MOE_ROWS = MOE_BLOCKS * MOE_T

C_Q = 0
C_KV = C_Q + Q_RANK
C_KR = C_KV + KV_RANK
C_KRS = C_KR + LANES
C_U = C_KRS + LANES
C_V = C_U + SGU_WIDTH
C_GA = C_V + SGU_WIDTH
C_GB = C_GA + D_MODEL
C_END = C_GB + D_MODEL

F32 = jnp.float32
BF16 = jnp.bfloat16


def _dot(a, b):
    return jnp.dot(a, b, preferred_element_type=F32)


def _rms(x, g):
    return x * lax.rsqrt(jnp.mean(x * x, axis=-1, keepdims=True) + EPS) * g


def _fold_kernel(a_ref, b_ref, o_ref):
    o_ref[...] = Q_SCALE * jnp.dot(a_ref[...], b_ref[...], preferred_element_type=F32,
                                 precision=lax.Precision.HIGHEST)


def _fold(w_uq_nope, w_ukt):
    return pl.pallas_call(
        _fold_kernel,
        grid=(MLA_HEADS,),
        in_specs=[pl.BlockSpec((None, Q_RANK, QK_NOPE), lambda h: (h, 0, 0)),
                  pl.BlockSpec((None, QK_NOPE, KV_RANK), lambda h: (h, 0, 0))],
        out_specs=pl.BlockSpec((None, Q_RANK, KV_RANK), lambda h: (h, 0, 0)),
        out_shape=jax.ShapeDtypeStruct((MLA_HEADS, Q_RANK, KV_RANK), F32),
        name="fold",
    )(w_uq_nope, w_ukt)


def _rope_kernel(pos_ref, freq_ref, cos_ref, sin_ref):
    ang = pos_ref[...] * freq_ref[...]
    cos_ref[...] = jnp.cos(ang)
    sin_ref[...] = jnp.sin(ang)


def _rope_tables(pos8, freq8):
    rows = pos8.shape[0]
    blk = pl.BlockSpec((TM, LANES), lambda i: (i, 0))
    return pl.pallas_call(
        _rope_kernel,
        grid=(rows // TM,),
        in_specs=[blk, pl.BlockSpec((1, LANES), lambda i: (0, 0))],
        out_specs=[blk, blk],
        out_shape=[jax.ShapeDtypeStruct((rows, LANES), F32)] * 2,
        name="rope_tables",
    )(pos8, freq8)


def _inproj_kernel(x_ref, cos_ref, sin_ref, gattn_ref, win_ref, gcq_ref, gckv_ref, wqlat_ref, wqr_ref,
                   wqrs_ref, sgain_ref, sbias_ref, wsp_ref, bsp_ref, wosgu_ref,
                   q_ref, kt_ref, v_ref, p_ref, ga_ref):
    hb = _rms(x_ref[...], gattn_ref[...]).astype(BF16)

    def proj(a, b):
        return _dot(hb, win_ref[:, a:b])

    cos_t = cos_ref[...]
    sin_t = sin_ref[...]

    cqn = _rms(proj(C_Q, C_KV), gcq_ref[...]).astype(BF16)
    qlat = _dot(cqn, wqlat_ref[...])
    qr = _dot(cqn, wqr_ref[...])
    qs = _dot(cqn, wqrs_ref[...])
    cos_q = cos_t * Q_SCALE
    sin_q = sin_t * Q_SCALE
    for r in range(TMI // TQ):
        t0, t1 = r * TQ, (r + 1) * TQ
        for h in range(MLA_HEADS):
            a, b = h * LANES, (h + 1) * LANES
            q_ref[r, h * TQ:(h + 1) * TQ, :LANES] = qlat[t0:t1, a:b].astype(BF16)
            q_ref[r, h * TQ:(h + 1) * TQ, LANES:] = (
                qr[t0:t1, a:b] * cos_q[t0:t1] + qs[t0:t1, a:b] * sin_q[t0:t1]).astype(BF16)

    zk = proj(C_KV, C_U)
    ckvn = _rms(zk[:, :KV_RANK], gckv_ref[...])
    krope = zk[:, KV_RANK:KV_RANK + LANES] * cos_t + zk[:, KV_RANK + LANES:] * sin_t
    kt_ref[...] = jnp.concatenate([ckvn, krope], axis=1).T.astype(BF16)
    v_ref[...] = jnp.concatenate([ckvn, jnp.ones_like(ckvn)], axis=1).astype(BF16)

    u = jax.nn.gelu(proj(C_U, C_V))
    v = jax.nn.gelu(proj(C_V, C_GA))
    mu = jnp.mean(v, axis=-1, keepdims=True)
    vc = v - mu
    var = jnp.mean(vc * vc, axis=-1, keepdims=True)
    vb = (vc * lax.rsqrt(var + EPS) * sgain_ref[...] + sbias_ref[...]).astype(BF16)
    row = lax.broadcasted_iota(jnp.int32, (SGU_LEN, SGU_LEN), 0)
    col = lax.broadcasted_iota(jnp.int32, (SGU_LEN, SGU_LEN), 1)
    causal = (row >> CHUNK_SHIFT) >= (col >> CHUNK_SHIFT)
    low_half = col < SGU_GROUP_DIM
    row_blocks = []
    for r in range(TMI // SGU_LEN):
        pieces = []
        for p in range(SGU_WIDTH // LANES):
            blk = vb[r * SGU_LEN:(r + 1) * SGU_LEN, p * LANES:(p + 1) * LANES]
            w0 = jnp.where(causal, wsp_ref[2 * p], jnp.zeros_like(wsp_ref[2 * p]))
            w1 = jnp.where(causal, wsp_ref[2 * p + 1], jnp.zeros_like(wsp_ref[2 * p + 1]))
            sv = jnp.where(low_half, _dot(w0, blk), _dot(w1, blk)) + bsp_ref[p]
            pieces.append((u[r * SGU_LEN:(r + 1) * SGU_LEN, p * LANES:(p + 1) * LANES] * sv).astype(BF16))
        row_blocks.append(jnp.concatenate(pieces, axis=1))
    sgu = _dot(jnp.concatenate(row_blocks, axis=0), wosgu_ref[...])

    ga_ref[...] = jax.nn.sigmoid(proj(C_GA, C_GB)).astype(BF16)
    p_ref[...] = (jax.nn.sigmoid(proj(C_GB, C_END)) * sgu).astype(BF16)


def _inproj(x, cos_t, sin_t, gattn, win, gcq, gckv, wqlat, wqr, wqrs, sgain, sbias, wsp, bsp, wosgu):
    nt = N_TOK // TMI
    per_b = SEQ // TMI
    per_k = TK // TMI

    def const(shape):
        return pl.BlockSpec(shape, lambda i: (0,) * len(shape))

    return pl.pallas_call(
        _inproj_kernel,
        grid=(nt,),
        in_specs=[pl.BlockSpec((TMI, D_MODEL), lambda i: (i, 0)),
                  pl.BlockSpec((TMI, LANES), lambda i: (i, 0)), pl.BlockSpec((TMI, LANES), lambda i: (i, 0)),
                  const((1, D_MODEL)), const((D_MODEL, C_END)), const((1, Q_RANK)), const((1, KV_RANK)),
                  const((Q_RANK, MLA_HEADS * LANES)), const((Q_RANK, MLA_HEADS * LANES)),
                  const((Q_RANK, MLA_HEADS * LANES)),
                  const((1, SGU_WIDTH)), const((1, SGU_WIDTH)),
                  const((SGU_GROUPS, SGU_LEN, SGU_LEN)), const((SGU_GROUPS // 2, SGU_LEN, LANES)),
                  const((SGU_WIDTH, D_MODEL))],
        out_specs=[pl.BlockSpec((TMI // TQ, MLA_HEADS * TQ, QK_PAD), lambda i: (i, 0, 0)),
                   pl.BlockSpec((None, None, QK_PAD, TMI),
                                lambda i: (i // per_b, (i % per_b) // per_k, 0, i % per_k)),
                   pl.BlockSpec((None, None, TMI, V_PAD),
                                lambda i: (i // per_b, (i % per_b) // per_k, i % per_k, 0)),
                   pl.BlockSpec((TMI, D_MODEL), lambda i: (i, 0)),
                   pl.BlockSpec((TMI, D_MODEL), lambda i: (i, 0))],
        out_shape=[jax.ShapeDtypeStruct((N_TOK // TQ, MLA_HEADS * TQ, QK_PAD), BF16),
                   jax.ShapeDtypeStruct((BATCH, SEQ // TK, QK_PAD, TK), BF16),
                   jax.ShapeDtypeStruct((BATCH, SEQ // TK, TK, V_PAD), BF16),
                   jax.ShapeDtypeStruct((N_TOK, D_MODEL), BF16),
                   jax.ShapeDtypeStruct((N_TOK, D_MODEL), BF16)],
        compiler_params=pltpu.CompilerParams(dimension_semantics=("arbitrary",),
                                             vmem_limit_bytes=56 * 1024 * 1024),
        name="inproj",
    )(x, cos_t, sin_t, gattn, win, gcq, gckv, wqlat, wqr, wqrs, sgain, sbias, wsp, bsp, wosgu)


def _attn_kernel(q_ref, kt_ref, v_ref, wuv_ref, o_ref, m_ref, acc_ref, s_ref, p_ref, a_ref):
    qi = pl.program_id(1)
    m_ref[...] = jnp.full(m_ref.shape, NEG, F32)
    acc_ref[...] = jnp.zeros(acc_ref.shape, F32)

    def scores(j, slot):
        s_ref[slot] = _dot(q_ref[...], kt_ref[j])

    def update(j, slot, masked):
        if masked:
            q_chunk = ((qi * TQ - j * TK) >> CHUNK_SHIFT) + (
                lax.broadcasted_iota(jnp.int32, (TQ, TK), 0) >> CHUNK_SHIFT)
            k_chunk = lax.broadcasted_iota(jnp.int32, (TQ, TK), 1) >> CHUNK_SHIFT
            visible = k_chunk <= q_chunk
        for h in range(MLA_HEADS):
            rows = slice(h * TQ, (h + 1) * TQ)
            s = s_ref[slot, rows, :]
            if masked:
                s = jnp.where(visible, s, NEG)
            m_prev = m_ref[rows, :]
            m_new = jnp.maximum(m_prev, jnp.max(s, axis=-1, keepdims=True))
            m_ref[rows, :] = m_new
            alpha = jnp.exp2(m_prev - m_new)
            p_ref[rows, :] = jnp.exp2(s - jnp.concatenate([m_new] * (TK // LANES), axis=1)).astype(BF16)
            a_ref[rows, :] = alpha
        alpha = a_ref[...]
        acc_ref[...] = jnp.concatenate([alpha, alpha], axis=1) * acc_ref[...] + _dot(p_ref[...], v_ref[j])

    diag = (qi * TQ) >> TK_SHIFT
    scores(0, 0)

    def run(first, count, last_masked):
        for u in range(count):
            last = u == count - 1
            if not (last and last_masked):
                scores(first + u + 1, (u + 1) % 2)
            update(first + u, u % 2, last and last_masked)

    def body(t, carry):
        run(ATTN_UNROLL * t, ATTN_UNROLL, False)
        return carry

    trips = diag >> ATTN_UNROLL_SHIFT
    lax.fori_loop(0, trips, body, 0)
    done = trips << ATTN_UNROLL_SHIFT
    for r in range(ATTN_UNROLL):
        @pl.when(diag - done == r)
        def _(r=r):
            run(done, r + 1, True)

    o_lat = (acc_ref[:, :KV_RANK] / acc_ref[:, KV_RANK:]).astype(BF16)
    o_cat = jnp.concatenate([o_lat[h * TQ:(h + 1) * TQ] for h in range(MLA_HEADS)], axis=1)
    o_ref[...] = _dot(o_cat, wuv_ref[...]).astype(BF16)


def _attention(q, kt, v, wuv):
    nk = SEQ // TK
    return pl.pallas_call(
        _attn_kernel,
        grid=(BATCH, SEQ // TQ),
        in_specs=[pl.BlockSpec((None, MLA_HEADS * TQ, QK_PAD), lambda b, i: (b * (SEQ // TQ) + i, 0, 0)),
                  pl.BlockSpec((None, nk, QK_PAD, TK), lambda b, i: (b, 0, 0, 0)),
                  pl.BlockSpec((None, nk, TK, V_PAD), lambda b, i: (b, 0, 0, 0)),
                  pl.BlockSpec((MLA_HEADS * KV_RANK, MLA_WIDTH), lambda b, i: (0, 0))],
        out_specs=pl.BlockSpec((None, TQ, MLA_WIDTH), lambda b, i: (b, i, 0)),
        out_shape=jax.ShapeDtypeStruct((BATCH, SEQ, MLA_WIDTH), BF16),
        scratch_shapes=[pltpu.VMEM((MLA_HEADS * TQ, LANES), F32),
                        pltpu.VMEM((MLA_HEADS * TQ, V_PAD), F32),
                        pltpu.VMEM((2, MLA_HEADS * TQ, TK), F32),
                        pltpu.VMEM((MLA_HEADS * TQ, TK), BF16),
                        pltpu.VMEM((MLA_HEADS * TQ, LANES), F32)],
        compiler_params=pltpu.CompilerParams(dimension_semantics=("arbitrary", "arbitrary"),
                                             vmem_limit_bytes=40 * 1024 * 1024),
        name="attention",
    )(q, kt, v, wuv)


def _mix_kernel(attn_ref, ga_ref, p_ref, x_ref, woa_ref, wout_ref, gffn_ref, wr_ref, br_ref,
                x1_ref, h2_ref, mf_ref, slot_col_ref, slot_row_ref, c8_ref, loff8_ref):
    a = _dot(attn_ref[...], woa_ref[...])
    mix = (ga_ref[...].astype(F32) * a + p_ref[...].astype(F32)).astype(BF16)
    x1 = x_ref[...] + _dot(mix, wout_ref[...])
    x1_ref[...] = x1
    h2 = _rms(x1, gffn_ref[...])
    h2_ref[...] = h2.astype(BF16)

    hi = h2.astype(BF16)
    lo = (h2 - hi.astype(F32)).astype(BF16)
    r1 = _dot(hi, wr_ref[...])
    r2 = _dot(lo, wr_ref[:, :LANES])
    logits_all = r1[:, :LANES] + r1[:, LANES:] + r2 + br_ref[...]

    for r in range(TMX // TM):
        mf, slots, c8_rows, loff8_rows = _route_tile(logits_all[r * TM:(r + 1) * TM])
        mf_ref[r * TM:(r + 1) * TM, :] = mf
        slot_col_ref[r * TM:(r + 1) * TM, :] = slots.astype(jnp.int32)
        slot_row_ref[8 * r:8 * (r + 1), :] = slots.T[:8].astype(jnp.int32)
        c8_ref[8 * r:8 * (r + 1), :] = c8_rows.astype(jnp.int32)
        loff8_ref[8 * r:8 * (r + 1), :] = loff8_rows.astype(jnp.int32)


def _route_tile(logits):
    lane_i = lax.broadcasted_iota(jnp.int32, (TM, LANES), 1)
    lane = lane_i.astype(F32)
    lane_group = (lane_i >> 3).astype(F32)
    ninf = -jnp.inf
    is_group = (lane_i >= N_EXPERTS) & (lane_i < N_EXPERTS + N_GROUPS)
    lg = jnp.where(is_group, logits, ninf)
    gmax = jnp.max(lg, axis=-1, keepdims=True)
    gsum = jnp.sum(jnp.exp(lg - gmax), axis=-1, keepdims=True)
    p_top = 1.0 / gsum
    g_idx = jnp.min(jnp.where(lg == gmax, lane - N_EXPERTS, float(N_GROUPS)), axis=-1, keepdims=True)
    le = jnp.where((lane_i < N_EXPERTS) & (lane_group == g_idx), logits, ninf)
    t1 = jnp.max(le, axis=-1, keepdims=True)
    e1 = jnp.min(jnp.where(le == t1, lane, float(LANES)), axis=-1, keepdims=True)
    le2 = jnp.where(lane == e1, ninf, le)
    t2 = jnp.max(le2, axis=-1, keepdims=True)
    e2 = jnp.min(jnp.where(le2 == t2, lane, float(LANES)), axis=-1, keepdims=True)
    ex = jnp.exp(t2 - t1)
    w1 = p_top / (1.0 + ex)
    w2 = p_top * ex / (1.0 + ex)

    sel1 = lane == e1
    sel2 = lane == e2
    onehot = jnp.where(sel1 | sel2, 1.0, 0.0)
    rr = lax.broadcasted_iota(jnp.int32, (TM, TM), 0)
    cc = lax.broadcasted_iota(jnp.int32, (TM, TM), 1)
    ltri = jnp.where(cc < rr, 1.0, 0.0).astype(BF16)
    rank = _dot(ltri, onehot.astype(BF16))
    cnt = jnp.sum(onehot, axis=0, keepdims=True)
    c8 = jnp.floor((cnt + (CHUNK_ROWS - 1)) * (1.0 / CHUNK_ROWS))
    ur = lax.broadcasted_iota(jnp.int32, (LANES, LANES), 0)
    uc = lax.broadcasted_iota(jnp.int32, (LANES, LANES), 1)
    upper = jnp.where(ur < uc, 1.0, 0.0).astype(BF16)
    c8_rows = jnp.broadcast_to(c8, (8, LANES))
    loff8_rows = _dot(c8_rows.astype(BF16), upper)
    slot_all = CHUNK_ROWS * loff8_rows[0:1] + rank
    slot1 = jnp.sum(jnp.where(sel1, slot_all, 0.0), axis=-1, keepdims=True)
    slot2 = jnp.sum(jnp.where(sel2, slot_all, 0.0), axis=-1, keepdims=True)

    slots = jnp.where(lane_i == 0, slot1, jnp.where(lane_i == 1, slot2, 0.0))
    return jnp.where(lane_i == 0, w1, w2), slots, c8_rows, loff8_rows


def _mix(attn, ga, p, x, woa, wout, gffn, wr, br):
    nt = N_TOK // TM
    sub = TMX // TM

    def const(shape):
        return pl.BlockSpec(shape, lambda i: (0,) * len(shape))

    def rows(width):
        return pl.BlockSpec((TMX, width), lambda i: (i, 0))

    return pl.pallas_call(
        _mix_kernel,
        grid=(N_TOK // TMX,),
        in_specs=[rows(MLA_WIDTH), rows(D_MODEL), rows(D_MODEL), rows(D_MODEL),
                  const((MLA_WIDTH, D_MODEL)), const((D_MODEL, D_MODEL)), const((1, D_MODEL)),
                  const((D_MODEL, 2 * LANES)), const((1, LANES))],
        out_specs=[rows(D_MODEL), rows(D_MODEL), rows(LANES), rows(LANES),
                   pl.BlockSpec((8 * sub, TM), lambda i: (i, 0)),
                   pl.BlockSpec((8 * sub, LANES), lambda i: (i, 0)),
                   pl.BlockSpec((8 * sub, LANES), lambda i: (i, 0))],
        out_shape=[jax.ShapeDtypeStruct((N_TOK, D_MODEL), F32),
                   jax.ShapeDtypeStruct((N_TOK, D_MODEL), BF16),
                   jax.ShapeDtypeStruct((N_TOK, LANES), F32),
                   jax.ShapeDtypeStruct((N_TOK, LANES), jnp.int32),
                   jax.ShapeDtypeStruct((nt * 8, TM), jnp.int32),
                   jax.ShapeDtypeStruct((nt * 8, LANES), jnp.int32),
                   jax.ShapeDtypeStruct((nt * 8, LANES), jnp.int32)],
        compiler_params=pltpu.CompilerParams(dimension_semantics=("arbitrary",),
                                             vmem_limit_bytes=40 * 1024 * 1024),
        name="mix",
    )(attn, ga, p, x, woa, wout, gffn, wr, br)


def _chunk_copy(src, dst, s8, d8, sem):
    return pltpu.make_async_copy(src.at[pl.ds(pl.multiple_of(s8 * CHUNK_ROWS, CHUNK_ROWS), CHUNK_ROWS)],
                                 dst.at[pl.ds(pl.multiple_of(d8 * CHUNK_ROWS, CHUNK_ROWS), CHUNK_ROWS)], sem)


def _for_each_chunk(t, ntile_ref, gchunk_ref, fn):
    n = ntile_ref[t]
    base = t * TILE_CHUNKS
    groups = n >> CHUNK_UNROLL_SHIFT

    def group(g, carry):
        for u in range(CHUNK_UNROLL):
            k = g * CHUNK_UNROLL + u
            fn(k, gchunk_ref[base + k])
        return carry

    lax.fori_loop(0, groups, group, 0)

    def single(k, carry):
        fn(k, gchunk_ref[base + k])
        return carry

    lax.fori_loop(groups << CHUNK_UNROLL_SHIFT, n, single, 0)


def _wait_chunks(n, src, dst, sem):
    groups = n >> CHUNK_UNROLL_SHIFT
    rows = CHUNK_UNROLL * CHUNK_ROWS

    def group(g, carry):
        pltpu.make_async_copy(src.at[pl.ds(0, rows)], dst.at[pl.ds(0, rows)], sem).wait()
        return carry

    lax.fori_loop(0, groups, group, 0)

    def single(k, carry):
        _chunk_copy(src, dst, 0, 0, sem).wait()
        return carry

    lax.fori_loop(groups << CHUNK_UNROLL_SHIFT, n, single, 0)


def _dispatch_kernel(ntile_ref, gchunk_ref, zstart8_ref, zcnt8_ref, nvb_ref,
                     slot_ref, h2_ref, xs_hbm, sbuf, zbuf, sem, zsem):
    t = pl.program_id(0)
    last = pl.num_programs(0) - 1
    cur = t % 2

    def zero_chunk_copy(d8):
        return _chunk_copy(zbuf, xs_hbm, 0, d8, zsem)

    def zero_block_copy(b):
        return pltpu.make_async_copy(
            zbuf, xs_hbm.at[pl.ds(pl.multiple_of(b * MOE_T, MOE_T), MOE_T)], zsem)

    def for_each_zero(chunk_fn, block_fn):
        def per_expert(e, carry):
            def per_chunk(j, carry2):
                chunk_fn(zstart8_ref[e] + j)
                return carry2

            lax.fori_loop(0, zcnt8_ref[e], per_chunk, 0)
            return carry

        lax.fori_loop(0, N_EXPERTS, per_expert, 0)

        def per_block(b, carry):
            block_fn(b)
            return carry

        lax.fori_loop(nvb_ref[0], MOE_BLOCKS, per_block, 0)

    @pl.when(t == 0)
    def _():
        zbuf[...] = jnp.zeros(zbuf.shape, F32)
        for_each_zero(lambda d8: zero_chunk_copy(d8).start(), lambda b: zero_block_copy(b).start())

    row = lax.broadcasted_iota(jnp.int32, (TILE_ROWS, TM), 0)
    slots = slot_ref[...]
    perm = jnp.where((row == slots[0:1, :]) | (row == slots[1:2, :]), 1.0, 0.0).astype(BF16)
    sbuf[cur] = _dot(perm, h2_ref[...])
    _for_each_chunk(t, ntile_ref, gchunk_ref,
                    lambda lc, gc: _chunk_copy(sbuf.at[cur], xs_hbm, lc, gc, sem.at[cur]).start())

    def wait_tile(tile, slot):
        _wait_chunks(ntile_ref[tile], sbuf.at[slot], xs_hbm, sem.at[slot])

    @pl.when(t > 0)
    def _():
        wait_tile(t - 1, 1 - cur)

    @pl.when(t == last)
    def _():
        wait_tile(t, cur)
        for_each_zero(lambda d8: zero_chunk_copy(d8).wait(), lambda b: zero_block_copy(b).wait())


def _dispatch(tables, slot_rows, h2):
    return pl.pallas_call(
        _dispatch_kernel,
        grid_spec=pltpu.PrefetchScalarGridSpec(
            num_scalar_prefetch=len(tables),
            grid=(N_TOK // TM,),
            in_specs=[pl.BlockSpec((8, TM), lambda i, *_: (i, 0)),
                      pl.BlockSpec((TM, D_MODEL), lambda i, *_: (i, 0))],
            out_specs=pl.BlockSpec(memory_space=pl.ANY),
            scratch_shapes=[pltpu.VMEM((2, TILE_ROWS, D_MODEL), F32), pltpu.VMEM((MOE_T, D_MODEL), F32),
                            pltpu.SemaphoreType.DMA((2,)), pltpu.SemaphoreType.DMA(())]),
        out_shape=jax.ShapeDtypeStruct((MOE_ROWS, D_MODEL), F32),
        compiler_params=pltpu.CompilerParams(dimension_semantics=("arbitrary",)),
        name="dispatch",
    )(*tables, slot_rows, h2)


def _expert_kernel(blk_e_ref, nvb_ref, xs_ref, wg_ref, wu_ref, wd_ref, ys_ref, wgb, wub, wdb):
    i = pl.program_id(0)
    live = i < nvb_ref[0]

    @pl.when(live & ((i == 0) | (blk_e_ref[i] != blk_e_ref[jnp.maximum(i - 1, 0)])))
    def _():
        wgb[...] = wg_ref[...].astype(BF16)
        wub[...] = wu_ref[...].astype(BF16)
        wdb[...] = wd_ref[...].astype(BF16)

    @pl.when(live)
    def _():
        xb = xs_ref[...].astype(BF16)
        g = _dot(xb, wgb[...])
        u = _dot(xb, wub[...])
        hid = (jax.nn.silu(g) * u).astype(BF16)
        ys_ref[...] = _dot(hid, wdb[...])

    @pl.when(jnp.logical_not(live))
    def _():
        ys_ref[...] = jnp.zeros(ys_ref.shape, ys_ref.dtype)


def _experts(blk_e, nvb, xs, wg, wu, wd):
    def row_block(i, be, nv):
        return (jnp.minimum(i, nv[0] - 1), 0)

    def weight(i, be, nv):
        return (be[jnp.minimum(i, nv[0] - 1)], 0, 0)

    return pl.pallas_call(
        _expert_kernel,
        grid_spec=pltpu.PrefetchScalarGridSpec(
            num_scalar_prefetch=2,
            grid=(MOE_BLOCKS,),
            in_specs=[pl.BlockSpec((MOE_T, D_MODEL), row_block),
                      pl.BlockSpec((None, D_MODEL, D_EXPERT), weight),
                      pl.BlockSpec((None, D_MODEL, D_EXPERT), weight),
                      pl.BlockSpec((None, D_EXPERT, D_MODEL), weight)],
            out_specs=pl.BlockSpec((MOE_T, D_MODEL), lambda i, be, nv: (i, 0)),
            scratch_shapes=[pltpu.VMEM((D_MODEL, D_EXPERT), BF16), pltpu.VMEM((D_MODEL, D_EXPERT), BF16),
                            pltpu.VMEM((D_EXPERT, D_MODEL), BF16)]),
        out_shape=jax.ShapeDtypeStruct((MOE_ROWS, D_MODEL), F32),
        compiler_params=pltpu.CompilerParams(dimension_semantics=("arbitrary",),
                                             vmem_limit_bytes=40 * 1024 * 1024),
        name="experts",
    )(blk_e, nvb, xs, wg, wu, wd)


def _combine_kernel(ntile_ref, gchunk_ref,
                    slot_ref, x1_ref, mf_ref, gfin_ref, ys_hbm, o_ref, ybuf, sem):
    t = pl.program_id(0)
    cur = t % 2

    def fetch(tile, slot):
        _for_each_chunk(tile, ntile_ref, gchunk_ref,
                        lambda lc, gc: _chunk_copy(ys_hbm, ybuf.at[slot], gc, lc, sem.at[slot]).start())

    @pl.when(t == 0)
    def _():
        ybuf[...] = jnp.zeros(ybuf.shape, F32)
        fetch(0, 0)

    @pl.when(t + 1 < pl.num_programs(0))
    def _():
        fetch(t + 1, 1 - cur)

    _wait_chunks(ntile_ref[t], ys_hbm, ybuf.at[cur], sem.at[cur])

    yb = ybuf[cur].astype(BF16)
    col = lax.broadcasted_iota(jnp.int32, (TM, TILE_ROWS), 1)
    slots = slot_ref[...]
    y1 = _dot(jnp.where(col == slots[:, 0:1], 1.0, 0.0).astype(BF16), yb)
    y2 = _dot(jnp.where(col == slots[:, 1:2], 1.0, 0.0).astype(BF16), yb)
    mf = mf_ref[...]
    x2 = x1_ref[...] + mf[:, 0:1] * y1 + mf[:, 1:2] * y2
    o_ref[...] = _rms(x2, gfin_ref[...])


def _combine(tables, slot_cols, x1, mf, gfin, ys):
    return pl.pallas_call(
        _combine_kernel,
        grid_spec=pltpu.PrefetchScalarGridSpec(
            num_scalar_prefetch=len(tables),
            grid=(N_TOK // TM,),
            in_specs=[pl.BlockSpec((TM, LANES), lambda i, *_: (i, 0)),
                      pl.BlockSpec((TM, D_MODEL), lambda i, *_: (i, 0)),
                      pl.BlockSpec((TM, LANES), lambda i, *_: (i, 0)),
                      pl.BlockSpec((1, D_MODEL), lambda i, *_: (0, 0)),
                      pl.BlockSpec(memory_space=pl.ANY)],
            out_specs=pl.BlockSpec((TM, D_MODEL), lambda i, *_: (i, 0)),
            scratch_shapes=[pltpu.VMEM((2, TILE_ROWS, D_MODEL), F32), pltpu.SemaphoreType.DMA((2,))]),
        out_shape=jax.ShapeDtypeStruct((N_TOK, D_MODEL), F32),
        compiler_params=pltpu.CompilerParams(dimension_semantics=("arbitrary",)),
        name="combine",
    )(*tables, slot_cols, x1, mf, gfin, ys)


def kernel(x, positions, g_attn_norm, w_in, g_cq, w_uq, g_ckv, w_uk, w_uv, w_o_attn, sgu_gain, sgu_bias, w_spatial, b_spatial, w_o_sgu, w_out, g_ffn_norm, w_router_group, b_router_group, w_router_expert, b_router_expert, w_exp_gate, w_exp_up, w_exp_down, g_final):
    assert x.shape == (BATCH, SEQ, D_MODEL) and w_in.shape[0] == 1
    half = QK_ROPE // 2
    swap = jnp.concatenate([jnp.arange(half, QK_ROPE), jnp.arange(0, half)])

    def pad_cols(w, width):
        return jnp.pad(w, ((0, 0), (0, width - w.shape[1])))

    wi = w_in[0]
    c0 = Q_RANK + KV_RANK
    kr = wi[:, c0:c0 + QK_ROPE]
    c1 = c0 + QK_ROPE
    win = jnp.concatenate([
        wi[:, :c0], pad_cols(kr, LANES), pad_cols(kr[:, swap], LANES), wi[:, c1:]], axis=1).astype(BF16)

    wq = w_uq[0].reshape(Q_RANK, MLA_HEADS, QK_NOPE + QK_ROPE)
    wq_nope = wq[:, :, :QK_NOPE].transpose(1, 0, 2)
    wq_rope = wq[:, :, QK_NOPE:]
    w_ukt = w_uk[0].reshape(KV_RANK, MLA_HEADS, QK_NOPE).transpose(1, 2, 0)
    wqlat = _fold(wq_nope, w_ukt).transpose(1, 0, 2).reshape(Q_RANK, MLA_HEADS * LANES).astype(BF16)

    def rope_cols(w):
        return jnp.pad(w, ((0, 0), (0, 0), (0, LANES - QK_ROPE))).reshape(Q_RANK, MLA_HEADS * LANES).astype(BF16)

    wqr = rope_cols(wq_rope)
    wqrs = rope_cols(wq_rope[:, :, swap])

    per_row = LANES // half
    freqs = ROPE_THETA ** (-jnp.arange(0, QK_ROPE, 2, dtype=F32) / QK_ROPE)
    pos8 = jnp.repeat(positions.astype(F32).reshape(N_TOK // per_row, per_row), half, axis=1)
    cos8, sin8 = _rope_tables(pos8, jnp.tile(freqs, per_row)[None, :])
    cos16 = cos8.reshape(N_TOK, half)
    sin16 = sin8.reshape(N_TOK, half)
    cos_t = pad_cols(jnp.concatenate([cos16, cos16], axis=1), LANES)
    sin_t = pad_cols(jnp.concatenate([-sin16, sin16], axis=1), LANES)

    head_of_col = jnp.arange(MLA_WIDTH) // V_HEAD
    wuv = jnp.where(head_of_col[None, None, :] == jnp.arange(MLA_HEADS)[:, None, None],
                    w_uv[0][None], 0.0).astype(BF16)
    wuv = wuv.reshape(MLA_HEADS * KV_RANK, MLA_WIDTH)

    wsp = w_spatial[0].astype(BF16)
    bs = b_spatial[0]
    bsp = jnp.repeat(bs.reshape(SGU_GROUPS // 2, 2, SGU_LEN).transpose(0, 2, 1), SGU_GROUP_DIM, axis=2)

    wr32 = jnp.concatenate([w_router_expert[0].transpose(1, 0, 2).reshape(D_MODEL, N_EXPERTS),
                            w_router_group[0]], axis=1)
    wr32 = pad_cols(wr32, LANES)
    wr_hi = wr32.astype(BF16)
    wr_lo = (wr32 - wr_hi.astype(F32)).astype(BF16)
    wr = jnp.concatenate([wr_hi, wr_lo], axis=1)
    br = pad_cols(jnp.concatenate([b_router_expert[0].reshape(-1), b_router_group[0]])[None, :], LANES)

    xf = x.reshape(N_TOK, D_MODEL)
    q, kt, v, p, ga = _inproj(
        xf, cos_t, sin_t, g_attn_norm, win, g_cq, g_ckv, wqlat, wqr, wqrs,
        sgu_gain, sgu_bias, wsp, bsp, w_o_sgu[0].astype(BF16))
    attn = _attention(q, kt, v, wuv)
    x1, h2, mf, slot_cols, slot_rows, c8_rows, loff8_rows = _mix(
        attn.reshape(N_TOK, MLA_WIDTH), ga, p, xf, w_o_attn[0].astype(BF16), w_out[0].astype(BF16),
        g_ffn_norm, wr, br)

    blk8 = MOE_T // CHUNK_ROWS
    c8 = c8_rows[::8, :N_EXPERTS]
    loff8 = loff8_rows[::8, :N_EXPERTS]
    tot8 = jnp.sum(c8, axis=0)
    pad8 = (tot8 + blk8 - 1) // blk8 * blk8
    gend8 = jnp.cumsum(pad8)
    gstart8 = gend8 - pad8
    gbase8 = gstart8[None, :] + jnp.cumsum(c8, axis=0) - c8
    nvb = (gend8[-1:] // blk8).astype(jnp.int32)
    blk_start8 = jnp.arange(MOE_BLOCKS, dtype=jnp.int32) * blk8
    blk_e = jnp.minimum(jnp.sum((gend8[None, :] <= blk_start8[:, None]).astype(jnp.int32), axis=1),
                        N_EXPERTS - 1)
    ntile = jnp.sum(c8, axis=1)
    k = jnp.arange(TILE_CHUNKS, dtype=jnp.int32)
    run_end = loff8 + c8
    in_run = (loff8[:, None, :] <= k[None, :, None]) & (k[None, :, None] < run_end[:, None, :])
    gchunk = jnp.sum(jnp.where(in_run, (gbase8 - loff8)[:, None, :], 0), axis=-1) + k[None, :]
    run_tables = (ntile, gchunk.reshape(-1).astype(jnp.int32))

    xs = _dispatch(run_tables + ((gstart8 + tot8).astype(jnp.int32), (pad8 - tot8).astype(jnp.int32), nvb),
                   slot_rows, h2)
    ys = _experts(blk_e, nvb, xs, w_exp_gate[0], w_exp_up[0], w_exp_down[0])
    out = _combine(run_tables, slot_cols, x1, mf, g_final.reshape(1, D_MODEL), ys)
    return out.reshape(BATCH, SEQ, D_MODEL)
```

```python
import functools

import jax
import jax.numpy as jnp
from jax import lax
from jax.experimental import pallas as pl
from jax.experimental.pallas import tpu as pltpu

D_MODEL = 1024
BATCH = 2
SEQ = 8192
N_TOK = BATCH * SEQ
CHUNK = 64
EPS = 1e-6
MLA_HEADS = 8
Q_RANK = 256
KV_RANK = 128
QK_NOPE = 64
QK_ROPE = 32
V_HEAD = 64
MLA_WIDTH = MLA_HEADS * V_HEAD
ROPE_THETA = 10000.0
SGU_GROUPS = 8
SGU_GROUP_DIM = 64
SGU_WIDTH = SGU_GROUPS * SGU_GROUP_DIM
SGU_LEN = 128
N_GROUPS = 4
EXPERTS_PER_GROUP = 8
N_EXPERTS = N_GROUPS * EXPERTS_PER_GROUP
TOP_K = 2
D_EXPERT = 256

LANES = 128
QK_PAD = 2 * LANES
V_PAD = 2 * LANES
SCALE = (QK_NOPE + QK_ROPE) ** -0.5
LOG2E = 1.4426950408889634
Q_SCALE = SCALE * LOG2E
NEG = -1e30

TM = 256
TMI = 512
TMX = 512
TQ = 128
TK = 512
ATTN_UNROLL_SHIFT = 1
ATTN_UNROLL = 1 << ATTN_UNROLL_SHIFT
TK_SHIFT = TK.bit_length() - 1
CHUNK_SHIFT = CHUNK.bit_length() - 1
assert 1 << TK_SHIFT == TK and 1 << CHUNK_SHIFT == CHUNK
MOE_T = 512
CHUNK_ROWS = 8
TILE_ROWS = -(-(TOP_K * TM + N_EXPERTS * (CHUNK_ROWS - 1) + CHUNK_ROWS) // 256) * 256
TILE_CHUNKS = TILE_ROWS // CHUNK_ROWS
PAIR_SLOTS = TILE_CHUNKS // 2
CHUNK_UNROLL_SHIFT = 2
CHUNK_UNROLL = 1 << CHUNK_UNROLL_SHIFT
MOE_ROWS_MAX = (N_TOK * TOP_K + (N_TOK // TM) * N_EXPERTS * (CHUNK_ROWS - 1)
                + N_EXPERTS * (MOE_T - CHUNK_ROWS))
MOE_BLOCKS = -(-MOE_ROWS_MAX // MOE_T)
MOE_ROWS = MOE_BLOCKS * MOE_T

C_Q = 0
C_KV = C_Q + Q_RANK
C_KR = C_KV + KV_RANK
C_KRS = C_KR + LANES
C_U = C_KRS + LANES
C_V = C_U + SGU_WIDTH
C_GA = C_V + SGU_WIDTH
C_GB = C_GA + D_MODEL
C_END = C_GB + D_MODEL

F32 = jnp.float32
BF16 = jnp.bfloat16


def _dot(a, b):
    return jnp.dot(a, b, preferred_element_type=F32)


def _rms(x, g):
    return x * lax.rsqrt(jnp.mean(x * x, axis=-1, keepdims=True) + EPS) * g


def _fold_kernel(a_ref, b_ref, o_ref):
    o_ref[...] = Q_SCALE * jnp.dot(a_ref[...], b_ref[...], preferred_element_type=F32,
                                 precision=lax.Precision.HIGHEST)


def _fold(w_uq_nope, w_ukt):
    return pl.pallas_call(
        _fold_kernel,
        grid=(MLA_HEADS,),
        in_specs=[pl.BlockSpec((None, Q_RANK, QK_NOPE), lambda h: (h, 0, 0)),
                  pl.BlockSpec((None, QK_NOPE, KV_RANK), lambda h: (h, 0, 0))],
        out_specs=pl.BlockSpec((None, Q_RANK, KV_RANK), lambda h: (h, 0, 0)),
        out_shape=jax.ShapeDtypeStruct((MLA_HEADS, Q_RANK, KV_RANK), F32),
        name="fold",
    )(w_uq_nope, w_ukt)


def _rope_kernel(pos_ref, freq_ref, cos_ref, sin_ref):
    ang = pos_ref[...] * freq_ref[...]
    cos_ref[...] = jnp.cos(ang)
    sin_ref[...] = jnp.sin(ang)


def _rope_tables(pos8, freq8):
    rows = pos8.shape[0]
    blk = pl.BlockSpec((TM, LANES), lambda i: (i, 0))
    return pl.pallas_call(
        _rope_kernel,
        grid=(rows // TM,),
        in_specs=[blk, pl.BlockSpec((1, LANES), lambda i: (0, 0))],
        out_specs=[blk, blk],
        out_shape=[jax.ShapeDtypeStruct((rows, LANES), F32)] * 2,
        name="rope_tables",
    )(pos8, freq8)


def _inproj_kernel(x_ref, cos_ref, sin_ref, gattn_ref, win_ref, gcq_ref, gckv_ref, wqlat_ref, wqr_ref,
                   wqrs_ref, sgain_ref, sbias_ref, wsp_ref, bsp_ref, wosgu_ref,
                   q_ref, kt_ref, v_ref, p_ref, ga_ref):
    hb = _rms(x_ref[...], gattn_ref[...]).astype(BF16)

    def proj(a, b):
        return _dot(hb, win_ref[:, a:b])

    cos_t = cos_ref[...]
    sin_t = sin_ref[...]

    cqn = _rms(proj(C_Q, C_KV), gcq_ref[...]).astype(BF16)
    qlat = _dot(cqn, wqlat_ref[...])
    qr = _dot(cqn, wqr_ref[...])
    qs = _dot(cqn, wqrs_ref[...])
    cos_q = cos_t * Q_SCALE
    sin_q = sin_t * Q_SCALE
    for r in range(TMI // TQ):
        t0, t1 = r * TQ, (r + 1) * TQ
        for h in range(MLA_HEADS):
            a, b = h * LANES, (h + 1) * LANES
            q_ref[r, h * TQ:(h + 1) * TQ, :LANES] = qlat[t0:t1, a:b].astype(BF16)
            q_ref[r, h * TQ:(h + 1) * TQ, LANES:] = (
                qr[t0:t1, a:b] * cos_q[t0:t1] + qs[t0:t1, a:b] * sin_q[t0:t1]).astype(BF16)

    zk = proj(C_KV, C_U)
    ckvn = _rms(zk[:, :KV_RANK], gckv_ref[...])
    krope = zk[:, KV_RANK:KV_RANK + LANES] * cos_t + zk[:, KV_RANK + LANES:] * sin_t
    kt_ref[...] = jnp.concatenate([ckvn, krope], axis=1).T.astype(BF16)
    v_ref[...] = jnp.concatenate([ckvn, jnp.ones_like(ckvn)], axis=1).astype(BF16)

    u = jax.nn.gelu(proj(C_U, C_V))
    v = jax.nn.gelu(proj(C_V, C_GA))
    mu = jnp.mean(v, axis=-1, keepdims=True)
    vc = v - mu
    var = jnp.mean(vc * vc, axis=-1, keepdims=True)
    vb = (vc * lax.rsqrt(var + EPS) * sgain_ref[...] + sbias_ref[...]).astype(BF16)
    row = lax.broadcasted_iota(jnp.int32, (SGU_LEN, SGU_LEN), 0)
    col = lax.broadcasted_iota(jnp.int32, (SGU_LEN, SGU_LEN), 1)
    causal = (row >> CHUNK_SHIFT) >= (col >> CHUNK_SHIFT)
    low_half = col < SGU_GROUP_DIM
    zero_w = jnp.zeros((SGU_LEN, SGU_LEN), BF16)
    w_pairs = [jnp.concatenate([jnp.where(causal, wsp_ref[2 * p], zero_w),
                                jnp.where(causal, wsp_ref[2 * p + 1], zero_w)], axis=1)
               for p in range(SGU_WIDTH // LANES)]
    row_blocks = []
    for r in range(TMI // SGU_LEN):
        pieces = []
        for p in range(SGU_WIDTH // LANES):
            blk = vb[r * SGU_LEN:(r + 1) * SGU_LEN, p * LANES:(p + 1) * LANES]
            stacked = jnp.concatenate([jnp.where(low_half, blk, zero_w), jnp.where(low_half, zero_w, blk)], axis=0)
            sv = _dot(w_pairs[p], stacked) + bsp_ref[p]
            pieces.append((u[r * SGU_LEN:(r + 1) * SGU_LEN, p * LANES:(p + 1) * LANES] * sv).astype(BF16))
        row_blocks.append(jnp.concatenate(pieces, axis=1))
    sgu = _dot(jnp.concatenate(row_blocks, axis=0), wosgu_ref[...])

    ga_ref[...] = jax.nn.sigmoid(proj(C_GA, C_GB)).astype(BF16)
    p_ref[...] = (jax.nn.sigmoid(proj(C_GB, C_END)) * sgu).astype(BF16)


def _inproj(x, cos_t, sin_t, gattn, win, gcq, gckv, wqlat, wqr, wqrs, sgain, sbias, wsp, bsp, wosgu):
    nt = N_TOK // TMI
    per_b = SEQ // TMI
    per_k = TK // TMI

    def const(shape):
        return pl.BlockSpec(shape, lambda i: (0,) * len(shape))

    return pl.pallas_call(
        _inproj_kernel,
        grid=(nt,),
        in_specs=[pl.BlockSpec((TMI, D_MODEL), lambda i: (i, 0)),
                  pl.BlockSpec((TMI, LANES), lambda i: (i, 0)), pl.BlockSpec((TMI, LANES), lambda i: (i, 0)),
                  const((1, D_MODEL)), const((D_MODEL, C_END)), const((1, Q_RANK)), const((1, KV_RANK)),
                  const((Q_RANK, MLA_HEADS * LANES)), const((Q_RANK, MLA_HEADS * LANES)),
                  const((Q_RANK, MLA_HEADS * LANES)),
                  const((1, SGU_WIDTH)), const((1, SGU_WIDTH)),
                  const((SGU_GROUPS, SGU_LEN, SGU_LEN)), const((SGU_GROUPS // 2, SGU_LEN, LANES)),
                  const((SGU_WIDTH, D_MODEL))],
        out_specs=[pl.BlockSpec((TMI // TQ, MLA_HEADS * TQ, QK_PAD), lambda i: (i, 0, 0)),
                   pl.BlockSpec((None, None, QK_PAD, TMI),
                                lambda i: (i // per_b, (i % per_b) // per_k, 0, i % per_k)),
                   pl.BlockSpec((None, None, TMI, V_PAD),
                                lambda i: (i // per_b, (i % per_b) // per_k, i % per_k, 0)),
                   pl.BlockSpec((TMI, D_MODEL), lambda i: (i, 0)),
                   pl.BlockSpec((TMI, D_MODEL), lambda i: (i, 0))],
        out_shape=[jax.ShapeDtypeStruct((N_TOK // TQ, MLA_HEADS * TQ, QK_PAD), BF16),
                   jax.ShapeDtypeStruct((BATCH, SEQ // TK, QK_PAD, TK), BF16),
                   jax.ShapeDtypeStruct((BATCH, SEQ // TK, TK, V_PAD), BF16),
                   jax.ShapeDtypeStruct((N_TOK, D_MODEL), BF16),
                   jax.ShapeDtypeStruct((N_TOK, D_MODEL), BF16)],
        compiler_params=pltpu.CompilerParams(dimension_semantics=("arbitrary",),
                                             vmem_limit_bytes=56 * 1024 * 1024),
        name="inproj",
    )(x, cos_t, sin_t, gattn, win, gcq, gckv, wqlat, wqr, wqrs, sgain, sbias, wsp, bsp, wosgu)


def _attn_kernel(q_ref, kt_ref, v_ref, wuv_ref, o_ref, m_ref, acc_ref, s_ref, p_ref, a_ref):
    qi = pl.program_id(1)
    m_ref[...] = jnp.full(m_ref.shape, NEG, F32)
    acc_ref[...] = jnp.zeros(acc_ref.shape, F32)

    def scores(j, slot):
        s_ref[slot] = _dot(q_ref[...], kt_ref[j])

    def update(j, slot, masked):
        if masked:
            q_chunk = ((qi * TQ - j * TK) >> CHUNK_SHIFT) + (
                lax.broadcasted_iota(jnp.int32, (TQ, TK), 0) >> CHUNK_SHIFT)
            k_chunk = lax.broadcasted_iota(jnp.int32, (TQ, TK), 1) >> CHUNK_SHIFT
            visible = k_chunk <= q_chunk
        for h in range(MLA_HEADS):
            rows = slice(h * TQ, (h + 1) * TQ)
            s = s_ref[slot, rows, :]
            if masked:
                s = jnp.where(visible, s, NEG)
            m_prev = m_ref[rows, :]
            m_new = jnp.maximum(m_prev, jnp.max(s, axis=-1, keepdims=True))
            m_ref[rows, :] = m_new
            a_ref[rows, :] = jnp.exp2(m_prev - m_new)
            s = s_ref[slot, rows, :]
            if masked:
                s = jnp.where(visible, s, NEG)
            p_ref[rows, :] = jnp.exp2(s - jnp.concatenate([m_new] * (TK // LANES), axis=1)).astype(BF16)
        alpha = a_ref[...]
        acc_ref[...] = jnp.concatenate([alpha, alpha], axis=1) * acc_ref[...] + _dot(p_ref[...], v_ref[j])

    diag = (qi * TQ) >> TK_SHIFT
    scores(0, 0)

    def run(first, count, last_masked):
        for u in range(count):
            last = u == count - 1
            if not (last and last_masked):
                scores(first + u + 1, (u + 1) % 2)
            update(first + u, u % 2, last and last_masked)

    def body(t, carry):
        run(ATTN_UNROLL * t, ATTN_UNROLL, False)
        return carry

    trips = diag >> ATTN_UNROLL_SHIFT
    lax.fori_loop(0, trips, body, 0)
    done = trips << ATTN_UNROLL_SHIFT
    for r in range(ATTN_UNROLL):
        @pl.when(diag - done == r)
        def _(r=r):
            run(done, r + 1, True)

    o_lat = (acc_ref[:, :KV_RANK] / acc_ref[:, KV_RANK:]).astype(BF16)
    o_cat = jnp.concatenate([o_lat[h * TQ:(h + 1) * TQ] for h in range(MLA_HEADS)], axis=1)
    o_ref[...] = _dot(o_cat, wuv_ref[...]).astype(BF16)


def _attention(q, kt, v, wuv):
    nk = SEQ // TK
    return pl.pallas_call(
        _attn_kernel,
        grid=(BATCH, SEQ // TQ),
        in_specs=[pl.BlockSpec((None, MLA_HEADS * TQ, QK_PAD), lambda b, i: (b * (SEQ // TQ) + i, 0, 0)),
                  pl.BlockSpec((None, nk, QK_PAD, TK), lambda b, i: (b, 0, 0, 0)),
                  pl.BlockSpec((None, nk, TK, V_PAD), lambda b, i: (b, 0, 0, 0)),
                  pl.BlockSpec((MLA_HEADS * KV_RANK, MLA_WIDTH), lambda b, i: (0, 0))],
        out_specs=pl.BlockSpec((None, TQ, MLA_WIDTH), lambda b, i: (b, i, 0)),
        out_shape=jax.ShapeDtypeStruct((BATCH, SEQ, MLA_WIDTH), BF16),
        scratch_shapes=[pltpu.VMEM((MLA_HEADS * TQ, LANES), F32),
                        pltpu.VMEM((MLA_HEADS * TQ, V_PAD), F32),
                        pltpu.VMEM((2, MLA_HEADS * TQ, TK), F32),
                        pltpu.VMEM((MLA_HEADS * TQ, TK), BF16),
                        pltpu.VMEM((MLA_HEADS * TQ, LANES), F32)],
        compiler_params=pltpu.CompilerParams(dimension_semantics=("arbitrary", "arbitrary"),
                                             vmem_limit_bytes=40 * 1024 * 1024),
        name="attention",
    )(q, kt, v, wuv)


def _mix_kernel(attn_ref, ga_ref, p_ref, x_ref, woa_ref, wout_ref, gffn_ref, wr_ref, br_ref,
                x1_ref, h2_ref, mf_ref, slot_col_ref, slot_row_ref, c8_ref, loff8_ref):
    a = _dot(attn_ref[...], woa_ref[...])
    mix = (ga_ref[...].astype(F32) * a + p_ref[...].astype(F32)).astype(BF16)
    x1 = x_ref[...] + _dot(mix, wout_ref[...])
    x1_ref[...] = x1
    h2 = _rms(x1, gffn_ref[...])
    h2_ref[...] = h2.astype(BF16)

    hi = h2.astype(BF16)
    lo = (h2 - hi.astype(F32)).astype(BF16)
    r1 = _dot(hi, wr_ref[...])
    r2 = _dot(lo, wr_ref[:, :LANES])
    logits_all = r1[:, :LANES] + r1[:, LANES:] + r2 + br_ref[...]

    for r in range(TMX // TM):
        mf, slots, c8_rows, loff8_rows = _route_tile(logits_all[r * TM:(r + 1) * TM])
        mf_ref[r * TM:(r + 1) * TM, :] = mf
        slot_col_ref[r * TM:(r + 1) * TM, :] = slots.astype(jnp.int32)
        slot_row_ref[8 * r:8 * (r + 1), :] = slots.T[:8].astype(jnp.int32)
        c8_ref[8 * r:8 * (r + 1), :] = c8_rows.astype(jnp.int32)
        loff8_ref[8 * r:8 * (r + 1), :] = loff8_rows.astype(jnp.int32)


def _route_tile(logits):
    lane_i = lax.broadcasted_iota(jnp.int32, (TM, LANES), 1)
    lane = lane_i.astype(F32)
    lane_group = (lane_i >> 3).astype(F32)
    ninf = -jnp.inf
    is_group = (lane_i >= N_EXPERTS) & (lane_i < N_EXPERTS + N_GROUPS)
    lg = jnp.where(is_group, logits, ninf)
    gmax = jnp.max(lg, axis=-1, keepdims=True)
    gsum = jnp.sum(jnp.exp(lg - gmax), axis=-1, keepdims=True)
    p_top = 1.0 / gsum
    g_idx = jnp.min(jnp.where(lg == gmax, lane - N_EXPERTS, float(N_GROUPS)), axis=-1, keepdims=True)
    le = jnp.where((lane_i < N_EXPERTS) & (lane_group == g_idx), logits, ninf)
    t1 = jnp.max(le, axis=-1, keepdims=True)
    e1 = jnp.min(jnp.where(le == t1, lane, float(LANES)), axis=-1, keepdims=True)
    le2 = jnp.where(lane == e1, ninf, le)
    t2 = jnp.max(le2, axis=-1, keepdims=True)
    e2 = jnp.min(jnp.where(le2 == t2, lane, float(LANES)), axis=-1, keepdims=True)
    ex = jnp.exp(t2 - t1)
    w1 = p_top / (1.0 + ex)
    w2 = p_top * ex / (1.0 + ex)

    sel1 = lane == e1
    sel2 = lane == e2
    onehot = jnp.where(sel1 | sel2, 1.0, 0.0)
    rr = lax.broadcasted_iota(jnp.int32, (TM, TM), 0)
    cc = lax.broadcasted_iota(jnp.int32, (TM, TM), 1)
    ltri = jnp.where(cc < rr, 1.0, 0.0).astype(BF16)
    rank = _dot(ltri, onehot.astype(BF16))
    cnt = jnp.sum(onehot, axis=0, keepdims=True)
    c8 = jnp.floor((cnt + (CHUNK_ROWS - 1)) * (1.0 / CHUNK_ROWS))
    ur = lax.broadcasted_iota(jnp.int32, (LANES, LANES), 0)
    uc = lax.broadcasted_iota(jnp.int32, (LANES, LANES), 1)
    upper = jnp.where(ur < uc, 1.0, 0.0).astype(BF16)
    c8_rows = jnp.broadcast_to(c8, (8, LANES))
    loff8_rows = _dot(c8_rows.astype(BF16), upper)
    slot_all = CHUNK_ROWS * loff8_rows[0:1] + rank
    slot1 = jnp.sum(jnp.where(sel1, slot_all, 0.0), axis=-1, keepdims=True)
    slot2 = jnp.sum(jnp.where(sel2, slot_all, 0.0), axis=-1, keepdims=True)

    slots = jnp.where(lane_i == 0, slot1, jnp.where(lane_i == 1, slot2, 0.0))
    return jnp.where(lane_i == 0, w1, w2), slots, c8_rows, loff8_rows


def _mix(attn, ga, p, x, woa, wout, gffn, wr, br):
    nt = N_TOK // TM
    sub = TMX // TM

    def const(shape):
        return pl.BlockSpec(shape, lambda i: (0,) * len(shape))

    def rows(width):
        return pl.BlockSpec((TMX, width), lambda i: (i, 0))

    return pl.pallas_call(
        _mix_kernel,
        grid=(N_TOK // TMX,),
        in_specs=[rows(MLA_WIDTH), rows(D_MODEL), rows(D_MODEL), rows(D_MODEL),
                  const((MLA_WIDTH, D_MODEL)), const((D_MODEL, D_MODEL)), const((1, D_MODEL)),
                  const((D_MODEL, 2 * LANES)), const((1, LANES))],
        out_specs=[rows(D_MODEL), rows(D_MODEL), rows(LANES), rows(LANES),
                   pl.BlockSpec((8 * sub, TM), lambda i: (i, 0)),
                   pl.BlockSpec((8 * sub, LANES), lambda i: (i, 0)),
                   pl.BlockSpec((8 * sub, LANES), lambda i: (i, 0))],
        out_shape=[jax.ShapeDtypeStruct((N_TOK, D_MODEL), F32),
                   jax.ShapeDtypeStruct((N_TOK, D_MODEL), BF16),
                   jax.ShapeDtypeStruct((N_TOK, LANES), F32),
                   jax.ShapeDtypeStruct((N_TOK, LANES), jnp.int32),
                   jax.ShapeDtypeStruct((nt * 8, TM), jnp.int32),
                   jax.ShapeDtypeStruct((nt * 8, LANES), jnp.int32),
                   jax.ShapeDtypeStruct((nt * 8, LANES), jnp.int32)],
        compiler_params=pltpu.CompilerParams(dimension_semantics=("arbitrary",),
                                             vmem_limit_bytes=40 * 1024 * 1024),
        name="mix",
    )(attn, ga, p, x, woa, wout, gffn, wr, br)


def _chunk_copy(src, dst, s8, d8, sem, nchunks=1):
    rows = nchunks * CHUNK_ROWS
    return pltpu.make_async_copy(src.at[pl.ds(pl.multiple_of(s8 * CHUNK_ROWS, CHUNK_ROWS), rows)],
                                 dst.at[pl.ds(pl.multiple_of(d8 * CHUNK_ROWS, CHUNK_ROWS), rows)], sem)


def _unrolled_loop(n, fn):
    groups = n >> CHUNK_UNROLL_SHIFT

    def group(g, carry):
        for u in range(CHUNK_UNROLL):
            fn(g * CHUNK_UNROLL + u)
        return carry

    lax.fori_loop(0, groups, group, 0)

    def single(k, carry):
        fn(k)
        return carry

    lax.fori_loop(groups << CHUNK_UNROLL_SHIFT, n, single, 0)


def _for_each_copy(t, copy_tables, fn):
    npair_ref, nsingle_ref, lpair_ref, gpair_ref, lsingle_ref, gsingle_ref = copy_tables
    pb = t * PAIR_SLOTS
    sb = t * N_EXPERTS
    _unrolled_loop(npair_ref[t], lambda k: fn(lpair_ref[pb + k], gpair_ref[pb + k], 2))
    _unrolled_loop(nsingle_ref[t], lambda k: fn(lsingle_ref[sb + k], gsingle_ref[sb + k], 1))


def _tile_chunks(t, copy_tables):
    return 2 * copy_tables[0][t] + copy_tables[1][t]


def _wait_chunks(n, src, dst, sem):
    groups = n >> CHUNK_UNROLL_SHIFT
    rows = CHUNK_UNROLL * CHUNK_ROWS

    def group(g, carry):
        pltpu.make_async_copy(src.at[pl.ds(0, rows)], dst.at[pl.ds(0, rows)], sem).wait()
        return carry

    lax.fori_loop(0, groups, group, 0)

    def single(k, carry):
        _chunk_copy(src, dst, 0, 0, sem).wait()
        return carry

    lax.fori_loop(groups << CHUNK_UNROLL_SHIFT, n, single, 0)


def _dispatch_kernel(npair_ref, nsingle_ref, lpair_ref, gpair_ref, lsingle_ref, gsingle_ref,
                     zstart8_ref, zcnt8_ref, nvb_ref,
                     slot_ref, h2_ref, xs_hbm, sbuf, zbuf, sem, zsem):
    copy_tables = (npair_ref, nsingle_ref, lpair_ref, gpair_ref, lsingle_ref, gsingle_ref)
    t = pl.program_id(0)
    last = pl.num_programs(0) - 1
    cur = t % 2

    def zero_chunk_copy(d8):
        return _chunk_copy(zbuf, xs_hbm, 0, d8, zsem)

    def zero_block_copy(b):
        return pltpu.make_async_copy(
            zbuf, xs_hbm.at[pl.ds(pl.multiple_of(b * MOE_T, MOE_T), MOE_T)], zsem)

    def for_each_zero(chunk_fn, block_fn):
        def per_expert(e, carry):
            def per_chunk(j, carry2):
                chunk_fn(zstart8_ref[e] + j)
                return carry2

            lax.fori_loop(0, zcnt8_ref[e], per_chunk, 0)
            return carry

        lax.fori_loop(0, N_EXPERTS, per_expert, 0)

        def per_block(b, carry):
            block_fn(b)
            return carry

        lax.fori_loop(nvb_ref[0], MOE_BLOCKS, per_block, 0)

    @pl.when(t == 0)
    def _():
        zbuf[...] = jnp.zeros(zbuf.shape, F32)
        for_each_zero(lambda d8: zero_chunk_copy(d8).start(), lambda b: zero_block_copy(b).start())

    row = lax.broadcasted_iota(jnp.int32, (TILE_ROWS, TM), 0)
    slots = slot_ref[...]
    perm = jnp.where((row == slots[0:1, :]) | (row == slots[1:2, :]), 1.0, 0.0).astype(BF16)
    sbuf[cur] = _dot(perm, h2_ref[...])
    _for_each_copy(t, copy_tables,
                   lambda lc, gc, n: _chunk_copy(sbuf.at[cur], xs_hbm, lc, gc, sem.at[cur], n).start())

    def wait_tile(tile, slot):
        _wait_chunks(_tile_chunks(tile, copy_tables), sbuf.at[slot], xs_hbm, sem.at[slot])

    @pl.when(t > 0)
    def _():
        wait_tile(t - 1, 1 - cur)

    @pl.when(t == last)
    def _():
        wait_tile(t, cur)
        for_each_zero(lambda d8: zero_chunk_copy(d8).wait(), lambda b: zero_block_copy(b).wait())


def _dispatch(tables, slot_rows, h2):
    return pl.pallas_call(
        _dispatch_kernel,
        grid_spec=pltpu.PrefetchScalarGridSpec(
            num_scalar_prefetch=len(tables),
            grid=(N_TOK // TM,),
            in_specs=[pl.BlockSpec((8, TM), lambda i, *_: (i, 0)),
                      pl.BlockSpec((TM, D_MODEL), lambda i, *_: (i, 0))],
            out_specs=pl.BlockSpec(memory_space=pl.ANY),
            scratch_shapes=[pltpu.VMEM((2, TILE_ROWS, D_MODEL), F32), pltpu.VMEM((MOE_T, D_MODEL), F32),
                            pltpu.SemaphoreType.DMA((2,)), pltpu.SemaphoreType.DMA(())]),
        out_shape=jax.ShapeDtypeStruct((MOE_ROWS, D_MODEL), F32),
        compiler_params=pltpu.CompilerParams(dimension_semantics=("arbitrary",)),
        name="dispatch",
    )(*tables, slot_rows, h2)


def _expert_kernel(blk_e_ref, nvb_ref, xs_ref, wg_ref, wu_ref, wd_ref, ys_ref, wgb, wub, wdb):
    i = pl.program_id(0)
    live = i < nvb_ref[0]

    @pl.when(live & ((i == 0) | (blk_e_ref[i] != blk_e_ref[jnp.maximum(i - 1, 0)])))
    def _():
        wgb[...] = wg_ref[...].astype(BF16)
        wub[...] = wu_ref[...].astype(BF16)
        wdb[...] = wd_ref[...].astype(BF16)

    @pl.when(live)
    def _():
        xb = xs_ref[...].astype(BF16)
        g = _dot(xb, wgb[...])
        u = _dot(xb, wub[...])
        hid = (jax.nn.silu(g) * u).astype(BF16)
        ys_ref[...] = _dot(hid, wdb[...])

    @pl.when(jnp.logical_not(live))
    def _():
        ys_ref[...] = jnp.zeros(ys_ref.shape, ys_ref.dtype)


def _experts(blk_e, nvb, xs, wg, wu, wd):
    def row_block(i, be, nv):
        return (jnp.minimum(i, nv[0] - 1), 0)

    def weight(i, be, nv):
        return (be[jnp.minimum(i, nv[0] - 1)], 0, 0)

    return pl.pallas_call(
        _expert_kernel,
        grid_spec=pltpu.PrefetchScalarGridSpec(
            num_scalar_prefetch=2,
            grid=(MOE_BLOCKS,),
            in_specs=[pl.BlockSpec((MOE_T, D_MODEL), row_block),
                      pl.BlockSpec((None, D_MODEL, D_EXPERT), weight),
                      pl.BlockSpec((None, D_MODEL, D_EXPERT), weight),
                      pl.BlockSpec((None, D_EXPERT, D_MODEL), weight)],
            out_specs=pl.BlockSpec((MOE_T, D_MODEL), lambda i, be, nv: (i, 0)),
            scratch_shapes=[pltpu.VMEM((D_MODEL, D_EXPERT), BF16), pltpu.VMEM((D_MODEL, D_EXPERT), BF16),
                            pltpu.VMEM((D_EXPERT, D_MODEL), BF16)]),
        out_shape=jax.ShapeDtypeStruct((MOE_ROWS, D_MODEL), F32),
        compiler_params=pltpu.CompilerParams(dimension_semantics=("arbitrary",),
                                             vmem_limit_bytes=40 * 1024 * 1024),
        name="experts",
    )(blk_e, nvb, xs, wg, wu, wd)


def _combine_kernel(npair_ref, nsingle_ref, lpair_ref, gpair_ref, lsingle_ref, gsingle_ref,
                    slot_ref, x1_ref, mf_ref, gfin_ref, ys_hbm, o_ref, ybuf, sem):
    copy_tables = (npair_ref, nsingle_ref, lpair_ref, gpair_ref, lsingle_ref, gsingle_ref)
    t = pl.program_id(0)
    cur = t % 2

    def fetch(tile, slot):
        _for_each_copy(tile, copy_tables,
                       lambda lc, gc, n: _chunk_copy(ys_hbm, ybuf.at[slot], gc, lc, sem.at[slot], n).start())

    @pl.when(t == 0)
    def _():
        ybuf[...] = jnp.zeros(ybuf.shape, F32)
        fetch(0, 0)

    @pl.when(t + 1 < pl.num_programs(0))
    def _():
        fetch(t + 1, 1 - cur)

    _wait_chunks(_tile_chunks(t, copy_tables), ys_hbm, ybuf.at[cur], sem.at[cur])

    yb = ybuf[cur].astype(BF16)
    col = lax.broadcasted_iota(jnp.int32, (TM, TILE_ROWS), 1)
    slots = slot_ref[...]
    y1 = _dot(jnp.where(col == slots[:, 0:1], 1.0, 0.0).astype(BF16), yb)
    y2 = _dot(jnp.where(col == slots[:, 1:2], 1.0, 0.0).astype(BF16), yb)
    mf = mf_ref[...]
    x2 = x1_ref[...] + mf[:, 0:1] * y1 + mf[:, 1:2] * y2
    o_ref[...] = _rms(x2, gfin_ref[...])


def _combine(tables, slot_cols, x1, mf, gfin, ys):
    return pl.pallas_call(
        _combine_kernel,
        grid_spec=pltpu.PrefetchScalarGridSpec(
            num_scalar_prefetch=len(tables),
            grid=(N_TOK // TM,),
            in_specs=[pl.BlockSpec((TM, LANES), lambda i, *_: (i, 0)),
                      pl.BlockSpec((TM, D_MODEL), lambda i, *_: (i, 0)),
                      pl.BlockSpec((TM, LANES), lambda i, *_: (i, 0)),
                      pl.BlockSpec((1, D_MODEL), lambda i, *_: (0, 0)),
                      pl.BlockSpec(memory_space=pl.ANY)],
            out_specs=pl.BlockSpec((TM, D_MODEL), lambda i, *_: (i, 0)),
            scratch_shapes=[pltpu.VMEM((2, TILE_ROWS, D_MODEL), F32), pltpu.SemaphoreType.DMA((2,))]),
        out_shape=jax.ShapeDtypeStruct((N_TOK, D_MODEL), F32),
        compiler_params=pltpu.CompilerParams(dimension_semantics=("arbitrary",)),
        name="combine",
    )(*tables, slot_cols, x1, mf, gfin, ys)


def kernel(x, positions, g_attn_norm, w_in, g_cq, w_uq, g_ckv, w_uk, w_uv, w_o_attn, sgu_gain, sgu_bias, w_spatial, b_spatial, w_o_sgu, w_out, g_ffn_norm, w_router_group, b_router_group, w_router_expert, b_router_expert, w_exp_gate, w_exp_up, w_exp_down, g_final):
    assert x.shape == (BATCH, SEQ, D_MODEL) and w_in.shape[0] == 1
    half = QK_ROPE // 2
    swap = jnp.concatenate([jnp.arange(half, QK_ROPE), jnp.arange(0, half)])

    def pad_cols(w, width):
        return jnp.pad(w, ((0, 0), (0, width - w.shape[1])))

    wi = w_in[0]
    c0 = Q_RANK + KV_RANK
    kr = wi[:, c0:c0 + QK_ROPE]
    c1 = c0 + QK_ROPE
    win = jnp.concatenate([
        wi[:, :c0], pad_cols(kr, LANES), pad_cols(kr[:, swap], LANES), wi[:, c1:]], axis=1).astype(BF16)

    wq = w_uq[0].reshape(Q_RANK, MLA_HEADS, QK_NOPE + QK_ROPE)
    wq_nope = wq[:, :, :QK_NOPE].transpose(1, 0, 2)
    wq_rope = wq[:, :, QK_NOPE:]
    w_ukt = w_uk[0].reshape(KV_RANK, MLA_HEADS, QK_NOPE).transpose(1, 2, 0)
    wqlat = _fold(wq_nope, w_ukt).transpose(1, 0, 2).reshape(Q_RANK, MLA_HEADS * LANES).astype(BF16)

    def rope_cols(w):
        return jnp.pad(w, ((0, 0), (0, 0), (0, LANES - QK_ROPE))).reshape(Q_RANK, MLA_HEADS * LANES).astype(BF16)

    wqr = rope_cols(wq_rope)
    wqrs = rope_cols(wq_rope[:, :, swap])

    per_row = LANES // half
    freqs = ROPE_THETA ** (-jnp.arange(0, QK_ROPE, 2, dtype=F32) / QK_ROPE)
    pos8 = jnp.repeat(positions.astype(F32).reshape(N_TOK // per_row, per_row), half, axis=1)
    cos8, sin8 = _rope_tables(pos8, jnp.tile(freqs, per_row)[None, :])
    cos16 = cos8.reshape(N_TOK, half)
    sin16 = sin8.reshape(N_TOK, half)
    cos_t = pad_cols(jnp.concatenate([cos16, cos16], axis=1), LANES)
    sin_t = pad_cols(jnp.concatenate([-sin16, sin16], axis=1), LANES)

    head_of_col = jnp.arange(MLA_WIDTH) // V_HEAD
    wuv = jnp.where(head_of_col[None, None, :] == jnp.arange(MLA_HEADS)[:, None, None],
                    w_uv[0][None], 0.0).astype(BF16)
    wuv = wuv.reshape(MLA_HEADS * KV_RANK, MLA_WIDTH)

    wsp = w_spatial[0].astype(BF16)
    bs = b_spatial[0]
    bsp = jnp.repeat(bs.reshape(SGU_GROUPS // 2, 2, SGU_LEN).transpose(0, 2, 1), SGU_GROUP_DIM, axis=2)

    wr32 = jnp.concatenate([w_router_expert[0].transpose(1, 0, 2).reshape(D_MODEL, N_EXPERTS),
                            w_router_group[0]], axis=1)
    wr32 = pad_cols(wr32, LANES)
    wr_hi = wr32.astype(BF16)
    wr_lo = (wr32 - wr_hi.astype(F32)).astype(BF16)
    wr = jnp.concatenate([wr_hi, wr_lo], axis=1)
    br = pad_cols(jnp.concatenate([b_router_expert[0].reshape(-1), b_router_group[0]])[None, :], LANES)

    xf = x.reshape(N_TOK, D_MODEL)
    q, kt, v, p, ga = _inproj(
        xf, cos_t, sin_t, g_attn_norm, win, g_cq, g_ckv, wqlat, wqr, wqrs,
        sgu_gain, sgu_bias, wsp, bsp, w_o_sgu[0].astype(BF16))
    attn = _attention(q, kt, v, wuv)
    x1, h2, mf, slot_cols, slot_rows, c8_rows, loff8_rows = _mix(
        attn.reshape(N_TOK, MLA_WIDTH), ga, p, xf, w_o_attn[0].astype(BF16), w_out[0].astype(BF16),
        g_ffn_norm, wr, br)

    blk8 = MOE_T // CHUNK_ROWS
    c8 = c8_rows[::8, :N_EXPERTS]
    loff8 = loff8_rows[::8, :N_EXPERTS]
    tot8 = jnp.sum(c8, axis=0)
    pad8 = (tot8 + blk8 - 1) // blk8 * blk8
    gend8 = jnp.cumsum(pad8)
    gstart8 = gend8 - pad8
    gbase8 = gstart8[None, :] + jnp.cumsum(c8, axis=0) - c8
    nvb = (gend8[-1:] // blk8).astype(jnp.int32)
    blk_start8 = jnp.arange(MOE_BLOCKS, dtype=jnp.int32) * blk8
    blk_e = jnp.minimum(jnp.sum((gend8[None, :] <= blk_start8[:, None]).astype(jnp.int32), axis=1),
                        N_EXPERTS - 1)
    def flat_list(count, slots, first_local, first_global, step):
        off = jnp.cumsum(count, axis=1) - count
        k = jnp.arange(slots, dtype=jnp.int32)
        in_run = (off[:, None, :] <= k[None, :, None]) & (k[None, :, None] < (off + count)[:, None, :])

        def pick(first):
            return (jnp.sum(jnp.where(in_run, (first - step * off)[:, None, :], 0), axis=-1)
                    + step * k[None, :]).reshape(-1).astype(jnp.int32)

        return jnp.sum(count, axis=1).astype(jnp.int32), pick(first_local), pick(first_global)

    npair, lpair, gpair = flat_list(c8 // 2, PAIR_SLOTS, loff8, gbase8, 2)
    nsingle, lsingle, gsingle = flat_list(c8 % 2, N_EXPERTS, loff8 + c8 - 1, gbase8 + c8 - 1, 0)
    run_tables = (npair, nsingle, lpair, gpair, lsingle, gsingle)

    xs = _dispatch(run_tables + ((gstart8 + tot8).astype(jnp.int32), (pad8 - tot8).astype(jnp.int32), nvb),
                   slot_rows, h2)
    ys = _experts(blk_e, nvb, xs, w_exp_gate[0], w_exp_up[0], w_exp_down[0])
    out = _combine(run_tables, slot_cols, x1, mf, g_final.reshape(1, D_MODEL), ys)
    return out.reshape(BATCH, SEQ, D_MODEL)
```

```python
import functools

import jax
import jax.numpy as jnp
from jax import lax
from jax.experimental import pallas as pl
from jax.experimental.pallas import tpu as pltpu

D_MODEL = 1024
BATCH = 2
SEQ = 8192
N_TOK = BATCH * SEQ
CHUNK = 64
EPS = 1e-6
MLA_HEADS = 8
Q_RANK = 256
KV_RANK = 128
QK_NOPE = 64
QK_ROPE = 32
V_HEAD = 64
MLA_WIDTH = MLA_HEADS * V_HEAD
ROPE_THETA = 10000.0
SGU_GROUPS = 8
SGU_GROUP_DIM = 64
SGU_WIDTH = SGU_GROUPS * SGU_GROUP_DIM
SGU_LEN = 128
N_GROUPS = 4
EXPERTS_PER_GROUP = 8
N_EXPERTS = N_GROUPS * EXPERTS_PER_GROUP
TOP_K = 2
D_EXPERT = 256

LANES = 128
QK_PAD = 2 * LANES
V_PAD = 2 * LANES
SCALE = (QK_NOPE + QK_ROPE) ** -0.5
LOG2E = 1.4426950408889634
Q_SCALE = SCALE * LOG2E
NEG = -1e30

TM = 256
TMI = 512
TMX = 512
TQ = 128
TK = 512
ATTN_UNROLL_SHIFT = 1
ATTN_UNROLL = 1 << ATTN_UNROLL_SHIFT
TK_SHIFT = TK.bit_length() - 1
CHUNK_SHIFT = CHUNK.bit_length() - 1
assert 1 << TK_SHIFT == TK and 1 << CHUNK_SHIFT == CHUNK
MOE_T = 512
CHUNK_ROWS = 8
TILE_ROWS = -(-(TOP_K * TM + N_EXPERTS * (CHUNK_ROWS - 1) + CHUNK_ROWS) // 256) * 256
TILE_CHUNKS = TILE_ROWS // CHUNK_ROWS
PAIR_SLOTS = TILE_CHUNKS // 2
CHUNK_UNROLL_SHIFT = 2
CHUNK_UNROLL = 1 << CHUNK_UNROLL_SHIFT
MOE_ROWS_MAX = (N_TOK * TOP_K + (N_TOK // TM) * N_EXPERTS * (CHUNK_ROWS - 1)
                + N_EXPERTS * (MOE_T - CHUNK_ROWS))
MOE_BLOCKS = -(-MOE_ROWS_MAX // MOE_T)
MOE_ROWS = MOE_BLOCKS * MOE_T

C_Q = 0
C_KV = C_Q + Q_RANK
C_KR = C_KV + KV_RANK
C_KRS = C_KR + LANES
C_U = C_KRS + LANES
C_V = C_U + SGU_WIDTH
C_GA = C_V + SGU_WIDTH
C_GB = C_GA + D_MODEL
C_END = C_GB + D_MODEL

F32 = jnp.float32
BF16 = jnp.bfloat16


def _dot(a, b):
    return jnp.dot(a, b, preferred_element_type=F32)


def _rms(x, g):
    return x * lax.rsqrt(jnp.mean(x * x, axis=-1, keepdims=True) + EPS) * g


def _fold_kernel(a_ref, b_ref, o_ref):
    o_ref[...] = Q_SCALE * jnp.dot(a_ref[...], b_ref[...], preferred_element_type=F32,
                                 precision=lax.Precision.HIGHEST)


def _fold(w_uq_nope, w_ukt):
    return pl.pallas_call(
        _fold_kernel,
        grid=(MLA_HEADS,),
        in_specs=[pl.BlockSpec((None, Q_RANK, QK_NOPE), lambda h: (h, 0, 0)),
                  pl.BlockSpec((None, QK_NOPE, KV_RANK), lambda h: (h, 0, 0))],
        out_specs=pl.BlockSpec((None, Q_RANK, KV_RANK), lambda h: (h, 0, 0)),
        out_shape=jax.ShapeDtypeStruct((MLA_HEADS, Q_RANK, KV_RANK), F32),
        name="fold",
    )(w_uq_nope, w_ukt)


def _rope_kernel(pos_ref, freq_ref, cos_ref, sin_ref):
    ang = pos_ref[...] * freq_ref[...]
    cos_ref[...] = jnp.cos(ang)
    sin_ref[...] = jnp.sin(ang)


def _rope_tables(pos8, freq8):
    rows = pos8.shape[0]
    blk = pl.BlockSpec((TM, LANES), lambda i: (i, 0))
    return pl.pallas_call(
        _rope_kernel,
        grid=(rows // TM,),
        in_specs=[blk, pl.BlockSpec((1, LANES), lambda i: (0, 0))],
        out_specs=[blk, blk],
        out_shape=[jax.ShapeDtypeStruct((rows, LANES), F32)] * 2,
        name="rope_tables",
    )(pos8, freq8)


def _inproj_kernel(x_ref, cos_ref, sin_ref, gattn_ref, win_ref, gcq_ref, gckv_ref, wqlat_ref, wqr_ref,
                   wqrs_ref, sgain_ref, sbias_ref, wsp_ref, bsp_ref, wosgu_ref,
                   q_ref, kt_ref, v_ref, p_ref, ga_ref):
    hb = _rms(x_ref[...], gattn_ref[...]).astype(BF16)

    def proj(a, b):
        return _dot(hb, win_ref[:, a:b])

    cos_t = cos_ref[...]
    sin_t = sin_ref[...]

    cqn = _rms(proj(C_Q, C_KV), gcq_ref[...]).astype(BF16)
    qlat = _dot(cqn, wqlat_ref[...])
    qr = _dot(cqn, wqr_ref[...])
    qs = _dot(cqn, wqrs_ref[...])
    cos_q = cos_t * Q_SCALE
    sin_q = sin_t * Q_SCALE
    for r in range(TMI // TQ):
        t0, t1 = r * TQ, (r + 1) * TQ
        for h in range(MLA_HEADS):
            a, b = h * LANES, (h + 1) * LANES
            q_ref[r, h * TQ:(h + 1) * TQ, :LANES] = qlat[t0:t1, a:b].astype(BF16)
            q_ref[r, h * TQ:(h + 1) * TQ, LANES:] = (
                qr[t0:t1, a:b] * cos_q[t0:t1] + qs[t0:t1, a:b] * sin_q[t0:t1]).astype(BF16)

    zk = proj(C_KV, C_U)
    ckvn = _rms(zk[:, :KV_RANK], gckv_ref[...])
    krope = zk[:, KV_RANK:KV_RANK + LANES] * cos_t + zk[:, KV_RANK + LANES:] * sin_t
    kt_ref[...] = jnp.concatenate([ckvn, krope], axis=1).T.astype(BF16)
    v_ref[...] = jnp.concatenate([ckvn, jnp.ones_like(ckvn)], axis=1).astype(BF16)

    u = jax.nn.gelu(proj(C_U, C_V))
    v = jax.nn.gelu(proj(C_V, C_GA))
    mu = jnp.mean(v, axis=-1, keepdims=True)
    vc = v - mu
    var = jnp.mean(vc * vc, axis=-1, keepdims=True)
    vb = (vc * lax.rsqrt(var + EPS) * sgain_ref[...] + sbias_ref[...]).astype(BF16)
    row = lax.broadcasted_iota(jnp.int32, (SGU_LEN, SGU_LEN), 0)
    col = lax.broadcasted_iota(jnp.int32, (SGU_LEN, SGU_LEN), 1)
    causal = (row >> CHUNK_SHIFT) >= (col >> CHUNK_SHIFT)
    low_half = col < SGU_GROUP_DIM
    zero_w = jnp.zeros((SGU_LEN, SGU_LEN), BF16)
    w_pairs = [jnp.concatenate([jnp.where(causal, wsp_ref[2 * p], zero_w),
                                jnp.where(causal, wsp_ref[2 * p + 1], zero_w)], axis=1)
               for p in range(SGU_WIDTH // LANES)]
    row_blocks = []
    for r in range(TMI // SGU_LEN):
        pieces = []
        for p in range(SGU_WIDTH // LANES):
            blk = vb[r * SGU_LEN:(r + 1) * SGU_LEN, p * LANES:(p + 1) * LANES]
            stacked = jnp.concatenate([jnp.where(low_half, blk, zero_w), jnp.where(low_half, zero_w, blk)], axis=0)
            sv = _dot(w_pairs[p], stacked) + bsp_ref[p]
            pieces.append((u[r * SGU_LEN:(r + 1) * SGU_LEN, p * LANES:(p + 1) * LANES] * sv).astype(BF16))
        row_blocks.append(jnp.concatenate(pieces, axis=1))
    sgu = _dot(jnp.concatenate(row_blocks, axis=0), wosgu_ref[...])

    ga_ref[...] = jax.nn.sigmoid(proj(C_GA, C_GB)).astype(BF16)
    p_ref[...] = (jax.nn.sigmoid(proj(C_GB, C_END)) * sgu).astype(BF16)


def _inproj(x, cos_t, sin_t, gattn, win, gcq, gckv, wqlat, wqr, wqrs, sgain, sbias, wsp, bsp, wosgu):
    nt = N_TOK // TMI
    per_b = SEQ // TMI
    per_k = TK // TMI

    def const(shape):
        return pl.BlockSpec(shape, lambda i: (0,) * len(shape))

    return pl.pallas_call(
        _inproj_kernel,
        grid=(nt,),
        in_specs=[pl.BlockSpec((TMI, D_MODEL), lambda i: (i, 0)),
                  pl.BlockSpec((TMI, LANES), lambda i: (i, 0)), pl.BlockSpec((TMI, LANES), lambda i: (i, 0)),
                  const((1, D_MODEL)), const((D_MODEL, C_END)), const((1, Q_RANK)), const((1, KV_RANK)),
                  const((Q_RANK, MLA_HEADS * LANES)), const((Q_RANK, MLA_HEADS * LANES)),
                  const((Q_RANK, MLA_HEADS * LANES)),
                  const((1, SGU_WIDTH)), const((1, SGU_WIDTH)),
                  const((SGU_GROUPS, SGU_LEN, SGU_LEN)), const((SGU_GROUPS // 2, SGU_LEN, LANES)),
                  const((SGU_WIDTH, D_MODEL))],
        out_specs=[pl.BlockSpec((TMI // TQ, MLA_HEADS * TQ, QK_PAD), lambda i: (i, 0, 0)),
                   pl.BlockSpec((None, None, QK_PAD, TMI),
                                lambda i: (i // per_b, (i % per_b) // per_k, 0, i % per_k)),
                   pl.BlockSpec((None, None, TMI, V_PAD),
                                lambda i: (i // per_b, (i % per_b) // per_k, i % per_k, 0)),
                   pl.BlockSpec((TMI, D_MODEL), lambda i: (i, 0)),
                   pl.BlockSpec((TMI, D_MODEL), lambda i: (i, 0))],
        out_shape=[jax.ShapeDtypeStruct((N_TOK // TQ, MLA_HEADS * TQ, QK_PAD), BF16),
                   jax.ShapeDtypeStruct((BATCH, SEQ // TK, QK_PAD, TK), BF16),
                   jax.ShapeDtypeStruct((BATCH, SEQ // TK, TK, V_PAD), BF16),
                   jax.ShapeDtypeStruct((N_TOK, D_MODEL), BF16),
                   jax.ShapeDtypeStruct((N_TOK, D_MODEL), BF16)],
        compiler_params=pltpu.CompilerParams(dimension_semantics=("arbitrary",),
                                             vmem_limit_bytes=56 * 1024 * 1024),
        name="inproj",
    )(x, cos_t, sin_t, gattn, win, gcq, gckv, wqlat, wqr, wqrs, sgain, sbias, wsp, bsp, wosgu)


def _attn_kernel(q_ref, kt_ref, v_ref, wuv_ref, o_ref, m_ref, acc_ref, s_ref, p_ref, a_ref):
    qi = pl.program_id(1)
    m_ref[...] = jnp.full(m_ref.shape, NEG, F32)
    acc_ref[...] = jnp.zeros(acc_ref.shape, F32)

    def scores(j, slot):
        s_ref[slot] = _dot(q_ref[...], kt_ref[j])

    def update(j, slot, width=None):
        masked = width is not None
        w = width if masked else TK
        if masked:
            q_chunk = ((qi * TQ - j * TK) >> CHUNK_SHIFT) + (
                lax.broadcasted_iota(jnp.int32, (TQ, w), 0) >> CHUNK_SHIFT)
            k_chunk = lax.broadcasted_iota(jnp.int32, (TQ, w), 1) >> CHUNK_SHIFT
            visible = k_chunk <= q_chunk
        for h in range(MLA_HEADS):
            rows = slice(h * TQ, (h + 1) * TQ)
            s = s_ref[slot, rows, :w]
            if masked:
                s = jnp.where(visible, s, NEG)
            m_prev = m_ref[rows, :]
            m_new = jnp.maximum(m_prev, jnp.max(s, axis=-1, keepdims=True))
            m_ref[rows, :] = m_new
            a_ref[rows, :] = jnp.exp2(m_prev - m_new)
            s = s_ref[slot, rows, :w]
            if masked:
                s = jnp.where(visible, s, NEG)
            p_ref[rows, :w] = jnp.exp2(s - jnp.concatenate([m_new] * (w // LANES), axis=1)).astype(BF16)
        alpha = a_ref[...]
        acc_ref[...] = jnp.concatenate([alpha, alpha], axis=1) * acc_ref[...] + _dot(
            p_ref[:, :w], v_ref[j, :w, :])

    diag = (qi * TQ) >> TK_SHIFT
    scores(0, 0)

    def run(first, count, last_masked):
        for u in range(count):
            if not (last_masked and u == count - 1):
                scores(first + u + 1, (u + 1) % 2)
                update(first + u, u % 2)
            else:
                for sub in range(TK // TQ):
                    @pl.when((qi & (TK // TQ - 1)) == sub)
                    def _(u=u, sub=sub):
                        update(first + u, u % 2, (sub + 1) * TQ)

    def body(t, carry):
        run(ATTN_UNROLL * t, ATTN_UNROLL, False)
        return carry

    trips = diag >> ATTN_UNROLL_SHIFT
    lax.fori_loop(0, trips, body, 0)
    done = trips << ATTN_UNROLL_SHIFT
    for r in range(ATTN_UNROLL):
        @pl.when(diag - done == r)
        def _(r=r):
            run(done, r + 1, True)

    o_lat = (acc_ref[:, :KV_RANK] / acc_ref[:, KV_RANK:]).astype(BF16)
    o_cat = jnp.concatenate([o_lat[h * TQ:(h + 1) * TQ] for h in range(MLA_HEADS)], axis=1)
    o_ref[...] = _dot(o_cat, wuv_ref[...]).astype(BF16)


def _attention(q, kt, v, wuv):
    nk = SEQ // TK
    return pl.pallas_call(
        _attn_kernel,
        grid=(BATCH, SEQ // TQ),
        in_specs=[pl.BlockSpec((None, MLA_HEADS * TQ, QK_PAD), lambda b, i: (b * (SEQ // TQ) + i, 0, 0)),
                  pl.BlockSpec((None, nk, QK_PAD, TK), lambda b, i: (b, 0, 0, 0)),
                  pl.BlockSpec((None, nk, TK, V_PAD), lambda b, i: (b, 0, 0, 0)),
                  pl.BlockSpec((MLA_HEADS * KV_RANK, MLA_WIDTH), lambda b, i: (0, 0))],
        out_specs=pl.BlockSpec((None, TQ, MLA_WIDTH), lambda b, i: (b, i, 0)),
        out_shape=jax.ShapeDtypeStruct((BATCH, SEQ, MLA_WIDTH), BF16),
        scratch_shapes=[pltpu.VMEM((MLA_HEADS * TQ, LANES), F32),
                        pltpu.VMEM((MLA_HEADS * TQ, V_PAD), F32),
                        pltpu.VMEM((2, MLA_HEADS * TQ, TK), F32),
                        pltpu.VMEM((MLA_HEADS * TQ, TK), BF16),
                        pltpu.VMEM((MLA_HEADS * TQ, LANES), F32)],
        compiler_params=pltpu.CompilerParams(dimension_semantics=("arbitrary", "arbitrary"),
                                             vmem_limit_bytes=40 * 1024 * 1024),
        name="attention",
    )(q, kt, v, wuv)


def _mix_kernel(attn_ref, ga_ref, p_ref, x_ref, woa_ref, wout_ref, gffn_ref, wr_ref, br_ref,
                x1_ref, h2_ref, mf_ref, slot_col_ref, slot_row_ref, c8_ref, loff8_ref):
    a = _dot(attn_ref[...], woa_ref[...])
    mix = (ga_ref[...].astype(F32) * a + p_ref[...].astype(F32)).astype(BF16)
    x1 = x_ref[...] + _dot(mix, wout_ref[...])
    x1_ref[...] = x1
    h2 = _rms(x1, gffn_ref[...])
    h2_ref[...] = h2.astype(BF16)

    hi = h2.astype(BF16)
    lo = (h2 - hi.astype(F32)).astype(BF16)
    r1 = _dot(hi, wr_ref[...])
    r2 = _dot(lo, wr_ref[:, :LANES])
    logits_all = r1[:, :LANES] + r1[:, LANES:] + r2 + br_ref[...]

    for r in range(TMX // TM):
        mf, slots, c8_rows, loff8_rows = _route_tile(logits_all[r * TM:(r + 1) * TM])
        mf_ref[r * TM:(r + 1) * TM, :] = mf
        slot_col_ref[r * TM:(r + 1) * TM, :] = slots.astype(jnp.int32)
        slot_row_ref[8 * r:8 * (r + 1), :] = slots.T[:8].astype(jnp.int32)
        c8_ref[8 * r:8 * (r + 1), :] = c8_rows.astype(jnp.int32)
        loff8_ref[8 * r:8 * (r + 1), :] = loff8_rows.astype(jnp.int32)


def _route_tile(logits):
    lane_i = lax.broadcasted_iota(jnp.int32, (TM, LANES), 1)
    lane = lane_i.astype(F32)
    lane_group = (lane_i >> 3).astype(F32)
    ninf = -jnp.inf
    is_group = (lane_i >= N_EXPERTS) & (lane_i < N_EXPERTS + N_GROUPS)
    lg = jnp.where(is_group, logits, ninf)
    gmax = jnp.max(lg, axis=-1, keepdims=True)
    gsum = jnp.sum(jnp.exp(lg - gmax), axis=-1, keepdims=True)
    p_top = 1.0 / gsum
    g_idx = jnp.min(jnp.where(lg == gmax, lane - N_EXPERTS, float(N_GROUPS)), axis=-1, keepdims=True)
    le = jnp.where((lane_i < N_EXPERTS) & (lane_group == g_idx), logits, ninf)
    t1 = jnp.max(le, axis=-1, keepdims=True)
    e1 = jnp.min(jnp.where(le == t1, lane, float(LANES)), axis=-1, keepdims=True)
    le2 = jnp.where(lane == e1, ninf, le)
    t2 = jnp.max(le2, axis=-1, keepdims=True)
    e2 = jnp.min(jnp.where(le2 == t2, lane, float(LANES)), axis=-1, keepdims=True)
    ex = jnp.exp(t2 - t1)
    w1 = p_top / (1.0 + ex)
    w2 = p_top * ex / (1.0 + ex)

    sel1 = lane == e1
    sel2 = lane == e2
    onehot = jnp.where(sel1 | sel2, 1.0, 0.0)
    rr = lax.broadcasted_iota(jnp.int32, (TM, TM), 0)
    cc = lax.broadcasted_iota(jnp.int32, (TM, TM), 1)
    ltri = jnp.where(cc < rr, 1.0, 0.0).astype(BF16)
    rank = _dot(ltri, onehot.astype(BF16))
    cnt = jnp.sum(onehot, axis=0, keepdims=True)
    c8 = jnp.floor((cnt + (CHUNK_ROWS - 1)) * (1.0 / CHUNK_ROWS))
    ur = lax.broadcasted_iota(jnp.int32, (LANES, LANES), 0)
    uc = lax.broadcasted_iota(jnp.int32, (LANES, LANES), 1)
    upper = jnp.where(ur < uc, 1.0, 0.0).astype(BF16)
    c8_rows = jnp.broadcast_to(c8, (8, LANES))
    loff8_rows = _dot(c8_rows.astype(BF16), upper)
    slot_all = CHUNK_ROWS * loff8_rows[0:1] + rank
    slot1 = jnp.sum(jnp.where(sel1, slot_all, 0.0), axis=-1, keepdims=True)
    slot2 = jnp.sum(jnp.where(sel2, slot_all, 0.0), axis=-1, keepdims=True)

    slots = jnp.where(lane_i == 0, slot1, jnp.where(lane_i == 1, slot2, 0.0))
    return jnp.where(lane_i == 0, w1, w2), slots, c8_rows, loff8_rows


def _mix(attn, ga, p, x, woa, wout, gffn, wr, br):
    nt = N_TOK // TM
    sub = TMX // TM

    def const(shape):
        return pl.BlockSpec(shape, lambda i: (0,) * len(shape))

    def rows(width):
        return pl.BlockSpec((TMX, width), lambda i: (i, 0))

    return pl.pallas_call(
        _mix_kernel,
        grid=(N_TOK // TMX,),
        in_specs=[rows(MLA_WIDTH), rows(D_MODEL), rows(D_MODEL), rows(D_MODEL),
                  const((MLA_WIDTH, D_MODEL)), const((D_MODEL, D_MODEL)), const((1, D_MODEL)),
                  const((D_MODEL, 2 * LANES)), const((1, LANES))],
        out_specs=[rows(D_MODEL), rows(D_MODEL), rows(LANES), rows(LANES),
                   pl.BlockSpec((8 * sub, TM), lambda i: (i, 0)),
                   pl.BlockSpec((8 * sub, LANES), lambda i: (i, 0)),
                   pl.BlockSpec((8 * sub, LANES), lambda i: (i, 0))],
        out_shape=[jax.ShapeDtypeStruct((N_TOK, D_MODEL), F32),
                   jax.ShapeDtypeStruct((N_TOK, D_MODEL), BF16),
                   jax.ShapeDtypeStruct((N_TOK, LANES), F32),
                   jax.ShapeDtypeStruct((N_TOK, LANES), jnp.int32),
                   jax.ShapeDtypeStruct((nt * 8, TM), jnp.int32),
                   jax.ShapeDtypeStruct((nt * 8, LANES), jnp.int32),
                   jax.ShapeDtypeStruct((nt * 8, LANES), jnp.int32)],
        compiler_params=pltpu.CompilerParams(dimension_semantics=("arbitrary",),
                                             vmem_limit_bytes=40 * 1024 * 1024),
        name="mix",
    )(attn, ga, p, x, woa, wout, gffn, wr, br)


def _chunk_copy(src, dst, s8, d8, sem, nchunks=1):
    rows = nchunks * CHUNK_ROWS
    return pltpu.make_async_copy(src.at[pl.ds(pl.multiple_of(s8 * CHUNK_ROWS, CHUNK_ROWS), rows)],
                                 dst.at[pl.ds(pl.multiple_of(d8 * CHUNK_ROWS, CHUNK_ROWS), rows)], sem)


def _unrolled_loop(n, fn):
    groups = n >> CHUNK_UNROLL_SHIFT

    def group(g, carry):
        for u in range(CHUNK_UNROLL):
            fn(g * CHUNK_UNROLL + u)
        return carry

    lax.fori_loop(0, groups, group, 0)

    def single(k, carry):
        fn(k)
        return carry

    lax.fori_loop(groups << CHUNK_UNROLL_SHIFT, n, single, 0)


def _for_each_copy(t, copy_tables, fn):
    npair_ref, nsingle_ref, lpair_ref, gpair_ref, lsingle_ref, gsingle_ref = copy_tables
    pb = t * PAIR_SLOTS
    sb = t * N_EXPERTS
    _unrolled_loop(npair_ref[t], lambda k: fn(lpair_ref[pb + k], gpair_ref[pb + k], 2))
    _unrolled_loop(nsingle_ref[t], lambda k: fn(lsingle_ref[sb + k], gsingle_ref[sb + k], 1))


def _tile_chunks(t, copy_tables):
    return 2 * copy_tables[0][t] + copy_tables[1][t]


def _wait_chunks(n, src, dst, sem):
    groups = n >> CHUNK_UNROLL_SHIFT
    rows = CHUNK_UNROLL * CHUNK_ROWS

    def group(g, carry):
        pltpu.make_async_copy(src.at[pl.ds(0, rows)], dst.at[pl.ds(0, rows)], sem).wait()
        return carry

    lax.fori_loop(0, groups, group, 0)

    def single(k, carry):
        _chunk_copy(src, dst, 0, 0, sem).wait()
        return carry

    lax.fori_loop(groups << CHUNK_UNROLL_SHIFT, n, single, 0)


def _dispatch_kernel(npair_ref, nsingle_ref, lpair_ref, gpair_ref, lsingle_ref, gsingle_ref,
                     zstart8_ref, zcnt8_ref, nvb_ref,
                     slot_ref, h2_ref, xs_hbm, sbuf, zbuf, sem, zsem):
    copy_tables = (npair_ref, nsingle_ref, lpair_ref, gpair_ref, lsingle_ref, gsingle_ref)
    t = pl.program_id(0)
    last = pl.num_programs(0) - 1
    cur = t % 2

    def zero_chunk_copy(d8):
        return _chunk_copy(zbuf, xs_hbm, 0, d8, zsem)

    def zero_block_copy(b):
        return pltpu.make_async_copy(
            zbuf, xs_hbm.at[pl.ds(pl.multiple_of(b * MOE_T, MOE_T), MOE_T)], zsem)

    def for_each_zero(chunk_fn, block_fn):
        def per_expert(e, carry):
            def per_chunk(j, carry2):
                chunk_fn(zstart8_ref[e] + j)
                return carry2

            lax.fori_loop(0, zcnt8_ref[e], per_chunk, 0)
            return carry

        lax.fori_loop(0, N_EXPERTS, per_expert, 0)

        def per_block(b, carry):
            block_fn(b)
            return carry

        lax.fori_loop(nvb_ref[0], MOE_BLOCKS, per_block, 0)

    @pl.when(t == 0)
    def _():
        zbuf[...] = jnp.zeros(zbuf.shape, F32)
        for_each_zero(lambda d8: zero_chunk_copy(d8).start(), lambda b: zero_block_copy(b).start())

    row = lax.broadcasted_iota(jnp.int32, (TILE_ROWS, TM), 0)
    slots = slot_ref[...]
    perm = jnp.where((row == slots[0:1, :]) | (row == slots[1:2, :]), 1.0, 0.0).astype(BF16)
    sbuf[cur] = _dot(perm, h2_ref[...])
    _for_each_copy(t, copy_tables,
                   lambda lc, gc, n: _chunk_copy(sbuf.at[cur], xs_hbm, lc, gc, sem.at[cur], n).start())

    def wait_tile(tile, slot):
        _wait_chunks(_tile_chunks(tile, copy_tables), sbuf.at[slot], xs_hbm, sem.at[slot])

    @pl.when(t > 0)
    def _():
        wait_tile(t - 1, 1 - cur)

    @pl.when(t == last)
    def _():
        wait_tile(t, cur)
        for_each_zero(lambda d8: zero_chunk_copy(d8).wait(), lambda b: zero_block_copy(b).wait())


def _dispatch(tables, slot_rows, h2):
    return pl.pallas_call(
        _dispatch_kernel,
        grid_spec=pltpu.PrefetchScalarGridSpec(
            num_scalar_prefetch=len(tables),
            grid=(N_TOK // TM,),
            in_specs=[pl.BlockSpec((8, TM), lambda i, *_: (i, 0)),
                      pl.BlockSpec((TM, D_MODEL), lambda i, *_: (i, 0))],
            out_specs=pl.BlockSpec(memory_space=pl.ANY),
            scratch_shapes=[pltpu.VMEM((2, TILE_ROWS, D_MODEL), F32), pltpu.VMEM((MOE_T, D_MODEL), F32),
                            pltpu.SemaphoreType.DMA((2,)), pltpu.SemaphoreType.DMA(())]),
        out_shape=jax.ShapeDtypeStruct((MOE_ROWS, D_MODEL), F32),
        compiler_params=pltpu.CompilerParams(dimension_semantics=("arbitrary",)),
        name="dispatch",
    )(*tables, slot_rows, h2)


def _expert_kernel(blk_e_ref, nvb_ref, xs_ref, wg_ref, wu_ref, wd_ref, ys_ref, wgb, wub, wdb):
    i = pl.program_id(0)
    live = i < nvb_ref[0]

    @pl.when(live & ((i == 0) | (blk_e_ref[i] != blk_e_ref[jnp.maximum(i - 1, 0)])))
    def _():
        wgb[...] = wg_ref[...].astype(BF16)
        wub[...] = wu_ref[...].astype(BF16)
        wdb[...] = wd_ref[...].astype(BF16)

    @pl.when(live)
    def _():
        xb = xs_ref[...].astype(BF16)
        g = _dot(xb, wgb[...])
        u = _dot(xb, wub[...])
        hid = (jax.nn.silu(g) * u).astype(BF16)
        ys_ref[...] = _dot(hid, wdb[...])

    @pl.when(jnp.logical_not(live))
    def _():
        ys_ref[...] = jnp.zeros(ys_ref.shape, ys_ref.dtype)


def _experts(blk_e, nvb, xs, wg, wu, wd):
    def row_block(i, be, nv):
        return (jnp.minimum(i, nv[0] - 1), 0)

    def weight(i, be, nv):
        return (be[jnp.minimum(i, nv[0] - 1)], 0, 0)

    return pl.pallas_call(
        _expert_kernel,
        grid_spec=pltpu.PrefetchScalarGridSpec(
            num_scalar_prefetch=2,
            grid=(MOE_BLOCKS,),
            in_specs=[pl.BlockSpec((MOE_T, D_MODEL), row_block),
                      pl.BlockSpec((None, D_MODEL, D_EXPERT), weight),
                      pl.BlockSpec((None, D_MODEL, D_EXPERT), weight),
                      pl.BlockSpec((None, D_EXPERT, D_MODEL), weight)],
            out_specs=pl.BlockSpec((MOE_T, D_MODEL), lambda i, be, nv: (i, 0)),
            scratch_shapes=[pltpu.VMEM((D_MODEL, D_EXPERT), BF16), pltpu.VMEM((D_MODEL, D_EXPERT), BF16),
                            pltpu.VMEM((D_EXPERT, D_MODEL), BF16)]),
        out_shape=jax.ShapeDtypeStruct((MOE_ROWS, D_MODEL), F32),
        compiler_params=pltpu.CompilerParams(dimension_semantics=("arbitrary",),
                                             vmem_limit_bytes=40 * 1024 * 1024),
        name="experts",
    )(blk_e, nvb, xs, wg, wu, wd)


def _combine_kernel(npair_ref, nsingle_ref, lpair_ref, gpair_ref, lsingle_ref, gsingle_ref,
                    slot_ref, x1_ref, mf_ref, gfin_ref, ys_hbm, o_ref, ybuf, sem):
    copy_tables = (npair_ref, nsingle_ref, lpair_ref, gpair_ref, lsingle_ref, gsingle_ref)
    t = pl.program_id(0)
    cur = t % 2

    def fetch(tile, slot):
        _for_each_copy(tile, copy_tables,
                       lambda lc, gc, n: _chunk_copy(ys_hbm, ybuf.at[slot], gc, lc, sem.at[slot], n).start())

    @pl.when(t == 0)
    def _():
        ybuf[...] = jnp.zeros(ybuf.shape, F32)
        fetch(0, 0)

    @pl.when(t + 1 < pl.num_programs(0))
    def _():
        fetch(t + 1, 1 - cur)

    _wait_chunks(_tile_chunks(t, copy_tables), ys_hbm, ybuf.at[cur], sem.at[cur])

    yb = ybuf[cur].astype(BF16)
    col = lax.broadcasted_iota(jnp.int32, (TM, TILE_ROWS), 1)
    slots = slot_ref[...]
    y1 = _dot(jnp.where(col == slots[:, 0:1], 1.0, 0.0).astype(BF16), yb)
    y2 = _dot(jnp.where(col == slots[:, 1:2], 1.0, 0.0).astype(BF16), yb)
    mf = mf_ref[...]
    x2 = x1_ref[...] + mf[:, 0:1] * y1 + mf[:, 1:2] * y2
    o_ref[...] = _rms(x2, gfin_ref[...])


def _combine(tables, slot_cols, x1, mf, gfin, ys):
    return pl.pallas_call(
        _combine_kernel,
        grid_spec=pltpu.PrefetchScalarGridSpec(
            num_scalar_prefetch=len(tables),
            grid=(N_TOK // TM,),
            in_specs=[pl.BlockSpec((TM, LANES), lambda i, *_: (i, 0)),
                      pl.BlockSpec((TM, D_MODEL), lambda i, *_: (i, 0)),
                      pl.BlockSpec((TM, LANES), lambda i, *_: (i, 0)),
                      pl.BlockSpec((1, D_MODEL), lambda i, *_: (0, 0)),
                      pl.BlockSpec(memory_space=pl.ANY)],
            out_specs=pl.BlockSpec((TM, D_MODEL), lambda i, *_: (i, 0)),
            scratch_shapes=[pltpu.VMEM((2, TILE_ROWS, D_MODEL), F32), pltpu.SemaphoreType.DMA((2,))]),
        out_shape=jax.ShapeDtypeStruct((N_TOK, D_MODEL), F32),
        compiler_params=pltpu.CompilerParams(dimension_semantics=("arbitrary",)),
        name="combine",
    )(*tables, slot_cols, x1, mf, gfin, ys)


def kernel(x, positions, g_attn_norm, w_in, g_cq, w_uq, g_ckv, w_uk, w_uv, w_o_attn, sgu_gain, sgu_bias, w_spatial, b_spatial, w_o_sgu, w_out, g_ffn_norm, w_router_group, b_router_group, w_router_expert, b_router_expert, w_exp_gate, w_exp_up, w_exp_down, g_final):
    assert x.shape == (BATCH, SEQ, D_MODEL) and w_in.shape[0] == 1
    half = QK_ROPE // 2
    swap = jnp.concatenate([jnp.arange(half, QK_ROPE), jnp.arange(0, half)])

    def pad_cols(w, width):
        return jnp.pad(w, ((0, 0), (0, width - w.shape[1])))

    wi = w_in[0]
    c0 = Q_RANK + KV_RANK
    kr = wi[:, c0:c0 + QK_ROPE]
    c1 = c0 + QK_ROPE
    win = jnp.concatenate([
        wi[:, :c0], pad_cols(kr, LANES), pad_cols(kr[:, swap], LANES), wi[:, c1:]], axis=1).astype(BF16)

    wq = w_uq[0].reshape(Q_RANK, MLA_HEADS, QK_NOPE + QK_ROPE)
    wq_nope = wq[:, :, :QK_NOPE].transpose(1, 0, 2)
    wq_rope = wq[:, :, QK_NOPE:]
    w_ukt = w_uk[0].reshape(KV_RANK, MLA_HEADS, QK_NOPE).transpose(1, 2, 0)
    wqlat = _fold(wq_nope, w_ukt).transpose(1, 0, 2).reshape(Q_RANK, MLA_HEADS * LANES).astype(BF16)

    def rope_cols(w):
        return jnp.pad(w, ((0, 0), (0, 0), (0, LANES - QK_ROPE))).reshape(Q_RANK, MLA_HEADS * LANES).astype(BF16)

    wqr = rope_cols(wq_rope)
    wqrs = rope_cols(wq_rope[:, :, swap])

    per_row = LANES // half
    freqs = ROPE_THETA ** (-jnp.arange(0, QK_ROPE, 2, dtype=F32) / QK_ROPE)
    pos8 = jnp.repeat(positions.astype(F32).reshape(N_TOK // per_row, per_row), half, axis=1)
    cos8, sin8 = _rope_tables(pos8, jnp.tile(freqs, per_row)[None, :])
    cos16 = cos8.reshape(N_TOK, half)
    sin16 = sin8.reshape(N_TOK, half)
    cos_t = pad_cols(jnp.concatenate([cos16, cos16], axis=1), LANES)
    sin_t = pad_cols(jnp.concatenate([-sin16, sin16], axis=1), LANES)

    head_of_col = jnp.arange(MLA_WIDTH) // V_HEAD
    wuv = jnp.where(head_of_col[None, None, :] == jnp.arange(MLA_HEADS)[:, None, None],
                    w_uv[0][None], 0.0).astype(BF16)
    wuv = wuv.reshape(MLA_HEADS * KV_RANK, MLA_WIDTH)

    wsp = w_spatial[0].astype(BF16)
    bs = b_spatial[0]
    bsp = jnp.repeat(bs.reshape(SGU_GROUPS // 2, 2, SGU_LEN).transpose(0, 2, 1), SGU_GROUP_DIM, axis=2)

    wr32 = jnp.concatenate([w_router_expert[0].transpose(1, 0, 2).reshape(D_MODEL, N_EXPERTS),
                            w_router_group[0]], axis=1)
    wr32 = pad_cols(wr32, LANES)
    wr_hi = wr32.astype(BF16)
    wr_lo = (wr32 - wr_hi.astype(F32)).astype(BF16)
    wr = jnp.concatenate([wr_hi, wr_lo], axis=1)
    br = pad_cols(jnp.concatenate([b_router_expert[0].reshape(-1), b_router_group[0]])[None, :], LANES)

    xf = x.reshape(N_TOK, D_MODEL)
    q, kt, v, p, ga = _inproj(
        xf, cos_t, sin_t, g_attn_norm, win, g_cq, g_ckv, wqlat, wqr, wqrs,
        sgu_gain, sgu_bias, wsp, bsp, w_o_sgu[0].astype(BF16))
    attn = _attention(q, kt, v, wuv)
    x1, h2, mf, slot_cols, slot_rows, c8_rows, loff8_rows = _mix(
        attn.reshape(N_TOK, MLA_WIDTH), ga, p, xf, w_o_attn[0].astype(BF16), w_out[0].astype(BF16),
        g_ffn_norm, wr, br)

    blk8 = MOE_T // CHUNK_ROWS
    c8 = c8_rows[::8, :N_EXPERTS]
    loff8 = loff8_rows[::8, :N_EXPERTS]
    tot8 = jnp.sum(c8, axis=0)
    pad8 = (tot8 + blk8 - 1) // blk8 * blk8
    gend8 = jnp.cumsum(pad8)
    gstart8 = gend8 - pad8
    gbase8 = gstart8[None, :] + jnp.cumsum(c8, axis=0) - c8
    nvb = (gend8[-1:] // blk8).astype(jnp.int32)
    blk_start8 = jnp.arange(MOE_BLOCKS, dtype=jnp.int32) * blk8
    blk_e = jnp.minimum(jnp.sum((gend8[None, :] <= blk_start8[:, None]).astype(jnp.int32), axis=1),
                        N_EXPERTS - 1)
    def flat_list(count, slots, first_local, first_global, step):
        off = jnp.cumsum(count, axis=1) - count
        k = jnp.arange(slots, dtype=jnp.int32)
        in_run = (off[:, None, :] <= k[None, :, None]) & (k[None, :, None] < (off + count)[:, None, :])

        def pick(first):
            return (jnp.sum(jnp.where(in_run, (first - step * off)[:, None, :], 0), axis=-1)
                    + step * k[None, :]).reshape(-1).astype(jnp.int32)

        return jnp.sum(count, axis=1).astype(jnp.int32), pick(first_local), pick(first_global)

    npair, lpair, gpair = flat_list(c8 // 2, PAIR_SLOTS, loff8, gbase8, 2)
    nsingle, lsingle, gsingle = flat_list(c8 % 2, N_EXPERTS, loff8 + c8 - 1, gbase8 + c8 - 1, 0)
    run_tables = (npair, nsingle, lpair, gpair, lsingle, gsingle)

    xs = _dispatch(run_tables + ((gstart8 + tot8).astype(jnp.int32), (pad8 - tot8).astype(jnp.int32), nvb),
                   slot_rows, h2)
    ys = _experts(blk_e, nvb, xs, w_exp_gate[0], w_exp_up[0], w_exp_down[0])
    out = _combine(run_tables, slot_cols, x1, mf, g_final.reshape(1, D_MODEL), ys)
    return out.reshape(BATCH, SEQ, D_MODEL)
```

```python
import jax
import jax.numpy as jnp
from jax import lax
from jax.experimental import pallas as pl
from jax.experimental.pallas import tpu as pltpu

D_MODEL = 1024
BATCH = 2
SEQ = 8192
N_TOK = BATCH * SEQ
CHUNK = 64
EPS = 1e-6
MLA_HEADS = 8
Q_RANK = 256
KV_RANK = 128
QK_NOPE = 64
QK_ROPE = 32
V_HEAD = 64
MLA_WIDTH = MLA_HEADS * V_HEAD
ROPE_THETA = 10000.0
SGU_GROUPS = 8
SGU_GROUP_DIM = 64
SGU_WIDTH = SGU_GROUPS * SGU_GROUP_DIM
SGU_LEN = 128
N_GROUPS = 4
EXPERTS_PER_GROUP = 8
N_EXPERTS = N_GROUPS * EXPERTS_PER_GROUP
TOP_K = 2
D_EXPERT = 256

LANES = 128
QK_PAD = 2 * LANES
V_PAD = 2 * LANES
SCALE = (QK_NOPE + QK_ROPE) ** -0.5
LOG2E = 1.4426950408889634
Q_SCALE = SCALE * LOG2E
NEG = -1e30

TM = 256
TMI = 512
TMX = 512
TQ = 128
TK = 512
ATTN_UNROLL_SHIFT = 1
ATTN_UNROLL = 1 << ATTN_UNROLL_SHIFT
TK_SHIFT = TK.bit_length() - 1
CHUNK_SHIFT = CHUNK.bit_length() - 1
assert 1 << TK_SHIFT == TK and 1 << CHUNK_SHIFT == CHUNK
MOE_T = 512
CHUNK_ROWS = 8
TILE_ROWS = -(-(TOP_K * TM + N_EXPERTS * (CHUNK_ROWS - 1) + CHUNK_ROWS) // 256) * 256
TILE_CHUNKS = TILE_ROWS // CHUNK_ROWS
PAIR_SLOTS = TILE_CHUNKS // 2
CHUNK_UNROLL_SHIFT = 2
CHUNK_UNROLL = 1 << CHUNK_UNROLL_SHIFT
MOE_ROWS_MAX = (N_TOK * TOP_K + (N_TOK // TM) * N_EXPERTS * (CHUNK_ROWS - 1)
                + N_EXPERTS * (MOE_T - CHUNK_ROWS))
MOE_BLOCKS = -(-MOE_ROWS_MAX // MOE_T)
MOE_ROWS = MOE_BLOCKS * MOE_T

C_Q = 0
C_KV = C_Q + Q_RANK
C_KR = C_KV + KV_RANK
C_KRS = C_KR + LANES
C_U = C_KRS + LANES
C_V = C_U + SGU_WIDTH
C_GA = C_V + SGU_WIDTH
C_GB = C_GA + D_MODEL
C_END = C_GB + D_MODEL

F32 = jnp.float32
BF16 = jnp.bfloat16


def _dot(a, b):
    return jnp.dot(a, b, preferred_element_type=F32)


def _rms(x, g):
    return x * lax.rsqrt(jnp.mean(x * x, axis=-1, keepdims=True) + EPS) * g


def _fold_kernel(a_ref, b_ref, o_ref):
    o_ref[...] = Q_SCALE * jnp.dot(a_ref[...], b_ref[...], preferred_element_type=F32,
                                 precision=lax.Precision.HIGHEST)


def _fold(w_uq_nope, w_ukt):
    return pl.pallas_call(
        _fold_kernel,
        grid=(MLA_HEADS,),
        in_specs=[pl.BlockSpec((None, Q_RANK, QK_NOPE), lambda h: (h, 0, 0)),
                  pl.BlockSpec((None, QK_NOPE, KV_RANK), lambda h: (h, 0, 0))],
        out_specs=pl.BlockSpec((None, Q_RANK, KV_RANK), lambda h: (h, 0, 0)),
        out_shape=jax.ShapeDtypeStruct((MLA_HEADS, Q_RANK, KV_RANK), F32),
        name="fold",
    )(w_uq_nope, w_ukt)


def _rope_kernel(pos_ref, freq_ref, cos_ref, sin_ref):
    ang = pos_ref[...] * freq_ref[...]
    cos_ref[...] = jnp.cos(ang)
    sin_ref[...] = jnp.sin(ang)


def _rope_tables(pos8, freq8):
    rows = pos8.shape[0]
    blk = pl.BlockSpec((TM, LANES), lambda i: (i, 0))
    return pl.pallas_call(
        _rope_kernel,
        grid=(rows // TM,),
        in_specs=[blk, pl.BlockSpec((1, LANES), lambda i: (0, 0))],
        out_specs=[blk, blk],
        out_shape=[jax.ShapeDtypeStruct((rows, LANES), F32)] * 2,
        name="rope_tables",
    )(pos8, freq8)


def _inproj_kernel(x_ref, cos_ref, sin_ref, gattn_ref, win_ref, gcq_ref, gckv_ref, wqlat_ref, wqr_ref,
                   wqrs_ref, sgain_ref, sbias_ref, wsp_ref, bsp_ref, wosgu_ref,
                   q_ref, kt_ref, v_ref, p_ref, ga_ref):
    hb = _rms(x_ref[...], gattn_ref[...]).astype(BF16)

    def proj(a, b):
        return _dot(hb, win_ref[:, a:b])

    cos_t = cos_ref[...]
    sin_t = sin_ref[...]

    cqn = _rms(proj(C_Q, C_KV), gcq_ref[...]).astype(BF16)
    qlat = _dot(cqn, wqlat_ref[...])
    qr = _dot(cqn, wqr_ref[...])
    qs = _dot(cqn, wqrs_ref[...])
    cos_q = cos_t * Q_SCALE
    sin_q = sin_t * Q_SCALE
    for r in range(TMI // TQ):
        t0, t1 = r * TQ, (r + 1) * TQ
        for h in range(MLA_HEADS):
            a, b = h * LANES, (h + 1) * LANES
            q_ref[r, h * TQ:(h + 1) * TQ, :LANES] = qlat[t0:t1, a:b].astype(BF16)
            q_ref[r, h * TQ:(h + 1) * TQ, LANES:] = (
                qr[t0:t1, a:b] * cos_q[t0:t1] + qs[t0:t1, a:b] * sin_q[t0:t1]).astype(BF16)

    zk = proj(C_KV, C_U)
    ckvn = _rms(zk[:, :KV_RANK], gckv_ref[...])
    krope = zk[:, KV_RANK:KV_RANK + LANES] * cos_t + zk[:, KV_RANK + LANES:] * sin_t
    kt_ref[...] = jnp.concatenate([ckvn, krope], axis=1).T.astype(BF16)
    v_ref[...] = jnp.concatenate([ckvn, jnp.ones_like(ckvn)], axis=1).astype(BF16)

    u = jax.nn.gelu(proj(C_U, C_V))
    v = jax.nn.gelu(proj(C_V, C_GA))
    mu = jnp.mean(v, axis=-1, keepdims=True)
    vc = v - mu
    var = jnp.mean(vc * vc, axis=-1, keepdims=True)
    vb = (vc * lax.rsqrt(var + EPS) * sgain_ref[...] + sbias_ref[...]).astype(BF16)
    row = lax.broadcasted_iota(jnp.int32, (SGU_LEN, SGU_LEN), 0)
    col = lax.broadcasted_iota(jnp.int32, (SGU_LEN, SGU_LEN), 1)
    causal = (row >> CHUNK_SHIFT) >= (col >> CHUNK_SHIFT)
    low_half = col < SGU_GROUP_DIM
    zero_w = jnp.zeros((SGU_LEN, SGU_LEN), BF16)
    w_pairs = [jnp.concatenate([jnp.where(causal, wsp_ref[2 * p], zero_w),
                                jnp.where(causal, wsp_ref[2 * p + 1], zero_w)], axis=1)
               for p in range(SGU_WIDTH // LANES)]
    row_blocks = []
    for r in range(TMI // SGU_LEN):
        pieces = []
        for p in range(SGU_WIDTH // LANES):
            blk = vb[r * SGU_LEN:(r + 1) * SGU_LEN, p * LANES:(p + 1) * LANES]
            stacked = jnp.concatenate([jnp.where(low_half, blk, zero_w), jnp.where(low_half, zero_w, blk)], axis=0)
            sv = _dot(w_pairs[p], stacked) + bsp_ref[p]
            pieces.append((u[r * SGU_LEN:(r + 1) * SGU_LEN, p * LANES:(p + 1) * LANES] * sv).astype(BF16))
        row_blocks.append(jnp.concatenate(pieces, axis=1))
    sgu = _dot(jnp.concatenate(row_blocks, axis=0), wosgu_ref[...])

    ga_ref[...] = jax.nn.sigmoid(proj(C_GA, C_GB)).astype(BF16)
    p_ref[...] = (jax.nn.sigmoid(proj(C_GB, C_END)) * sgu).astype(BF16)


def _inproj(x, cos_t, sin_t, gattn, win, gcq, gckv, wqlat, wqr, wqrs, sgain, sbias, wsp, bsp, wosgu):
    nt = N_TOK // TMI
    per_b = SEQ // TMI
    per_k = TK // TMI

    def const(shape):
        return pl.BlockSpec(shape, lambda i: (0,) * len(shape))

    return pl.pallas_call(
        _inproj_kernel,
        grid=(nt,),
        in_specs=[pl.BlockSpec((TMI, D_MODEL), lambda i: (i, 0)),
                  pl.BlockSpec((TMI, LANES), lambda i: (i, 0)), pl.BlockSpec((TMI, LANES), lambda i: (i, 0)),
                  const((1, D_MODEL)), const((D_MODEL, C_END)), const((1, Q_RANK)), const((1, KV_RANK)),
                  const((Q_RANK, MLA_HEADS * LANES)), const((Q_RANK, MLA_HEADS * LANES)),
                  const((Q_RANK, MLA_HEADS * LANES)),
                  const((1, SGU_WIDTH)), const((1, SGU_WIDTH)),
                  const((SGU_GROUPS, SGU_LEN, SGU_LEN)), const((SGU_GROUPS // 2, SGU_LEN, LANES)),
                  const((SGU_WIDTH, D_MODEL))],
        out_specs=[pl.BlockSpec((TMI // TQ, MLA_HEADS * TQ, QK_PAD), lambda i: (i, 0, 0)),
                   pl.BlockSpec((None, None, QK_PAD, TMI),
                                lambda i: (i // per_b, (i % per_b) // per_k, 0, i % per_k)),
                   pl.BlockSpec((None, None, TMI, V_PAD),
                                lambda i: (i // per_b, (i % per_b) // per_k, i % per_k, 0)),
                   pl.BlockSpec((TMI, D_MODEL), lambda i: (i, 0)),
                   pl.BlockSpec((TMI, D_MODEL), lambda i: (i, 0))],
        out_shape=[jax.ShapeDtypeStruct((N_TOK // TQ, MLA_HEADS * TQ, QK_PAD), BF16),
                   jax.ShapeDtypeStruct((BATCH, SEQ // TK, QK_PAD, TK), BF16),
                   jax.ShapeDtypeStruct((BATCH, SEQ // TK, TK, V_PAD), BF16),
                   jax.ShapeDtypeStruct((N_TOK, D_MODEL), BF16),
                   jax.ShapeDtypeStruct((N_TOK, D_MODEL), BF16)],
        compiler_params=pltpu.CompilerParams(dimension_semantics=("arbitrary",),
                                             vmem_limit_bytes=56 * 1024 * 1024),
        name="inproj",
    )(x, cos_t, sin_t, gattn, win, gcq, gckv, wqlat, wqr, wqrs, sgain, sbias, wsp, bsp, wosgu)


def _attn_kernel(q_ref, kt_ref, v_ref, wuv_ref, o_ref, m_ref, acc_ref, s_ref, p_ref, a_ref):
    qi = pl.program_id(1)
    m_ref[...] = jnp.full(m_ref.shape, NEG, F32)
    acc_ref[...] = jnp.zeros(acc_ref.shape, F32)

    def scores(j, slot):
        s_ref[slot] = _dot(q_ref[...], kt_ref[j])

    def update(j, slot, width=None):
        masked = width is not None
        w = width if masked else TK
        if masked:
            q_chunk = ((qi * TQ - j * TK) >> CHUNK_SHIFT) + (
                lax.broadcasted_iota(jnp.int32, (TQ, w), 0) >> CHUNK_SHIFT)
            k_chunk = lax.broadcasted_iota(jnp.int32, (TQ, w), 1) >> CHUNK_SHIFT
            visible = k_chunk <= q_chunk
        for h in range(MLA_HEADS):
            rows = slice(h * TQ, (h + 1) * TQ)
            s = s_ref[slot, rows, :w]
            if masked:
                s = jnp.where(visible, s, NEG)
            m_prev = m_ref[rows, :]
            m_new = jnp.maximum(m_prev, jnp.max(s, axis=-1, keepdims=True))
            m_ref[rows, :] = m_new
            a_ref[rows, :] = jnp.exp2(m_prev - m_new)
            s = s_ref[slot, rows, :w]
            if masked:
                s = jnp.where(visible, s, NEG)
            p_ref[rows, :w] = jnp.exp2(s - jnp.concatenate([m_new] * (w // LANES), axis=1)).astype(BF16)
        alpha = a_ref[...]
        acc_ref[...] = jnp.concatenate([alpha, alpha], axis=1) * acc_ref[...] + _dot(
            p_ref[:, :w], v_ref[j, :w, :])

    diag = (qi * TQ) >> TK_SHIFT
    scores(0, 0)

    def run(first, count, last_masked):
        for u in range(count):
            if not (last_masked and u == count - 1):
                scores(first + u + 1, (u + 1) % 2)
                update(first + u, u % 2)
            else:
                for sub in range(TK // TQ):
                    @pl.when((qi & (TK // TQ - 1)) == sub)
                    def _(u=u, sub=sub):
                        update(first + u, u % 2, (sub + 1) * TQ)

    def body(t, carry):
        run(ATTN_UNROLL * t, ATTN_UNROLL, False)
        return carry

    trips = diag >> ATTN_UNROLL_SHIFT
    lax.fori_loop(0, trips, body, 0)
    done = trips << ATTN_UNROLL_SHIFT
    for r in range(ATTN_UNROLL):
        @pl.when(diag - done == r)
        def _(r=r):
            run(done, r + 1, True)

    o_lat = (acc_ref[:, :KV_RANK] / acc_ref[:, KV_RANK:]).astype(BF16)
    o_cat = jnp.concatenate([o_lat[h * TQ:(h + 1) * TQ] for h in range(MLA_HEADS)], axis=1)
    o_ref[...] = _dot(o_cat, wuv_ref[...]).astype(BF16)


def _attention(q, kt, v, wuv):
    nk = SEQ // TK
    return pl.pallas_call(
        _attn_kernel,
        grid=(BATCH, SEQ // TQ),
        in_specs=[pl.BlockSpec((None, MLA_HEADS * TQ, QK_PAD), lambda b, i: (b * (SEQ // TQ) + i, 0, 0)),
                  pl.BlockSpec((None, nk, QK_PAD, TK), lambda b, i: (b, 0, 0, 0)),
                  pl.BlockSpec((None, nk, TK, V_PAD), lambda b, i: (b, 0, 0, 0)),
                  pl.BlockSpec((MLA_HEADS * KV_RANK, MLA_WIDTH), lambda b, i: (0, 0))],
        out_specs=pl.BlockSpec((None, TQ, MLA_WIDTH), lambda b, i: (b, i, 0)),
        out_shape=jax.ShapeDtypeStruct((BATCH, SEQ, MLA_WIDTH), BF16),
        scratch_shapes=[pltpu.VMEM((MLA_HEADS * TQ, LANES), F32),
                        pltpu.VMEM((MLA_HEADS * TQ, V_PAD), F32),
                        pltpu.VMEM((2, MLA_HEADS * TQ, TK), F32),
                        pltpu.VMEM((MLA_HEADS * TQ, TK), BF16),
                        pltpu.VMEM((MLA_HEADS * TQ, LANES), F32)],
        compiler_params=pltpu.CompilerParams(dimension_semantics=("arbitrary", "arbitrary"),
                                             vmem_limit_bytes=40 * 1024 * 1024),
        name="attention",
    )(q, kt, v, wuv)


def _mix_kernel(attn_ref, ga_ref, p_ref, x_ref, woa_ref, wout_ref, gffn_ref, wr_ref, br_ref,
                x1_ref, h2_ref, mf_ref, slot_col_ref, slot_row_ref, c8_ref, loff8_ref):
    a = _dot(attn_ref[...], woa_ref[...])
    mix = (ga_ref[...].astype(F32) * a + p_ref[...].astype(F32)).astype(BF16)
    x1 = x_ref[...] + _dot(mix, wout_ref[...])
    x1_ref[...] = x1
    h2 = _rms(x1, gffn_ref[...])
    h2_ref[...] = h2.astype(BF16)

    hi = h2.astype(BF16)
    lo = (h2 - hi.astype(F32)).astype(BF16)
    r1 = _dot(hi, wr_ref[...])
    r2 = _dot(lo, wr_ref[:, :LANES])
    logits_all = r1[:, :LANES] + r1[:, LANES:] + r2 + br_ref[...]

    for r in range(TMX // TM):
        mf, slots, c8_rows, loff8_rows = _route_tile(logits_all[r * TM:(r + 1) * TM])
        mf_ref[r * TM:(r + 1) * TM, :] = mf
        slot_col_ref[r * TM:(r + 1) * TM, :] = slots.astype(jnp.int32)
        slot_row_ref[8 * r:8 * (r + 1), :] = slots.T[:8].astype(jnp.int32)
        c8_ref[8 * r:8 * (r + 1), :] = c8_rows.astype(jnp.int32)
        loff8_ref[8 * r:8 * (r + 1), :] = loff8_rows.astype(jnp.int32)


def _route_tile(logits):
    lane_i = lax.broadcasted_iota(jnp.int32, (TM, LANES), 1)
    lane = lane_i.astype(F32)
    lane_group = (lane_i >> 3).astype(F32)
    ninf = -jnp.inf
    is_group = (lane_i >= N_EXPERTS) & (lane_i < N_EXPERTS + N_GROUPS)
    lg = jnp.where(is_group, logits, ninf)
    gmax = jnp.max(lg, axis=-1, keepdims=True)
    gsum = jnp.sum(jnp.exp(lg - gmax), axis=-1, keepdims=True)
    p_top = 1.0 / gsum
    g_idx = jnp.min(jnp.where(lg == gmax, lane - N_EXPERTS, float(N_GROUPS)), axis=-1, keepdims=True)
    le = jnp.where((lane_i < N_EXPERTS) & (lane_group == g_idx), logits, ninf)
    t1 = jnp.max(le, axis=-1, keepdims=True)
    e1 = jnp.min(jnp.where(le == t1, lane, float(LANES)), axis=-1, keepdims=True)
    le2 = jnp.where(lane == e1, ninf, le)
    t2 = jnp.max(le2, axis=-1, keepdims=True)
    e2 = jnp.min(jnp.where(le2 == t2, lane, float(LANES)), axis=-1, keepdims=True)
    ex = jnp.exp(t2 - t1)
    w1 = p_top / (1.0 + ex)
    w2 = p_top * ex / (1.0 + ex)

    sel1 = lane == e1
    sel2 = lane == e2
    onehot = jnp.where(sel1 | sel2, 1.0, 0.0)
    rr = lax.broadcasted_iota(jnp.int32, (TM, TM), 0)
    cc = lax.broadcasted_iota(jnp.int32, (TM, TM), 1)
    ltri = jnp.where(cc < rr, 1.0, 0.0).astype(BF16)
    rank = _dot(ltri, onehot.astype(BF16))
    cnt = jnp.sum(onehot, axis=0, keepdims=True)
    c8 = jnp.floor((cnt + (CHUNK_ROWS - 1)) * (1.0 / CHUNK_ROWS))
    ur = lax.broadcasted_iota(jnp.int32, (LANES, LANES), 0)
    uc = lax.broadcasted_iota(jnp.int32, (LANES, LANES), 1)
    upper = jnp.where(ur < uc, 1.0, 0.0).astype(BF16)
    c8_rows = jnp.broadcast_to(c8, (8, LANES))
    loff8_rows = _dot(c8_rows.astype(BF16), upper)
    slot_all = CHUNK_ROWS * loff8_rows[0:1] + rank
    slot1 = jnp.sum(jnp.where(sel1, slot_all, 0.0), axis=-1, keepdims=True)
    slot2 = jnp.sum(jnp.where(sel2, slot_all, 0.0), axis=-1, keepdims=True)

    slots = jnp.where(lane_i == 0, slot1, jnp.where(lane_i == 1, slot2, 0.0))
    return jnp.where(lane_i == 0, w1, w2), slots, c8_rows, loff8_rows


def _mix(attn, ga, p, x, woa, wout, gffn, wr, br):
    nt = N_TOK // TM
    sub = TMX // TM

    def const(shape):
        return pl.BlockSpec(shape, lambda i: (0,) * len(shape))

    def rows(width):
        return pl.BlockSpec((TMX, width), lambda i: (i, 0))

    return pl.pallas_call(
        _mix_kernel,
        grid=(N_TOK // TMX,),
        in_specs=[rows(MLA_WIDTH), rows(D_MODEL), rows(D_MODEL), rows(D_MODEL),
                  const((MLA_WIDTH, D_MODEL)), const((D_MODEL, D_MODEL)), const((1, D_MODEL)),
                  const((D_MODEL, 2 * LANES)), const((1, LANES))],
        out_specs=[rows(D_MODEL), rows(D_MODEL), rows(LANES), rows(LANES),
                   pl.BlockSpec((8 * sub, TM), lambda i: (i, 0)),
                   pl.BlockSpec((8 * sub, LANES), lambda i: (i, 0)),
                   pl.BlockSpec((8 * sub, LANES), lambda i: (i, 0))],
        out_shape=[jax.ShapeDtypeStruct((N_TOK, D_MODEL), F32),
                   jax.ShapeDtypeStruct((N_TOK, D_MODEL), BF16),
                   jax.ShapeDtypeStruct((N_TOK, LANES), F32),
                   jax.ShapeDtypeStruct((N_TOK, LANES), jnp.int32),
                   jax.ShapeDtypeStruct((nt * 8, TM), jnp.int32),
                   jax.ShapeDtypeStruct((nt * 8, LANES), jnp.int32),
                   jax.ShapeDtypeStruct((nt * 8, LANES), jnp.int32)],
        compiler_params=pltpu.CompilerParams(dimension_semantics=("arbitrary",),
                                             vmem_limit_bytes=40 * 1024 * 1024),
        name="mix",
    )(attn, ga, p, x, woa, wout, gffn, wr, br)


def _chunk_copy(src, dst, s8, d8, sem, nchunks=1):
    rows = nchunks * CHUNK_ROWS
    return pltpu.make_async_copy(src.at[pl.ds(pl.multiple_of(s8 * CHUNK_ROWS, CHUNK_ROWS), rows)],
                                 dst.at[pl.ds(pl.multiple_of(d8 * CHUNK_ROWS, CHUNK_ROWS), rows)], sem)


def _unrolled_loop(n, fn):
    groups = n >> CHUNK_UNROLL_SHIFT

    def group(g, carry):
        for u in range(CHUNK_UNROLL):
            fn(g * CHUNK_UNROLL + u)
        return carry

    lax.fori_loop(0, groups, group, 0)

    def single(k, carry):
        fn(k)
        return carry

    lax.fori_loop(groups << CHUNK_UNROLL_SHIFT, n, single, 0)


def _for_each_copy(t, copy_tables, fn):
    npair_ref, nsingle_ref, lpair_ref, gpair_ref, lsingle_ref, gsingle_ref = copy_tables
    pb = t * PAIR_SLOTS
    sb = t * N_EXPERTS
    _unrolled_loop(npair_ref[t], lambda k: fn(lpair_ref[pb + k], gpair_ref[pb + k], 2))
    _unrolled_loop(nsingle_ref[t], lambda k: fn(lsingle_ref[sb + k], gsingle_ref[sb + k], 1))


def _tile_chunks(t, copy_tables):
    return 2 * copy_tables[0][t] + copy_tables[1][t]


def _wait_chunks(n, src, dst, sem):
    groups = n >> CHUNK_UNROLL_SHIFT
    rows = CHUNK_UNROLL * CHUNK_ROWS

    def group(g, carry):
        pltpu.make_async_copy(src.at[pl.ds(0, rows)], dst.at[pl.ds(0, rows)], sem).wait()
        return carry

    lax.fori_loop(0, groups, group, 0)

    def single(k, carry):
        _chunk_copy(src, dst, 0, 0, sem).wait()
        return carry

    lax.fori_loop(groups << CHUNK_UNROLL_SHIFT, n, single, 0)


def _dispatch_kernel(npair_ref, nsingle_ref, lpair_ref, gpair_ref, lsingle_ref, gsingle_ref,
                     zstart8_ref, zcnt8_ref, nvb_ref,
                     slot_ref, h2_ref, xs_hbm, sbuf, zbuf, sem, zsem):
    copy_tables = (npair_ref, nsingle_ref, lpair_ref, gpair_ref, lsingle_ref, gsingle_ref)
    t = pl.program_id(0)
    last = pl.num_programs(0) - 1
    cur = t % 2

    def zero_chunk_copy(d8):
        return _chunk_copy(zbuf, xs_hbm, 0, d8, zsem)

    def zero_block_copy(b):
        return pltpu.make_async_copy(
            zbuf, xs_hbm.at[pl.ds(pl.multiple_of(b * MOE_T, MOE_T), MOE_T)], zsem)

    def for_each_zero(chunk_fn, block_fn):
        def per_expert(e, carry):
            def per_chunk(j, carry2):
                chunk_fn(zstart8_ref[e] + j)
                return carry2

            lax.fori_loop(0, zcnt8_ref[e], per_chunk, 0)
            return carry

        lax.fori_loop(0, N_EXPERTS, per_expert, 0)

        def per_block(b, carry):
            block_fn(b)
            return carry

        lax.fori_loop(nvb_ref[0], MOE_BLOCKS, per_block, 0)

    @pl.when(t == 0)
    def _():
        zbuf[...] = jnp.zeros(zbuf.shape, F32)
        for_each_zero(lambda d8: zero_chunk_copy(d8).start(), lambda b: zero_block_copy(b).start())

    row = lax.broadcasted_iota(jnp.int32, (TILE_ROWS, TM), 0)
    slots = slot_ref[...]
    perm = jnp.where((row == slots[0:1, :]) | (row == slots[1:2, :]), 1.0, 0.0).astype(BF16)
    sbuf[cur] = _dot(perm, h2_ref[...])
    _for_each_copy(t, copy_tables,
                   lambda lc, gc, n: _chunk_copy(sbuf.at[cur], xs_hbm, lc, gc, sem.at[cur], n).start())

    def wait_tile(tile, slot):
        _wait_chunks(_tile_chunks(tile, copy_tables), sbuf.at[slot], xs_hbm, sem.at[slot])

    @pl.when(t > 0)
    def _():
        wait_tile(t - 1, 1 - cur)

    @pl.when(t == last)
    def _():
        wait_tile(t, cur)
        for_each_zero(lambda d8: zero_chunk_copy(d8).wait(), lambda b: zero_block_copy(b).wait())


def _dispatch(tables, slot_rows, h2):
    return pl.pallas_call(
        _dispatch_kernel,
        grid_spec=pltpu.PrefetchScalarGridSpec(
            num_scalar_prefetch=len(tables),
            grid=(N_TOK // TM,),
            in_specs=[pl.BlockSpec((8, TM), lambda i, *_: (i, 0)),
                      pl.BlockSpec((TM, D_MODEL), lambda i, *_: (i, 0))],
            out_specs=pl.BlockSpec(memory_space=pl.ANY),
            scratch_shapes=[pltpu.VMEM((2, TILE_ROWS, D_MODEL), F32), pltpu.VMEM((MOE_T, D_MODEL), F32),
                            pltpu.SemaphoreType.DMA((2,)), pltpu.SemaphoreType.DMA(())]),
        out_shape=jax.ShapeDtypeStruct((MOE_ROWS, D_MODEL), F32),
        compiler_params=pltpu.CompilerParams(dimension_semantics=("arbitrary",)),
        name="dispatch",
    )(*tables, slot_rows, h2)


def _expert_kernel(blk_e_ref, nvb_ref, xs_ref, wg_ref, wu_ref, wd_ref, ys_ref, wgb, wub, wdb):
    i = pl.program_id(0)
    live = i < nvb_ref[0]

    @pl.when(live & ((i == 0) | (blk_e_ref[i] != blk_e_ref[jnp.maximum(i - 1, 0)])))
    def _():
        wgb[...] = wg_ref[...].astype(BF16)
        wub[...] = wu_ref[...].astype(BF16)
        wdb[...] = wd_ref[...].astype(BF16)

    @pl.when(live)
    def _():
        xb = xs_ref[...].astype(BF16)
        g = _dot(xb, wgb[...])
        u = _dot(xb, wub[...])
        hid = (jax.nn.silu(g) * u).astype(BF16)
        ys_ref[...] = _dot(hid, wdb[...])

    @pl.when(jnp.logical_not(live))
    def _():
        ys_ref[...] = jnp.zeros(ys_ref.shape, ys_ref.dtype)


def _experts(blk_e, nvb, xs, wg, wu, wd):
    def row_block(i, be, nv):
        return (jnp.minimum(i, nv[0] - 1), 0)

    def weight(i, be, nv):
        return (be[jnp.minimum(i, nv[0] - 1)], 0, 0)

    return pl.pallas_call(
        _expert_kernel,
        grid_spec=pltpu.PrefetchScalarGridSpec(
            num_scalar_prefetch=2,
            grid=(MOE_BLOCKS,),
            in_specs=[pl.BlockSpec((MOE_T, D_MODEL), row_block),
                      pl.BlockSpec((None, D_MODEL, D_EXPERT), weight),
                      pl.BlockSpec((None, D_MODEL, D_EXPERT), weight),
                      pl.BlockSpec((None, D_EXPERT, D_MODEL), weight)],
            out_specs=pl.BlockSpec((MOE_T, D_MODEL), lambda i, be, nv: (i, 0)),
            scratch_shapes=[pltpu.VMEM((D_MODEL, D_EXPERT), BF16), pltpu.VMEM((D_MODEL, D_EXPERT), BF16),
                            pltpu.VMEM((D_EXPERT, D_MODEL), BF16)]),
        out_shape=jax.ShapeDtypeStruct((MOE_ROWS, D_MODEL), F32),
        compiler_params=pltpu.CompilerParams(dimension_semantics=("arbitrary",),
                                             vmem_limit_bytes=40 * 1024 * 1024),
        name="experts",
    )(blk_e, nvb, xs, wg, wu, wd)


def _combine_kernel(npair_ref, nsingle_ref, lpair_ref, gpair_ref, lsingle_ref, gsingle_ref,
                    slot_ref, x1_ref, mf_ref, gfin_ref, ys_hbm, o_ref, ybuf, sem):
    copy_tables = (npair_ref, nsingle_ref, lpair_ref, gpair_ref, lsingle_ref, gsingle_ref)
    t = pl.program_id(0)
    cur = t % 2

    def fetch(tile, slot):
        _for_each_copy(tile, copy_tables,
                       lambda lc, gc, n: _chunk_copy(ys_hbm, ybuf.at[slot], gc, lc, sem.at[slot], n).start())

    @pl.when(t == 0)
    def _():
        ybuf[...] = jnp.zeros(ybuf.shape, F32)
        fetch(0, 0)

    @pl.when(t + 1 < pl.num_programs(0))
    def _():
        fetch(t + 1, 1 - cur)

    _wait_chunks(_tile_chunks(t, copy_tables), ys_hbm, ybuf.at[cur], sem.at[cur])

    yb = ybuf[cur].astype(BF16)
    col = lax.broadcasted_iota(jnp.int32, (TM, TILE_ROWS), 1)
    slots = slot_ref[...]
    y1 = _dot(jnp.where(col == slots[:, 0:1], 1.0, 0.0).astype(BF16), yb)
    y2 = _dot(jnp.where(col == slots[:, 1:2], 1.0, 0.0).astype(BF16), yb)
    mf = mf_ref[...]
    x2 = x1_ref[...] + mf[:, 0:1] * y1 + mf[:, 1:2] * y2
    o_ref[...] = _rms(x2, gfin_ref[...])


def _combine(tables, slot_cols, x1, mf, gfin, ys):
    return pl.pallas_call(
        _combine_kernel,
        grid_spec=pltpu.PrefetchScalarGridSpec(
            num_scalar_prefetch=len(tables),
            grid=(N_TOK // TM,),
            in_specs=[pl.BlockSpec((TM, LANES), lambda i, *_: (i, 0)),
                      pl.BlockSpec((TM, D_MODEL), lambda i, *_: (i, 0)),
                      pl.BlockSpec((TM, LANES), lambda i, *_: (i, 0)),
                      pl.BlockSpec((1, D_MODEL), lambda i, *_: (0, 0)),
                      pl.BlockSpec(memory_space=pl.ANY)],
            out_specs=pl.BlockSpec((TM, D_MODEL), lambda i, *_: (i, 0)),
            scratch_shapes=[pltpu.VMEM((2, TILE_ROWS, D_MODEL), F32), pltpu.SemaphoreType.DMA((2,))]),
        out_shape=jax.ShapeDtypeStruct((N_TOK, D_MODEL), F32),
        compiler_params=pltpu.CompilerParams(dimension_semantics=("arbitrary",)),
        name="combine",
    )(*tables, slot_cols, x1, mf, gfin, ys)


def kernel(x, positions, g_attn_norm, w_in, g_cq, w_uq, g_ckv, w_uk, w_uv, w_o_attn, sgu_gain, sgu_bias, w_spatial, b_spatial, w_o_sgu, w_out, g_ffn_norm, w_router_group, b_router_group, w_router_expert, b_router_expert, w_exp_gate, w_exp_up, w_exp_down, g_final):
    assert x.shape == (BATCH, SEQ, D_MODEL) and w_in.shape[0] == 1
    half = QK_ROPE // 2
    swap = jnp.concatenate([jnp.arange(half, QK_ROPE), jnp.arange(0, half)])

    def pad_cols(w, width):
        return jnp.pad(w, ((0, 0), (0, width - w.shape[1])))

    wi = w_in[0]
    c0 = Q_RANK + KV_RANK
    kr = wi[:, c0:c0 + QK_ROPE]
    c1 = c0 + QK_ROPE
    win = jnp.concatenate([
        wi[:, :c0], pad_cols(kr, LANES), pad_cols(kr[:, swap], LANES), wi[:, c1:]], axis=1).astype(BF16)

    wq = w_uq[0].reshape(Q_RANK, MLA_HEADS, QK_NOPE + QK_ROPE)
    wq_nope = wq[:, :, :QK_NOPE].transpose(1, 0, 2)
    wq_rope = wq[:, :, QK_NOPE:]
    w_ukt = w_uk[0].reshape(KV_RANK, MLA_HEADS, QK_NOPE).transpose(1, 2, 0)
    wqlat = _fold(wq_nope, w_ukt).transpose(1, 0, 2).reshape(Q_RANK, MLA_HEADS * LANES).astype(BF16)

    def rope_cols(w):
        return jnp.pad(w, ((0, 0), (0, 0), (0, LANES - QK_ROPE))).reshape(Q_RANK, MLA_HEADS * LANES).astype(BF16)

    wqr = rope_cols(wq_rope)
    wqrs = rope_cols(wq_rope[:, :, swap])

    per_row = LANES // half
    freqs = ROPE_THETA ** (-jnp.arange(0, QK_ROPE, 2, dtype=F32) / QK_ROPE)
    pos8 = jnp.repeat(positions.astype(F32).reshape(N_TOK // per_row, per_row), half, axis=1)
    cos8, sin8 = _rope_tables(pos8, jnp.tile(freqs, per_row)[None, :])
    cos16 = cos8.reshape(N_TOK, half)
    sin16 = sin8.reshape(N_TOK, half)
    cos_t = pad_cols(jnp.concatenate([cos16, cos16], axis=1), LANES)
    sin_t = pad_cols(jnp.concatenate([-sin16, sin16], axis=1), LANES)

    head_of_col = jnp.arange(MLA_WIDTH) // V_HEAD
    wuv = jnp.where(head_of_col[None, None, :] == jnp.arange(MLA_HEADS)[:, None, None],
                    w_uv[0][None], 0.0).astype(BF16)
    wuv = wuv.reshape(MLA_HEADS * KV_RANK, MLA_WIDTH)

    wsp = w_spatial[0].astype(BF16)
    bs = b_spatial[0]
    bsp = jnp.repeat(bs.reshape(SGU_GROUPS // 2, 2, SGU_LEN).transpose(0, 2, 1), SGU_GROUP_DIM, axis=2)

    wr32 = jnp.concatenate([w_router_expert[0].transpose(1, 0, 2).reshape(D_MODEL, N_EXPERTS),
                            w_router_group[0]], axis=1)
    wr32 = pad_cols(wr32, LANES)
    wr_hi = wr32.astype(BF16)
    wr_lo = (wr32 - wr_hi.astype(F32)).astype(BF16)
    wr = jnp.concatenate([wr_hi, wr_lo], axis=1)
    br = pad_cols(jnp.concatenate([b_router_expert[0].reshape(-1), b_router_group[0]])[None, :], LANES)

    xf = x.reshape(N_TOK, D_MODEL)
    q, kt, v, p, ga = _inproj(
        xf, cos_t, sin_t, g_attn_norm, win, g_cq, g_ckv, wqlat, wqr, wqrs,
        sgu_gain, sgu_bias, wsp, bsp, w_o_sgu[0].astype(BF16))
    attn = _attention(q, kt, v, wuv)
    x1, h2, mf, slot_cols, slot_rows, c8_rows, loff8_rows = _mix(
        attn.reshape(N_TOK, MLA_WIDTH), ga, p, xf, w_o_attn[0].astype(BF16), w_out[0].astype(BF16),
        g_ffn_norm, wr, br)

    blk8 = MOE_T // CHUNK_ROWS
    c8 = c8_rows[::8, :N_EXPERTS]
    loff8 = loff8_rows[::8, :N_EXPERTS]
    tot8 = jnp.sum(c8, axis=0)
    pad8 = (tot8 + blk8 - 1) // blk8 * blk8
    gend8 = jnp.cumsum(pad8)
    gstart8 = gend8 - pad8
    gbase8 = gstart8[None, :] + jnp.cumsum(c8, axis=0) - c8
    nvb = (gend8[-1:] // blk8).astype(jnp.int32)
    blk_start8 = jnp.arange(MOE_BLOCKS, dtype=jnp.int32) * blk8
    blk_e = jnp.minimum(jnp.sum((gend8[None, :] <= blk_start8[:, None]).astype(jnp.int32), axis=1),
                        N_EXPERTS - 1)
    def flat_list(count, slots, first_local, first_global, step):
        off = jnp.cumsum(count, axis=1) - count
        k = jnp.arange(slots, dtype=jnp.int32)
        in_run = (off[:, None, :] <= k[None, :, None]) & (k[None, :, None] < (off + count)[:, None, :])

        def pick(first):
            return (jnp.sum(jnp.where(in_run, (first - step * off)[:, None, :], 0), axis=-1)
                    + step * k[None, :]).reshape(-1).astype(jnp.int32)

        return jnp.sum(count, axis=1).astype(jnp.int32), pick(first_local), pick(first_global)

    npair, lpair, gpair = flat_list(c8 // 2, PAIR_SLOTS, loff8, gbase8, 2)
    nsingle, lsingle, gsingle = flat_list(c8 % 2, N_EXPERTS, loff8 + c8 - 1, gbase8 + c8 - 1, 0)
    run_tables = (npair, nsingle, lpair, gpair, lsingle, gsingle)

    xs = _dispatch(run_tables + ((gstart8 + tot8).astype(jnp.int32), (pad8 - tot8).astype(jnp.int32), nvb),
                   slot_rows, h2)
    ys = _experts(blk_e, nvb, xs, w_exp_gate[0], w_exp_up[0], w_exp_down[0])
    out = _combine(run_tables, slot_cols, x1, mf, g_final.reshape(1, D_MODEL), ys)
    return out.reshape(BATCH, SEQ, D_MODEL)
```

```python
import jax
import jax.numpy as jnp
from jax import lax
from jax.experimental import pallas as pl
from jax.experimental.pallas import tpu as pltpu

D_MODEL = 1024
BATCH = 2
SEQ = 8192
N_TOK = BATCH * SEQ
CHUNK = 64
EPS = 1e-6
MLA_HEADS = 8
Q_RANK = 256
KV_RANK = 128
QK_NOPE = 64
QK_ROPE = 32
V_HEAD = 64
MLA_WIDTH = MLA_HEADS * V_HEAD
ROPE_THETA = 10000.0
SGU_GROUPS = 8
SGU_GROUP_DIM = 64
SGU_WIDTH = SGU_GROUPS * SGU_GROUP_DIM
SGU_LEN = 128
N_GROUPS = 4
EXPERTS_PER_GROUP = 8
N_EXPERTS = N_GROUPS * EXPERTS_PER_GROUP
TOP_K = 2
D_EXPERT = 256

LANES = 128
QK_PAD = 2 * LANES
V_PAD = 2 * LANES
SCALE = (QK_NOPE + QK_ROPE) ** -0.5
LOG2E = 1.4426950408889634
Q_SCALE = SCALE * LOG2E
NEG = -1e30

TM = 256
TMI = 512
TMX = 1024
TQ = 128
TK = 512
ATTN_UNROLL_SHIFT = 1
ATTN_UNROLL = 1 << ATTN_UNROLL_SHIFT
TK_SHIFT = TK.bit_length() - 1
CHUNK_SHIFT = CHUNK.bit_length() - 1
assert 1 << TK_SHIFT == TK and 1 << CHUNK_SHIFT == CHUNK
MOE_T = 512
CHUNK_ROWS = 8
TILE_ROWS = -(-(TOP_K * TM + N_EXPERTS * (CHUNK_ROWS - 1) + CHUNK_ROWS) // 256) * 256
TILE_CHUNKS = TILE_ROWS // CHUNK_ROWS
PAIR_SLOTS = TILE_CHUNKS // 2
CHUNK_UNROLL_SHIFT = 2
CHUNK_UNROLL = 1 << CHUNK_UNROLL_SHIFT
MOE_ROWS_MAX = (N_TOK * TOP_K + (N_TOK // TM) * N_EXPERTS * (CHUNK_ROWS - 1)
                + N_EXPERTS * (MOE_T - CHUNK_ROWS))
MOE_BLOCKS = -(-MOE_ROWS_MAX // MOE_T)
MOE_ROWS = MOE_BLOCKS * MOE_T

C_Q = 0
C_KV = C_Q + Q_RANK
C_KR = C_KV + KV_RANK
C_KRS = C_KR + LANES
C_U = C_KRS + LANES
C_V = C_U + SGU_WIDTH
C_GA = C_V + SGU_WIDTH
C_GB = C_GA + D_MODEL
C_END = C_GB + D_MODEL

F32 = jnp.float32
BF16 = jnp.bfloat16


def _dot(a, b):
    return jnp.dot(a, b, preferred_element_type=F32)


def _rms(x, g):
    return x * lax.rsqrt(jnp.mean(x * x, axis=-1, keepdims=True) + EPS) * g


def _fold_kernel(a_ref, b_ref, o_ref):
    o_ref[...] = Q_SCALE * jnp.dot(a_ref[...], b_ref[...], preferred_element_type=F32,
                                 precision=lax.Precision.HIGHEST)


def _fold(w_uq_nope, w_ukt):
    return pl.pallas_call(
        _fold_kernel,
        grid=(MLA_HEADS,),
        in_specs=[pl.BlockSpec((None, Q_RANK, QK_NOPE), lambda h: (h, 0, 0)),
                  pl.BlockSpec((None, QK_NOPE, KV_RANK), lambda h: (h, 0, 0))],
        out_specs=pl.BlockSpec((None, Q_RANK, KV_RANK), lambda h: (h, 0, 0)),
        out_shape=jax.ShapeDtypeStruct((MLA_HEADS, Q_RANK, KV_RANK), F32),
        name="fold",
    )(w_uq_nope, w_ukt)


def _rope_kernel(pos_ref, freq_ref, cos_ref, sin_ref):
    ang = pos_ref[...] * freq_ref[...]
    cos_ref[...] = jnp.cos(ang)
    sin_ref[...] = jnp.sin(ang)


def _rope_tables(pos8, freq8):
    rows = pos8.shape[0]
    blk = pl.BlockSpec((TM, LANES), lambda i: (i, 0))
    return pl.pallas_call(
        _rope_kernel,
        grid=(rows // TM,),
        in_specs=[blk, pl.BlockSpec((1, LANES), lambda i: (0, 0))],
        out_specs=[blk, blk],
        out_shape=[jax.ShapeDtypeStruct((rows, LANES), F32)] * 2,
        name="rope_tables",
    )(pos8, freq8)


def _inproj_kernel(x_ref, cos_ref, sin_ref, gattn_ref, win_ref, gcq_ref, gckv_ref, wqlat_ref, wqr_ref,
                   wqrs_ref, sgain_ref, sbias_ref, wsp_ref, bsp_ref, wosgu_ref,
                   q_ref, kt_ref, v_ref, p_ref, ga_ref):
    hb = _rms(x_ref[...], gattn_ref[...]).astype(BF16)

    def proj(a, b):
        return _dot(hb, win_ref[:, a:b])

    cos_t = cos_ref[...]
    sin_t = sin_ref[...]

    cqn = _rms(proj(C_Q, C_KV), gcq_ref[...]).astype(BF16)
    qlat = _dot(cqn, wqlat_ref[...])
    groups = MLA_HEADS * QK_ROPE // LANES
    rot = (_dot(cqn, wqr_ref[...]) * jnp.concatenate([cos_t * Q_SCALE] * groups, axis=1)
           + _dot(cqn, wqrs_ref[...]) * jnp.concatenate([sin_t * Q_SCALE] * groups, axis=1))
    lane = lax.broadcasted_iota(jnp.int32, (TQ, LANES), 1)
    for r in range(TMI // TQ):
        t0, t1 = r * TQ, (r + 1) * TQ
        for h in range(MLA_HEADS):
            a, b = h * LANES, (h + 1) * LANES
            q_ref[r, h * TQ:(h + 1) * TQ, :LANES] = qlat[t0:t1, a:b].astype(BF16)
            g0 = (h * QK_ROPE) // LANES * LANES
            off = (h * QK_ROPE) % LANES
            window = (lane >= off) & (lane < off + QK_ROPE)
            q_ref[r, h * TQ:(h + 1) * TQ, LANES:] = jnp.where(
                window, rot[t0:t1, g0:g0 + LANES], 0.0).astype(BF16)

    zk = proj(C_KV, C_U)
    ckvn = _rms(zk[:, :KV_RANK], gckv_ref[...])
    krope = zk[:, KV_RANK:KV_RANK + LANES] * cos_t + zk[:, KV_RANK + LANES:] * sin_t
    kt_ref[...] = jnp.concatenate([ckvn, krope], axis=1).T.astype(BF16)
    v_ref[...] = jnp.concatenate([ckvn, jnp.ones_like(ckvn)], axis=1).astype(BF16)

    u = jax.nn.gelu(proj(C_U, C_V))
    v = jax.nn.gelu(proj(C_V, C_GA))
    mu = jnp.mean(v, axis=-1, keepdims=True)
    vc = v - mu
    var = jnp.mean(vc * vc, axis=-1, keepdims=True)
    vb = (vc * lax.rsqrt(var + EPS) * sgain_ref[...] + sbias_ref[...]).astype(BF16)
    row = lax.broadcasted_iota(jnp.int32, (SGU_LEN, SGU_LEN), 0)
    col = lax.broadcasted_iota(jnp.int32, (SGU_LEN, SGU_LEN), 1)
    causal = (row >> CHUNK_SHIFT) >= (col >> CHUNK_SHIFT)
    low_half = col < SGU_GROUP_DIM
    zero_w = jnp.zeros((SGU_LEN, SGU_LEN), BF16)
    w_pairs = [jnp.concatenate([jnp.where(causal, wsp_ref[2 * p], zero_w),
                                jnp.where(causal, wsp_ref[2 * p + 1], zero_w)], axis=1)
               for p in range(SGU_WIDTH // LANES)]
    row_blocks = []
    for r in range(TMI // SGU_LEN):
        pieces = []
        for p in range(SGU_WIDTH // LANES):
            blk = vb[r * SGU_LEN:(r + 1) * SGU_LEN, p * LANES:(p + 1) * LANES]
            stacked = jnp.concatenate([jnp.where(low_half, blk, zero_w), jnp.where(low_half, zero_w, blk)], axis=0)
            sv = _dot(w_pairs[p], stacked) + bsp_ref[p]
            pieces.append((u[r * SGU_LEN:(r + 1) * SGU_LEN, p * LANES:(p + 1) * LANES] * sv).astype(BF16))
        row_blocks.append(jnp.concatenate(pieces, axis=1))
    sgu = _dot(jnp.concatenate(row_blocks, axis=0), wosgu_ref[...])

    ga_ref[...] = jax.nn.sigmoid(proj(C_GA, C_GB)).astype(BF16)
    p_ref[...] = (jax.nn.sigmoid(proj(C_GB, C_END)) * sgu).astype(BF16)


def _inproj(x, cos_t, sin_t, gattn, win, gcq, gckv, wqlat, wqr, wqrs, sgain, sbias, wsp, bsp, wosgu):
    nt = N_TOK // TMI
    per_b = SEQ // TMI
    per_k = TK // TMI

    def const(shape):
        return pl.BlockSpec(shape, lambda i: (0,) * len(shape))

    return pl.pallas_call(
        _inproj_kernel,
        grid=(nt,),
        in_specs=[pl.BlockSpec((TMI, D_MODEL), lambda i: (i, 0)),
                  pl.BlockSpec((TMI, LANES), lambda i: (i, 0)), pl.BlockSpec((TMI, LANES), lambda i: (i, 0)),
                  const((1, D_MODEL)), const((D_MODEL, C_END)), const((1, Q_RANK)), const((1, KV_RANK)),
                  const((Q_RANK, MLA_HEADS * LANES)), const((Q_RANK, MLA_HEADS * QK_ROPE)),
                  const((Q_RANK, MLA_HEADS * QK_ROPE)),
                  const((1, SGU_WIDTH)), const((1, SGU_WIDTH)),
                  const((SGU_GROUPS, SGU_LEN, SGU_LEN)), const((SGU_GROUPS // 2, SGU_LEN, LANES)),
                  const((SGU_WIDTH, D_MODEL))],
        out_specs=[pl.BlockSpec((TMI // TQ, MLA_HEADS * TQ, QK_PAD), lambda i: (i, 0, 0)),
                   pl.BlockSpec((None, None, QK_PAD, TMI),
                                lambda i: (i // per_b, (i % per_b) // per_k, 0, i % per_k)),
                   pl.BlockSpec((None, None, TMI, V_PAD),
                                lambda i: (i // per_b, (i % per_b) // per_k, i % per_k, 0)),
                   pl.BlockSpec((TMI, D_MODEL), lambda i: (i, 0)),
                   pl.BlockSpec((TMI, D_MODEL), lambda i: (i, 0))],
        out_shape=[jax.ShapeDtypeStruct((N_TOK // TQ, MLA_HEADS * TQ, QK_PAD), BF16),
                   jax.ShapeDtypeStruct((BATCH, SEQ // TK, QK_PAD, TK), BF16),
                   jax.ShapeDtypeStruct((BATCH, SEQ // TK, TK, V_PAD), BF16),
                   jax.ShapeDtypeStruct((N_TOK, D_MODEL), BF16),
                   jax.ShapeDtypeStruct((N_TOK, D_MODEL), BF16)],
        compiler_params=pltpu.CompilerParams(dimension_semantics=("arbitrary",),
                                             vmem_limit_bytes=56 * 1024 * 1024),
        name="inproj",
    )(x, cos_t, sin_t, gattn, win, gcq, gckv, wqlat, wqr, wqrs, sgain, sbias, wsp, bsp, wosgu)


def _attn_kernel(q_ref, kt_ref, v_ref, wuv_ref, o_ref, m_ref, acc_ref, s_ref, p_ref, a_ref):
    qi = pl.program_id(1)
    m_ref[...] = jnp.full(m_ref.shape, NEG, F32)
    acc_ref[...] = jnp.zeros(acc_ref.shape, F32)

    def scores(j, slot):
        s_ref[slot] = _dot(q_ref[...], kt_ref[j])

    def update(j, slot, width=None):
        masked = width is not None
        w = width if masked else TK
        if masked:
            q_chunk = ((qi * TQ - j * TK) >> CHUNK_SHIFT) + (
                lax.broadcasted_iota(jnp.int32, (TQ, w), 0) >> CHUNK_SHIFT)
            k_chunk = lax.broadcasted_iota(jnp.int32, (TQ, w), 1) >> CHUNK_SHIFT
            visible = k_chunk <= q_chunk
        for h in range(MLA_HEADS):
            rows = slice(h * TQ, (h + 1) * TQ)
            s = s_ref[slot, rows, :w]
            if masked:
                s = jnp.where(visible, s, NEG)
            m_prev = m_ref[rows, :]
            m_new = jnp.maximum(m_prev, jnp.max(s, axis=-1, keepdims=True))
            m_ref[rows, :] = m_new
            a_ref[rows, :] = jnp.exp2(m_prev - m_new)
            s = s_ref[slot, rows, :w]
            if masked:
                s = jnp.where(visible, s, NEG)
            p_ref[rows, :w] = jnp.exp2(s - jnp.concatenate([m_new] * (w // LANES), axis=1)).astype(BF16)
        alpha = a_ref[...]
        acc_ref[...] = jnp.concatenate([alpha, alpha], axis=1) * acc_ref[...] + _dot(
            p_ref[:, :w], v_ref[j, :w, :])

    diag = (qi * TQ) >> TK_SHIFT
    scores(0, 0)

    def run(first, count, last_masked):
        for u in range(count):
            if not (last_masked and u == count - 1):
                scores(first + u + 1, (u + 1) % 2)
                update(first + u, u % 2)
            else:
                for sub in range(TK // TQ):
                    @pl.when((qi & (TK // TQ - 1)) == sub)
                    def _(u=u, sub=sub):
                        update(first + u, u % 2, (sub + 1) * TQ)

    def body(t, carry):
        run(ATTN_UNROLL * t, ATTN_UNROLL, False)
        return carry

    trips = diag >> ATTN_UNROLL_SHIFT
    lax.fori_loop(0, trips, body, 0)
    done = trips << ATTN_UNROLL_SHIFT
    for r in range(ATTN_UNROLL):
        @pl.when(diag - done == r)
        def _(r=r):
            run(done, r + 1, True)

    o_lat = (acc_ref[:, :KV_RANK] / acc_ref[:, KV_RANK:]).astype(BF16)
    o_cat = jnp.concatenate([o_lat[h * TQ:(h + 1) * TQ] for h in range(MLA_HEADS)], axis=1)
    o_ref[...] = _dot(o_cat, wuv_ref[...]).astype(BF16)


def _attention(q, kt, v, wuv):
    nk = SEQ // TK
    return pl.pallas_call(
        _attn_kernel,
        grid=(BATCH, SEQ // TQ),
        in_specs=[pl.BlockSpec((None, MLA_HEADS * TQ, QK_PAD), lambda b, i: (b * (SEQ // TQ) + i, 0, 0)),
                  pl.BlockSpec((None, nk, QK_PAD, TK), lambda b, i: (b, 0, 0, 0)),
                  pl.BlockSpec((None, nk, TK, V_PAD), lambda b, i: (b, 0, 0, 0)),
                  pl.BlockSpec((MLA_HEADS * KV_RANK, MLA_WIDTH), lambda b, i: (0, 0))],
        out_specs=pl.BlockSpec((None, TQ, MLA_WIDTH), lambda b, i: (b, i, 0)),
        out_shape=jax.ShapeDtypeStruct((BATCH, SEQ, MLA_WIDTH), BF16),
        scratch_shapes=[pltpu.VMEM((MLA_HEADS * TQ, LANES), F32),
                        pltpu.VMEM((MLA_HEADS * TQ, V_PAD), F32),
                        pltpu.VMEM((2, MLA_HEADS * TQ, TK), F32),
                        pltpu.VMEM((MLA_HEADS * TQ, TK), BF16),
                        pltpu.VMEM((MLA_HEADS * TQ, LANES), F32)],
        compiler_params=pltpu.CompilerParams(dimension_semantics=("arbitrary", "arbitrary"),
                                             vmem_limit_bytes=40 * 1024 * 1024),
        name="attention",
    )(q, kt, v, wuv)


def _mix_kernel(attn_ref, ga_ref, p_ref, x_ref, woa_ref, wout_ref, gffn_ref, wr_ref, br_ref,
                x1_ref, h2_ref, mf_ref, slot_col_ref, slot_row_ref, c8_ref, loff8_ref):
    a = _dot(attn_ref[...], woa_ref[...])
    mix = (ga_ref[...].astype(F32) * a + p_ref[...].astype(F32)).astype(BF16)
    x1 = x_ref[...] + _dot(mix, wout_ref[...])
    x1_ref[...] = x1
    h2 = _rms(x1, gffn_ref[...])
    h2_ref[...] = h2.astype(BF16)

    hi = h2.astype(BF16)
    lo = (h2 - hi.astype(F32)).astype(BF16)
    r1 = _dot(hi, wr_ref[...])
    r2 = _dot(lo, wr_ref[:, :LANES])
    logits_all = r1[:, :LANES] + r1[:, LANES:] + r2 + br_ref[...]

    for r in range(TMX // TM):
        mf, slots, c8_rows, loff8_rows = _route_tile(logits_all[r * TM:(r + 1) * TM])
        mf_ref[r * TM:(r + 1) * TM, :] = mf
        slot_col_ref[r * TM:(r + 1) * TM, :] = slots.astype(jnp.int32)
        slot_row_ref[8 * r:8 * (r + 1), :] = slots.T[:8].astype(jnp.int32)
        c8_ref[8 * r:8 * (r + 1), :] = c8_rows.astype(jnp.int32)
        loff8_ref[8 * r:8 * (r + 1), :] = loff8_rows.astype(jnp.int32)


def _route_tile(logits):
    lane_i = lax.broadcasted_iota(jnp.int32, (TM, LANES), 1)
    lane = lane_i.astype(F32)
    lane_group = (lane_i >> 3).astype(F32)
    ninf = -jnp.inf
    is_group = (lane_i >= N_EXPERTS) & (lane_i < N_EXPERTS + N_GROUPS)
    lg = jnp.where(is_group, logits, ninf)
    gmax = jnp.max(lg, axis=-1, keepdims=True)
    gsum = jnp.sum(jnp.exp(lg - gmax), axis=-1, keepdims=True)
    p_top = 1.0 / gsum
    g_idx = jnp.min(jnp.where(lg == gmax, lane - N_EXPERTS, float(N_GROUPS)), axis=-1, keepdims=True)
    le = jnp.where((lane_i < N_EXPERTS) & (lane_group == g_idx), logits, ninf)
    t1 = jnp.max(le, axis=-1, keepdims=True)
    e1 = jnp.min(jnp.where(le == t1, lane, float(LANES)), axis=-1, keepdims=True)
    le2 = jnp.where(lane == e1, ninf, le)
    t2 = jnp.max(le2, axis=-1, keepdims=True)
    e2 = jnp.min(jnp.where(le2 == t2, lane, float(LANES)), axis=-1, keepdims=True)
    ex = jnp.exp(t2 - t1)
    w1 = p_top / (1.0 + ex)
    w2 = p_top * ex / (1.0 + ex)

    sel1 = lane == e1
    sel2 = lane == e2
    onehot = jnp.where(sel1 | sel2, 1.0, 0.0)
    rr = lax.broadcasted_iota(jnp.int32, (TM, TM), 0)
    cc = lax.broadcasted_iota(jnp.int32, (TM, TM), 1)
    ltri = jnp.where(cc < rr, 1.0, 0.0).astype(BF16)
    rank = _dot(ltri, onehot.astype(BF16))
    cnt = jnp.sum(onehot, axis=0, keepdims=True)
    c8 = jnp.floor((cnt + (CHUNK_ROWS - 1)) * (1.0 / CHUNK_ROWS))
    ur = lax.broadcasted_iota(jnp.int32, (LANES, LANES), 0)
    uc = lax.broadcasted_iota(jnp.int32, (LANES, LANES), 1)
    upper = jnp.where(ur < uc, 1.0, 0.0).astype(BF16)
    c8_rows = jnp.broadcast_to(c8, (8, LANES))
    loff8_rows = _dot(c8_rows.astype(BF16), upper)
    slot_all = CHUNK_ROWS * loff8_rows[0:1] + rank
    slot1 = jnp.sum(jnp.where(sel1, slot_all, 0.0), axis=-1, keepdims=True)
    slot2 = jnp.sum(jnp.where(sel2, slot_all, 0.0), axis=-1, keepdims=True)

    slots = jnp.where(lane_i == 0, slot1, jnp.where(lane_i == 1, slot2, 0.0))
    return jnp.where(lane_i == 0, w1, w2), slots, c8_rows, loff8_rows


def _mix(attn, ga, p, x, woa, wout, gffn, wr, br):
    nt = N_TOK // TM
    sub = TMX // TM

    def const(shape):
        return pl.BlockSpec(shape, lambda i: (0,) * len(shape))

    def rows(width):
        return pl.BlockSpec((TMX, width), lambda i: (i, 0))

    return pl.pallas_call(
        _mix_kernel,
        grid=(N_TOK // TMX,),
        in_specs=[rows(MLA_WIDTH), rows(D_MODEL), rows(D_MODEL), rows(D_MODEL),
                  const((MLA_WIDTH, D_MODEL)), const((D_MODEL, D_MODEL)), const((1, D_MODEL)),
                  const((D_MODEL, 2 * LANES)), const((1, LANES))],
        out_specs=[rows(D_MODEL), rows(D_MODEL), rows(LANES), rows(LANES),
                   pl.BlockSpec((8 * sub, TM), lambda i: (i, 0)),
                   pl.BlockSpec((8 * sub, LANES), lambda i: (i, 0)),
                   pl.BlockSpec((8 * sub, LANES), lambda i: (i, 0))],
        out_shape=[jax.ShapeDtypeStruct((N_TOK, D_MODEL), F32),
                   jax.ShapeDtypeStruct((N_TOK, D_MODEL), BF16),
                   jax.ShapeDtypeStruct((N_TOK, LANES), F32),
                   jax.ShapeDtypeStruct((N_TOK, LANES), jnp.int32),
                   jax.ShapeDtypeStruct((nt * 8, TM), jnp.int32),
                   jax.ShapeDtypeStruct((nt * 8, LANES), jnp.int32),
                   jax.ShapeDtypeStruct((nt * 8, LANES), jnp.int32)],
        compiler_params=pltpu.CompilerParams(dimension_semantics=("arbitrary",),
                                             vmem_limit_bytes=40 * 1024 * 1024),
        name="mix",
    )(attn, ga, p, x, woa, wout, gffn, wr, br)


def _chunk_copy(src, dst, s8, d8, sem, nchunks=1):
    rows = nchunks * CHUNK_ROWS
    return pltpu.make_async_copy(src.at[pl.ds(pl.multiple_of(s8 * CHUNK_ROWS, CHUNK_ROWS), rows)],
                                 dst.at[pl.ds(pl.multiple_of(d8 * CHUNK_ROWS, CHUNK_ROWS), rows)], sem)


def _unrolled_loop(n, fn):
    groups = n >> CHUNK_UNROLL_SHIFT

    def group(g, carry):
        for u in range(CHUNK_UNROLL):
            fn(g * CHUNK_UNROLL + u)
        return carry

    lax.fori_loop(0, groups, group, 0)

    def single(k, carry):
        fn(k)
        return carry

    lax.fori_loop(groups << CHUNK_UNROLL_SHIFT, n, single, 0)


def _for_each_copy(t, copy_tables, fn):
    npair_ref, nsingle_ref, lpair_ref, gpair_ref, lsingle_ref, gsingle_ref = copy_tables
    pb = t * PAIR_SLOTS
    sb = t * N_EXPERTS
    _unrolled_loop(npair_ref[t], lambda k: fn(lpair_ref[pb + k], gpair_ref[pb + k], 2))
    _unrolled_loop(nsingle_ref[t], lambda k: fn(lsingle_ref[sb + k], gsingle_ref[sb + k], 1))


def _tile_chunks(t, copy_tables):
    return 2 * copy_tables[0][t] + copy_tables[1][t]


def _wait_chunks(n, src, dst, sem):
    groups = n >> CHUNK_UNROLL_SHIFT
    rows = CHUNK_UNROLL * CHUNK_ROWS

    def group(g, carry):
        pltpu.make_async_copy(src.at[pl.ds(0, rows)], dst.at[pl.ds(0, rows)], sem).wait()
        return carry

    lax.fori_loop(0, groups, group, 0)

    def single(k, carry):
        _chunk_copy(src, dst, 0, 0, sem).wait()
        return carry

    lax.fori_loop(groups << CHUNK_UNROLL_SHIFT, n, single, 0)


def _dispatch_kernel(npair_ref, nsingle_ref, lpair_ref, gpair_ref, lsingle_ref, gsingle_ref,
                     zstart8_ref, zcnt8_ref, nvb_ref,
                     slot_ref, h2_ref, xs_hbm, sbuf, zbuf, sem, zsem):
    copy_tables = (npair_ref, nsingle_ref, lpair_ref, gpair_ref, lsingle_ref, gsingle_ref)
    t = pl.program_id(0)
    last = pl.num_programs(0) - 1
    cur = t % 2

    def zero_chunk_copy(d8):
        return _chunk_copy(zbuf, xs_hbm, 0, d8, zsem)

    def zero_block_copy(b):
        return pltpu.make_async_copy(
            zbuf, xs_hbm.at[pl.ds(pl.multiple_of(b * MOE_T, MOE_T), MOE_T)], zsem)

    def for_each_zero(chunk_fn, block_fn):
        def per_expert(e, carry):
            def per_chunk(j, carry2):
                chunk_fn(zstart8_ref[e] + j)
                return carry2

            lax.fori_loop(0, zcnt8_ref[e], per_chunk, 0)
            return carry

        lax.fori_loop(0, N_EXPERTS, per_expert, 0)

        def per_block(b, carry):
            block_fn(b)
            return carry

        lax.fori_loop(nvb_ref[0], MOE_BLOCKS, per_block, 0)

    @pl.when(t == 0)
    def _():
        zbuf[...] = jnp.zeros(zbuf.shape, F32)
        for_each_zero(lambda d8: zero_chunk_copy(d8).start(), lambda b: zero_block_copy(b).start())

    row = lax.broadcasted_iota(jnp.int32, (TILE_ROWS, TM), 0)
    slots = slot_ref[...]
    perm = jnp.where((row == slots[0:1, :]) | (row == slots[1:2, :]), 1.0, 0.0).astype(BF16)
    sbuf[cur] = _dot(perm, h2_ref[...])
    _for_each_copy(t, copy_tables,
                   lambda lc, gc, n: _chunk_copy(sbuf.at[cur], xs_hbm, lc, gc, sem.at[cur], n).start())

    def wait_tile(tile, slot):
        _wait_chunks(_tile_chunks(tile, copy_tables), sbuf.at[slot], xs_hbm, sem.at[slot])

    @pl.when(t > 0)
    def _():
        wait_tile(t - 1, 1 - cur)

    @pl.when(t == last)
    def _():
        wait_tile(t, cur)
        for_each_zero(lambda d8: zero_chunk_copy(d8).wait(), lambda b: zero_block_copy(b).wait())


def _dispatch(tables, slot_rows, h2):
    return pl.pallas_call(
        _dispatch_kernel,
        grid_spec=pltpu.PrefetchScalarGridSpec(
            num_scalar_prefetch=len(tables),
            grid=(N_TOK // TM,),
            in_specs=[pl.BlockSpec((8, TM), lambda i, *_: (i, 0)),
                      pl.BlockSpec((TM, D_MODEL), lambda i, *_: (i, 0))],
            out_specs=pl.BlockSpec(memory_space=pl.ANY),
            scratch_shapes=[pltpu.VMEM((2, TILE_ROWS, D_MODEL), F32), pltpu.VMEM((MOE_T, D_MODEL), F32),
                            pltpu.SemaphoreType.DMA((2,)), pltpu.SemaphoreType.DMA(())]),
        out_shape=jax.ShapeDtypeStruct((MOE_ROWS, D_MODEL), F32),
        compiler_params=pltpu.CompilerParams(dimension_semantics=("arbitrary",)),
        name="dispatch",
    )(*tables, slot_rows, h2)


def _expert_kernel(blk_e_ref, nvb_ref, xs_ref, wg_ref, wu_ref, wd_ref, ys_ref, wgb, wub, wdb):
    i = pl.program_id(0)
    live = i < nvb_ref[0]

    @pl.when(live & ((i == 0) | (blk_e_ref[i] != blk_e_ref[jnp.maximum(i - 1, 0)])))
    def _():
        wgb[...] = wg_ref[...].astype(BF16)
        wub[...] = wu_ref[...].astype(BF16)
        wdb[...] = wd_ref[...].astype(BF16)

    @pl.when(live)
    def _():
        xb = xs_ref[...].astype(BF16)
        g = _dot(xb, wgb[...])
        u = _dot(xb, wub[...])
        hid = (jax.nn.silu(g) * u).astype(BF16)
        ys_ref[...] = _dot(hid, wdb[...])

    @pl.when(jnp.logical_not(live))
    def _():
        ys_ref[...] = jnp.zeros(ys_ref.shape, ys_ref.dtype)


def _experts(blk_e, nvb, xs, wg, wu, wd):
    def row_block(i, be, nv):
        return (jnp.minimum(i, nv[0] - 1), 0)

    def weight(i, be, nv):
        return (be[jnp.minimum(i, nv[0] - 1)], 0, 0)

    return pl.pallas_call(
        _expert_kernel,
        grid_spec=pltpu.PrefetchScalarGridSpec(
            num_scalar_prefetch=2,
            grid=(MOE_BLOCKS,),
            in_specs=[pl.BlockSpec((MOE_T, D_MODEL), row_block),
                      pl.BlockSpec((None, D_MODEL, D_EXPERT), weight),
                      pl.BlockSpec((None, D_MODEL, D_EXPERT), weight),
                      pl.BlockSpec((None, D_EXPERT, D_MODEL), weight)],
            out_specs=pl.BlockSpec((MOE_T, D_MODEL), lambda i, be, nv: (i, 0)),
            scratch_shapes=[pltpu.VMEM((D_MODEL, D_EXPERT), BF16), pltpu.VMEM((D_MODEL, D_EXPERT), BF16),
                            pltpu.VMEM((D_EXPERT, D_MODEL), BF16)]),
        out_shape=jax.ShapeDtypeStruct((MOE_ROWS, D_MODEL), F32),
        compiler_params=pltpu.CompilerParams(dimension_semantics=("arbitrary",),
                                             vmem_limit_bytes=40 * 1024 * 1024),
        name="experts",
    )(blk_e, nvb, xs, wg, wu, wd)


def _combine_kernel(npair_ref, nsingle_ref, lpair_ref, gpair_ref, lsingle_ref, gsingle_ref,
                    slot_ref, x1_ref, mf_ref, gfin_ref, ys_hbm, o_ref, ybuf, sem):
    copy_tables = (npair_ref, nsingle_ref, lpair_ref, gpair_ref, lsingle_ref, gsingle_ref)
    t = pl.program_id(0)
    cur = t % 2

    def fetch(tile, slot):
        _for_each_copy(tile, copy_tables,
                       lambda lc, gc, n: _chunk_copy(ys_hbm, ybuf.at[slot], gc, lc, sem.at[slot], n).start())

    @pl.when(t == 0)
    def _():
        ybuf[...] = jnp.zeros(ybuf.shape, F32)
        fetch(0, 0)

    @pl.when(t + 1 < pl.num_programs(0))
    def _():
        fetch(t + 1, 1 - cur)

    _wait_chunks(_tile_chunks(t, copy_tables), ys_hbm, ybuf.at[cur], sem.at[cur])

    yb = ybuf[cur].astype(BF16)
    col = lax.broadcasted_iota(jnp.int32, (TM, TILE_ROWS), 1)
    slots = slot_ref[...]
    y1 = _dot(jnp.where(col == slots[:, 0:1], 1.0, 0.0).astype(BF16), yb)
    y2 = _dot(jnp.where(col == slots[:, 1:2], 1.0, 0.0).astype(BF16), yb)
    mf = mf_ref[...]
    x2 = x1_ref[...] + mf[:, 0:1] * y1 + mf[:, 1:2] * y2
    o_ref[...] = _rms(x2, gfin_ref[...])


def _combine(tables, slot_cols, x1, mf, gfin, ys):
    return pl.pallas_call(
        _combine_kernel,
        grid_spec=pltpu.PrefetchScalarGridSpec(
            num_scalar_prefetch=len(tables),
            grid=(N_TOK // TM,),
            in_specs=[pl.BlockSpec((TM, LANES), lambda i, *_: (i, 0)),
                      pl.BlockSpec((TM, D_MODEL), lambda i, *_: (i, 0)),
                      pl.BlockSpec((TM, LANES), lambda i, *_: (i, 0)),
                      pl.BlockSpec((1, D_MODEL), lambda i, *_: (0, 0)),
                      pl.BlockSpec(memory_space=pl.ANY)],
            out_specs=pl.BlockSpec((TM, D_MODEL), lambda i, *_: (i, 0)),
            scratch_shapes=[pltpu.VMEM((2, TILE_ROWS, D_MODEL), F32), pltpu.SemaphoreType.DMA((2,))]),
        out_shape=jax.ShapeDtypeStruct((N_TOK, D_MODEL), F32),
        compiler_params=pltpu.CompilerParams(dimension_semantics=("arbitrary",)),
        name="combine",
    )(*tables, slot_cols, x1, mf, gfin, ys)


def kernel(x, positions, g_attn_norm, w_in, g_cq, w_uq, g_ckv, w_uk, w_uv, w_o_attn, sgu_gain, sgu_bias, w_spatial, b_spatial, w_o_sgu, w_out, g_ffn_norm, w_router_group, b_router_group, w_router_expert, b_router_expert, w_exp_gate, w_exp_up, w_exp_down, g_final):
    assert x.shape == (BATCH, SEQ, D_MODEL) and w_in.shape[0] == 1
    half = QK_ROPE // 2
    swap = jnp.concatenate([jnp.arange(half, QK_ROPE), jnp.arange(0, half)])

    def pad_cols(w, width):
        return jnp.pad(w, ((0, 0), (0, width - w.shape[1])))

    wi = w_in[0]
    c0 = Q_RANK + KV_RANK
    kr = wi[:, c0:c0 + QK_ROPE]
    c1 = c0 + QK_ROPE
    win = jnp.concatenate([
        wi[:, :c0], jnp.tile(kr, (1, LANES // QK_ROPE)), jnp.tile(kr[:, swap], (1, LANES // QK_ROPE)),
        wi[:, c1:]], axis=1).astype(BF16)

    wq = w_uq[0].reshape(Q_RANK, MLA_HEADS, QK_NOPE + QK_ROPE)
    wq_nope = wq[:, :, :QK_NOPE].transpose(1, 0, 2)
    wq_rope = wq[:, :, QK_NOPE:]
    w_ukt = w_uk[0].reshape(KV_RANK, MLA_HEADS, QK_NOPE).transpose(1, 2, 0)
    wqlat = _fold(wq_nope, w_ukt).transpose(1, 0, 2).reshape(Q_RANK, MLA_HEADS * LANES).astype(BF16)

    wqr = wq_rope.reshape(Q_RANK, MLA_HEADS * QK_ROPE).astype(BF16)
    wqrs = wq_rope[:, :, swap].reshape(Q_RANK, MLA_HEADS * QK_ROPE).astype(BF16)

    per_row = LANES // half
    freqs = ROPE_THETA ** (-jnp.arange(0, QK_ROPE, 2, dtype=F32) / QK_ROPE)
    pos8 = jnp.repeat(positions.astype(F32).reshape(N_TOK // per_row, per_row), half, axis=1)
    cos8, sin8 = _rope_tables(pos8, jnp.tile(freqs, per_row)[None, :])
    cos16 = cos8.reshape(N_TOK, half)
    sin16 = sin8.reshape(N_TOK, half)
    cos_t = jnp.tile(jnp.concatenate([cos16, cos16], axis=1), (1, LANES // QK_ROPE))
    sin_t = jnp.tile(jnp.concatenate([-sin16, sin16], axis=1), (1, LANES // QK_ROPE))

    head_of_col = jnp.arange(MLA_WIDTH) // V_HEAD
    wuv = jnp.where(head_of_col[None, None, :] == jnp.arange(MLA_HEADS)[:, None, None],
                    w_uv[0][None], 0.0).astype(BF16)
    wuv = wuv.reshape(MLA_HEADS * KV_RANK, MLA_WIDTH)

    wsp = w_spatial[0].astype(BF16)
    bs = b_spatial[0]
    bsp = jnp.repeat(bs.reshape(SGU_GROUPS // 2, 2, SGU_LEN).transpose(0, 2, 1), SGU_GROUP_DIM, axis=2)

    wr32 = jnp.concatenate([w_router_expert[0].transpose(1, 0, 2).reshape(D_MODEL, N_EXPERTS),
                            w_router_group[0]], axis=1)
    wr32 = pad_cols(wr32, LANES)
    wr_hi = wr32.astype(BF16)
    wr_lo = (wr32 - wr_hi.astype(F32)).astype(BF16)
    wr = jnp.concatenate([wr_hi, wr_lo], axis=1)
    br = pad_cols(jnp.concatenate([b_router_expert[0].reshape(-1), b_router_group[0]])[None, :], LANES)

    xf = x.reshape(N_TOK, D_MODEL)
    q, kt, v, p, ga = _inproj(
        xf, cos_t, sin_t, g_attn_norm, win, g_cq, g_ckv, wqlat, wqr, wqrs,
        sgu_gain, sgu_bias, wsp, bsp, w_o_sgu[0].astype(BF16))
    attn = _attention(q, kt, v, wuv)
    x1, h2, mf, slot_cols, slot_rows, c8_rows, loff8_rows = _mix(
        attn.reshape(N_TOK, MLA_WIDTH), ga, p, xf, w_o_attn[0].astype(BF16), w_out[0].astype(BF16),
        g_ffn_norm, wr, br)

    blk8 = MOE_T // CHUNK_ROWS
    c8 = c8_rows[::8, :N_EXPERTS]
    loff8 = loff8_rows[::8, :N_EXPERTS]
    tot8 = jnp.sum(c8, axis=0)
    pad8 = (tot8 + blk8 - 1) // blk8 * blk8
    gend8 = jnp.cumsum(pad8)
    gstart8 = gend8 - pad8
    gbase8 = gstart8[None, :] + jnp.cumsum(c8, axis=0) - c8
    nvb = (gend8[-1:] // blk8).astype(jnp.int32)
    blk_start8 = jnp.arange(MOE_BLOCKS, dtype=jnp.int32) * blk8
    blk_e = jnp.minimum(jnp.sum((gend8[None, :] <= blk_start8[:, None]).astype(jnp.int32), axis=1),
                        N_EXPERTS - 1)
    def flat_list(count, slots, first_local, first_global, step):
        off = jnp.cumsum(count, axis=1) - count
        k = jnp.arange(slots, dtype=jnp.int32)
        in_run = (off[:, None, :] <= k[None, :, None]) & (k[None, :, None] < (off + count)[:, None, :])

        def pick(first):
            return (jnp.sum(jnp.where(in_run, (first - step * off)[:, None, :], 0), axis=-1)
                    + step * k[None, :]).reshape(-1).astype(jnp.int32)

        return jnp.sum(count, axis=1).astype(jnp.int32), pick(first_local), pick(first_global)

    npair, lpair, gpair = flat_list(c8 // 2, PAIR_SLOTS, loff8, gbase8, 2)
    nsingle, lsingle, gsingle = flat_list(c8 % 2, N_EXPERTS, loff8 + c8 - 1, gbase8 + c8 - 1, 0)
    run_tables = (npair, nsingle, lpair, gpair, lsingle, gsingle)

    xs = _dispatch(run_tables + ((gstart8 + tot8).astype(jnp.int32), (pad8 - tot8).astype(jnp.int32), nvb),
                   slot_rows, h2)
    ys = _experts(blk_e, nvb, xs, w_exp_gate[0], w_exp_up[0], w_exp_down[0])
    out = _combine(run_tables, slot_cols, x1, mf, g_final.reshape(1, D_MODEL), ys)
    return out.reshape(BATCH, SEQ, D_MODEL)
```

```python
import jax
import jax.numpy as jnp
from jax import lax
from jax.experimental import pallas as pl
from jax.experimental.pallas import tpu as pltpu

D_MODEL = 1024
BATCH = 2
SEQ = 8192
N_TOK = BATCH * SEQ
CHUNK = 64
EPS = 1e-6
MLA_HEADS = 8
Q_RANK = 256
KV_RANK = 128
QK_NOPE = 64
QK_ROPE = 32
V_HEAD = 64
MLA_WIDTH = MLA_HEADS * V_HEAD
ROPE_THETA = 10000.0
SGU_GROUPS = 8
SGU_GROUP_DIM = 64
SGU_WIDTH = SGU_GROUPS * SGU_GROUP_DIM
SGU_LEN = 128
N_GROUPS = 4
EXPERTS_PER_GROUP = 8
N_EXPERTS = N_GROUPS * EXPERTS_PER_GROUP
TOP_K = 2
D_EXPERT = 256

LANES = 128
QK_PAD = 2 * LANES
V_PAD = 2 * LANES
SCALE = (QK_NOPE + QK_ROPE) ** -0.5
LOG2E = 1.4426950408889634
Q_SCALE = SCALE * LOG2E
NEG = -1e30

TM = 256
TMI = 512
TMX = 1024
TQ = 128
TK = 512
ATTN_UNROLL_SHIFT = 1
ATTN_UNROLL = 1 << ATTN_UNROLL_SHIFT
TK_SHIFT = TK.bit_length() - 1
CHUNK_SHIFT = CHUNK.bit_length() - 1
assert 1 << TK_SHIFT == TK and 1 << CHUNK_SHIFT == CHUNK
MOE_T = 512
CHUNK_ROWS = 16
TILE_ROWS = -(-(TOP_K * TM + N_EXPERTS * (CHUNK_ROWS - 1) + CHUNK_ROWS) // 256) * 256
TILE_CHUNKS = TILE_ROWS // CHUNK_ROWS
PAIR_SLOTS = TILE_CHUNKS // 2
CHUNK_UNROLL_SHIFT = 2
CHUNK_UNROLL = 1 << CHUNK_UNROLL_SHIFT
MOE_ROWS_MAX = (N_TOK * TOP_K + (N_TOK // TM) * N_EXPERTS * (CHUNK_ROWS - 1)
                + N_EXPERTS * (MOE_T - CHUNK_ROWS))
MOE_BLOCKS = -(-MOE_ROWS_MAX // MOE_T)
MOE_ROWS = MOE_BLOCKS * MOE_T

C_Q = 0
C_KV = C_Q + Q_RANK
C_KR = C_KV + KV_RANK
C_KRS = C_KR + LANES
C_U = C_KRS + LANES
C_V = C_U + SGU_WIDTH
C_GA = C_V + SGU_WIDTH
C_GB = C_GA + D_MODEL
C_END = C_GB + D_MODEL

F32 = jnp.float32
BF16 = jnp.bfloat16


def _dot(a, b):
    return jnp.dot(a, b, preferred_element_type=F32)


def _rms(x, g):
    return x * lax.rsqrt(jnp.mean(x * x, axis=-1, keepdims=True) + EPS) * g


def _fold_kernel(a_ref, b_ref, o_ref):
    o_ref[...] = Q_SCALE * jnp.dot(a_ref[...], b_ref[...], preferred_element_type=F32,
                                 precision=lax.Precision.HIGHEST)


def _fold(w_uq_nope, w_ukt):
    return pl.pallas_call(
        _fold_kernel,
        grid=(MLA_HEADS,),
        in_specs=[pl.BlockSpec((None, Q_RANK, QK_NOPE), lambda h: (h, 0, 0)),
                  pl.BlockSpec((None, QK_NOPE, KV_RANK), lambda h: (h, 0, 0))],
        out_specs=pl.BlockSpec((None, Q_RANK, KV_RANK), lambda h: (h, 0, 0)),
        out_shape=jax.ShapeDtypeStruct((MLA_HEADS, Q_RANK, KV_RANK), F32),
        name="fold",
    )(w_uq_nope, w_ukt)


def _rope_kernel(pos_ref, freq_ref, cos_ref, sin_ref):
    ang = pos_ref[...] * freq_ref[...]
    cos_ref[...] = jnp.cos(ang)
    sin_ref[...] = jnp.sin(ang)


def _rope_tables(pos8, freq8):
    rows = pos8.shape[0]
    blk = pl.BlockSpec((TM, LANES), lambda i: (i, 0))
    return pl.pallas_call(
        _rope_kernel,
        grid=(rows // TM,),
        in_specs=[blk, pl.BlockSpec((1, LANES), lambda i: (0, 0))],
        out_specs=[blk, blk],
        out_shape=[jax.ShapeDtypeStruct((rows, LANES), F32)] * 2,
        name="rope_tables",
    )(pos8, freq8)


def _inproj_kernel(x_ref, cos_ref, sin_ref, gattn_ref, win_ref, gcq_ref, gckv_ref, wqlat_ref, wqr_ref,
                   wqrs_ref, sgain_ref, sbias_ref, wsp_ref, bsp_ref, wosgu_ref,
                   q_ref, kt_ref, v_ref, p_ref, ga_ref):
    hb = _rms(x_ref[...], gattn_ref[...]).astype(BF16)

    def proj(a, b):
        return _dot(hb, win_ref[:, a:b])

    cos_t = cos_ref[...]
    sin_t = sin_ref[...]

    cqn = _rms(proj(C_Q, C_KV), gcq_ref[...]).astype(BF16)
    qlat = _dot(cqn, wqlat_ref[...])
    groups = MLA_HEADS * QK_ROPE // LANES
    rot = (_dot(cqn, wqr_ref[...]) * jnp.concatenate([cos_t * Q_SCALE] * groups, axis=1)
           + _dot(cqn, wqrs_ref[...]) * jnp.concatenate([sin_t * Q_SCALE] * groups, axis=1))
    lane = lax.broadcasted_iota(jnp.int32, (TQ, LANES), 1)
    for r in range(TMI // TQ):
        t0, t1 = r * TQ, (r + 1) * TQ
        for h in range(MLA_HEADS):
            a, b = h * LANES, (h + 1) * LANES
            q_ref[r, h * TQ:(h + 1) * TQ, :LANES] = qlat[t0:t1, a:b].astype(BF16)
            g0 = (h * QK_ROPE) // LANES * LANES
            off = (h * QK_ROPE) % LANES
            window = (lane >= off) & (lane < off + QK_ROPE)
            q_ref[r, h * TQ:(h + 1) * TQ, LANES:] = jnp.where(
                window, rot[t0:t1, g0:g0 + LANES], 0.0).astype(BF16)

    zk = proj(C_KV, C_U)
    ckvn = _rms(zk[:, :KV_RANK], gckv_ref[...])
    krope = zk[:, KV_RANK:KV_RANK + LANES] * cos_t + zk[:, KV_RANK + LANES:] * sin_t
    kt_ref[...] = jnp.concatenate([ckvn, krope], axis=1).T.astype(BF16)
    v_ref[...] = jnp.concatenate([ckvn, jnp.ones_like(ckvn)], axis=1).astype(BF16)

    u = jax.nn.gelu(proj(C_U, C_V))
    v = jax.nn.gelu(proj(C_V, C_GA))
    mu = jnp.mean(v, axis=-1, keepdims=True)
    vc = v - mu
    var = jnp.mean(vc * vc, axis=-1, keepdims=True)
    vb = (vc * lax.rsqrt(var + EPS) * sgain_ref[...] + sbias_ref[...]).astype(BF16)
    row = lax.broadcasted_iota(jnp.int32, (SGU_LEN, SGU_LEN), 0)
    col = lax.broadcasted_iota(jnp.int32, (SGU_LEN, SGU_LEN), 1)
    causal = (row >> CHUNK_SHIFT) >= (col >> CHUNK_SHIFT)
    low_half = col < SGU_GROUP_DIM
    zero_w = jnp.zeros((SGU_LEN, SGU_LEN), BF16)
    w_pairs = [jnp.concatenate([jnp.where(causal, wsp_ref[2 * p], zero_w),
                                jnp.where(causal, wsp_ref[2 * p + 1], zero_w)], axis=1)
               for p in range(SGU_WIDTH // LANES)]
    row_blocks = []
    for r in range(TMI // SGU_LEN):
        pieces = []
        for p in range(SGU_WIDTH // LANES):
            blk = vb[r * SGU_LEN:(r + 1) * SGU_LEN, p * LANES:(p + 1) * LANES]
            stacked = jnp.concatenate([jnp.where(low_half, blk, zero_w), jnp.where(low_half, zero_w, blk)], axis=0)
            sv = _dot(w_pairs[p], stacked) + bsp_ref[p]
            pieces.append((u[r * SGU_LEN:(r + 1) * SGU_LEN, p * LANES:(p + 1) * LANES] * sv).astype(BF16))
        row_blocks.append(jnp.concatenate(pieces, axis=1))
    sgu = _dot(jnp.concatenate(row_blocks, axis=0), wosgu_ref[...])

    ga_ref[...] = jax.nn.sigmoid(proj(C_GA, C_GB)).astype(BF16)
    p_ref[...] = (jax.nn.sigmoid(proj(C_GB, C_END)) * sgu).astype(BF16)


def _inproj(x, cos_t, sin_t, gattn, win, gcq, gckv, wqlat, wqr, wqrs, sgain, sbias, wsp, bsp, wosgu):
    nt = N_TOK // TMI
    per_b = SEQ // TMI
    per_k = TK // TMI

    def const(shape):
        return pl.BlockSpec(shape, lambda i: (0,) * len(shape))

    return pl.pallas_call(
        _inproj_kernel,
        grid=(nt,),
        in_specs=[pl.BlockSpec((TMI, D_MODEL), lambda i: (i, 0)),
                  pl.BlockSpec((TMI, LANES), lambda i: (i, 0)), pl.BlockSpec((TMI, LANES), lambda i: (i, 0)),
                  const((1, D_MODEL)), const((D_MODEL, C_END)), const((1, Q_RANK)), const((1, KV_RANK)),
                  const((Q_RANK, MLA_HEADS * LANES)), const((Q_RANK, MLA_HEADS * QK_ROPE)),
                  const((Q_RANK, MLA_HEADS * QK_ROPE)),
                  const((1, SGU_WIDTH)), const((1, SGU_WIDTH)),
                  const((SGU_GROUPS, SGU_LEN, SGU_LEN)), const((SGU_GROUPS // 2, SGU_LEN, LANES)),
                  const((SGU_WIDTH, D_MODEL))],
        out_specs=[pl.BlockSpec((TMI // TQ, MLA_HEADS * TQ, QK_PAD), lambda i: (i, 0, 0)),
                   pl.BlockSpec((None, None, QK_PAD, TMI),
                                lambda i: (i // per_b, (i % per_b) // per_k, 0, i % per_k)),
                   pl.BlockSpec((None, None, TMI, V_PAD),
                                lambda i: (i // per_b, (i % per_b) // per_k, i % per_k, 0)),
                   pl.BlockSpec((TMI, D_MODEL), lambda i: (i, 0)),
                   pl.BlockSpec((TMI, D_MODEL), lambda i: (i, 0))],
        out_shape=[jax.ShapeDtypeStruct((N_TOK // TQ, MLA_HEADS * TQ, QK_PAD), BF16),
                   jax.ShapeDtypeStruct((BATCH, SEQ // TK, QK_PAD, TK), BF16),
                   jax.ShapeDtypeStruct((BATCH, SEQ // TK, TK, V_PAD), BF16),
                   jax.ShapeDtypeStruct((N_TOK, D_MODEL), BF16),
                   jax.ShapeDtypeStruct((N_TOK, D_MODEL), BF16)],
        compiler_params=pltpu.CompilerParams(dimension_semantics=("arbitrary",),
                                             vmem_limit_bytes=56 * 1024 * 1024),
        name="inproj",
    )(x, cos_t, sin_t, gattn, win, gcq, gckv, wqlat, wqr, wqrs, sgain, sbias, wsp, bsp, wosgu)


def _attn_kernel(q_ref, kt_ref, v_ref, wuv_ref, o_ref, m_ref, acc_ref, s_ref, p_ref, a_ref):
    qi = pl.program_id(1)
    m_ref[...] = jnp.full(m_ref.shape, NEG, F32)
    acc_ref[...] = jnp.zeros(acc_ref.shape, F32)

    def scores(j, slot):
        s_ref[slot] = _dot(q_ref[...], kt_ref[j])

    def update(j, slot, width=None):
        masked = width is not None
        w = width if masked else TK
        if masked:
            q_chunk = ((qi * TQ - j * TK) >> CHUNK_SHIFT) + (
                lax.broadcasted_iota(jnp.int32, (TQ, w), 0) >> CHUNK_SHIFT)
            k_chunk = lax.broadcasted_iota(jnp.int32, (TQ, w), 1) >> CHUNK_SHIFT
            visible = k_chunk <= q_chunk
        for h in range(MLA_HEADS):
            rows = slice(h * TQ, (h + 1) * TQ)
            s = s_ref[slot, rows, :w]
            if masked:
                s = jnp.where(visible, s, NEG)
            m_prev = m_ref[rows, :]
            m_new = jnp.maximum(m_prev, jnp.max(s, axis=-1, keepdims=True))
            m_ref[rows, :] = m_new
            a_ref[rows, :] = jnp.exp2(m_prev - m_new)
            s = s_ref[slot, rows, :w]
            if masked:
                s = jnp.where(visible, s, NEG)
            p_ref[rows, :w] = jnp.exp2(s - jnp.concatenate([m_new] * (w // LANES), axis=1)).astype(BF16)
        alpha = a_ref[...]
        acc_ref[...] = jnp.concatenate([alpha, alpha], axis=1) * acc_ref[...] + _dot(
            p_ref[:, :w], v_ref[j, :w, :])

    diag = (qi * TQ) >> TK_SHIFT
    scores(0, 0)

    def run(first, count, last_masked):
        for u in range(count):
            if not (last_masked and u == count - 1):
                scores(first + u + 1, (u + 1) % 2)
                update(first + u, u % 2)
            else:
                for sub in range(TK // TQ):
                    @pl.when((qi & (TK // TQ - 1)) == sub)
                    def _(u=u, sub=sub):
                        update(first + u, u % 2, (sub + 1) * TQ)

    def body(t, carry):
        run(ATTN_UNROLL * t, ATTN_UNROLL, False)
        return carry

    trips = diag >> ATTN_UNROLL_SHIFT
    lax.fori_loop(0, trips, body, 0)
    done = trips << ATTN_UNROLL_SHIFT
    for r in range(ATTN_UNROLL):
        @pl.when(diag - done == r)
        def _(r=r):
            run(done, r + 1, True)

    o_lat = (acc_ref[:, :KV_RANK] / acc_ref[:, KV_RANK:]).astype(BF16)
    o_cat = jnp.concatenate([o_lat[h * TQ:(h + 1) * TQ] for h in range(MLA_HEADS)], axis=1)
    o_ref[...] = _dot(o_cat, wuv_ref[...]).astype(BF16)


def _attention(q, kt, v, wuv):
    nk = SEQ // TK
    return pl.pallas_call(
        _attn_kernel,
        grid=(BATCH, SEQ // TQ),
        in_specs=[pl.BlockSpec((None, MLA_HEADS * TQ, QK_PAD), lambda b, i: (b * (SEQ // TQ) + i, 0, 0)),
                  pl.BlockSpec((None, nk, QK_PAD, TK), lambda b, i: (b, 0, 0, 0)),
                  pl.BlockSpec((None, nk, TK, V_PAD), lambda b, i: (b, 0, 0, 0)),
                  pl.BlockSpec((MLA_HEADS * KV_RANK, MLA_WIDTH), lambda b, i: (0, 0))],
        out_specs=pl.BlockSpec((None, TQ, MLA_WIDTH), lambda b, i: (b, i, 0)),
        out_shape=jax.ShapeDtypeStruct((BATCH, SEQ, MLA_WIDTH), BF16),
        scratch_shapes=[pltpu.VMEM((MLA_HEADS * TQ, LANES), F32),
                        pltpu.VMEM((MLA_HEADS * TQ, V_PAD), F32),
                        pltpu.VMEM((2, MLA_HEADS * TQ, TK), F32),
                        pltpu.VMEM((MLA_HEADS * TQ, TK), BF16),
                        pltpu.VMEM((MLA_HEADS * TQ, LANES), F32)],
        compiler_params=pltpu.CompilerParams(dimension_semantics=("arbitrary", "arbitrary"),
                                             vmem_limit_bytes=40 * 1024 * 1024),
        name="attention",
    )(q, kt, v, wuv)


def _mix_kernel(attn_ref, ga_ref, p_ref, x_ref, woa_ref, wout_ref, gffn_ref, wr_ref, br_ref,
                x1_ref, h2_ref, mf_ref, slot_col_ref, slot_row_ref, c8_ref, loff8_ref):
    a = _dot(attn_ref[...], woa_ref[...])
    mix = (ga_ref[...].astype(F32) * a + p_ref[...].astype(F32)).astype(BF16)
    x1 = x_ref[...] + _dot(mix, wout_ref[...])
    x1_ref[...] = x1
    h2 = _rms(x1, gffn_ref[...])
    h2_ref[...] = h2.astype(BF16)

    hi = h2.astype(BF16)
    lo = (h2 - hi.astype(F32)).astype(BF16)
    r1 = _dot(hi, wr_ref[...])
    r2 = _dot(lo, wr_ref[:, :LANES])
    logits_all = r1[:, :LANES] + r1[:, LANES:] + r2 + br_ref[...]

    for r in range(TMX // TM):
        mf, slots, c8_rows, loff8_rows = _route_tile(logits_all[r * TM:(r + 1) * TM])
        mf_ref[r * TM:(r + 1) * TM, :] = mf
        slot_col_ref[r * TM:(r + 1) * TM, :] = slots.astype(jnp.int32)
        slot_row_ref[8 * r:8 * (r + 1), :] = slots.T[:8].astype(jnp.int32)
        c8_ref[8 * r:8 * (r + 1), :] = c8_rows.astype(jnp.int32)
        loff8_ref[8 * r:8 * (r + 1), :] = loff8_rows.astype(jnp.int32)


def _route_tile(logits):
    lane_i = lax.broadcasted_iota(jnp.int32, (TM, LANES), 1)
    lane = lane_i.astype(F32)
    lane_group = (lane_i >> 3).astype(F32)
    ninf = -jnp.inf
    is_group = (lane_i >= N_EXPERTS) & (lane_i < N_EXPERTS + N_GROUPS)
    lg = jnp.where(is_group, logits, ninf)
    gmax = jnp.max(lg, axis=-1, keepdims=True)
    gsum = jnp.sum(jnp.exp(lg - gmax), axis=-1, keepdims=True)
    p_top = 1.0 / gsum
    g_idx = jnp.min(jnp.where(lg == gmax, lane - N_EXPERTS, float(N_GROUPS)), axis=-1, keepdims=True)
    le = jnp.where((lane_i < N_EXPERTS) & (lane_group == g_idx), logits, ninf)
    t1 = jnp.max(le, axis=-1, keepdims=True)
    e1 = jnp.min(jnp.where(le == t1, lane, float(LANES)), axis=-1, keepdims=True)
    le2 = jnp.where(lane == e1, ninf, le)
    t2 = jnp.max(le2, axis=-1, keepdims=True)
    e2 = jnp.min(jnp.where(le2 == t2, lane, float(LANES)), axis=-1, keepdims=True)
    ex = jnp.exp(t2 - t1)
    w1 = p_top / (1.0 + ex)
    w2 = p_top * ex / (1.0 + ex)

    sel1 = lane == e1
    sel2 = lane == e2
    onehot = jnp.where(sel1 | sel2, 1.0, 0.0)
    rr = lax.broadcasted_iota(jnp.int32, (TM, TM), 0)
    cc = lax.broadcasted_iota(jnp.int32, (TM, TM), 1)
    ltri = jnp.where(cc < rr, 1.0, 0.0).astype(BF16)
    rank = _dot(ltri, onehot.astype(BF16))
    cnt = jnp.sum(onehot, axis=0, keepdims=True)
    c8 = jnp.floor((cnt + (CHUNK_ROWS - 1)) * (1.0 / CHUNK_ROWS))
    ur = lax.broadcasted_iota(jnp.int32, (LANES, LANES), 0)
    uc = lax.broadcasted_iota(jnp.int32, (LANES, LANES), 1)
    upper = jnp.where(ur < uc, 1.0, 0.0).astype(BF16)
    c8_rows = jnp.broadcast_to(c8, (8, LANES))
    loff8_rows = _dot(c8_rows.astype(BF16), upper)
    slot_all = CHUNK_ROWS * loff8_rows[0:1] + rank
    slot1 = jnp.sum(jnp.where(sel1, slot_all, 0.0), axis=-1, keepdims=True)
    slot2 = jnp.sum(jnp.where(sel2, slot_all, 0.0), axis=-1, keepdims=True)

    slots = jnp.where(lane_i == 0, slot1, jnp.where(lane_i == 1, slot2, 0.0))
    return jnp.where(lane_i == 0, w1, w2), slots, c8_rows, loff8_rows


def _mix(attn, ga, p, x, woa, wout, gffn, wr, br):
    nt = N_TOK // TM
    sub = TMX // TM

    def const(shape):
        return pl.BlockSpec(shape, lambda i: (0,) * len(shape))

    def rows(width):
        return pl.BlockSpec((TMX, width), lambda i: (i, 0))

    return pl.pallas_call(
        _mix_kernel,
        grid=(N_TOK // TMX,),
        in_specs=[rows(MLA_WIDTH), rows(D_MODEL), rows(D_MODEL), rows(D_MODEL),
                  const((MLA_WIDTH, D_MODEL)), const((D_MODEL, D_MODEL)), const((1, D_MODEL)),
                  const((D_MODEL, 2 * LANES)), const((1, LANES))],
        out_specs=[rows(D_MODEL), rows(D_MODEL), rows(LANES), rows(LANES),
                   pl.BlockSpec((8 * sub, TM), lambda i: (i, 0)),
                   pl.BlockSpec((8 * sub, LANES), lambda i: (i, 0)),
                   pl.BlockSpec((8 * sub, LANES), lambda i: (i, 0))],
        out_shape=[jax.ShapeDtypeStruct((N_TOK, D_MODEL), F32),
                   jax.ShapeDtypeStruct((N_TOK, D_MODEL), BF16),
                   jax.ShapeDtypeStruct((N_TOK, LANES), F32),
                   jax.ShapeDtypeStruct((N_TOK, LANES), jnp.int32),
                   jax.ShapeDtypeStruct((nt * 8, TM), jnp.int32),
                   jax.ShapeDtypeStruct((nt * 8, LANES), jnp.int32),
                   jax.ShapeDtypeStruct((nt * 8, LANES), jnp.int32)],
        compiler_params=pltpu.CompilerParams(dimension_semantics=("arbitrary",),
                                             vmem_limit_bytes=40 * 1024 * 1024),
        name="mix",
    )(attn, ga, p, x, woa, wout, gffn, wr, br)


def _chunk_copy(src, dst, s8, d8, sem, nchunks=1):
    rows = nchunks * CHUNK_ROWS
    return pltpu.make_async_copy(src.at[pl.ds(pl.multiple_of(s8 * CHUNK_ROWS, CHUNK_ROWS), rows)],
                                 dst.at[pl.ds(pl.multiple_of(d8 * CHUNK_ROWS, CHUNK_ROWS), rows)], sem)


def _unrolled_loop(n, fn):
    groups = n >> CHUNK_UNROLL_SHIFT

    def group(g, carry):
        for u in range(CHUNK_UNROLL):
            fn(g * CHUNK_UNROLL + u)
        return carry

    lax.fori_loop(0, groups, group, 0)

    def single(k, carry):
        fn(k)
        return carry

    lax.fori_loop(groups << CHUNK_UNROLL_SHIFT, n, single, 0)


def _for_each_copy(t, copy_tables, fn):
    npair_ref, nsingle_ref, lpair_ref, gpair_ref, lsingle_ref, gsingle_ref = copy_tables
    pb = t * PAIR_SLOTS
    sb = t * N_EXPERTS
    _unrolled_loop(npair_ref[t], lambda k: fn(lpair_ref[pb + k], gpair_ref[pb + k], 2))
    _unrolled_loop(nsingle_ref[t], lambda k: fn(lsingle_ref[sb + k], gsingle_ref[sb + k], 1))


def _tile_chunks(t, copy_tables):
    return 2 * copy_tables[0][t] + copy_tables[1][t]


def _wait_chunks(n, src, dst, sem):
    groups = n >> CHUNK_UNROLL_SHIFT
    rows = CHUNK_UNROLL * CHUNK_ROWS

    def group(g, carry):
        pltpu.make_async_copy(src.at[pl.ds(0, rows)], dst.at[pl.ds(0, rows)], sem).wait()
        return carry

    lax.fori_loop(0, groups, group, 0)

    def single(k, carry):
        _chunk_copy(src, dst, 0, 0, sem).wait()
        return carry

    lax.fori_loop(groups << CHUNK_UNROLL_SHIFT, n, single, 0)


def _dispatch_kernel(npair_ref, nsingle_ref, lpair_ref, gpair_ref, lsingle_ref, gsingle_ref,
                     zstart8_ref, zcnt8_ref, nvb_ref,
                     slot_ref, h2_ref, xs_hbm, sbuf, zbuf, sem, zsem):
    copy_tables = (npair_ref, nsingle_ref, lpair_ref, gpair_ref, lsingle_ref, gsingle_ref)
    t = pl.program_id(0)
    last = pl.num_programs(0) - 1
    cur = t % 2

    def zero_chunk_copy(d8):
        return _chunk_copy(zbuf, xs_hbm, 0, d8, zsem)

    def zero_block_copy(b):
        return pltpu.make_async_copy(
            zbuf, xs_hbm.at[pl.ds(pl.multiple_of(b * MOE_T, MOE_T), MOE_T)], zsem)

    def for_each_zero(chunk_fn, block_fn):
        def per_expert(e, carry):
            def per_chunk(j, carry2):
                chunk_fn(zstart8_ref[e] + j)
                return carry2

            lax.fori_loop(0, zcnt8_ref[e], per_chunk, 0)
            return carry

        lax.fori_loop(0, N_EXPERTS, per_expert, 0)

        def per_block(b, carry):
            block_fn(b)
            return carry

        lax.fori_loop(nvb_ref[0], MOE_BLOCKS, per_block, 0)

    @pl.when(t == 0)
    def _():
        zbuf[...] = jnp.zeros(zbuf.shape, BF16)
        for_each_zero(lambda d8: zero_chunk_copy(d8).start(), lambda b: zero_block_copy(b).start())

    row = lax.broadcasted_iota(jnp.int32, (TILE_ROWS, TM), 0)
    slots = slot_ref[...]
    perm = jnp.where((row == slots[0:1, :]) | (row == slots[1:2, :]), 1.0, 0.0).astype(BF16)
    sbuf[cur] = _dot(perm, h2_ref[...]).astype(BF16)
    _for_each_copy(t, copy_tables,
                   lambda lc, gc, n: _chunk_copy(sbuf.at[cur], xs_hbm, lc, gc, sem.at[cur], n).start())

    def wait_tile(tile, slot):
        _wait_chunks(_tile_chunks(tile, copy_tables), sbuf.at[slot], xs_hbm, sem.at[slot])

    @pl.when(t > 0)
    def _():
        wait_tile(t - 1, 1 - cur)

    @pl.when(t == last)
    def _():
        wait_tile(t, cur)
        for_each_zero(lambda d8: zero_chunk_copy(d8).wait(), lambda b: zero_block_copy(b).wait())


def _dispatch(tables, slot_rows, h2):
    return pl.pallas_call(
        _dispatch_kernel,
        grid_spec=pltpu.PrefetchScalarGridSpec(
            num_scalar_prefetch=len(tables),
            grid=(N_TOK // TM,),
            in_specs=[pl.BlockSpec((8, TM), lambda i, *_: (i, 0)),
                      pl.BlockSpec((TM, D_MODEL), lambda i, *_: (i, 0))],
            out_specs=pl.BlockSpec(memory_space=pl.ANY),
            scratch_shapes=[pltpu.VMEM((2, TILE_ROWS, D_MODEL), BF16), pltpu.VMEM((MOE_T, D_MODEL), BF16),
                            pltpu.SemaphoreType.DMA((2,)), pltpu.SemaphoreType.DMA(())]),
        out_shape=jax.ShapeDtypeStruct((MOE_ROWS, D_MODEL), BF16),
        compiler_params=pltpu.CompilerParams(dimension_semantics=("arbitrary",)),
        name="dispatch",
    )(*tables, slot_rows, h2)


def _expert_kernel(blk_e_ref, nvb_ref, xs_ref, wg_ref, wu_ref, wd_ref, ys_ref, wgb, wub, wdb):
    i = pl.program_id(0)
    live = i < nvb_ref[0]

    @pl.when(live & ((i == 0) | (blk_e_ref[i] != blk_e_ref[jnp.maximum(i - 1, 0)])))
    def _():
        wgb[...] = wg_ref[...].astype(BF16)
        wub[...] = wu_ref[...].astype(BF16)
        wdb[...] = wd_ref[...].astype(BF16)

    @pl.when(live)
    def _():
        xb = xs_ref[...]
        g = _dot(xb, wgb[...])
        u = _dot(xb, wub[...])
        hid = (jax.nn.silu(g) * u).astype(BF16)
        ys_ref[...] = _dot(hid, wdb[...]).astype(BF16)

    @pl.when(jnp.logical_not(live))
    def _():
        ys_ref[...] = jnp.zeros(ys_ref.shape, ys_ref.dtype)


def _experts(blk_e, nvb, xs, wg, wu, wd):
    def row_block(i, be, nv):
        return (jnp.minimum(i, nv[0] - 1), 0)

    def weight(i, be, nv):
        return (be[jnp.minimum(i, nv[0] - 1)], 0, 0)

    return pl.pallas_call(
        _expert_kernel,
        grid_spec=pltpu.PrefetchScalarGridSpec(
            num_scalar_prefetch=2,
            grid=(MOE_BLOCKS,),
            in_specs=[pl.BlockSpec((MOE_T, D_MODEL), row_block),
                      pl.BlockSpec((None, D_MODEL, D_EXPERT), weight),
                      pl.BlockSpec((None, D_MODEL, D_EXPERT), weight),
                      pl.BlockSpec((None, D_EXPERT, D_MODEL), weight)],
            out_specs=pl.BlockSpec((MOE_T, D_MODEL), lambda i, be, nv: (i, 0)),
            scratch_shapes=[pltpu.VMEM((D_MODEL, D_EXPERT), BF16), pltpu.VMEM((D_MODEL, D_EXPERT), BF16),
                            pltpu.VMEM((D_EXPERT, D_MODEL), BF16)]),
        out_shape=jax.ShapeDtypeStruct((MOE_ROWS, D_MODEL), BF16),
        compiler_params=pltpu.CompilerParams(dimension_semantics=("arbitrary",),
                                             vmem_limit_bytes=40 * 1024 * 1024),
        name="experts",
    )(blk_e, nvb, xs, wg, wu, wd)


def _combine_kernel(npair_ref, nsingle_ref, lpair_ref, gpair_ref, lsingle_ref, gsingle_ref,
                    slot_ref, x1_ref, mf_ref, gfin_ref, ys_hbm, o_ref, ybuf, sem):
    copy_tables = (npair_ref, nsingle_ref, lpair_ref, gpair_ref, lsingle_ref, gsingle_ref)
    t = pl.program_id(0)
    cur = t % 2

    def fetch(tile, slot):
        _for_each_copy(tile, copy_tables,
                       lambda lc, gc, n: _chunk_copy(ys_hbm, ybuf.at[slot], gc, lc, sem.at[slot], n).start())

    @pl.when(t == 0)
    def _():
        ybuf[...] = jnp.zeros(ybuf.shape, BF16)
        fetch(0, 0)

    @pl.when(t + 1 < pl.num_programs(0))
    def _():
        fetch(t + 1, 1 - cur)

    _wait_chunks(_tile_chunks(t, copy_tables), ys_hbm, ybuf.at[cur], sem.at[cur])

    yb = ybuf[cur]
    col = lax.broadcasted_iota(jnp.int32, (TM, TILE_ROWS), 1)
    slots = slot_ref[...]
    y1 = _dot(jnp.where(col == slots[:, 0:1], 1.0, 0.0).astype(BF16), yb)
    y2 = _dot(jnp.where(col == slots[:, 1:2], 1.0, 0.0).astype(BF16), yb)
    mf = mf_ref[...]
    x2 = x1_ref[...] + mf[:, 0:1] * y1 + mf[:, 1:2] * y2
    o_ref[...] = _rms(x2, gfin_ref[...])


def _combine(tables, slot_cols, x1, mf, gfin, ys):
    return pl.pallas_call(
        _combine_kernel,
        grid_spec=pltpu.PrefetchScalarGridSpec(
            num_scalar_prefetch=len(tables),
            grid=(N_TOK // TM,),
            in_specs=[pl.BlockSpec((TM, LANES), lambda i, *_: (i, 0)),
                      pl.BlockSpec((TM, D_MODEL), lambda i, *_: (i, 0)),
                      pl.BlockSpec((TM, LANES), lambda i, *_: (i, 0)),
                      pl.BlockSpec((1, D_MODEL), lambda i, *_: (0, 0)),
                      pl.BlockSpec(memory_space=pl.ANY)],
            out_specs=pl.BlockSpec((TM, D_MODEL), lambda i, *_: (i, 0)),
            scratch_shapes=[pltpu.VMEM((2, TILE_ROWS, D_MODEL), BF16), pltpu.SemaphoreType.DMA((2,))]),
        out_shape=jax.ShapeDtypeStruct((N_TOK, D_MODEL), F32),
        compiler_params=pltpu.CompilerParams(dimension_semantics=("arbitrary",)),
        name="combine",
    )(*tables, slot_cols, x1, mf, gfin, ys)


def kernel(x, positions, g_attn_norm, w_in, g_cq, w_uq, g_ckv, w_uk, w_uv, w_o_attn, sgu_gain, sgu_bias, w_spatial, b_spatial, w_o_sgu, w_out, g_ffn_norm, w_router_group, b_router_group, w_router_expert, b_router_expert, w_exp_gate, w_exp_up, w_exp_down, g_final):
    assert x.shape == (BATCH, SEQ, D_MODEL) and w_in.shape[0] == 1
    half = QK_ROPE // 2
    swap = jnp.concatenate([jnp.arange(half, QK_ROPE), jnp.arange(0, half)])

    def pad_cols(w, width):
        return jnp.pad(w, ((0, 0), (0, width - w.shape[1])))

    wi = w_in[0]
    c0 = Q_RANK + KV_RANK
    kr = wi[:, c0:c0 + QK_ROPE]
    c1 = c0 + QK_ROPE
    win = jnp.concatenate([
        wi[:, :c0], jnp.tile(kr, (1, LANES // QK_ROPE)), jnp.tile(kr[:, swap], (1, LANES // QK_ROPE)),
        wi[:, c1:]], axis=1).astype(BF16)

    wq = w_uq[0].reshape(Q_RANK, MLA_HEADS, QK_NOPE + QK_ROPE)
    wq_nope = wq[:, :, :QK_NOPE].transpose(1, 0, 2)
    wq_rope = wq[:, :, QK_NOPE:]
    w_ukt = w_uk[0].reshape(KV_RANK, MLA_HEADS, QK_NOPE).transpose(1, 2, 0)
    wqlat = _fold(wq_nope, w_ukt).transpose(1, 0, 2).reshape(Q_RANK, MLA_HEADS * LANES).astype(BF16)

    wqr = wq_rope.reshape(Q_RANK, MLA_HEADS * QK_ROPE).astype(BF16)
    wqrs = wq_rope[:, :, swap].reshape(Q_RANK, MLA_HEADS * QK_ROPE).astype(BF16)

    per_row = LANES // half
    freqs = ROPE_THETA ** (-jnp.arange(0, QK_ROPE, 2, dtype=F32) / QK_ROPE)
    pos8 = jnp.repeat(positions.astype(F32).reshape(N_TOK // per_row, per_row), half, axis=1)
    cos8, sin8 = _rope_tables(pos8, jnp.tile(freqs, per_row)[None, :])
    cos16 = cos8.reshape(N_TOK, half)
    sin16 = sin8.reshape(N_TOK, half)
    cos_t = jnp.tile(jnp.concatenate([cos16, cos16], axis=1), (1, LANES // QK_ROPE))
    sin_t = jnp.tile(jnp.concatenate([-sin16, sin16], axis=1), (1, LANES // QK_ROPE))

    head_of_col = jnp.arange(MLA_WIDTH) // V_HEAD
    wuv = jnp.where(head_of_col[None, None, :] == jnp.arange(MLA_HEADS)[:, None, None],
                    w_uv[0][None], 0.0).astype(BF16)
    wuv = wuv.reshape(MLA_HEADS * KV_RANK, MLA_WIDTH)

    wsp = w_spatial[0].astype(BF16)
    bs = b_spatial[0]
    bsp = jnp.repeat(bs.reshape(SGU_GROUPS // 2, 2, SGU_LEN).transpose(0, 2, 1), SGU_GROUP_DIM, axis=2)

    wr32 = jnp.concatenate([w_router_expert[0].transpose(1, 0, 2).reshape(D_MODEL, N_EXPERTS),
                            w_router_group[0]], axis=1)
    wr32 = pad_cols(wr32, LANES)
    wr_hi = wr32.astype(BF16)
    wr_lo = (wr32 - wr_hi.astype(F32)).astype(BF16)
    wr = jnp.concatenate([wr_hi, wr_lo], axis=1)
    br = pad_cols(jnp.concatenate([b_router_expert[0].reshape(-1), b_router_group[0]])[None, :], LANES)

    xf = x.reshape(N_TOK, D_MODEL)
    q, kt, v, p, ga = _inproj(
        xf, cos_t, sin_t, g_attn_norm, win, g_cq, g_ckv, wqlat, wqr, wqrs,
        sgu_gain, sgu_bias, wsp, bsp, w_o_sgu[0].astype(BF16))
    attn = _attention(q, kt, v, wuv)
    x1, h2, mf, slot_cols, slot_rows, c8_rows, loff8_rows = _mix(
        attn.reshape(N_TOK, MLA_WIDTH), ga, p, xf, w_o_attn[0].astype(BF16), w_out[0].astype(BF16),
        g_ffn_norm, wr, br)

    blk8 = MOE_T // CHUNK_ROWS
    c8 = c8_rows[::8, :N_EXPERTS]
    loff8 = loff8_rows[::8, :N_EXPERTS]
    tot8 = jnp.sum(c8, axis=0)
    pad8 = (tot8 + blk8 - 1) // blk8 * blk8
    gend8 = jnp.cumsum(pad8)
    gstart8 = gend8 - pad8
    gbase8 = gstart8[None, :] + jnp.cumsum(c8, axis=0) - c8
    nvb = (gend8[-1:] // blk8).astype(jnp.int32)
    blk_start8 = jnp.arange(MOE_BLOCKS, dtype=jnp.int32) * blk8
    blk_e = jnp.minimum(jnp.sum((gend8[None, :] <= blk_start8[:, None]).astype(jnp.int32), axis=1),
                        N_EXPERTS - 1)
    def flat_list(count, slots, first_local, first_global, step):
        off = jnp.cumsum(count, axis=1) - count
        k = jnp.arange(slots, dtype=jnp.int32)
        in_run = (off[:, None, :] <= k[None, :, None]) & (k[None, :, None] < (off + count)[:, None, :])

        def pick(first):
            return (jnp.sum(jnp.where(in_run, (first - step * off)[:, None, :], 0), axis=-1)
                    + step * k[None, :]).reshape(-1).astype(jnp.int32)

        return jnp.sum(count, axis=1).astype(jnp.int32), pick(first_local), pick(first_global)

    npair, lpair, gpair = flat_list(c8 // 2, PAIR_SLOTS, loff8, gbase8, 2)
    nsingle, lsingle, gsingle = flat_list(c8 % 2, N_EXPERTS, loff8 + c8 - 1, gbase8 + c8 - 1, 0)
    run_tables = (npair, nsingle, lpair, gpair, lsingle, gsingle)

    xs = _dispatch(run_tables + ((gstart8 + tot8).astype(jnp.int32), (pad8 - tot8).astype(jnp.int32), nvb),
                   slot_rows, h2)
    ys = _experts(blk_e, nvb, xs, w_exp_gate[0], w_exp_up[0], w_exp_down[0])
    out = _combine(run_tables, slot_cols, x1, mf, g_final.reshape(1, D_MODEL), ys)
    return out.reshape(BATCH, SEQ, D_MODEL)
```

```python
import jax
import jax.numpy as jnp
from jax import lax
from jax.experimental import pallas as pl
from jax.experimental.pallas import tpu as pltpu

D_MODEL = 1024
BATCH = 2
SEQ = 8192
N_TOK = BATCH * SEQ
CHUNK = 64
EPS = 1e-6
MLA_HEADS = 8
Q_RANK = 256
KV_RANK = 128
QK_NOPE = 64
QK_ROPE = 32
V_HEAD = 64
MLA_WIDTH = MLA_HEADS * V_HEAD
ROPE_THETA = 10000.0
SGU_GROUPS = 8
SGU_GROUP_DIM = 64
SGU_WIDTH = SGU_GROUPS * SGU_GROUP_DIM
SGU_LEN = 128
N_GROUPS = 4
EXPERTS_PER_GROUP = 8
N_EXPERTS = N_GROUPS * EXPERTS_PER_GROUP
TOP_K = 2
D_EXPERT = 256

LANES = 128
SUBLANES = 8
QK_PAD = 2 * LANES
V_PAD = 2 * LANES
SCALE = (QK_NOPE + QK_ROPE) ** -0.5
LOG2E = 1.4426950408889634
Q_SCALE = SCALE * LOG2E
NEG = -1e30

TM = 256
TMI = 512
TMX = 1024
TQ = 128
TK = 512
ATTN_UNROLL_SHIFT = 1
ATTN_UNROLL = 1 << ATTN_UNROLL_SHIFT
TK_SHIFT = TK.bit_length() - 1
CHUNK_SHIFT = CHUNK.bit_length() - 1
assert 1 << TK_SHIFT == TK and 1 << CHUNK_SHIFT == CHUNK
MOE_T = 512
CHUNK_ROWS = 16
TILE_ROWS = -(-(TOP_K * TM + N_EXPERTS * (CHUNK_ROWS - 1) + CHUNK_ROWS) // 256) * 256
TILE_CHUNKS = TILE_ROWS // CHUNK_ROWS
PAIR_SLOTS = TILE_CHUNKS // 2
CHUNK_UNROLL_SHIFT = 2
CHUNK_UNROLL = 1 << CHUNK_UNROLL_SHIFT
MOE_ROWS_MAX = (N_TOK * TOP_K + (N_TOK // TM) * N_EXPERTS * (CHUNK_ROWS - 1)
                + N_EXPERTS * (MOE_T - CHUNK_ROWS))
MOE_BLOCKS = -(-MOE_ROWS_MAX // MOE_T)
MOE_ROWS = MOE_BLOCKS * MOE_T

C_Q = 0
C_KV = C_Q + Q_RANK
C_KR = C_KV + KV_RANK
C_KRS = C_KR + LANES
C_U = C_KRS + LANES
C_V = C_U + SGU_WIDTH
C_GA = C_V + SGU_WIDTH
C_GB = C_GA + D_MODEL
C_END = C_GB + D_MODEL

F32 = jnp.float32
BF16 = jnp.bfloat16


def _dot(a, b):
    return jnp.dot(a, b, preferred_element_type=F32)


def _rms(x, g):
    return x * lax.rsqrt(jnp.mean(x * x, axis=-1, keepdims=True) + EPS) * g


def _fold_kernel(a_ref, b_ref, o_ref):
    o_ref[...] = Q_SCALE * jnp.dot(a_ref[...], b_ref[...], preferred_element_type=F32,
                                 precision=lax.Precision.HIGHEST)


def _fold(w_uq_nope, w_ukt):
    return pl.pallas_call(
        _fold_kernel,
        grid=(MLA_HEADS,),
        in_specs=[pl.BlockSpec((None, Q_RANK, QK_NOPE), lambda h: (h, 0, 0)),
                  pl.BlockSpec((None, QK_NOPE, KV_RANK), lambda h: (h, 0, 0))],
        out_specs=pl.BlockSpec((None, Q_RANK, KV_RANK), lambda h: (h, 0, 0)),
        out_shape=jax.ShapeDtypeStruct((MLA_HEADS, Q_RANK, KV_RANK), F32),
        name="fold",
    )(w_uq_nope, w_ukt)


def _rope_kernel(pos_ref, freq_ref, cos_ref, sin_ref):
    ang = pos_ref[...] * freq_ref[...]
    cos_ref[...] = jnp.cos(ang)
    sin_ref[...] = jnp.sin(ang)


def _rope_tables(pos8, freq8):
    rows = pos8.shape[0]
    blk = pl.BlockSpec((TM, LANES), lambda i: (i, 0))
    return pl.pallas_call(
        _rope_kernel,
        grid=(rows // TM,),
        in_specs=[blk, pl.BlockSpec((1, LANES), lambda i: (0, 0))],
        out_specs=[blk, blk],
        out_shape=[jax.ShapeDtypeStruct((rows, LANES), F32)] * 2,
        name="rope_tables",
    )(pos8, freq8)


def _inproj_kernel(x_ref, cos_ref, sin_ref, gattn_ref, win_ref, gcq_ref, gckv_ref, wqlat_ref, wqr_ref,
                   wqrs_ref, sgain_ref, sbias_ref, wsp_ref, bsp_ref, wosgu_ref,
                   q_ref, kt_ref, v_ref, p_ref, ga_ref):
    hb = _rms(x_ref[...], gattn_ref[...]).astype(BF16)

    def proj(a, b):
        return _dot(hb, win_ref[:, a:b])

    cos_t = cos_ref[...]
    sin_t = sin_ref[...]

    cqn = _rms(proj(C_Q, C_KV), gcq_ref[...]).astype(BF16)
    qlat = _dot(cqn, wqlat_ref[...])
    groups = MLA_HEADS * QK_ROPE // LANES
    rot = (_dot(cqn, wqr_ref[...]) * jnp.concatenate([cos_t * Q_SCALE] * groups, axis=1)
           + _dot(cqn, wqrs_ref[...]) * jnp.concatenate([sin_t * Q_SCALE] * groups, axis=1))
    lane = lax.broadcasted_iota(jnp.int32, (TQ, LANES), 1)
    for r in range(TMI // TQ):
        t0, t1 = r * TQ, (r + 1) * TQ
        for h in range(MLA_HEADS):
            a, b = h * LANES, (h + 1) * LANES
            q_ref[r, h * TQ:(h + 1) * TQ, :LANES] = qlat[t0:t1, a:b].astype(BF16)
            g0 = (h * QK_ROPE) // LANES * LANES
            off = (h * QK_ROPE) % LANES
            window = (lane >= off) & (lane < off + QK_ROPE)
            q_ref[r, h * TQ:(h + 1) * TQ, LANES:] = jnp.where(
                window, rot[t0:t1, g0:g0 + LANES], 0.0).astype(BF16)

    zk = proj(C_KV, C_U)
    ckvn = _rms(zk[:, :KV_RANK], gckv_ref[...])
    krope = zk[:, KV_RANK:KV_RANK + LANES] * cos_t + zk[:, KV_RANK + LANES:] * sin_t
    kt_ref[...] = jnp.concatenate([ckvn, krope], axis=1).T.astype(BF16)
    v_ref[...] = jnp.concatenate([ckvn, jnp.ones_like(ckvn)], axis=1).astype(BF16)

    u = jax.nn.gelu(proj(C_U, C_V))
    v = jax.nn.gelu(proj(C_V, C_GA))
    mu = jnp.mean(v, axis=-1, keepdims=True)
    vc = v - mu
    var = jnp.mean(vc * vc, axis=-1, keepdims=True)
    vb = (vc * lax.rsqrt(var + EPS) * sgain_ref[...] + sbias_ref[...]).astype(BF16)
    row = lax.broadcasted_iota(jnp.int32, (SGU_LEN, SGU_LEN), 0)
    col = lax.broadcasted_iota(jnp.int32, (SGU_LEN, SGU_LEN), 1)
    causal = (row >> CHUNK_SHIFT) >= (col >> CHUNK_SHIFT)
    low_half = col < SGU_GROUP_DIM
    zero_w = jnp.zeros((SGU_LEN, SGU_LEN), BF16)
    w_pairs = [jnp.concatenate([jnp.where(causal, wsp_ref[2 * p], zero_w),
                                jnp.where(causal, wsp_ref[2 * p + 1], zero_w)], axis=1)
               for p in range(SGU_WIDTH // LANES)]
    row_blocks = []
    for r in range(TMI // SGU_LEN):
        pieces = []
        for p in range(SGU_WIDTH // LANES):
            blk = vb[r * SGU_LEN:(r + 1) * SGU_LEN, p * LANES:(p + 1) * LANES]
            stacked = jnp.concatenate([jnp.where(low_half, blk, zero_w), jnp.where(low_half, zero_w, blk)], axis=0)
            sv = _dot(w_pairs[p], stacked) + bsp_ref[p]
            pieces.append((u[r * SGU_LEN:(r + 1) * SGU_LEN, p * LANES:(p + 1) * LANES] * sv).astype(BF16))
        row_blocks.append(jnp.concatenate(pieces, axis=1))
    sgu = _dot(jnp.concatenate(row_blocks, axis=0), wosgu_ref[...])

    ga_ref[...] = jax.nn.sigmoid(proj(C_GA, C_GB)).astype(BF16)
    p_ref[...] = (jax.nn.sigmoid(proj(C_GB, C_END)) * sgu).astype(BF16)


def _inproj(x, cos_t, sin_t, gattn, win, gcq, gckv, wqlat, wqr, wqrs, sgain, sbias, wsp, bsp, wosgu):
    nt = N_TOK // TMI
    per_b = SEQ // TMI
    per_k = TK // TMI

    def const(shape):
        return pl.BlockSpec(shape, lambda i: (0,) * len(shape))

    return pl.pallas_call(
        _inproj_kernel,
        grid=(nt,),
        in_specs=[pl.BlockSpec((TMI, D_MODEL), lambda i: (i, 0)),
                  pl.BlockSpec((TMI, LANES), lambda i: (i, 0)), pl.BlockSpec((TMI, LANES), lambda i: (i, 0)),
                  const((1, D_MODEL)), const((D_MODEL, C_END)), const((1, Q_RANK)), const((1, KV_RANK)),
                  const((Q_RANK, MLA_HEADS * LANES)), const((Q_RANK, MLA_HEADS * QK_ROPE)),
                  const((Q_RANK, MLA_HEADS * QK_ROPE)),
                  const((1, SGU_WIDTH)), const((1, SGU_WIDTH)),
                  const((SGU_GROUPS, SGU_LEN, SGU_LEN)), const((SGU_GROUPS // 2, SGU_LEN, LANES)),
                  const((SGU_WIDTH, D_MODEL))],
        out_specs=[pl.BlockSpec((TMI // TQ, MLA_HEADS * TQ, QK_PAD), lambda i: (i, 0, 0)),
                   pl.BlockSpec((None, None, QK_PAD, TMI),
                                lambda i: (i // per_b, (i % per_b) // per_k, 0, i % per_k)),
                   pl.BlockSpec((None, None, TMI, V_PAD),
                                lambda i: (i // per_b, (i % per_b) // per_k, i % per_k, 0)),
                   pl.BlockSpec((TMI, D_MODEL), lambda i: (i, 0)),
                   pl.BlockSpec((TMI, D_MODEL), lambda i: (i, 0))],
        out_shape=[jax.ShapeDtypeStruct((N_TOK // TQ, MLA_HEADS * TQ, QK_PAD), BF16),
                   jax.ShapeDtypeStruct((BATCH, SEQ // TK, QK_PAD, TK), BF16),
                   jax.ShapeDtypeStruct((BATCH, SEQ // TK, TK, V_PAD), BF16),
                   jax.ShapeDtypeStruct((N_TOK, D_MODEL), BF16),
                   jax.ShapeDtypeStruct((N_TOK, D_MODEL), BF16)],
        compiler_params=pltpu.CompilerParams(dimension_semantics=("arbitrary",),
                                             vmem_limit_bytes=56 * 1024 * 1024),
        name="inproj",
    )(x, cos_t, sin_t, gattn, win, gcq, gckv, wqlat, wqr, wqrs, sgain, sbias, wsp, bsp, wosgu)


def _attn_kernel(q_ref, kt_ref, v_ref, wuv_ref, o_ref, m_ref, acc_ref, s_ref, p_ref, a_ref):
    qi = pl.program_id(1)
    m_ref[...] = jnp.full(m_ref.shape, NEG, F32)
    acc_ref[...] = jnp.zeros(acc_ref.shape, F32)

    def scores(j, slot):
        s_ref[slot] = _dot(q_ref[...], kt_ref[j])

    def update(j, slot, width=None):
        masked = width is not None
        w = width if masked else TK
        if masked:
            q_chunk = ((qi * TQ - j * TK) >> CHUNK_SHIFT) + (
                lax.broadcasted_iota(jnp.int32, (TQ, w), 0) >> CHUNK_SHIFT)
            k_chunk = lax.broadcasted_iota(jnp.int32, (TQ, w), 1) >> CHUNK_SHIFT
            visible = k_chunk <= q_chunk
        for h in range(MLA_HEADS):
            rows = slice(h * TQ, (h + 1) * TQ)
            s = s_ref[slot, rows, :w]
            if masked:
                s = jnp.where(visible, s, NEG)
            m_prev = m_ref[rows, :]
            m_new = jnp.maximum(m_prev, jnp.max(s, axis=-1, keepdims=True))
            m_ref[rows, :] = m_new
            a_ref[rows, :] = jnp.exp2(m_prev - m_new)
            s = s_ref[slot, rows, :w]
            if masked:
                s = jnp.where(visible, s, NEG)
            p_ref[rows, :w] = jnp.exp2(s - jnp.concatenate([m_new] * (w // LANES), axis=1)).astype(BF16)
        alpha = a_ref[...]
        acc_ref[...] = jnp.concatenate([alpha, alpha], axis=1) * acc_ref[...] + _dot(
            p_ref[:, :w], v_ref[j, :w, :])

    diag = (qi * TQ) >> TK_SHIFT
    scores(0, 0)

    def run(first, count, last_masked):
        for u in range(count):
            if not (last_masked and u == count - 1):
                scores(first + u + 1, (u + 1) % 2)
                update(first + u, u % 2)
            else:
                for sub in range(TK // TQ):
                    @pl.when((qi & (TK // TQ - 1)) == sub)
                    def _(u=u, sub=sub):
                        update(first + u, u % 2, (sub + 1) * TQ)

    def body(t, carry):
        run(ATTN_UNROLL * t, ATTN_UNROLL, False)
        return carry

    trips = diag >> ATTN_UNROLL_SHIFT
    lax.fori_loop(0, trips, body, 0)
    done = trips << ATTN_UNROLL_SHIFT
    for r in range(ATTN_UNROLL):
        @pl.when(diag - done == r)
        def _(r=r):
            run(done, r + 1, True)

    o_lat = (acc_ref[:, :KV_RANK] / acc_ref[:, KV_RANK:]).astype(BF16)
    o_cat = jnp.concatenate([o_lat[h * TQ:(h + 1) * TQ] for h in range(MLA_HEADS)], axis=1)
    o_ref[...] = _dot(o_cat, wuv_ref[...]).astype(BF16)


def _attention(q, kt, v, wuv):
    nk = SEQ // TK
    return pl.pallas_call(
        _attn_kernel,
        grid=(BATCH, SEQ // TQ),
        in_specs=[pl.BlockSpec((None, MLA_HEADS * TQ, QK_PAD), lambda b, i: (b * (SEQ // TQ) + i, 0, 0)),
                  pl.BlockSpec((None, nk, QK_PAD, TK), lambda b, i: (b, 0, 0, 0)),
                  pl.BlockSpec((None, nk, TK, V_PAD), lambda b, i: (b, 0, 0, 0)),
                  pl.BlockSpec((MLA_HEADS * KV_RANK, MLA_WIDTH), lambda b, i: (0, 0))],
        out_specs=pl.BlockSpec((None, TQ, MLA_WIDTH), lambda b, i: (b, i, 0)),
        out_shape=jax.ShapeDtypeStruct((BATCH, SEQ, MLA_WIDTH), BF16),
        scratch_shapes=[pltpu.VMEM((MLA_HEADS * TQ, LANES), F32),
                        pltpu.VMEM((MLA_HEADS * TQ, V_PAD), F32),
                        pltpu.VMEM((2, MLA_HEADS * TQ, TK), F32),
                        pltpu.VMEM((MLA_HEADS * TQ, TK), BF16),
                        pltpu.VMEM((MLA_HEADS * TQ, LANES), F32)],
        compiler_params=pltpu.CompilerParams(dimension_semantics=("arbitrary", "arbitrary"),
                                             vmem_limit_bytes=40 * 1024 * 1024),
        name="attention",
    )(q, kt, v, wuv)


def _mix_kernel(attn_ref, ga_ref, p_ref, x_ref, woa_ref, wout_ref, gffn_ref, wr_ref, br_ref,
                x1_ref, h2_ref, mf_ref, slot_col_ref, slot_row_ref, run_chunks_ref, run_start_ref):
    a = _dot(attn_ref[...], woa_ref[...])
    mix = (ga_ref[...].astype(F32) * a + p_ref[...].astype(F32)).astype(BF16)
    x1 = x_ref[...] + _dot(mix, wout_ref[...])
    x1_ref[...] = x1
    h2 = _rms(x1, gffn_ref[...])
    h2_ref[...] = h2.astype(BF16)

    hi = h2.astype(BF16)
    lo = (h2 - hi.astype(F32)).astype(BF16)
    r1 = _dot(hi, wr_ref[...])
    r2 = _dot(lo, wr_ref[:, :LANES])
    logits_all = r1[:, :LANES] + r1[:, LANES:] + r2 + br_ref[...]

    for r in range(TMX // TM):
        mf, slots, run_chunks_rows, run_start_rows = _route_tile(logits_all[r * TM:(r + 1) * TM])
        mf_ref[r * TM:(r + 1) * TM, :] = mf
        slot_col_ref[r * TM:(r + 1) * TM, :] = slots.astype(jnp.int32)
        tile_rows = slice(SUBLANES * r, SUBLANES * (r + 1))
        slot_row_ref[tile_rows, :] = slots.T[:SUBLANES].astype(jnp.int32)
        run_chunks_ref[tile_rows, :] = run_chunks_rows.astype(jnp.int32)
        run_start_ref[tile_rows, :] = run_start_rows.astype(jnp.int32)


def _route_tile(logits):
    lane_i = lax.broadcasted_iota(jnp.int32, (TM, LANES), 1)
    lane = lane_i.astype(F32)
    lane_group = (lane_i >> 3).astype(F32)
    ninf = -jnp.inf
    is_group = (lane_i >= N_EXPERTS) & (lane_i < N_EXPERTS + N_GROUPS)
    lg = jnp.where(is_group, logits, ninf)
    gmax = jnp.max(lg, axis=-1, keepdims=True)
    gsum = jnp.sum(jnp.exp(lg - gmax), axis=-1, keepdims=True)
    p_top = 1.0 / gsum
    g_idx = jnp.min(jnp.where(lg == gmax, lane - N_EXPERTS, float(N_GROUPS)), axis=-1, keepdims=True)
    le = jnp.where((lane_i < N_EXPERTS) & (lane_group == g_idx), logits, ninf)
    t1 = jnp.max(le, axis=-1, keepdims=True)
    e1 = jnp.min(jnp.where(le == t1, lane, float(LANES)), axis=-1, keepdims=True)
    le2 = jnp.where(lane == e1, ninf, le)
    t2 = jnp.max(le2, axis=-1, keepdims=True)
    e2 = jnp.min(jnp.where(le2 == t2, lane, float(LANES)), axis=-1, keepdims=True)
    ex = jnp.exp(t2 - t1)
    w1 = p_top / (1.0 + ex)
    w2 = p_top * ex / (1.0 + ex)

    sel1 = lane == e1
    sel2 = lane == e2
    onehot = jnp.where(sel1 | sel2, 1.0, 0.0)
    rr = lax.broadcasted_iota(jnp.int32, (TM, TM), 0)
    cc = lax.broadcasted_iota(jnp.int32, (TM, TM), 1)
    ltri = jnp.where(cc < rr, 1.0, 0.0).astype(BF16)
    rank = _dot(ltri, onehot.astype(BF16))
    cnt = jnp.sum(onehot, axis=0, keepdims=True)
    run_chunks = jnp.floor((cnt + (CHUNK_ROWS - 1)) * (1.0 / CHUNK_ROWS))
    ur = lax.broadcasted_iota(jnp.int32, (LANES, LANES), 0)
    uc = lax.broadcasted_iota(jnp.int32, (LANES, LANES), 1)
    upper = jnp.where(ur < uc, 1.0, 0.0).astype(BF16)
    run_chunks_rows = jnp.broadcast_to(run_chunks, (SUBLANES, LANES))
    run_start_rows = _dot(run_chunks_rows.astype(BF16), upper)
    slot_all = CHUNK_ROWS * run_start_rows[0:1] + rank
    slot1 = jnp.sum(jnp.where(sel1, slot_all, 0.0), axis=-1, keepdims=True)
    slot2 = jnp.sum(jnp.where(sel2, slot_all, 0.0), axis=-1, keepdims=True)

    slots = jnp.where(lane_i == 0, slot1, jnp.where(lane_i == 1, slot2, 0.0))
    return jnp.where(lane_i == 0, w1, w2), slots, run_chunks_rows, run_start_rows


def _mix(attn, ga, p, x, woa, wout, gffn, wr, br):
    nt = N_TOK // TM
    sub = TMX // TM

    def const(shape):
        return pl.BlockSpec(shape, lambda i: (0,) * len(shape))

    def rows(width):
        return pl.BlockSpec((TMX, width), lambda i: (i, 0))

    return pl.pallas_call(
        _mix_kernel,
        grid=(N_TOK // TMX,),
        in_specs=[rows(MLA_WIDTH), rows(D_MODEL), rows(D_MODEL), rows(D_MODEL),
                  const((MLA_WIDTH, D_MODEL)), const((D_MODEL, D_MODEL)), const((1, D_MODEL)),
                  const((D_MODEL, 2 * LANES)), const((1, LANES))],
        out_specs=[rows(D_MODEL), rows(D_MODEL), rows(LANES), rows(LANES),
                   pl.BlockSpec((SUBLANES * sub, TM), lambda i: (i, 0)),
                   pl.BlockSpec((SUBLANES * sub, LANES), lambda i: (i, 0)),
                   pl.BlockSpec((SUBLANES * sub, LANES), lambda i: (i, 0))],
        out_shape=[jax.ShapeDtypeStruct((N_TOK, D_MODEL), F32),
                   jax.ShapeDtypeStruct((N_TOK, D_MODEL), BF16),
                   jax.ShapeDtypeStruct((N_TOK, LANES), F32),
                   jax.ShapeDtypeStruct((N_TOK, LANES), jnp.int32),
                   jax.ShapeDtypeStruct((nt * SUBLANES, TM), jnp.int32),
                   jax.ShapeDtypeStruct((nt * SUBLANES, LANES), jnp.int32),
                   jax.ShapeDtypeStruct((nt * SUBLANES, LANES), jnp.int32)],
        compiler_params=pltpu.CompilerParams(dimension_semantics=("arbitrary",),
                                             vmem_limit_bytes=40 * 1024 * 1024),
        name="mix",
    )(attn, ga, p, x, woa, wout, gffn, wr, br)


def _chunk_copy(src, dst, src_chunk, dst_chunk, sem, nchunks=1):
    rows = nchunks * CHUNK_ROWS
    return pltpu.make_async_copy(
        src.at[pl.ds(pl.multiple_of(src_chunk * CHUNK_ROWS, CHUNK_ROWS), rows)],
        dst.at[pl.ds(pl.multiple_of(dst_chunk * CHUNK_ROWS, CHUNK_ROWS), rows)], sem)


def _unrolled_loop(n, fn):
    groups = n >> CHUNK_UNROLL_SHIFT

    def group(g, carry):
        for u in range(CHUNK_UNROLL):
            fn(g * CHUNK_UNROLL + u)
        return carry

    lax.fori_loop(0, groups, group, 0)

    def single(k, carry):
        fn(k)
        return carry

    lax.fori_loop(groups << CHUNK_UNROLL_SHIFT, n, single, 0)


def _for_each_copy(t, copy_tables, fn):
    npair_ref, nsingle_ref, lpair_ref, gpair_ref, lsingle_ref, gsingle_ref = copy_tables
    pb = t * PAIR_SLOTS
    sb = t * N_EXPERTS
    _unrolled_loop(npair_ref[t], lambda k: fn(lpair_ref[pb + k], gpair_ref[pb + k], 2))
    _unrolled_loop(nsingle_ref[t], lambda k: fn(lsingle_ref[sb + k], gsingle_ref[sb + k], 1))


def _tile_chunks(t, copy_tables):
    return 2 * copy_tables[0][t] + copy_tables[1][t]


def _wait_chunks(n, src, dst, sem):
    groups = n >> CHUNK_UNROLL_SHIFT
    rows = CHUNK_UNROLL * CHUNK_ROWS

    def group(g, carry):
        pltpu.make_async_copy(src.at[pl.ds(0, rows)], dst.at[pl.ds(0, rows)], sem).wait()
        return carry

    lax.fori_loop(0, groups, group, 0)

    def single(k, carry):
        _chunk_copy(src, dst, 0, 0, sem).wait()
        return carry

    lax.fori_loop(groups << CHUNK_UNROLL_SHIFT, n, single, 0)


def _dispatch_kernel(npair_ref, nsingle_ref, lpair_ref, gpair_ref, lsingle_ref, gsingle_ref,
                     zero_start_ref, zero_count_ref, nvb_ref,
                     slot_ref, h2_ref, xs_hbm, sbuf, zbuf, sem, zsem):
    copy_tables = (npair_ref, nsingle_ref, lpair_ref, gpair_ref, lsingle_ref, gsingle_ref)
    t = pl.program_id(0)
    last = pl.num_programs(0) - 1
    cur = t % 2

    def zero_chunk_copy(dst_chunk):
        return _chunk_copy(zbuf, xs_hbm, 0, dst_chunk, zsem)

    def zero_block_copy(b):
        return pltpu.make_async_copy(
            zbuf, xs_hbm.at[pl.ds(pl.multiple_of(b * MOE_T, MOE_T), MOE_T)], zsem)

    def for_each_zero(chunk_fn, block_fn):
        def per_expert(e, carry):
            def per_chunk(j, carry2):
                chunk_fn(zero_start_ref[e] + j)
                return carry2

            lax.fori_loop(0, zero_count_ref[e], per_chunk, 0)
            return carry

        lax.fori_loop(0, N_EXPERTS, per_expert, 0)

        def per_block(b, carry):
            block_fn(b)
            return carry

        lax.fori_loop(nvb_ref[0], MOE_BLOCKS, per_block, 0)

    @pl.when(t == 0)
    def _():
        zbuf[...] = jnp.zeros(zbuf.shape, BF16)
        for_each_zero(lambda dst_chunk: zero_chunk_copy(dst_chunk).start(), lambda b: zero_block_copy(b).start())

    row = lax.broadcasted_iota(jnp.int32, (TILE_ROWS, TM), 0)
    slots = slot_ref[...]
    perm = jnp.where((row == slots[0:1, :]) | (row == slots[1:2, :]), 1.0, 0.0).astype(BF16)
    sbuf[cur] = _dot(perm, h2_ref[...]).astype(BF16)
    _for_each_copy(t, copy_tables,
                   lambda lc, gc, n: _chunk_copy(sbuf.at[cur], xs_hbm, lc, gc, sem.at[cur], n).start())

    def wait_tile(tile, slot):
        _wait_chunks(_tile_chunks(tile, copy_tables), sbuf.at[slot], xs_hbm, sem.at[slot])

    @pl.when(t > 0)
    def _():
        wait_tile(t - 1, 1 - cur)

    @pl.when(t == last)
    def _():
        wait_tile(t, cur)
        for_each_zero(lambda dst_chunk: zero_chunk_copy(dst_chunk).wait(), lambda b: zero_block_copy(b).wait())


def _dispatch(tables, slot_rows, h2):
    return pl.pallas_call(
        _dispatch_kernel,
        grid_spec=pltpu.PrefetchScalarGridSpec(
            num_scalar_prefetch=len(tables),
            grid=(N_TOK // TM,),
            in_specs=[pl.BlockSpec((SUBLANES, TM), lambda i, *_: (i, 0)),
                      pl.BlockSpec((TM, D_MODEL), lambda i, *_: (i, 0))],
            out_specs=pl.BlockSpec(memory_space=pl.ANY),
            scratch_shapes=[pltpu.VMEM((2, TILE_ROWS, D_MODEL), BF16), pltpu.VMEM((MOE_T, D_MODEL), BF16),
                            pltpu.SemaphoreType.DMA((2,)), pltpu.SemaphoreType.DMA(())]),
        out_shape=jax.ShapeDtypeStruct((MOE_ROWS, D_MODEL), BF16),
        compiler_params=pltpu.CompilerParams(dimension_semantics=("arbitrary",)),
        name="dispatch",
    )(*tables, slot_rows, h2)


def _expert_kernel(blk_e_ref, nvb_ref, xs_ref, wg_ref, wu_ref, wd_ref, ys_ref, wgb, wub, wdb):
    i = pl.program_id(0)
    live = i < nvb_ref[0]

    @pl.when(live & ((i == 0) | (blk_e_ref[i] != blk_e_ref[jnp.maximum(i - 1, 0)])))
    def _():
        wgb[...] = wg_ref[...].astype(BF16)
        wub[...] = wu_ref[...].astype(BF16)
        wdb[...] = wd_ref[...].astype(BF16)

    @pl.when(live)
    def _():
        xb = xs_ref[...]
        g = _dot(xb, wgb[...])
        u = _dot(xb, wub[...])
        hid = (jax.nn.silu(g) * u).astype(BF16)
        ys_ref[...] = _dot(hid, wdb[...]).astype(BF16)

    @pl.when(jnp.logical_not(live))
    def _():
        ys_ref[...] = jnp.zeros(ys_ref.shape, ys_ref.dtype)


def _experts(blk_e, nvb, xs, wg, wu, wd):
    def row_block(i, be, nv):
        return (jnp.minimum(i, nv[0] - 1), 0)

    def weight(i, be, nv):
        return (be[jnp.minimum(i, nv[0] - 1)], 0, 0)

    return pl.pallas_call(
        _expert_kernel,
        grid_spec=pltpu.PrefetchScalarGridSpec(
            num_scalar_prefetch=2,
            grid=(MOE_BLOCKS,),
            in_specs=[pl.BlockSpec((MOE_T, D_MODEL), row_block),
                      pl.BlockSpec((None, D_MODEL, D_EXPERT), weight),
                      pl.BlockSpec((None, D_MODEL, D_EXPERT), weight),
                      pl.BlockSpec((None, D_EXPERT, D_MODEL), weight)],
            out_specs=pl.BlockSpec((MOE_T, D_MODEL), lambda i, be, nv: (i, 0)),
            scratch_shapes=[pltpu.VMEM((D_MODEL, D_EXPERT), BF16), pltpu.VMEM((D_MODEL, D_EXPERT), BF16),
                            pltpu.VMEM((D_EXPERT, D_MODEL), BF16)]),
        out_shape=jax.ShapeDtypeStruct((MOE_ROWS, D_MODEL), BF16),
        compiler_params=pltpu.CompilerParams(dimension_semantics=("arbitrary",),
                                             vmem_limit_bytes=40 * 1024 * 1024),
        name="experts",
    )(blk_e, nvb, xs, wg, wu, wd)


def _combine_kernel(npair_ref, nsingle_ref, lpair_ref, gpair_ref, lsingle_ref, gsingle_ref,
                    slot_ref, x1_ref, mf_ref, gfin_ref, ys_hbm, o_ref, ybuf, sem):
    copy_tables = (npair_ref, nsingle_ref, lpair_ref, gpair_ref, lsingle_ref, gsingle_ref)
    t = pl.program_id(0)
    cur = t % 2

    def fetch(tile, slot):
        _for_each_copy(tile, copy_tables,
                       lambda lc, gc, n: _chunk_copy(ys_hbm, ybuf.at[slot], gc, lc, sem.at[slot], n).start())

    @pl.when(t == 0)
    def _():
        ybuf[...] = jnp.zeros(ybuf.shape, BF16)
        fetch(0, 0)

    @pl.when(t + 1 < pl.num_programs(0))
    def _():
        fetch(t + 1, 1 - cur)

    _wait_chunks(_tile_chunks(t, copy_tables), ys_hbm, ybuf.at[cur], sem.at[cur])

    yb = ybuf[cur]
    col = lax.broadcasted_iota(jnp.int32, (TM, TILE_ROWS), 1)
    slots = slot_ref[...]
    y1 = _dot(jnp.where(col == slots[:, 0:1], 1.0, 0.0).astype(BF16), yb)
    y2 = _dot(jnp.where(col == slots[:, 1:2], 1.0, 0.0).astype(BF16), yb)
    mf = mf_ref[...]
    x2 = x1_ref[...] + mf[:, 0:1] * y1 + mf[:, 1:2] * y2
    o_ref[...] = _rms(x2, gfin_ref[...])


def _combine(tables, slot_cols, x1, mf, gfin, ys):
    return pl.pallas_call(
        _combine_kernel,
        grid_spec=pltpu.PrefetchScalarGridSpec(
            num_scalar_prefetch=len(tables),
            grid=(N_TOK // TM,),
            in_specs=[pl.BlockSpec((TM, LANES), lambda i, *_: (i, 0)),
                      pl.BlockSpec((TM, D_MODEL), lambda i, *_: (i, 0)),
                      pl.BlockSpec((TM, LANES), lambda i, *_: (i, 0)),
                      pl.BlockSpec((1, D_MODEL), lambda i, *_: (0, 0)),
                      pl.BlockSpec(memory_space=pl.ANY)],
            out_specs=pl.BlockSpec((TM, D_MODEL), lambda i, *_: (i, 0)),
            scratch_shapes=[pltpu.VMEM((2, TILE_ROWS, D_MODEL), BF16), pltpu.SemaphoreType.DMA((2,))]),
        out_shape=jax.ShapeDtypeStruct((N_TOK, D_MODEL), F32),
        compiler_params=pltpu.CompilerParams(dimension_semantics=("arbitrary",)),
        name="combine",
    )(*tables, slot_cols, x1, mf, gfin, ys)


def kernel(x, positions, g_attn_norm, w_in, g_cq, w_uq, g_ckv, w_uk, w_uv, w_o_attn, sgu_gain, sgu_bias, w_spatial, b_spatial, w_o_sgu, w_out, g_ffn_norm, w_router_group, b_router_group, w_router_expert, b_router_expert, w_exp_gate, w_exp_up, w_exp_down, g_final):
    assert x.shape == (BATCH, SEQ, D_MODEL) and w_in.shape[0] == 1
    half = QK_ROPE // 2
    swap = jnp.concatenate([jnp.arange(half, QK_ROPE), jnp.arange(0, half)])

    def pad_cols(w, width):
        return jnp.pad(w, ((0, 0), (0, width - w.shape[1])))

    wi = w_in[0]
    c0 = Q_RANK + KV_RANK
    kr = wi[:, c0:c0 + QK_ROPE]
    c1 = c0 + QK_ROPE
    win = jnp.concatenate([
        wi[:, :c0], jnp.tile(kr, (1, LANES // QK_ROPE)), jnp.tile(kr[:, swap], (1, LANES // QK_ROPE)),
        wi[:, c1:]], axis=1).astype(BF16)

    wq = w_uq[0].reshape(Q_RANK, MLA_HEADS, QK_NOPE + QK_ROPE)
    wq_nope = wq[:, :, :QK_NOPE].transpose(1, 0, 2)
    wq_rope = wq[:, :, QK_NOPE:]
    w_ukt = w_uk[0].reshape(KV_RANK, MLA_HEADS, QK_NOPE).transpose(1, 2, 0)
    wqlat = _fold(wq_nope, w_ukt).transpose(1, 0, 2).reshape(Q_RANK, MLA_HEADS * LANES).astype(BF16)

    wqr = wq_rope.reshape(Q_RANK, MLA_HEADS * QK_ROPE).astype(BF16)
    wqrs = wq_rope[:, :, swap].reshape(Q_RANK, MLA_HEADS * QK_ROPE).astype(BF16)

    per_row = LANES // half
    freqs = ROPE_THETA ** (-jnp.arange(0, QK_ROPE, 2, dtype=F32) / QK_ROPE)
    pos8 = jnp.repeat(positions.astype(F32).reshape(N_TOK // per_row, per_row), half, axis=1)
    cos8, sin8 = _rope_tables(pos8, jnp.tile(freqs, per_row)[None, :])
    cos16 = cos8.reshape(N_TOK, half)
    sin16 = sin8.reshape(N_TOK, half)
    cos_t = jnp.tile(jnp.concatenate([cos16, cos16], axis=1), (1, LANES // QK_ROPE))
    sin_t = jnp.tile(jnp.concatenate([-sin16, sin16], axis=1), (1, LANES // QK_ROPE))

    head_of_col = jnp.arange(MLA_WIDTH) // V_HEAD
    wuv = jnp.where(head_of_col[None, None, :] == jnp.arange(MLA_HEADS)[:, None, None],
                    w_uv[0][None], 0.0).astype(BF16)
    wuv = wuv.reshape(MLA_HEADS * KV_RANK, MLA_WIDTH)

    wsp = w_spatial[0].astype(BF16)
    bs = b_spatial[0]
    bsp = jnp.repeat(bs.reshape(SGU_GROUPS // 2, 2, SGU_LEN).transpose(0, 2, 1), SGU_GROUP_DIM, axis=2)

    wr32 = jnp.concatenate([w_router_expert[0].transpose(1, 0, 2).reshape(D_MODEL, N_EXPERTS),
                            w_router_group[0]], axis=1)
    wr32 = pad_cols(wr32, LANES)
    wr_hi = wr32.astype(BF16)
    wr_lo = (wr32 - wr_hi.astype(F32)).astype(BF16)
    wr = jnp.concatenate([wr_hi, wr_lo], axis=1)
    br = pad_cols(jnp.concatenate([b_router_expert[0].reshape(-1), b_router_group[0]])[None, :], LANES)

    xf = x.reshape(N_TOK, D_MODEL)
    q, kt, v, p, ga = _inproj(
        xf, cos_t, sin_t, g_attn_norm, win, g_cq, g_ckv, wqlat, wqr, wqrs,
        sgu_gain, sgu_bias, wsp, bsp, w_o_sgu[0].astype(BF16))
    attn = _attention(q, kt, v, wuv)
    x1, h2, mf, slot_cols, slot_rows, run_chunks_rows, run_start_rows = _mix(
        attn.reshape(N_TOK, MLA_WIDTH), ga, p, xf, w_o_attn[0].astype(BF16), w_out[0].astype(BF16),
        g_ffn_norm, wr, br)

    blk_chunks = MOE_T // CHUNK_ROWS
    run_chunks = run_chunks_rows[::SUBLANES, :N_EXPERTS]
    run_start = run_start_rows[::SUBLANES, :N_EXPERTS]
    seg_chunks = jnp.sum(run_chunks, axis=0)
    seg_padded = (seg_chunks + blk_chunks - 1) // blk_chunks * blk_chunks
    seg_end = jnp.cumsum(seg_padded)
    seg_start = seg_end - seg_padded
    run_dest = seg_start[None, :] + jnp.cumsum(run_chunks, axis=0) - run_chunks
    nvb = (seg_end[-1:] // blk_chunks).astype(jnp.int32)
    blk_first_chunk = jnp.arange(MOE_BLOCKS, dtype=jnp.int32) * blk_chunks
    blk_e = jnp.minimum(jnp.sum((seg_end[None, :] <= blk_first_chunk[:, None]).astype(jnp.int32), axis=1),
                        N_EXPERTS - 1)
    def flat_list(count, slots, first_local, first_global, step):
        off = jnp.cumsum(count, axis=1) - count
        k = jnp.arange(slots, dtype=jnp.int32)
        in_run = (off[:, None, :] <= k[None, :, None]) & (k[None, :, None] < (off + count)[:, None, :])

        def pick(first):
            return (jnp.sum(jnp.where(in_run, (first - step * off)[:, None, :], 0), axis=-1)
                    + step * k[None, :]).reshape(-1).astype(jnp.int32)

        return jnp.sum(count, axis=1).astype(jnp.int32), pick(first_local), pick(first_global)

    npair, lpair, gpair = flat_list(run_chunks // 2, PAIR_SLOTS, run_start, run_dest, 2)
    nsingle, lsingle, gsingle = flat_list(run_chunks % 2, N_EXPERTS, run_start + run_chunks - 1, run_dest + run_chunks - 1, 0)
    run_tables = (npair, nsingle, lpair, gpair, lsingle, gsingle)

    xs = _dispatch(run_tables + ((seg_start + seg_chunks).astype(jnp.int32), (seg_padded - seg_chunks).astype(jnp.int32), nvb),
                   slot_rows, h2)
    ys = _experts(blk_e, nvb, xs, w_exp_gate[0], w_exp_up[0], w_exp_down[0])
    out = _combine(run_tables, slot_cols, x1, mf, g_final.reshape(1, D_MODEL), ys)
    return out.reshape(BATCH, SEQ, D_MODEL)
```

```python
import jax
import jax.numpy as jnp
from jax import lax
from jax.experimental import pallas as pl
from jax.experimental.pallas import tpu as pltpu

D_MODEL = 1024
BATCH = 2
SEQ = 8192
N_TOK = BATCH * SEQ
CHUNK = 64
EPS = 1e-6
MLA_HEADS = 8
Q_RANK = 256
KV_RANK = 128
QK_NOPE = 64
QK_ROPE = 32
V_HEAD = 64
MLA_WIDTH = MLA_HEADS * V_HEAD
ROPE_THETA = 10000.0
SGU_GROUPS = 8
SGU_GROUP_DIM = 64
SGU_WIDTH = SGU_GROUPS * SGU_GROUP_DIM
SGU_LEN = 128
N_GROUPS = 4
EXPERTS_PER_GROUP = 8
N_EXPERTS = N_GROUPS * EXPERTS_PER_GROUP
TOP_K = 2
D_EXPERT = 256

LANES = 128
SUBLANES = 8
QK_PAD = 2 * LANES
V_PAD = 2 * LANES
SCALE = (QK_NOPE + QK_ROPE) ** -0.5
LOG2E = 1.4426950408889634
Q_SCALE = SCALE * LOG2E
NEG = -1e30

TM = 256
TMI = 512
TMX = 1024
TQ = 128
TK = 512
ATTN_UNROLL_SHIFT = 1
ATTN_UNROLL = 1 << ATTN_UNROLL_SHIFT
TK_SHIFT = TK.bit_length() - 1
CHUNK_SHIFT = CHUNK.bit_length() - 1
assert 1 << TK_SHIFT == TK and 1 << CHUNK_SHIFT == CHUNK
MOE_T = 512
CHUNK_ROWS = 16
TILE_ROWS = -(-(TOP_K * TM + N_EXPERTS * (CHUNK_ROWS - 1) + CHUNK_ROWS) // 256) * 256
SHORT_TILE_ROWS = TILE_ROWS - 256
TILE_CHUNKS = TILE_ROWS // CHUNK_ROWS
PAIR_SLOTS = TILE_CHUNKS // 2
CHUNK_UNROLL_SHIFT = 2
CHUNK_UNROLL = 1 << CHUNK_UNROLL_SHIFT
MOE_ROWS_MAX = (N_TOK * TOP_K + (N_TOK // TM) * N_EXPERTS * (CHUNK_ROWS - 1)
                + N_EXPERTS * (MOE_T - CHUNK_ROWS))
MOE_BLOCKS = -(-MOE_ROWS_MAX // MOE_T)
MOE_ROWS = MOE_BLOCKS * MOE_T

C_Q = 0
C_KV = C_Q + Q_RANK
C_KR = C_KV + KV_RANK
C_KRS = C_KR + LANES
C_U = C_KRS + LANES
C_V = C_U + SGU_WIDTH
C_GA = C_V + SGU_WIDTH
C_GB = C_GA + D_MODEL
C_END = C_GB + D_MODEL

F32 = jnp.float32
BF16 = jnp.bfloat16


def _dot(a, b):
    return jnp.dot(a, b, preferred_element_type=F32)


def _rms(x, g):
    return x * lax.rsqrt(jnp.mean(x * x, axis=-1, keepdims=True) + EPS) * g


def _fold_kernel(a_ref, b_ref, o_ref):
    o_ref[...] = Q_SCALE * jnp.dot(a_ref[...], b_ref[...], preferred_element_type=F32,
                                 precision=lax.Precision.HIGHEST)


def _fold(w_uq_nope, w_ukt):
    return pl.pallas_call(
        _fold_kernel,
        grid=(MLA_HEADS,),
        in_specs=[pl.BlockSpec((None, Q_RANK, QK_NOPE), lambda h: (h, 0, 0)),
                  pl.BlockSpec((None, QK_NOPE, KV_RANK), lambda h: (h, 0, 0))],
        out_specs=pl.BlockSpec((None, Q_RANK, KV_RANK), lambda h: (h, 0, 0)),
        out_shape=jax.ShapeDtypeStruct((MLA_HEADS, Q_RANK, KV_RANK), F32),
        name="fold",
    )(w_uq_nope, w_ukt)


def _rope_kernel(pos_ref, freq_ref, cos_ref, sin_ref):
    ang = pos_ref[...] * freq_ref[...]
    cos_ref[...] = jnp.cos(ang)
    sin_ref[...] = jnp.sin(ang)


def _rope_tables(pos8, freq8):
    rows = pos8.shape[0]
    blk = pl.BlockSpec((TM, LANES), lambda i: (i, 0))
    return pl.pallas_call(
        _rope_kernel,
        grid=(rows // TM,),
        in_specs=[blk, pl.BlockSpec((1, LANES), lambda i: (0, 0))],
        out_specs=[blk, blk],
        out_shape=[jax.ShapeDtypeStruct((rows, LANES), F32)] * 2,
        name="rope_tables",
    )(pos8, freq8)


def _inproj_kernel(x_ref, cos_ref, sin_ref, gattn_ref, win_ref, gcq_ref, gckv_ref, wqlat_ref, wqr_ref,
                   wqrs_ref, sgain_ref, sbias_ref, wsp_ref, bsp_ref, wosgu_ref,
                   q_ref, kt_ref, v_ref, p_ref, ga_ref):
    hb = _rms(x_ref[...], gattn_ref[...]).astype(BF16)

    def proj(a, b):
        return _dot(hb, win_ref[:, a:b])

    cos_t = cos_ref[...]
    sin_t = sin_ref[...]

    cqn = _rms(proj(C_Q, C_KV), gcq_ref[...]).astype(BF16)
    qlat = _dot(cqn, wqlat_ref[...])
    groups = MLA_HEADS * QK_ROPE // LANES
    rot = (_dot(cqn, wqr_ref[...]) * jnp.concatenate([cos_t * Q_SCALE] * groups, axis=1)
           + _dot(cqn, wqrs_ref[...]) * jnp.concatenate([sin_t * Q_SCALE] * groups, axis=1))
    lane = lax.broadcasted_iota(jnp.int32, (TQ, LANES), 1)
    for r in range(TMI // TQ):
        t0, t1 = r * TQ, (r + 1) * TQ
        for h in range(MLA_HEADS):
            a, b = h * LANES, (h + 1) * LANES
            q_ref[r, h * TQ:(h + 1) * TQ, :LANES] = qlat[t0:t1, a:b].astype(BF16)
            g0 = (h * QK_ROPE) // LANES * LANES
            off = (h * QK_ROPE) % LANES
            window = (lane >= off) & (lane < off + QK_ROPE)
            q_ref[r, h * TQ:(h + 1) * TQ, LANES:] = jnp.where(
                window, rot[t0:t1, g0:g0 + LANES], 0.0).astype(BF16)

    zk = proj(C_KV, C_U)
    ckvn = _rms(zk[:, :KV_RANK], gckv_ref[...])
    krope = zk[:, KV_RANK:KV_RANK + LANES] * cos_t + zk[:, KV_RANK + LANES:] * sin_t
    kt_ref[...] = jnp.concatenate([ckvn, krope], axis=1).T.astype(BF16)
    v_ref[...] = jnp.concatenate([ckvn, jnp.ones_like(ckvn)], axis=1).astype(BF16)

    u = jax.nn.gelu(proj(C_U, C_V))
    v = jax.nn.gelu(proj(C_V, C_GA))
    mu = jnp.mean(v, axis=-1, keepdims=True)
    vc = v - mu
    var = jnp.mean(vc * vc, axis=-1, keepdims=True)
    vb = (vc * lax.rsqrt(var + EPS) * sgain_ref[...] + sbias_ref[...]).astype(BF16)
    row = lax.broadcasted_iota(jnp.int32, (SGU_LEN, SGU_LEN), 0)
    col = lax.broadcasted_iota(jnp.int32, (SGU_LEN, SGU_LEN), 1)
    causal = (row >> CHUNK_SHIFT) >= (col >> CHUNK_SHIFT)
    low_half = col < SGU_GROUP_DIM
    zero_w = jnp.zeros((SGU_LEN, SGU_LEN), BF16)
    w_pairs = [jnp.concatenate([jnp.where(causal, wsp_ref[2 * p], zero_w),
                                jnp.where(causal, wsp_ref[2 * p + 1], zero_w)], axis=1)
               for p in range(SGU_WIDTH // LANES)]
    row_blocks = []
    for r in range(TMI // SGU_LEN):
        pieces = []
        for p in range(SGU_WIDTH // LANES):
            blk = vb[r * SGU_LEN:(r + 1) * SGU_LEN, p * LANES:(p + 1) * LANES]
            stacked = jnp.concatenate([jnp.where(low_half, blk, zero_w), jnp.where(low_half, zero_w, blk)], axis=0)
            sv = _dot(w_pairs[p], stacked) + bsp_ref[p]
            pieces.append((u[r * SGU_LEN:(r + 1) * SGU_LEN, p * LANES:(p + 1) * LANES] * sv).astype(BF16))
        row_blocks.append(jnp.concatenate(pieces, axis=1))
    sgu = _dot(jnp.concatenate(row_blocks, axis=0), wosgu_ref[...])

    ga_ref[...] = jax.nn.sigmoid(proj(C_GA, C_GB)).astype(BF16)
    p_ref[...] = (jax.nn.sigmoid(proj(C_GB, C_END)) * sgu).astype(BF16)


def _inproj(x, cos_t, sin_t, gattn, win, gcq, gckv, wqlat, wqr, wqrs, sgain, sbias, wsp, bsp, wosgu):
    nt = N_TOK // TMI
    per_b = SEQ // TMI
    per_k = TK // TMI

    def const(shape):
        return pl.BlockSpec(shape, lambda i: (0,) * len(shape))

    return pl.pallas_call(
        _inproj_kernel,
        grid=(nt,),
        in_specs=[pl.BlockSpec((TMI, D_MODEL), lambda i: (i, 0)),
                  pl.BlockSpec((TMI, LANES), lambda i: (i, 0)), pl.BlockSpec((TMI, LANES), lambda i: (i, 0)),
                  const((1, D_MODEL)), const((D_MODEL, C_END)), const((1, Q_RANK)), const((1, KV_RANK)),
                  const((Q_RANK, MLA_HEADS * LANES)), const((Q_RANK, MLA_HEADS * QK_ROPE)),
                  const((Q_RANK, MLA_HEADS * QK_ROPE)),
                  const((1, SGU_WIDTH)), const((1, SGU_WIDTH)),
                  const((SGU_GROUPS, SGU_LEN, SGU_LEN)), const((SGU_GROUPS // 2, SGU_LEN, LANES)),
                  const((SGU_WIDTH, D_MODEL))],
        out_specs=[pl.BlockSpec((TMI // TQ, MLA_HEADS * TQ, QK_PAD), lambda i: (i, 0, 0)),
                   pl.BlockSpec((None, None, QK_PAD, TMI),
                                lambda i: (i // per_b, (i % per_b) // per_k, 0, i % per_k)),
                   pl.BlockSpec((None, None, TMI, V_PAD),
                                lambda i: (i // per_b, (i % per_b) // per_k, i % per_k, 0)),
                   pl.BlockSpec((TMI, D_MODEL), lambda i: (i, 0)),
                   pl.BlockSpec((TMI, D_MODEL), lambda i: (i, 0))],
        out_shape=[jax.ShapeDtypeStruct((N_TOK // TQ, MLA_HEADS * TQ, QK_PAD), BF16),
                   jax.ShapeDtypeStruct((BATCH, SEQ // TK, QK_PAD, TK), BF16),
                   jax.ShapeDtypeStruct((BATCH, SEQ // TK, TK, V_PAD), BF16),
                   jax.ShapeDtypeStruct((N_TOK, D_MODEL), BF16),
                   jax.ShapeDtypeStruct((N_TOK, D_MODEL), BF16)],
        compiler_params=pltpu.CompilerParams(dimension_semantics=("arbitrary",),
                                             vmem_limit_bytes=56 * 1024 * 1024),
        name="inproj",
    )(x, cos_t, sin_t, gattn, win, gcq, gckv, wqlat, wqr, wqrs, sgain, sbias, wsp, bsp, wosgu)


def _attn_kernel(q_ref, kt_ref, v_ref, wuv_ref, o_ref, m_ref, acc_ref, s_ref, p_ref, a_ref):
    qi = pl.program_id(1)
    m_ref[...] = jnp.full(m_ref.shape, NEG, F32)
    acc_ref[...] = jnp.zeros(acc_ref.shape, F32)

    def scores(j, slot):
        s_ref[slot] = _dot(q_ref[...], kt_ref[j])

    def update(j, slot, width=None):
        masked = width is not None
        w = width if masked else TK
        if masked:
            q_chunk = ((qi * TQ - j * TK) >> CHUNK_SHIFT) + (
                lax.broadcasted_iota(jnp.int32, (TQ, w), 0) >> CHUNK_SHIFT)
            k_chunk = lax.broadcasted_iota(jnp.int32, (TQ, w), 1) >> CHUNK_SHIFT
            visible = k_chunk <= q_chunk
        for h in range(MLA_HEADS):
            rows = slice(h * TQ, (h + 1) * TQ)
            s = s_ref[slot, rows, :w]
            if masked:
                s = jnp.where(visible, s, NEG)
            m_prev = m_ref[rows, :]
            m_new = jnp.maximum(m_prev, jnp.max(s, axis=-1, keepdims=True))
            m_ref[rows, :] = m_new
            a_ref[rows, :] = jnp.exp2(m_prev - m_new)
            s = s_ref[slot, rows, :w]
            if masked:
                s = jnp.where(visible, s, NEG)
            p_ref[rows, :w] = jnp.exp2(s - jnp.concatenate([m_new] * (w // LANES), axis=1)).astype(BF16)
        alpha = a_ref[...]
        acc_ref[...] = jnp.concatenate([alpha, alpha], axis=1) * acc_ref[...] + _dot(
            p_ref[:, :w], v_ref[j, :w, :])

    diag = (qi * TQ) >> TK_SHIFT
    scores(0, 0)

    def run(first, count, last_masked):
        for u in range(count):
            if not (last_masked and u == count - 1):
                scores(first + u + 1, (u + 1) % 2)
                update(first + u, u % 2)
            else:
                for sub in range(TK // TQ):
                    @pl.when((qi & (TK // TQ - 1)) == sub)
                    def _(u=u, sub=sub):
                        update(first + u, u % 2, (sub + 1) * TQ)

    def body(t, carry):
        run(ATTN_UNROLL * t, ATTN_UNROLL, False)
        return carry

    trips = diag >> ATTN_UNROLL_SHIFT
    lax.fori_loop(0, trips, body, 0)
    done = trips << ATTN_UNROLL_SHIFT
    for r in range(ATTN_UNROLL):
        @pl.when(diag - done == r)
        def _(r=r):
            run(done, r + 1, True)

    o_lat = (acc_ref[:, :KV_RANK] / acc_ref[:, KV_RANK:]).astype(BF16)
    o_cat = jnp.concatenate([o_lat[h * TQ:(h + 1) * TQ] for h in range(MLA_HEADS)], axis=1)
    o_ref[...] = _dot(o_cat, wuv_ref[...]).astype(BF16)


def _attention(q, kt, v, wuv):
    nk = SEQ // TK
    return pl.pallas_call(
        _attn_kernel,
        grid=(BATCH, SEQ // TQ),
        in_specs=[pl.BlockSpec((None, MLA_HEADS * TQ, QK_PAD), lambda b, i: (b * (SEQ // TQ) + i, 0, 0)),
                  pl.BlockSpec((None, nk, QK_PAD, TK), lambda b, i: (b, 0, 0, 0)),
                  pl.BlockSpec((None, nk, TK, V_PAD), lambda b, i: (b, 0, 0, 0)),
                  pl.BlockSpec((MLA_HEADS * KV_RANK, MLA_WIDTH), lambda b, i: (0, 0))],
        out_specs=pl.BlockSpec((None, TQ, MLA_WIDTH), lambda b, i: (b, i, 0)),
        out_shape=jax.ShapeDtypeStruct((BATCH, SEQ, MLA_WIDTH), BF16),
        scratch_shapes=[pltpu.VMEM((MLA_HEADS * TQ, LANES), F32),
                        pltpu.VMEM((MLA_HEADS * TQ, V_PAD), F32),
                        pltpu.VMEM((2, MLA_HEADS * TQ, TK), F32),
                        pltpu.VMEM((MLA_HEADS * TQ, TK), BF16),
                        pltpu.VMEM((MLA_HEADS * TQ, LANES), F32)],
        compiler_params=pltpu.CompilerParams(dimension_semantics=("arbitrary", "arbitrary"),
                                             vmem_limit_bytes=40 * 1024 * 1024),
        name="attention",
    )(q, kt, v, wuv)


def _mix_kernel(attn_ref, ga_ref, p_ref, x_ref, woa_ref, wout_ref, gffn_ref, wr_ref, br_ref,
                x1_ref, h2_ref, mf_ref, slot_col_ref, slot_row_ref, run_chunks_ref, run_start_ref):
    a = _dot(attn_ref[...], woa_ref[...])
    mix = (ga_ref[...].astype(F32) * a + p_ref[...].astype(F32)).astype(BF16)
    x1 = x_ref[...] + _dot(mix, wout_ref[...])
    x1_ref[...] = x1
    h2 = _rms(x1, gffn_ref[...])
    h2_ref[...] = h2.astype(BF16)

    hi = h2.astype(BF16)
    lo = (h2 - hi.astype(F32)).astype(BF16)
    r1 = _dot(hi, wr_ref[...])
    r2 = _dot(lo, wr_ref[:, :LANES])
    logits_all = r1[:, :LANES] + r1[:, LANES:] + r2 + br_ref[...]

    for r in range(TMX // TM):
        mf, slots, run_chunks_rows, run_start_rows = _route_tile(logits_all[r * TM:(r + 1) * TM])
        mf_ref[r * TM:(r + 1) * TM, :] = mf
        slot_col_ref[r * TM:(r + 1) * TM, :] = slots.astype(jnp.int32)
        tile_rows = slice(SUBLANES * r, SUBLANES * (r + 1))
        slot_row_ref[tile_rows, :] = slots.T[:SUBLANES].astype(jnp.int32)
        run_chunks_ref[tile_rows, :] = run_chunks_rows.astype(jnp.int32)
        run_start_ref[tile_rows, :] = run_start_rows.astype(jnp.int32)


def _route_tile(logits):
    lane_i = lax.broadcasted_iota(jnp.int32, (TM, LANES), 1)
    lane = lane_i.astype(F32)
    lane_group = (lane_i >> 3).astype(F32)
    ninf = -jnp.inf
    is_group = (lane_i >= N_EXPERTS) & (lane_i < N_EXPERTS + N_GROUPS)
    lg = jnp.where(is_group, logits, ninf)
    gmax = jnp.max(lg, axis=-1, keepdims=True)
    gsum = jnp.sum(jnp.exp(lg - gmax), axis=-1, keepdims=True)
    p_top = 1.0 / gsum
    g_idx = jnp.min(jnp.where(lg == gmax, lane - N_EXPERTS, float(N_GROUPS)), axis=-1, keepdims=True)
    le = jnp.where((lane_i < N_EXPERTS) & (lane_group == g_idx), logits, ninf)
    t1 = jnp.max(le, axis=-1, keepdims=True)
    e1 = jnp.min(jnp.where(le == t1, lane, float(LANES)), axis=-1, keepdims=True)
    le2 = jnp.where(lane == e1, ninf, le)
    t2 = jnp.max(le2, axis=-1, keepdims=True)
    e2 = jnp.min(jnp.where(le2 == t2, lane, float(LANES)), axis=-1, keepdims=True)
    ex = jnp.exp(t2 - t1)
    w1 = p_top / (1.0 + ex)
    w2 = p_top * ex / (1.0 + ex)

    sel1 = lane == e1
    sel2 = lane == e2
    onehot = jnp.where(sel1 | sel2, 1.0, 0.0)
    rr = lax.broadcasted_iota(jnp.int32, (TM, TM), 0)
    cc = lax.broadcasted_iota(jnp.int32, (TM, TM), 1)
    ltri = jnp.where(cc < rr, 1.0, 0.0).astype(BF16)
    rank = _dot(ltri, onehot.astype(BF16))
    cnt = jnp.sum(onehot, axis=0, keepdims=True)
    run_chunks = jnp.floor((cnt + (CHUNK_ROWS - 1)) * (1.0 / CHUNK_ROWS))
    ur = lax.broadcasted_iota(jnp.int32, (LANES, LANES), 0)
    uc = lax.broadcasted_iota(jnp.int32, (LANES, LANES), 1)
    upper = jnp.where(ur < uc, 1.0, 0.0).astype(BF16)
    run_chunks_rows = jnp.broadcast_to(run_chunks, (SUBLANES, LANES))
    run_start_rows = _dot(run_chunks_rows.astype(BF16), upper)
    slot_all = CHUNK_ROWS * run_start_rows[0:1] + rank
    slot1 = jnp.sum(jnp.where(sel1, slot_all, 0.0), axis=-1, keepdims=True)
    slot2 = jnp.sum(jnp.where(sel2, slot_all, 0.0), axis=-1, keepdims=True)

    slots = jnp.where(lane_i == 0, slot1, jnp.where(lane_i == 1, slot2, 0.0))
    return jnp.where(lane_i == 0, w1, w2), slots, run_chunks_rows, run_start_rows


def _mix(attn, ga, p, x, woa, wout, gffn, wr, br):
    nt = N_TOK // TM
    sub = TMX // TM

    def const(shape):
        return pl.BlockSpec(shape, lambda i: (0,) * len(shape))

    def rows(width):
        return pl.BlockSpec((TMX, width), lambda i: (i, 0))

    return pl.pallas_call(
        _mix_kernel,
        grid=(N_TOK // TMX,),
        in_specs=[rows(MLA_WIDTH), rows(D_MODEL), rows(D_MODEL), rows(D_MODEL),
                  const((MLA_WIDTH, D_MODEL)), const((D_MODEL, D_MODEL)), const((1, D_MODEL)),
                  const((D_MODEL, 2 * LANES)), const((1, LANES))],
        out_specs=[rows(D_MODEL), rows(D_MODEL), rows(LANES), rows(LANES),
                   pl.BlockSpec((SUBLANES * sub, TM), lambda i: (i, 0)),
                   pl.BlockSpec((SUBLANES * sub, LANES), lambda i: (i, 0)),
                   pl.BlockSpec((SUBLANES * sub, LANES), lambda i: (i, 0))],
        out_shape=[jax.ShapeDtypeStruct((N_TOK, D_MODEL), F32),
                   jax.ShapeDtypeStruct((N_TOK, D_MODEL), BF16),
                   jax.ShapeDtypeStruct((N_TOK, LANES), F32),
                   jax.ShapeDtypeStruct((N_TOK, LANES), jnp.int32),
                   jax.ShapeDtypeStruct((nt * SUBLANES, TM), jnp.int32),
                   jax.ShapeDtypeStruct((nt * SUBLANES, LANES), jnp.int32),
                   jax.ShapeDtypeStruct((nt * SUBLANES, LANES), jnp.int32)],
        compiler_params=pltpu.CompilerParams(dimension_semantics=("arbitrary",),
                                             vmem_limit_bytes=40 * 1024 * 1024),
        name="mix",
    )(attn, ga, p, x, woa, wout, gffn, wr, br)


def _chunk_copy(src, dst, src_chunk, dst_chunk, sem, nchunks=1):
    rows = nchunks * CHUNK_ROWS
    return pltpu.make_async_copy(
        src.at[pl.ds(pl.multiple_of(src_chunk * CHUNK_ROWS, CHUNK_ROWS), rows)],
        dst.at[pl.ds(pl.multiple_of(dst_chunk * CHUNK_ROWS, CHUNK_ROWS), rows)], sem)


def _unrolled_loop(n, fn):
    groups = n >> CHUNK_UNROLL_SHIFT

    def group(g, carry):
        for u in range(CHUNK_UNROLL):
            fn(g * CHUNK_UNROLL + u)
        return carry

    lax.fori_loop(0, groups, group, 0)

    def single(k, carry):
        fn(k)
        return carry

    lax.fori_loop(groups << CHUNK_UNROLL_SHIFT, n, single, 0)


def _for_each_copy(t, copy_tables, fn):
    npair_ref, nsingle_ref, lpair_ref, gpair_ref, lsingle_ref, gsingle_ref = copy_tables
    pb = t * PAIR_SLOTS
    sb = t * N_EXPERTS
    _unrolled_loop(npair_ref[t], lambda k: fn(lpair_ref[pb + k], gpair_ref[pb + k], 2))
    _unrolled_loop(nsingle_ref[t], lambda k: fn(lsingle_ref[sb + k], gsingle_ref[sb + k], 1))


def _tile_chunks(t, copy_tables):
    return 2 * copy_tables[0][t] + copy_tables[1][t]


def _wait_chunks(n, src, dst, sem):
    groups = n >> CHUNK_UNROLL_SHIFT
    rows = CHUNK_UNROLL * CHUNK_ROWS

    def group(g, carry):
        pltpu.make_async_copy(src.at[pl.ds(0, rows)], dst.at[pl.ds(0, rows)], sem).wait()
        return carry

    lax.fori_loop(0, groups, group, 0)

    def single(k, carry):
        _chunk_copy(src, dst, 0, 0, sem).wait()
        return carry

    lax.fori_loop(groups << CHUNK_UNROLL_SHIFT, n, single, 0)


def _dispatch_kernel(npair_ref, nsingle_ref, lpair_ref, gpair_ref, lsingle_ref, gsingle_ref,
                     zero_start_ref, zero_count_ref, nvb_ref,
                     slot_ref, h2_ref, xs_hbm, sbuf, zbuf, sem, zsem):
    copy_tables = (npair_ref, nsingle_ref, lpair_ref, gpair_ref, lsingle_ref, gsingle_ref)
    t = pl.program_id(0)
    last = pl.num_programs(0) - 1
    cur = t % 2

    def zero_chunk_copy(dst_chunk):
        return _chunk_copy(zbuf, xs_hbm, 0, dst_chunk, zsem)

    def zero_block_copy(b):
        return pltpu.make_async_copy(
            zbuf, xs_hbm.at[pl.ds(pl.multiple_of(b * MOE_T, MOE_T), MOE_T)], zsem)

    def for_each_zero(chunk_fn, block_fn):
        def per_expert(e, carry):
            def per_chunk(j, carry2):
                chunk_fn(zero_start_ref[e] + j)
                return carry2

            lax.fori_loop(0, zero_count_ref[e], per_chunk, 0)
            return carry

        lax.fori_loop(0, N_EXPERTS, per_expert, 0)

        def per_block(b, carry):
            block_fn(b)
            return carry

        lax.fori_loop(nvb_ref[0], MOE_BLOCKS, per_block, 0)

    @pl.when(t == 0)
    def _():
        zbuf[...] = jnp.zeros(zbuf.shape, BF16)
        for_each_zero(lambda dst_chunk: zero_chunk_copy(dst_chunk).start(), lambda b: zero_block_copy(b).start())

    row = lax.broadcasted_iota(jnp.int32, (TILE_ROWS, TM), 0)
    slots = slot_ref[...]
    perm = jnp.where((row == slots[0:1, :]) | (row == slots[1:2, :]), 1.0, 0.0).astype(BF16)
    sbuf[cur] = _dot(perm, h2_ref[...]).astype(BF16)
    _for_each_copy(t, copy_tables,
                   lambda lc, gc, n: _chunk_copy(sbuf.at[cur], xs_hbm, lc, gc, sem.at[cur], n).start())

    def wait_tile(tile, slot):
        _wait_chunks(_tile_chunks(tile, copy_tables), sbuf.at[slot], xs_hbm, sem.at[slot])

    @pl.when(t > 0)
    def _():
        wait_tile(t - 1, 1 - cur)

    @pl.when(t == last)
    def _():
        wait_tile(t, cur)
        for_each_zero(lambda dst_chunk: zero_chunk_copy(dst_chunk).wait(), lambda b: zero_block_copy(b).wait())


def _dispatch(tables, slot_rows, h2):
    return pl.pallas_call(
        _dispatch_kernel,
        grid_spec=pltpu.PrefetchScalarGridSpec(
            num_scalar_prefetch=len(tables),
            grid=(N_TOK // TM,),
            in_specs=[pl.BlockSpec((SUBLANES, TM), lambda i, *_: (i, 0)),
                      pl.BlockSpec((TM, D_MODEL), lambda i, *_: (i, 0))],
            out_specs=pl.BlockSpec(memory_space=pl.ANY),
            scratch_shapes=[pltpu.VMEM((2, TILE_ROWS, D_MODEL), BF16), pltpu.VMEM((MOE_T, D_MODEL), BF16),
                            pltpu.SemaphoreType.DMA((2,)), pltpu.SemaphoreType.DMA(())]),
        out_shape=jax.ShapeDtypeStruct((MOE_ROWS, D_MODEL), BF16),
        compiler_params=pltpu.CompilerParams(dimension_semantics=("arbitrary",)),
        name="dispatch",
    )(*tables, slot_rows, h2)


def _expert_kernel(blk_e_ref, nvb_ref, xs_ref, wg_ref, wu_ref, wd_ref, ys_ref, wgb, wub, wdb):
    i = pl.program_id(0)
    live = i < nvb_ref[0]

    @pl.when(live & ((i == 0) | (blk_e_ref[i] != blk_e_ref[jnp.maximum(i - 1, 0)])))
    def _():
        wgb[...] = wg_ref[...].astype(BF16)
        wub[...] = wu_ref[...].astype(BF16)
        wdb[...] = wd_ref[...].astype(BF16)

    @pl.when(live)
    def _():
        xb = xs_ref[...]
        g = _dot(xb, wgb[...])
        u = _dot(xb, wub[...])
        hid = (jax.nn.silu(g) * u).astype(BF16)
        ys_ref[...] = _dot(hid, wdb[...]).astype(BF16)

    @pl.when(jnp.logical_not(live))
    def _():
        ys_ref[...] = jnp.zeros(ys_ref.shape, ys_ref.dtype)


def _experts(blk_e, nvb, xs, wg, wu, wd):
    def row_block(i, be, nv):
        return (jnp.minimum(i, nv[0] - 1), 0)

    def weight(i, be, nv):
        return (be[jnp.minimum(i, nv[0] - 1)], 0, 0)

    return pl.pallas_call(
        _expert_kernel,
        grid_spec=pltpu.PrefetchScalarGridSpec(
            num_scalar_prefetch=2,
            grid=(MOE_BLOCKS,),
            in_specs=[pl.BlockSpec((MOE_T, D_MODEL), row_block),
                      pl.BlockSpec((None, D_MODEL, D_EXPERT), weight),
                      pl.BlockSpec((None, D_MODEL, D_EXPERT), weight),
                      pl.BlockSpec((None, D_EXPERT, D_MODEL), weight)],
            out_specs=pl.BlockSpec((MOE_T, D_MODEL), lambda i, be, nv: (i, 0)),
            scratch_shapes=[pltpu.VMEM((D_MODEL, D_EXPERT), BF16), pltpu.VMEM((D_MODEL, D_EXPERT), BF16),
                            pltpu.VMEM((D_EXPERT, D_MODEL), BF16)]),
        out_shape=jax.ShapeDtypeStruct((MOE_ROWS, D_MODEL), BF16),
        compiler_params=pltpu.CompilerParams(dimension_semantics=("arbitrary",),
                                             vmem_limit_bytes=40 * 1024 * 1024),
        name="experts",
    )(blk_e, nvb, xs, wg, wu, wd)


def _combine_kernel(npair_ref, nsingle_ref, lpair_ref, gpair_ref, lsingle_ref, gsingle_ref,
                    slot_ref, x1_ref, mf_ref, gfin_ref, ys_hbm, o_ref, ybuf, sem):
    copy_tables = (npair_ref, nsingle_ref, lpair_ref, gpair_ref, lsingle_ref, gsingle_ref)
    t = pl.program_id(0)
    cur = t % 2

    def fetch(tile, slot):
        _for_each_copy(tile, copy_tables,
                       lambda lc, gc, n: _chunk_copy(ys_hbm, ybuf.at[slot], gc, lc, sem.at[slot], n).start())

    @pl.when(t == 0)
    def _():
        ybuf[...] = jnp.zeros(ybuf.shape, BF16)
        fetch(0, 0)

    @pl.when(t + 1 < pl.num_programs(0))
    def _():
        fetch(t + 1, 1 - cur)

    used_chunks = _tile_chunks(t, copy_tables)
    _wait_chunks(used_chunks, ys_hbm, ybuf.at[cur], sem.at[cur])

    def finish(rows):
        yb = ybuf[cur, :rows, :]
        col = lax.broadcasted_iota(jnp.int32, (TM, rows), 1)
        slots = slot_ref[...]
        y1 = _dot(jnp.where(col == slots[:, 0:1], 1.0, 0.0).astype(BF16), yb)
        y2 = _dot(jnp.where(col == slots[:, 1:2], 1.0, 0.0).astype(BF16), yb)
        mf = mf_ref[...]
        x2 = x1_ref[...] + mf[:, 0:1] * y1 + mf[:, 1:2] * y2
        o_ref[...] = _rms(x2, gfin_ref[...])

    short = used_chunks * CHUNK_ROWS <= SHORT_TILE_ROWS

    @pl.when(short)
    def _():
        finish(SHORT_TILE_ROWS)

    @pl.when(jnp.logical_not(short))
    def _():
        finish(TILE_ROWS)


def _combine(tables, slot_cols, x1, mf, gfin, ys):
    return pl.pallas_call(
        _combine_kernel,
        grid_spec=pltpu.PrefetchScalarGridSpec(
            num_scalar_prefetch=len(tables),
            grid=(N_TOK // TM,),
            in_specs=[pl.BlockSpec((TM, LANES), lambda i, *_: (i, 0)),
                      pl.BlockSpec((TM, D_MODEL), lambda i, *_: (i, 0)),
                      pl.BlockSpec((TM, LANES), lambda i, *_: (i, 0)),
                      pl.BlockSpec((1, D_MODEL), lambda i, *_: (0, 0)),
                      pl.BlockSpec(memory_space=pl.ANY)],
            out_specs=pl.BlockSpec((TM, D_MODEL), lambda i, *_: (i, 0)),
            scratch_shapes=[pltpu.VMEM((2, TILE_ROWS, D_MODEL), BF16), pltpu.SemaphoreType.DMA((2,))]),
        out_shape=jax.ShapeDtypeStruct((N_TOK, D_MODEL), F32),
        compiler_params=pltpu.CompilerParams(dimension_semantics=("arbitrary",)),
        name="combine",
    )(*tables, slot_cols, x1, mf, gfin, ys)


def kernel(x, positions, g_attn_norm, w_in, g_cq, w_uq, g_ckv, w_uk, w_uv, w_o_attn, sgu_gain, sgu_bias, w_spatial, b_spatial, w_o_sgu, w_out, g_ffn_norm, w_router_group, b_router_group, w_router_expert, b_router_expert, w_exp_gate, w_exp_up, w_exp_down, g_final):
    assert x.shape == (BATCH, SEQ, D_MODEL) and w_in.shape[0] == 1
    half = QK_ROPE // 2
    swap = jnp.concatenate([jnp.arange(half, QK_ROPE), jnp.arange(0, half)])

    def pad_cols(w, width):
        return jnp.pad(w, ((0, 0), (0, width - w.shape[1])))

    wi = w_in[0]
    c0 = Q_RANK + KV_RANK
    kr = wi[:, c0:c0 + QK_ROPE]
    c1 = c0 + QK_ROPE
    win = jnp.concatenate([
        wi[:, :c0], jnp.tile(kr, (1, LANES // QK_ROPE)), jnp.tile(kr[:, swap], (1, LANES // QK_ROPE)),
        wi[:, c1:]], axis=1).astype(BF16)

    wq = w_uq[0].reshape(Q_RANK, MLA_HEADS, QK_NOPE + QK_ROPE)
    wq_nope = wq[:, :, :QK_NOPE].transpose(1, 0, 2)
    wq_rope = wq[:, :, QK_NOPE:]
    w_ukt = w_uk[0].reshape(KV_RANK, MLA_HEADS, QK_NOPE).transpose(1, 2, 0)
    wqlat = _fold(wq_nope, w_ukt).transpose(1, 0, 2).reshape(Q_RANK, MLA_HEADS * LANES).astype(BF16)

    wqr = wq_rope.reshape(Q_RANK, MLA_HEADS * QK_ROPE).astype(BF16)
    wqrs = wq_rope[:, :, swap].reshape(Q_RANK, MLA_HEADS * QK_ROPE).astype(BF16)

    per_row = LANES // half
    freqs = ROPE_THETA ** (-jnp.arange(0, QK_ROPE, 2, dtype=F32) / QK_ROPE)
    pos8 = jnp.repeat(positions.astype(F32).reshape(N_TOK // per_row, per_row), half, axis=1)
    cos8, sin8 = _rope_tables(pos8, jnp.tile(freqs, per_row)[None, :])
    cos16 = cos8.reshape(N_TOK, half)
    sin16 = sin8.reshape(N_TOK, half)
    cos_t = jnp.tile(jnp.concatenate([cos16, cos16], axis=1), (1, LANES // QK_ROPE))
    sin_t = jnp.tile(jnp.concatenate([-sin16, sin16], axis=1), (1, LANES // QK_ROPE))

    head_of_col = jnp.arange(MLA_WIDTH) // V_HEAD
    wuv = jnp.where(head_of_col[None, None, :] == jnp.arange(MLA_HEADS)[:, None, None],
                    w_uv[0][None], 0.0).astype(BF16)
    wuv = wuv.reshape(MLA_HEADS * KV_RANK, MLA_WIDTH)

    wsp = w_spatial[0].astype(BF16)
    bs = b_spatial[0]
    bsp = jnp.repeat(bs.reshape(SGU_GROUPS // 2, 2, SGU_LEN).transpose(0, 2, 1), SGU_GROUP_DIM, axis=2)

    wr32 = jnp.concatenate([w_router_expert[0].transpose(1, 0, 2).reshape(D_MODEL, N_EXPERTS),
                            w_router_group[0]], axis=1)
    wr32 = pad_cols(wr32, LANES)
    wr_hi = wr32.astype(BF16)
    wr_lo = (wr32 - wr_hi.astype(F32)).astype(BF16)
    wr = jnp.concatenate([wr_hi, wr_lo], axis=1)
    br = pad_cols(jnp.concatenate([b_router_expert[0].reshape(-1), b_router_group[0]])[None, :], LANES)

    xf = x.reshape(N_TOK, D_MODEL)
    q, kt, v, p, ga = _inproj(
        xf, cos_t, sin_t, g_attn_norm, win, g_cq, g_ckv, wqlat, wqr, wqrs,
        sgu_gain, sgu_bias, wsp, bsp, w_o_sgu[0].astype(BF16))
    attn = _attention(q, kt, v, wuv)
    x1, h2, mf, slot_cols, slot_rows, run_chunks_rows, run_start_rows = _mix(
        attn.reshape(N_TOK, MLA_WIDTH), ga, p, xf, w_o_attn[0].astype(BF16), w_out[0].astype(BF16),
        g_ffn_norm, wr, br)

    blk_chunks = MOE_T // CHUNK_ROWS
    run_chunks = run_chunks_rows[::SUBLANES, :N_EXPERTS]
    run_start = run_start_rows[::SUBLANES, :N_EXPERTS]
    seg_chunks = jnp.sum(run_chunks, axis=0)
    seg_padded = (seg_chunks + blk_chunks - 1) // blk_chunks * blk_chunks
    seg_end = jnp.cumsum(seg_padded)
    seg_start = seg_end - seg_padded
    run_dest = seg_start[None, :] + jnp.cumsum(run_chunks, axis=0) - run_chunks
    nvb = (seg_end[-1:] // blk_chunks).astype(jnp.int32)
    blk_first_chunk = jnp.arange(MOE_BLOCKS, dtype=jnp.int32) * blk_chunks
    blk_e = jnp.minimum(jnp.sum((seg_end[None, :] <= blk_first_chunk[:, None]).astype(jnp.int32), axis=1),
                        N_EXPERTS - 1)
    def flat_list(count, slots, first_local, first_global, step):
        off = jnp.cumsum(count, axis=1) - count
        k = jnp.arange(slots, dtype=jnp.int32)
        in_run = (off[:, None, :] <= k[None, :, None]) & (k[None, :, None] < (off + count)[:, None, :])

        def pick(first):
            return (jnp.sum(jnp.where(in_run, (first - step * off)[:, None, :], 0), axis=-1)
                    + step * k[None, :]).reshape(-1).astype(jnp.int32)

        return jnp.sum(count, axis=1).astype(jnp.int32), pick(first_local), pick(first_global)

    npair, lpair, gpair = flat_list(run_chunks // 2, PAIR_SLOTS, run_start, run_dest, 2)
    nsingle, lsingle, gsingle = flat_list(run_chunks % 2, N_EXPERTS, run_start + run_chunks - 1, run_dest + run_chunks - 1, 0)
    run_tables = (npair, nsingle, lpair, gpair, lsingle, gsingle)

    xs = _dispatch(run_tables + ((seg_start + seg_chunks).astype(jnp.int32), (seg_padded - seg_chunks).astype(jnp.int32), nvb),
                   slot_rows, h2)
    ys = _experts(blk_e, nvb, xs, w_exp_gate[0], w_exp_up[0], w_exp_down[0])
    out = _combine(run_tables, slot_cols, x1, mf, g_final.reshape(1, D_MODEL), ys)
    return out.reshape(BATCH, SEQ, D_MODEL)
```

```python
import jax
import jax.numpy as jnp
from jax import lax
from jax.experimental import pallas as pl
from jax.experimental.pallas import tpu as pltpu

D_MODEL = 1024
BATCH = 2
SEQ = 8192
N_TOK = BATCH * SEQ
CHUNK = 64
EPS = 1e-6
MLA_HEADS = 8
Q_RANK = 256
KV_RANK = 128
QK_NOPE = 64
QK_ROPE = 32
V_HEAD = 64
MLA_WIDTH = MLA_HEADS * V_HEAD
ROPE_THETA = 10000.0
SGU_GROUPS = 8
SGU_GROUP_DIM = 64
SGU_WIDTH = SGU_GROUPS * SGU_GROUP_DIM
SGU_LEN = 128
N_GROUPS = 4
EXPERTS_PER_GROUP = 8
N_EXPERTS = N_GROUPS * EXPERTS_PER_GROUP
TOP_K = 2
D_EXPERT = 256

LANES = 128
SUBLANES = 8
QK_PAD = 2 * LANES
V_PAD = 2 * LANES
SCALE = (QK_NOPE + QK_ROPE) ** -0.5
LOG2E = 1.4426950408889634
Q_SCALE = SCALE * LOG2E
NEG = -1e30

TM = 256
TMI = 512
TMX = 1024
TQ = 128
TK = 512
ATTN_UNROLL_SHIFT = 1
ATTN_UNROLL = 1 << ATTN_UNROLL_SHIFT
TK_SHIFT = TK.bit_length() - 1
CHUNK_SHIFT = CHUNK.bit_length() - 1
assert 1 << TK_SHIFT == TK and 1 << CHUNK_SHIFT == CHUNK
MOE_T = 512
CHUNK_ROWS = 16
TILE_ROWS = -(-(TOP_K * TM + N_EXPERTS * (CHUNK_ROWS - 1) + CHUNK_ROWS) // 256) * 256
SHORT_TILE_ROWS = TILE_ROWS - 256
TILE_CHUNKS = TILE_ROWS // CHUNK_ROWS
PAIR_SLOTS = TILE_CHUNKS // 2
CHUNK_UNROLL_SHIFT = 2
CHUNK_UNROLL = 1 << CHUNK_UNROLL_SHIFT
MOE_ROWS_MAX = (N_TOK * TOP_K + (N_TOK // TM) * N_EXPERTS * (CHUNK_ROWS - 1)
                + N_EXPERTS * (MOE_T - CHUNK_ROWS))
STEP_BLOCKS = 2
MOE_BLOCKS = -(-MOE_ROWS_MAX // (MOE_T * STEP_BLOCKS)) * STEP_BLOCKS
MOE_ROWS = MOE_BLOCKS * MOE_T

C_Q = 0
C_KV = C_Q + Q_RANK
C_KR = C_KV + KV_RANK
C_KRS = C_KR + LANES
C_U = C_KRS + LANES
C_V = C_U + SGU_WIDTH
C_GA = C_V + SGU_WIDTH
C_GB = C_GA + D_MODEL
C_END = C_GB + D_MODEL

F32 = jnp.float32
BF16 = jnp.bfloat16


def _dot(a, b):
    return jnp.dot(a, b, preferred_element_type=F32)


def _rms(x, g):
    return x * lax.rsqrt(jnp.mean(x * x, axis=-1, keepdims=True) + EPS) * g


def _fold_kernel(a_ref, b_ref, o_ref):
    o_ref[...] = Q_SCALE * jnp.dot(a_ref[...], b_ref[...], preferred_element_type=F32,
                                 precision=lax.Precision.HIGHEST)


def _fold(w_uq_nope, w_ukt):
    return pl.pallas_call(
        _fold_kernel,
        grid=(MLA_HEADS,),
        in_specs=[pl.BlockSpec((None, Q_RANK, QK_NOPE), lambda h: (h, 0, 0)),
                  pl.BlockSpec((None, QK_NOPE, KV_RANK), lambda h: (h, 0, 0))],
        out_specs=pl.BlockSpec((None, Q_RANK, KV_RANK), lambda h: (h, 0, 0)),
        out_shape=jax.ShapeDtypeStruct((MLA_HEADS, Q_RANK, KV_RANK), F32),
        name="fold",
    )(w_uq_nope, w_ukt)


def _rope_kernel(pos_ref, freq_ref, cos_ref, sin_ref):
    ang = pos_ref[...] * freq_ref[...]
    cos_ref[...] = jnp.cos(ang)
    sin_ref[...] = jnp.sin(ang)


def _rope_tables(pos8, freq8):
    rows = pos8.shape[0]
    blk = pl.BlockSpec((TM, LANES), lambda i: (i, 0))
    return pl.pallas_call(
        _rope_kernel,
        grid=(rows // TM,),
        in_specs=[blk, pl.BlockSpec((1, LANES), lambda i: (0, 0))],
        out_specs=[blk, blk],
        out_shape=[jax.ShapeDtypeStruct((rows, LANES), F32)] * 2,
        name="rope_tables",
    )(pos8, freq8)


def _inproj_kernel(x_ref, cos_ref, sin_ref, gattn_ref, win_ref, gcq_ref, gckv_ref, wqlat_ref, wqr_ref,
                   wqrs_ref, sgain_ref, sbias_ref, wsp_ref, bsp_ref, wosgu_ref,
                   q_ref, kt_ref, v_ref, p_ref, ga_ref):
    hb = _rms(x_ref[...], gattn_ref[...]).astype(BF16)

    def proj(a, b):
        return _dot(hb, win_ref[:, a:b])

    cos_t = cos_ref[...]
    sin_t = sin_ref[...]

    cqn = _rms(proj(C_Q, C_KV), gcq_ref[...]).astype(BF16)
    qlat = _dot(cqn, wqlat_ref[...])
    groups = MLA_HEADS * QK_ROPE // LANES
    rot = (_dot(cqn, wqr_ref[...]) * jnp.concatenate([cos_t * Q_SCALE] * groups, axis=1)
           + _dot(cqn, wqrs_ref[...]) * jnp.concatenate([sin_t * Q_SCALE] * groups, axis=1))
    lane = lax.broadcasted_iota(jnp.int32, (TQ, LANES), 1)
    for r in range(TMI // TQ):
        t0, t1 = r * TQ, (r + 1) * TQ
        for h in range(MLA_HEADS):
            a, b = h * LANES, (h + 1) * LANES
            q_ref[r, h * TQ:(h + 1) * TQ, :LANES] = qlat[t0:t1, a:b].astype(BF16)
            g0 = (h * QK_ROPE) // LANES * LANES
            off = (h * QK_ROPE) % LANES
            window = (lane >= off) & (lane < off + QK_ROPE)
            q_ref[r, h * TQ:(h + 1) * TQ, LANES:] = jnp.where(
                window, rot[t0:t1, g0:g0 + LANES], 0.0).astype(BF16)

    zk = proj(C_KV, C_U)
    ckvn = _rms(zk[:, :KV_RANK], gckv_ref[...])
    krope = zk[:, KV_RANK:KV_RANK + LANES] * cos_t + zk[:, KV_RANK + LANES:] * sin_t
    kt_ref[...] = jnp.concatenate([ckvn, krope], axis=1).T.astype(BF16)
    v_ref[...] = jnp.concatenate([ckvn, jnp.ones_like(ckvn)], axis=1).astype(BF16)

    u = jax.nn.gelu(proj(C_U, C_V))
    v = jax.nn.gelu(proj(C_V, C_GA))
    mu = jnp.mean(v, axis=-1, keepdims=True)
    vc = v - mu
    var = jnp.mean(vc * vc, axis=-1, keepdims=True)
    vb = (vc * lax.rsqrt(var + EPS) * sgain_ref[...] + sbias_ref[...]).astype(BF16)
    row = lax.broadcasted_iota(jnp.int32, (SGU_LEN, SGU_LEN), 0)
    col = lax.broadcasted_iota(jnp.int32, (SGU_LEN, SGU_LEN), 1)
    causal = (row >> CHUNK_SHIFT) >= (col >> CHUNK_SHIFT)
    low_half = col < SGU_GROUP_DIM
    zero_w = jnp.zeros((SGU_LEN, SGU_LEN), BF16)
    w_pairs = [jnp.concatenate([jnp.where(causal, wsp_ref[2 * p], zero_w),
                                jnp.where(causal, wsp_ref[2 * p + 1], zero_w)], axis=1)
               for p in range(SGU_WIDTH // LANES)]
    row_blocks = []
    for r in range(TMI // SGU_LEN):
        pieces = []
        for p in range(SGU_WIDTH // LANES):
            blk = vb[r * SGU_LEN:(r + 1) * SGU_LEN, p * LANES:(p + 1) * LANES]
            stacked = jnp.concatenate([jnp.where(low_half, blk, zero_w), jnp.where(low_half, zero_w, blk)], axis=0)
            sv = _dot(w_pairs[p], stacked) + bsp_ref[p]
            pieces.append((u[r * SGU_LEN:(r + 1) * SGU_LEN, p * LANES:(p + 1) * LANES] * sv).astype(BF16))
        row_blocks.append(jnp.concatenate(pieces, axis=1))
    sgu = _dot(jnp.concatenate(row_blocks, axis=0), wosgu_ref[...])

    ga_ref[...] = jax.nn.sigmoid(proj(C_GA, C_GB)).astype(BF16)
    p_ref[...] = (jax.nn.sigmoid(proj(C_GB, C_END)) * sgu).astype(BF16)


def _inproj(x, cos_t, sin_t, gattn, win, gcq, gckv, wqlat, wqr, wqrs, sgain, sbias, wsp, bsp, wosgu):
    nt = N_TOK // TMI
    per_b = SEQ // TMI
    per_k = TK // TMI

    def const(shape):
        return pl.BlockSpec(shape, lambda i: (0,) * len(shape))

    return pl.pallas_call(
        _inproj_kernel,
        grid=(nt,),
        in_specs=[pl.BlockSpec((TMI, D_MODEL), lambda i: (i, 0)),
                  pl.BlockSpec((TMI, LANES), lambda i: (i, 0)), pl.BlockSpec((TMI, LANES), lambda i: (i, 0)),
                  const((1, D_MODEL)), const((D_MODEL, C_END)), const((1, Q_RANK)), const((1, KV_RANK)),
                  const((Q_RANK, MLA_HEADS * LANES)), const((Q_RANK, MLA_HEADS * QK_ROPE)),
                  const((Q_RANK, MLA_HEADS * QK_ROPE)),
                  const((1, SGU_WIDTH)), const((1, SGU_WIDTH)),
                  const((SGU_GROUPS, SGU_LEN, SGU_LEN)), const((SGU_GROUPS // 2, SGU_LEN, LANES)),
                  const((SGU_WIDTH, D_MODEL))],
        out_specs=[pl.BlockSpec((TMI // TQ, MLA_HEADS * TQ, QK_PAD), lambda i: (i, 0, 0)),
                   pl.BlockSpec((None, None, QK_PAD, TMI),
                                lambda i: (i // per_b, (i % per_b) // per_k, 0, i % per_k)),
                   pl.BlockSpec((None, None, TMI, V_PAD),
                                lambda i: (i // per_b, (i % per_b) // per_k, i % per_k, 0)),
                   pl.BlockSpec((TMI, D_MODEL), lambda i: (i, 0)),
                   pl.BlockSpec((TMI, D_MODEL), lambda i: (i, 0))],
        out_shape=[jax.ShapeDtypeStruct((N_TOK // TQ, MLA_HEADS * TQ, QK_PAD), BF16),
                   jax.ShapeDtypeStruct((BATCH, SEQ // TK, QK_PAD, TK), BF16),
                   jax.ShapeDtypeStruct((BATCH, SEQ // TK, TK, V_PAD), BF16),
                   jax.ShapeDtypeStruct((N_TOK, D_MODEL), BF16),
                   jax.ShapeDtypeStruct((N_TOK, D_MODEL), BF16)],
        compiler_params=pltpu.CompilerParams(dimension_semantics=("arbitrary",),
                                             vmem_limit_bytes=56 * 1024 * 1024),
        name="inproj",
    )(x, cos_t, sin_t, gattn, win, gcq, gckv, wqlat, wqr, wqrs, sgain, sbias, wsp, bsp, wosgu)


def _attn_kernel(q_ref, kt_ref, v_ref, wuv_ref, o_ref, m_ref, acc_ref, s_ref, p_ref, a_ref):
    qi = pl.program_id(1)
    m_ref[...] = jnp.full(m_ref.shape, NEG, F32)
    acc_ref[...] = jnp.zeros(acc_ref.shape, F32)

    def scores(j, slot):
        s_ref[slot] = _dot(q_ref[...], kt_ref[j])

    def update(j, slot, width=None):
        masked = width is not None
        w = width if masked else TK
        if masked:
            q_chunk = ((qi * TQ - j * TK) >> CHUNK_SHIFT) + (
                lax.broadcasted_iota(jnp.int32, (TQ, w), 0) >> CHUNK_SHIFT)
            k_chunk = lax.broadcasted_iota(jnp.int32, (TQ, w), 1) >> CHUNK_SHIFT
            visible = k_chunk <= q_chunk
        for h in range(MLA_HEADS):
            rows = slice(h * TQ, (h + 1) * TQ)
            s = s_ref[slot, rows, :w]
            if masked:
                s = jnp.where(visible, s, NEG)
            m_prev = m_ref[rows, :]
            m_new = jnp.maximum(m_prev, jnp.max(s, axis=-1, keepdims=True))
            m_ref[rows, :] = m_new
            a_ref[rows, :] = jnp.exp2(m_prev - m_new)
            s = s_ref[slot, rows, :w]
            if masked:
                s = jnp.where(visible, s, NEG)
            p_ref[rows, :w] = jnp.exp2(s - jnp.concatenate([m_new] * (w // LANES), axis=1)).astype(BF16)
        alpha = a_ref[...]
        acc_ref[...] = jnp.concatenate([alpha, alpha], axis=1) * acc_ref[...] + _dot(
            p_ref[:, :w], v_ref[j, :w, :])

    diag = (qi * TQ) >> TK_SHIFT
    scores(0, 0)

    def run(first, count, last_masked):
        for u in range(count):
            if not (last_masked and u == count - 1):
                scores(first + u + 1, (u + 1) % 2)
                update(first + u, u % 2)
            else:
                for sub in range(TK // TQ):
                    @pl.when((qi & (TK // TQ - 1)) == sub)
                    def _(u=u, sub=sub):
                        update(first + u, u % 2, (sub + 1) * TQ)

    def body(t, carry):
        run(ATTN_UNROLL * t, ATTN_UNROLL, False)
        return carry

    trips = diag >> ATTN_UNROLL_SHIFT
    lax.fori_loop(0, trips, body, 0)
    done = trips << ATTN_UNROLL_SHIFT
    for r in range(ATTN_UNROLL):
        @pl.when(diag - done == r)
        def _(r=r):
            run(done, r + 1, True)

    o_lat = (acc_ref[:, :KV_RANK] / acc_ref[:, KV_RANK:]).astype(BF16)
    o_cat = jnp.concatenate([o_lat[h * TQ:(h + 1) * TQ] for h in range(MLA_HEADS)], axis=1)
    o_ref[...] = _dot(o_cat, wuv_ref[...]).astype(BF16)


def _attention(q, kt, v, wuv):
    nk = SEQ // TK
    return pl.pallas_call(
        _attn_kernel,
        grid=(BATCH, SEQ // TQ),
        in_specs=[pl.BlockSpec((None, MLA_HEADS * TQ, QK_PAD), lambda b, i: (b * (SEQ // TQ) + i, 0, 0)),
                  pl.BlockSpec((None, nk, QK_PAD, TK), lambda b, i: (b, 0, 0, 0)),
                  pl.BlockSpec((None, nk, TK, V_PAD), lambda b, i: (b, 0, 0, 0)),
                  pl.BlockSpec((MLA_HEADS * KV_RANK, MLA_WIDTH), lambda b, i: (0, 0))],
        out_specs=pl.BlockSpec((None, TQ, MLA_WIDTH), lambda b, i: (b, i, 0)),
        out_shape=jax.ShapeDtypeStruct((BATCH, SEQ, MLA_WIDTH), BF16),
        scratch_shapes=[pltpu.VMEM((MLA_HEADS * TQ, LANES), F32),
                        pltpu.VMEM((MLA_HEADS * TQ, V_PAD), F32),
                        pltpu.VMEM((2, MLA_HEADS * TQ, TK), F32),
                        pltpu.VMEM((MLA_HEADS * TQ, TK), BF16),
                        pltpu.VMEM((MLA_HEADS * TQ, LANES), F32)],
        compiler_params=pltpu.CompilerParams(dimension_semantics=("arbitrary", "arbitrary"),
                                             vmem_limit_bytes=40 * 1024 * 1024),
        name="attention",
    )(q, kt, v, wuv)


def _mix_kernel(attn_ref, ga_ref, p_ref, x_ref, woa_ref, wout_ref, gffn_ref, wr_ref, br_ref,
                x1_ref, h2_ref, mf_ref, slot_col_ref, slot_row_ref, run_chunks_ref, run_start_ref):
    a = _dot(attn_ref[...], woa_ref[...])
    mix = (ga_ref[...].astype(F32) * a + p_ref[...].astype(F32)).astype(BF16)
    x1 = x_ref[...] + _dot(mix, wout_ref[...])
    x1_ref[...] = x1
    h2 = _rms(x1, gffn_ref[...])
    h2_ref[...] = h2.astype(BF16)

    hi = h2.astype(BF16)
    lo = (h2 - hi.astype(F32)).astype(BF16)
    r1 = _dot(hi, wr_ref[...])
    r2 = _dot(lo, wr_ref[:, :LANES])
    logits_all = r1[:, :LANES] + r1[:, LANES:] + r2 + br_ref[...]

    for r in range(TMX // TM):
        mf, slots, run_chunks_rows, run_start_rows = _route_tile(logits_all[r * TM:(r + 1) * TM])
        mf_ref[r * TM:(r + 1) * TM, :] = mf
        slot_col_ref[r * TM:(r + 1) * TM, :] = slots.astype(jnp.int32)
        tile_rows = slice(SUBLANES * r, SUBLANES * (r + 1))
        slot_row_ref[tile_rows, :] = slots.T[:SUBLANES].astype(jnp.int32)
        run_chunks_ref[tile_rows, :] = run_chunks_rows.astype(jnp.int32)
        run_start_ref[tile_rows, :] = run_start_rows.astype(jnp.int32)


def _route_tile(logits):
    lane_i = lax.broadcasted_iota(jnp.int32, (TM, LANES), 1)
    lane = lane_i.astype(F32)
    lane_group = (lane_i >> 3).astype(F32)
    ninf = -jnp.inf
    is_group = (lane_i >= N_EXPERTS) & (lane_i < N_EXPERTS + N_GROUPS)
    lg = jnp.where(is_group, logits, ninf)
    gmax = jnp.max(lg, axis=-1, keepdims=True)
    gsum = jnp.sum(jnp.exp(lg - gmax), axis=-1, keepdims=True)
    p_top = 1.0 / gsum
    g_idx = jnp.min(jnp.where(lg == gmax, lane - N_EXPERTS, float(N_GROUPS)), axis=-1, keepdims=True)
    le = jnp.where((lane_i < N_EXPERTS) & (lane_group == g_idx), logits, ninf)
    t1 = jnp.max(le, axis=-1, keepdims=True)
    e1 = jnp.min(jnp.where(le == t1, lane, float(LANES)), axis=-1, keepdims=True)
    le2 = jnp.where(lane == e1, ninf, le)
    t2 = jnp.max(le2, axis=-1, keepdims=True)
    e2 = jnp.min(jnp.where(le2 == t2, lane, float(LANES)), axis=-1, keepdims=True)
    ex = jnp.exp(t2 - t1)
    w1 = p_top / (1.0 + ex)
    w2 = p_top * ex / (1.0 + ex)

    sel1 = lane == e1
    sel2 = lane == e2
    onehot = jnp.where(sel1 | sel2, 1.0, 0.0)
    rr = lax.broadcasted_iota(jnp.int32, (TM, TM), 0)
    cc = lax.broadcasted_iota(jnp.int32, (TM, TM), 1)
    ltri = jnp.where(cc < rr, 1.0, 0.0).astype(BF16)
    rank = _dot(ltri, onehot.astype(BF16))
    cnt = jnp.sum(onehot, axis=0, keepdims=True)
    run_chunks = jnp.floor((cnt + (CHUNK_ROWS - 1)) * (1.0 / CHUNK_ROWS))
    ur = lax.broadcasted_iota(jnp.int32, (LANES, LANES), 0)
    uc = lax.broadcasted_iota(jnp.int32, (LANES, LANES), 1)
    upper = jnp.where(ur < uc, 1.0, 0.0).astype(BF16)
    run_chunks_rows = jnp.broadcast_to(run_chunks, (SUBLANES, LANES))
    run_start_rows = _dot(run_chunks_rows.astype(BF16), upper)
    slot_all = CHUNK_ROWS * run_start_rows[0:1] + rank
    slot1 = jnp.sum(jnp.where(sel1, slot_all, 0.0), axis=-1, keepdims=True)
    slot2 = jnp.sum(jnp.where(sel2, slot_all, 0.0), axis=-1, keepdims=True)

    slots = jnp.where(lane_i == 0, slot1, jnp.where(lane_i == 1, slot2, 0.0))
    return jnp.where(lane_i == 0, w1, w2), slots, run_chunks_rows, run_start_rows


def _mix(attn, ga, p, x, woa, wout, gffn, wr, br):
    nt = N_TOK // TM
    sub = TMX // TM

    def const(shape):
        return pl.BlockSpec(shape, lambda i: (0,) * len(shape))

    def rows(width):
        return pl.BlockSpec((TMX, width), lambda i: (i, 0))

    return pl.pallas_call(
        _mix_kernel,
        grid=(N_TOK // TMX,),
        in_specs=[rows(MLA_WIDTH), rows(D_MODEL), rows(D_MODEL), rows(D_MODEL),
                  const((MLA_WIDTH, D_MODEL)), const((D_MODEL, D_MODEL)), const((1, D_MODEL)),
                  const((D_MODEL, 2 * LANES)), const((1, LANES))],
        out_specs=[rows(D_MODEL), rows(D_MODEL), rows(LANES), rows(LANES),
                   pl.BlockSpec((SUBLANES * sub, TM), lambda i: (i, 0)),
                   pl.BlockSpec((SUBLANES * sub, LANES), lambda i: (i, 0)),
                   pl.BlockSpec((SUBLANES * sub, LANES), lambda i: (i, 0))],
        out_shape=[jax.ShapeDtypeStruct((N_TOK, D_MODEL), F32),
                   jax.ShapeDtypeStruct((N_TOK, D_MODEL), BF16),
                   jax.ShapeDtypeStruct((N_TOK, LANES), F32),
                   jax.ShapeDtypeStruct((N_TOK, LANES), jnp.int32),
                   jax.ShapeDtypeStruct((nt * SUBLANES, TM), jnp.int32),
                   jax.ShapeDtypeStruct((nt * SUBLANES, LANES), jnp.int32),
                   jax.ShapeDtypeStruct((nt * SUBLANES, LANES), jnp.int32)],
        compiler_params=pltpu.CompilerParams(dimension_semantics=("arbitrary",),
                                             vmem_limit_bytes=40 * 1024 * 1024),
        name="mix",
    )(attn, ga, p, x, woa, wout, gffn, wr, br)


def _chunk_copy(src, dst, src_chunk, dst_chunk, sem, nchunks=1):
    rows = nchunks * CHUNK_ROWS
    return pltpu.make_async_copy(
        src.at[pl.ds(pl.multiple_of(src_chunk * CHUNK_ROWS, CHUNK_ROWS), rows)],
        dst.at[pl.ds(pl.multiple_of(dst_chunk * CHUNK_ROWS, CHUNK_ROWS), rows)], sem)


def _unrolled_loop(n, fn):
    groups = n >> CHUNK_UNROLL_SHIFT

    def group(g, carry):
        for u in range(CHUNK_UNROLL):
            fn(g * CHUNK_UNROLL + u)
        return carry

    lax.fori_loop(0, groups, group, 0)

    def single(k, carry):
        fn(k)
        return carry

    lax.fori_loop(groups << CHUNK_UNROLL_SHIFT, n, single, 0)


def _for_each_copy(t, copy_tables, fn):
    npair_ref, nsingle_ref, lpair_ref, gpair_ref, lsingle_ref, gsingle_ref = copy_tables
    pb = t * PAIR_SLOTS
    sb = t * N_EXPERTS
    _unrolled_loop(npair_ref[t], lambda k: fn(lpair_ref[pb + k], gpair_ref[pb + k], 2))
    _unrolled_loop(nsingle_ref[t], lambda k: fn(lsingle_ref[sb + k], gsingle_ref[sb + k], 1))


def _tile_chunks(t, copy_tables):
    return 2 * copy_tables[0][t] + copy_tables[1][t]


def _wait_chunks(n, src, dst, sem):
    groups = n >> CHUNK_UNROLL_SHIFT
    rows = CHUNK_UNROLL * CHUNK_ROWS

    def group(g, carry):
        pltpu.make_async_copy(src.at[pl.ds(0, rows)], dst.at[pl.ds(0, rows)], sem).wait()
        return carry

    lax.fori_loop(0, groups, group, 0)

    def single(k, carry):
        _chunk_copy(src, dst, 0, 0, sem).wait()
        return carry

    lax.fori_loop(groups << CHUNK_UNROLL_SHIFT, n, single, 0)


def _dispatch_kernel(npair_ref, nsingle_ref, lpair_ref, gpair_ref, lsingle_ref, gsingle_ref,
                     zero_start_ref, zero_count_ref, nvb_ref,
                     slot_ref, h2_ref, xs_hbm, sbuf, zbuf, sem, zsem):
    copy_tables = (npair_ref, nsingle_ref, lpair_ref, gpair_ref, lsingle_ref, gsingle_ref)
    t = pl.program_id(0)
    last = pl.num_programs(0) - 1
    cur = t % 2

    def zero_chunk_copy(dst_chunk):
        return _chunk_copy(zbuf, xs_hbm, 0, dst_chunk, zsem)

    def zero_block_copy(b):
        return pltpu.make_async_copy(
            zbuf, xs_hbm.at[pl.ds(pl.multiple_of(b * MOE_T, MOE_T), MOE_T)], zsem)

    def for_each_zero(chunk_fn, block_fn):
        def per_expert(e, carry):
            def per_chunk(j, carry2):
                chunk_fn(zero_start_ref[e] + j)
                return carry2

            lax.fori_loop(0, zero_count_ref[e], per_chunk, 0)
            return carry

        lax.fori_loop(0, N_EXPERTS, per_expert, 0)

        def per_block(b, carry):
            block_fn(b)
            return carry

        lax.fori_loop(nvb_ref[0], MOE_BLOCKS, per_block, 0)

    @pl.when(t == 0)
    def _():
        zbuf[...] = jnp.zeros(zbuf.shape, BF16)
        for_each_zero(lambda dst_chunk: zero_chunk_copy(dst_chunk).start(), lambda b: zero_block_copy(b).start())

    row = lax.broadcasted_iota(jnp.int32, (TILE_ROWS, TM), 0)
    slots = slot_ref[...]
    perm = jnp.where((row == slots[0:1, :]) | (row == slots[1:2, :]), 1.0, 0.0).astype(BF16)
    sbuf[cur] = _dot(perm, h2_ref[...]).astype(BF16)
    _for_each_copy(t, copy_tables,
                   lambda lc, gc, n: _chunk_copy(sbuf.at[cur], xs_hbm, lc, gc, sem.at[cur], n).start())

    def wait_tile(tile, slot):
        _wait_chunks(_tile_chunks(tile, copy_tables), sbuf.at[slot], xs_hbm, sem.at[slot])

    @pl.when(t > 0)
    def _():
        wait_tile(t - 1, 1 - cur)

    @pl.when(t == last)
    def _():
        wait_tile(t, cur)
        for_each_zero(lambda dst_chunk: zero_chunk_copy(dst_chunk).wait(), lambda b: zero_block_copy(b).wait())


def _dispatch(tables, slot_rows, h2):
    return pl.pallas_call(
        _dispatch_kernel,
        grid_spec=pltpu.PrefetchScalarGridSpec(
            num_scalar_prefetch=len(tables),
            grid=(N_TOK // TM,),
            in_specs=[pl.BlockSpec((SUBLANES, TM), lambda i, *_: (i, 0)),
                      pl.BlockSpec((TM, D_MODEL), lambda i, *_: (i, 0))],
            out_specs=pl.BlockSpec(memory_space=pl.ANY),
            scratch_shapes=[pltpu.VMEM((2, TILE_ROWS, D_MODEL), BF16), pltpu.VMEM((MOE_T, D_MODEL), BF16),
                            pltpu.SemaphoreType.DMA((2,)), pltpu.SemaphoreType.DMA(())]),
        out_shape=jax.ShapeDtypeStruct((MOE_ROWS, D_MODEL), BF16),
        compiler_params=pltpu.CompilerParams(dimension_semantics=("arbitrary",)),
        name="dispatch",
    )(*tables, slot_rows, h2)


def _expert_kernel(blk_e_ref, nvb_ref, xs_ref, *refs):
    w_refs, ys_ref, caches = refs[:3 * STEP_BLOCKS], refs[3 * STEP_BLOCKS], refs[3 * STEP_BLOCKS + 1:]
    i = pl.program_id(0)
    for h in range(STEP_BLOCKS):
        wg_ref, wu_ref, wd_ref = w_refs[3 * h:3 * h + 3]
        wgb, wub, wdb = caches[3 * h:3 * h + 3]
        rows = slice(h * MOE_T, (h + 1) * MOE_T)
        blk = i * STEP_BLOCKS + h
        live = blk < nvb_ref[0]
        same_slot_before = jnp.maximum(blk - STEP_BLOCKS, 0)

        @pl.when(live & ((i == 0) | (blk_e_ref[blk] != blk_e_ref[same_slot_before])))
        def _(wg_ref=wg_ref, wu_ref=wu_ref, wd_ref=wd_ref, wgb=wgb, wub=wub, wdb=wdb):
            wgb[...] = wg_ref[...].astype(BF16)
            wub[...] = wu_ref[...].astype(BF16)
            wdb[...] = wd_ref[...].astype(BF16)

        @pl.when(live)
        def _(rows=rows, wgb=wgb, wub=wub, wdb=wdb):
            xb = xs_ref[rows, :]
            g = _dot(xb, wgb[...])
            u = _dot(xb, wub[...])
            hid = (jax.nn.silu(g) * u).astype(BF16)
            ys_ref[rows, :] = _dot(hid, wdb[...]).astype(BF16)

        @pl.when(jnp.logical_not(live))
        def _(rows=rows):
            ys_ref[rows, :] = jnp.zeros((MOE_T, D_MODEL), ys_ref.dtype)


def _experts(blk_e, nvb, xs, wg, wu, wd):
    def row_block(i, be, nv):
        return (jnp.minimum(i, (nv[0] - 1) // STEP_BLOCKS), 0)

    def weight(h):
        return lambda i, be, nv: (be[jnp.minimum(i * STEP_BLOCKS + h, nv[0] - 1)], 0, 0)

    weight_specs = []
    for h in range(STEP_BLOCKS):
        weight_specs += [pl.BlockSpec((None, D_MODEL, D_EXPERT), weight(h)),
                         pl.BlockSpec((None, D_MODEL, D_EXPERT), weight(h)),
                         pl.BlockSpec((None, D_EXPERT, D_MODEL), weight(h))]
    caches = [pltpu.VMEM((D_MODEL, D_EXPERT), BF16), pltpu.VMEM((D_MODEL, D_EXPERT), BF16),
              pltpu.VMEM((D_EXPERT, D_MODEL), BF16)] * STEP_BLOCKS
    return pl.pallas_call(
        _expert_kernel,
        grid_spec=pltpu.PrefetchScalarGridSpec(
            num_scalar_prefetch=2,
            grid=(MOE_BLOCKS // STEP_BLOCKS,),
            in_specs=[pl.BlockSpec((STEP_BLOCKS * MOE_T, D_MODEL), row_block)] + weight_specs,
            out_specs=pl.BlockSpec((STEP_BLOCKS * MOE_T, D_MODEL), lambda i, be, nv: (i, 0)),
            scratch_shapes=caches),
        out_shape=jax.ShapeDtypeStruct((MOE_ROWS, D_MODEL), BF16),
        compiler_params=pltpu.CompilerParams(dimension_semantics=("arbitrary",),
                                             vmem_limit_bytes=48 * 1024 * 1024),
        name="experts",
    )(blk_e, nvb, xs, *([wg, wu, wd] * STEP_BLOCKS))


def _combine_kernel(npair_ref, nsingle_ref, lpair_ref, gpair_ref, lsingle_ref, gsingle_ref,
                    slot_ref, x1_ref, mf_ref, gfin_ref, ys_hbm, o_ref, ybuf, sem):
    copy_tables = (npair_ref, nsingle_ref, lpair_ref, gpair_ref, lsingle_ref, gsingle_ref)
    t = pl.program_id(0)
    cur = t % 2

    def fetch(tile, slot):
        _for_each_copy(tile, copy_tables,
                       lambda lc, gc, n: _chunk_copy(ys_hbm, ybuf.at[slot], gc, lc, sem.at[slot], n).start())

    @pl.when(t == 0)
    def _():
        ybuf[...] = jnp.zeros(ybuf.shape, BF16)
        fetch(0, 0)

    @pl.when(t + 1 < pl.num_programs(0))
    def _():
        fetch(t + 1, 1 - cur)

    used_chunks = _tile_chunks(t, copy_tables)
    _wait_chunks(used_chunks, ys_hbm, ybuf.at[cur], sem.at[cur])

    def finish(rows):
        yb = ybuf[cur, :rows, :]
        col = lax.broadcasted_iota(jnp.int32, (TM, rows), 1)
        slots = slot_ref[...]
        y1 = _dot(jnp.where(col == slots[:, 0:1], 1.0, 0.0).astype(BF16), yb)
        y2 = _dot(jnp.where(col == slots[:, 1:2], 1.0, 0.0).astype(BF16), yb)
        mf = mf_ref[...]
        x2 = x1_ref[...] + mf[:, 0:1] * y1 + mf[:, 1:2] * y2
        o_ref[...] = _rms(x2, gfin_ref[...])

    short = used_chunks * CHUNK_ROWS <= SHORT_TILE_ROWS

    @pl.when(short)
    def _():
        finish(SHORT_TILE_ROWS)

    @pl.when(jnp.logical_not(short))
    def _():
        finish(TILE_ROWS)


def _combine(tables, slot_cols, x1, mf, gfin, ys):
    return pl.pallas_call(
        _combine_kernel,
        grid_spec=pltpu.PrefetchScalarGridSpec(
            num_scalar_prefetch=len(tables),
            grid=(N_TOK // TM,),
            in_specs=[pl.BlockSpec((TM, LANES), lambda i, *_: (i, 0)),
                      pl.BlockSpec((TM, D_MODEL), lambda i, *_: (i, 0)),
                      pl.BlockSpec((TM, LANES), lambda i, *_: (i, 0)),
                      pl.BlockSpec((1, D_MODEL), lambda i, *_: (0, 0)),
                      pl.BlockSpec(memory_space=pl.ANY)],
            out_specs=pl.BlockSpec((TM, D_MODEL), lambda i, *_: (i, 0)),
            scratch_shapes=[pltpu.VMEM((2, TILE_ROWS, D_MODEL), BF16), pltpu.SemaphoreType.DMA((2,))]),
        out_shape=jax.ShapeDtypeStruct((N_TOK, D_MODEL), F32),
        compiler_params=pltpu.CompilerParams(dimension_semantics=("arbitrary",)),
        name="combine",
    )(*tables, slot_cols, x1, mf, gfin, ys)


def kernel(x, positions, g_attn_norm, w_in, g_cq, w_uq, g_ckv, w_uk, w_uv, w_o_attn, sgu_gain, sgu_bias, w_spatial, b_spatial, w_o_sgu, w_out, g_ffn_norm, w_router_group, b_router_group, w_router_expert, b_router_expert, w_exp_gate, w_exp_up, w_exp_down, g_final):
    assert x.shape == (BATCH, SEQ, D_MODEL) and w_in.shape[0] == 1
    half = QK_ROPE // 2
    swap = jnp.concatenate([jnp.arange(half, QK_ROPE), jnp.arange(0, half)])

    def pad_cols(w, width):
        return jnp.pad(w, ((0, 0), (0, width - w.shape[1])))

    wi = w_in[0]
    c0 = Q_RANK + KV_RANK
    kr = wi[:, c0:c0 + QK_ROPE]
    c1 = c0 + QK_ROPE
    win = jnp.concatenate([
        wi[:, :c0], jnp.tile(kr, (1, LANES // QK_ROPE)), jnp.tile(kr[:, swap], (1, LANES // QK_ROPE)),
        wi[:, c1:]], axis=1).astype(BF16)

    wq = w_uq[0].reshape(Q_RANK, MLA_HEADS, QK_NOPE + QK_ROPE)
    wq_nope = wq[:, :, :QK_NOPE].transpose(1, 0, 2)
    wq_rope = wq[:, :, QK_NOPE:]
    w_ukt = w_uk[0].reshape(KV_RANK, MLA_HEADS, QK_NOPE).transpose(1, 2, 0)
    wqlat = _fold(wq_nope, w_ukt).transpose(1, 0, 2).reshape(Q_RANK, MLA_HEADS * LANES).astype(BF16)

    wqr = wq_rope.reshape(Q_RANK, MLA_HEADS * QK_ROPE).astype(BF16)
    wqrs = wq_rope[:, :, swap].reshape(Q_RANK, MLA_HEADS * QK_ROPE).astype(BF16)

    per_row = LANES // half
    freqs = ROPE_THETA ** (-jnp.arange(0, QK_ROPE, 2, dtype=F32) / QK_ROPE)
    pos8 = jnp.repeat(positions.astype(F32).reshape(N_TOK // per_row, per_row), half, axis=1)
    cos8, sin8 = _rope_tables(pos8, jnp.tile(freqs, per_row)[None, :])
    cos16 = cos8.reshape(N_TOK, half)
    sin16 = sin8.reshape(N_TOK, half)
    cos_t = jnp.tile(jnp.concatenate([cos16, cos16], axis=1), (1, LANES // QK_ROPE))
    sin_t = jnp.tile(jnp.concatenate([-sin16, sin16], axis=1), (1, LANES // QK_ROPE))

    head_of_col = jnp.arange(MLA_WIDTH) // V_HEAD
    wuv = jnp.where(head_of_col[None, None, :] == jnp.arange(MLA_HEADS)[:, None, None],
                    w_uv[0][None], 0.0).astype(BF16)
    wuv = wuv.reshape(MLA_HEADS * KV_RANK, MLA_WIDTH)

    wsp = w_spatial[0].astype(BF16)
    bs = b_spatial[0]
    bsp = jnp.repeat(bs.reshape(SGU_GROUPS // 2, 2, SGU_LEN).transpose(0, 2, 1), SGU_GROUP_DIM, axis=2)

    wr32 = jnp.concatenate([w_router_expert[0].transpose(1, 0, 2).reshape(D_MODEL, N_EXPERTS),
                            w_router_group[0]], axis=1)
    wr32 = pad_cols(wr32, LANES)
    wr_hi = wr32.astype(BF16)
    wr_lo = (wr32 - wr_hi.astype(F32)).astype(BF16)
    wr = jnp.concatenate([wr_hi, wr_lo], axis=1)
    br = pad_cols(jnp.concatenate([b_router_expert[0].reshape(-1), b_router_group[0]])[None, :], LANES)

    xf = x.reshape(N_TOK, D_MODEL)
    q, kt, v, p, ga = _inproj(
        xf, cos_t, sin_t, g_attn_norm, win, g_cq, g_ckv, wqlat, wqr, wqrs,
        sgu_gain, sgu_bias, wsp, bsp, w_o_sgu[0].astype(BF16))
    attn = _attention(q, kt, v, wuv)
    x1, h2, mf, slot_cols, slot_rows, run_chunks_rows, run_start_rows = _mix(
        attn.reshape(N_TOK, MLA_WIDTH), ga, p, xf, w_o_attn[0].astype(BF16), w_out[0].astype(BF16),
        g_ffn_norm, wr, br)

    blk_chunks = MOE_T // CHUNK_ROWS
    run_chunks = run_chunks_rows[::SUBLANES, :N_EXPERTS]
    run_start = run_start_rows[::SUBLANES, :N_EXPERTS]
    seg_chunks = jnp.sum(run_chunks, axis=0)
    seg_padded = (seg_chunks + blk_chunks - 1) // blk_chunks * blk_chunks
    seg_end = jnp.cumsum(seg_padded)
    seg_start = seg_end - seg_padded
    run_dest = seg_start[None, :] + jnp.cumsum(run_chunks, axis=0) - run_chunks
    nvb = (seg_end[-1:] // blk_chunks).astype(jnp.int32)
    blk_first_chunk = jnp.arange(MOE_BLOCKS, dtype=jnp.int32) * blk_chunks
    blk_e = jnp.minimum(jnp.sum((seg_end[None, :] <= blk_first_chunk[:, None]).astype(jnp.int32), axis=1),
                        N_EXPERTS - 1)
    def flat_list(count, slots, first_local, first_global, step):
        off = jnp.cumsum(count, axis=1) - count
        k = jnp.arange(slots, dtype=jnp.int32)
        in_run = (off[:, None, :] <= k[None, :, None]) & (k[None, :, None] < (off + count)[:, None, :])

        def pick(first):
            return (jnp.sum(jnp.where(in_run, (first - step * off)[:, None, :], 0), axis=-1)
                    + step * k[None, :]).reshape(-1).astype(jnp.int32)

        return jnp.sum(count, axis=1).astype(jnp.int32), pick(first_local), pick(first_global)

    npair, lpair, gpair = flat_list(run_chunks // 2, PAIR_SLOTS, run_start, run_dest, 2)
    nsingle, lsingle, gsingle = flat_list(run_chunks % 2, N_EXPERTS, run_start + run_chunks - 1, run_dest + run_chunks - 1, 0)
    run_tables = (npair, nsingle, lpair, gpair, lsingle, gsingle)

    xs = _dispatch(run_tables + ((seg_start + seg_chunks).astype(jnp.int32), (seg_padded - seg_chunks).astype(jnp.int32), nvb),
                   slot_rows, h2)
    ys = _experts(blk_e, nvb, xs, w_exp_gate[0], w_exp_up[0], w_exp_down[0])
    out = _combine(run_tables, slot_cols, x1, mf, g_final.reshape(1, D_MODEL), ys)
    return out.reshape(BATCH, SEQ, D_MODEL)
```

```python
import jax
import jax.numpy as jnp
from jax import lax
from jax.experimental import pallas as pl
from jax.experimental.pallas import tpu as pltpu

D_MODEL = 1024
BATCH = 2
SEQ = 8192
N_TOK = BATCH * SEQ
CHUNK = 64
EPS = 1e-6
MLA_HEADS = 8
Q_RANK = 256
KV_RANK = 128
QK_NOPE = 64
QK_ROPE = 32
V_HEAD = 64
MLA_WIDTH = MLA_HEADS * V_HEAD
ROPE_THETA = 10000.0
SGU_GROUPS = 8
SGU_GROUP_DIM = 64
SGU_WIDTH = SGU_GROUPS * SGU_GROUP_DIM
SGU_LEN = 128
N_GROUPS = 4
EXPERTS_PER_GROUP = 8
N_EXPERTS = N_GROUPS * EXPERTS_PER_GROUP
TOP_K = 2
D_EXPERT = 256

LANES = 128
SUBLANES = 8
QK_PAD = 2 * LANES
V_PAD = 2 * LANES
SCALE = (QK_NOPE + QK_ROPE) ** -0.5
LOG2E = 1.4426950408889634
Q_SCALE = SCALE * LOG2E
NEG = -1e30

TM = 256
TMI = 512
TMX = 1024
TQ = 256
TK = 512
ATTN_UNROLL_SHIFT = 1
ATTN_UNROLL = 1 << ATTN_UNROLL_SHIFT
TK_SHIFT = TK.bit_length() - 1
CHUNK_SHIFT = CHUNK.bit_length() - 1
assert 1 << TK_SHIFT == TK and 1 << CHUNK_SHIFT == CHUNK
MOE_T = 512
CHUNK_ROWS = 16
TILE_ROWS = -(-(TOP_K * TM + N_EXPERTS * (CHUNK_ROWS - 1) + CHUNK_ROWS) // 256) * 256
SHORT_TILE_ROWS = TILE_ROWS - 256
TILE_CHUNKS = TILE_ROWS // CHUNK_ROWS
PAIR_SLOTS = TILE_CHUNKS // 2
CHUNK_UNROLL_SHIFT = 2
CHUNK_UNROLL = 1 << CHUNK_UNROLL_SHIFT
MOE_ROWS_MAX = (N_TOK * TOP_K + (N_TOK // TM) * N_EXPERTS * (CHUNK_ROWS - 1)
                + N_EXPERTS * (MOE_T - CHUNK_ROWS))
STEP_BLOCKS = 2
MOE_BLOCKS = -(-MOE_ROWS_MAX // (MOE_T * STEP_BLOCKS)) * STEP_BLOCKS
MOE_ROWS = MOE_BLOCKS * MOE_T

C_Q = 0
C_KV = C_Q + Q_RANK
C_KR = C_KV + KV_RANK
C_KRS = C_KR + LANES
C_U = C_KRS + LANES
C_V = C_U + SGU_WIDTH
C_GA = C_V + SGU_WIDTH
C_GB = C_GA + D_MODEL
C_END = C_GB + D_MODEL

F32 = jnp.float32
BF16 = jnp.bfloat16


def _dot(a, b):
    return jnp.dot(a, b, preferred_element_type=F32)


def _rms(x, g):
    return x * lax.rsqrt(jnp.mean(x * x, axis=-1, keepdims=True) + EPS) * g


def _fold_kernel(a_ref, b_ref, o_ref):
    o_ref[...] = Q_SCALE * jnp.dot(a_ref[...], b_ref[...], preferred_element_type=F32,
                                 precision=lax.Precision.HIGHEST)


def _fold(w_uq_nope, w_ukt):
    return pl.pallas_call(
        _fold_kernel,
        grid=(MLA_HEADS,),
        in_specs=[pl.BlockSpec((None, Q_RANK, QK_NOPE), lambda h: (h, 0, 0)),
                  pl.BlockSpec((None, QK_NOPE, KV_RANK), lambda h: (h, 0, 0))],
        out_specs=pl.BlockSpec((None, Q_RANK, KV_RANK), lambda h: (h, 0, 0)),
        out_shape=jax.ShapeDtypeStruct((MLA_HEADS, Q_RANK, KV_RANK), F32),
        name="fold",
    )(w_uq_nope, w_ukt)


def _rope_kernel(pos_ref, freq_ref, cos_ref, sin_ref):
    ang = pos_ref[...] * freq_ref[...]
    cos_ref[...] = jnp.cos(ang)
    sin_ref[...] = jnp.sin(ang)


def _rope_tables(pos8, freq8):
    rows = pos8.shape[0]
    blk = pl.BlockSpec((TM, LANES), lambda i: (i, 0))
    return pl.pallas_call(
        _rope_kernel,
        grid=(rows // TM,),
        in_specs=[blk, pl.BlockSpec((1, LANES), lambda i: (0, 0))],
        out_specs=[blk, blk],
        out_shape=[jax.ShapeDtypeStruct((rows, LANES), F32)] * 2,
        name="rope_tables",
    )(pos8, freq8)


def _inproj_kernel(x_ref, cos_ref, sin_ref, gattn_ref, win_ref, gcq_ref, gckv_ref, wqlat_ref, wqr_ref,
                   wqrs_ref, sgain_ref, sbias_ref, wsp_ref, bsp_ref, wosgu_ref,
                   q_ref, kt_ref, v_ref, p_ref, ga_ref):
    hb = _rms(x_ref[...], gattn_ref[...]).astype(BF16)

    def proj(a, b):
        return _dot(hb, win_ref[:, a:b])

    cos_t = cos_ref[...]
    sin_t = sin_ref[...]

    cqn = _rms(proj(C_Q, C_KV), gcq_ref[...]).astype(BF16)
    qlat = _dot(cqn, wqlat_ref[...])
    groups = MLA_HEADS * QK_ROPE // LANES
    rot = (_dot(cqn, wqr_ref[...]) * jnp.concatenate([cos_t * Q_SCALE] * groups, axis=1)
           + _dot(cqn, wqrs_ref[...]) * jnp.concatenate([sin_t * Q_SCALE] * groups, axis=1))
    lane = lax.broadcasted_iota(jnp.int32, (TQ, LANES), 1)
    for r in range(TMI // TQ):
        t0, t1 = r * TQ, (r + 1) * TQ
        for h in range(MLA_HEADS):
            a, b = h * LANES, (h + 1) * LANES
            q_ref[r, h * TQ:(h + 1) * TQ, :LANES] = qlat[t0:t1, a:b].astype(BF16)
            g0 = (h * QK_ROPE) // LANES * LANES
            off = (h * QK_ROPE) % LANES
            window = (lane >= off) & (lane < off + QK_ROPE)
            q_ref[r, h * TQ:(h + 1) * TQ, LANES:] = jnp.where(
                window, rot[t0:t1, g0:g0 + LANES], 0.0).astype(BF16)

    zk = proj(C_KV, C_U)
    ckvn = _rms(zk[:, :KV_RANK], gckv_ref[...])
    krope = zk[:, KV_RANK:KV_RANK + LANES] * cos_t + zk[:, KV_RANK + LANES:] * sin_t
    kt_ref[...] = jnp.concatenate([ckvn, krope], axis=1).T.astype(BF16)
    v_ref[...] = jnp.concatenate([ckvn, jnp.ones_like(ckvn)], axis=1).astype(BF16)

    u = jax.nn.gelu(proj(C_U, C_V))
    v = jax.nn.gelu(proj(C_V, C_GA))
    mu = jnp.mean(v, axis=-1, keepdims=True)
    vc = v - mu
    var = jnp.mean(vc * vc, axis=-1, keepdims=True)
    vb = (vc * lax.rsqrt(var + EPS) * sgain_ref[...] + sbias_ref[...]).astype(BF16)
    row = lax.broadcasted_iota(jnp.int32, (SGU_LEN, SGU_LEN), 0)
    col = lax.broadcasted_iota(jnp.int32, (SGU_LEN, SGU_LEN), 1)
    causal = (row >> CHUNK_SHIFT) >= (col >> CHUNK_SHIFT)
    low_half = col < SGU_GROUP_DIM
    zero_w = jnp.zeros((SGU_LEN, SGU_LEN), BF16)
    w_pairs = [jnp.concatenate([jnp.where(causal, wsp_ref[2 * p], zero_w),
                                jnp.where(causal, wsp_ref[2 * p + 1], zero_w)], axis=1)
               for p in range(SGU_WIDTH // LANES)]
    row_blocks = []
    for r in range(TMI // SGU_LEN):
        pieces = []
        for p in range(SGU_WIDTH // LANES):
            blk = vb[r * SGU_LEN:(r + 1) * SGU_LEN, p * LANES:(p + 1) * LANES]
            stacked = jnp.concatenate([jnp.where(low_half, blk, zero_w), jnp.where(low_half, zero_w, blk)], axis=0)
            sv = _dot(w_pairs[p], stacked) + bsp_ref[p]
            pieces.append((u[r * SGU_LEN:(r + 1) * SGU_LEN, p * LANES:(p + 1) * LANES] * sv).astype(BF16))
        row_blocks.append(jnp.concatenate(pieces, axis=1))
    sgu = _dot(jnp.concatenate(row_blocks, axis=0), wosgu_ref[...])

    ga_ref[...] = jax.nn.sigmoid(proj(C_GA, C_GB)).astype(BF16)
    p_ref[...] = (jax.nn.sigmoid(proj(C_GB, C_END)) * sgu).astype(BF16)


def _inproj(x, cos_t, sin_t, gattn, win, gcq, gckv, wqlat, wqr, wqrs, sgain, sbias, wsp, bsp, wosgu):
    nt = N_TOK // TMI
    per_b = SEQ // TMI
    per_k = TK // TMI

    def const(shape):
        return pl.BlockSpec(shape, lambda i: (0,) * len(shape))

    return pl.pallas_call(
        _inproj_kernel,
        grid=(nt,),
        in_specs=[pl.BlockSpec((TMI, D_MODEL), lambda i: (i, 0)),
                  pl.BlockSpec((TMI, LANES), lambda i: (i, 0)), pl.BlockSpec((TMI, LANES), lambda i: (i, 0)),
                  const((1, D_MODEL)), const((D_MODEL, C_END)), const((1, Q_RANK)), const((1, KV_RANK)),
                  const((Q_RANK, MLA_HEADS * LANES)), const((Q_RANK, MLA_HEADS * QK_ROPE)),
                  const((Q_RANK, MLA_HEADS * QK_ROPE)),
                  const((1, SGU_WIDTH)), const((1, SGU_WIDTH)),
                  const((SGU_GROUPS, SGU_LEN, SGU_LEN)), const((SGU_GROUPS // 2, SGU_LEN, LANES)),
                  const((SGU_WIDTH, D_MODEL))],
        out_specs=[pl.BlockSpec((TMI // TQ, MLA_HEADS * TQ, QK_PAD), lambda i: (i, 0, 0)),
                   pl.BlockSpec((None, None, QK_PAD, TMI),
                                lambda i: (i // per_b, (i % per_b) // per_k, 0, i % per_k)),
                   pl.BlockSpec((None, None, TMI, V_PAD),
                                lambda i: (i // per_b, (i % per_b) // per_k, i % per_k, 0)),
                   pl.BlockSpec((TMI, D_MODEL), lambda i: (i, 0)),
                   pl.BlockSpec((TMI, D_MODEL), lambda i: (i, 0))],
        out_shape=[jax.ShapeDtypeStruct((N_TOK // TQ, MLA_HEADS * TQ, QK_PAD), BF16),
                   jax.ShapeDtypeStruct((BATCH, SEQ // TK, QK_PAD, TK), BF16),
                   jax.ShapeDtypeStruct((BATCH, SEQ // TK, TK, V_PAD), BF16),
                   jax.ShapeDtypeStruct((N_TOK, D_MODEL), BF16),
                   jax.ShapeDtypeStruct((N_TOK, D_MODEL), BF16)],
        compiler_params=pltpu.CompilerParams(dimension_semantics=("arbitrary",),
                                             vmem_limit_bytes=56 * 1024 * 1024),
        name="inproj",
    )(x, cos_t, sin_t, gattn, win, gcq, gckv, wqlat, wqr, wqrs, sgain, sbias, wsp, bsp, wosgu)


def _attn_kernel(q_ref, kt_ref, v_ref, wuv_ref, o_ref, m_ref, acc_ref, s_ref, p_ref, a_ref):
    qi = pl.program_id(1)
    m_ref[...] = jnp.full(m_ref.shape, NEG, F32)
    acc_ref[...] = jnp.zeros(acc_ref.shape, F32)

    def scores(j, slot):
        s_ref[slot] = _dot(q_ref[...], kt_ref[j])

    def update(j, slot, width=None):
        masked = width is not None
        w = width if masked else TK
        if masked:
            q_chunk = ((qi * TQ - j * TK) >> CHUNK_SHIFT) + (
                lax.broadcasted_iota(jnp.int32, (TQ, w), 0) >> CHUNK_SHIFT)
            k_chunk = lax.broadcasted_iota(jnp.int32, (TQ, w), 1) >> CHUNK_SHIFT
            visible = k_chunk <= q_chunk
        for h in range(MLA_HEADS):
            rows = slice(h * TQ, (h + 1) * TQ)
            s = s_ref[slot, rows, :w]
            if masked:
                s = jnp.where(visible, s, NEG)
            m_prev = m_ref[rows, :]
            m_new = jnp.maximum(m_prev, jnp.max(s, axis=-1, keepdims=True))
            m_ref[rows, :] = m_new
            a_ref[rows, :] = jnp.exp2(m_prev - m_new)
            s = s_ref[slot, rows, :w]
            if masked:
                s = jnp.where(visible, s, NEG)
            p_ref[rows, :w] = jnp.exp2(s - jnp.concatenate([m_new] * (w // LANES), axis=1)).astype(BF16)
        alpha = a_ref[...]
        acc_ref[...] = jnp.concatenate([alpha, alpha], axis=1) * acc_ref[...] + _dot(
            p_ref[:, :w], v_ref[j, :w, :])

    diag = (qi * TQ) >> TK_SHIFT
    scores(0, 0)

    def run(first, count, last_masked):
        for u in range(count):
            if not (last_masked and u == count - 1):
                scores(first + u + 1, (u + 1) % 2)
                update(first + u, u % 2)
            else:
                for sub in range(TK // TQ):
                    @pl.when((qi & (TK // TQ - 1)) == sub)
                    def _(u=u, sub=sub):
                        update(first + u, u % 2, (sub + 1) * TQ)

    def body(t, carry):
        run(ATTN_UNROLL * t, ATTN_UNROLL, False)
        return carry

    trips = diag >> ATTN_UNROLL_SHIFT
    lax.fori_loop(0, trips, body, 0)
    done = trips << ATTN_UNROLL_SHIFT
    for r in range(ATTN_UNROLL):
        @pl.when(diag - done == r)
        def _(r=r):
            run(done, r + 1, True)

    o_lat = (acc_ref[:, :KV_RANK] / acc_ref[:, KV_RANK:]).astype(BF16)
    o_cat = jnp.concatenate([o_lat[h * TQ:(h + 1) * TQ] for h in range(MLA_HEADS)], axis=1)
    o_ref[...] = _dot(o_cat, wuv_ref[...]).astype(BF16)


def _attention(q, kt, v, wuv):
    nk = SEQ // TK
    return pl.pallas_call(
        _attn_kernel,
        grid=(BATCH, SEQ // TQ),
        in_specs=[pl.BlockSpec((None, MLA_HEADS * TQ, QK_PAD), lambda b, i: (b * (SEQ // TQ) + i, 0, 0)),
                  pl.BlockSpec((None, nk, QK_PAD, TK), lambda b, i: (b, 0, 0, 0)),
                  pl.BlockSpec((None, nk, TK, V_PAD), lambda b, i: (b, 0, 0, 0)),
                  pl.BlockSpec((MLA_HEADS * KV_RANK, MLA_WIDTH), lambda b, i: (0, 0))],
        out_specs=pl.BlockSpec((None, TQ, MLA_WIDTH), lambda b, i: (b, i, 0)),
        out_shape=jax.ShapeDtypeStruct((BATCH, SEQ, MLA_WIDTH), BF16),
        scratch_shapes=[pltpu.VMEM((MLA_HEADS * TQ, LANES), F32),
                        pltpu.VMEM((MLA_HEADS * TQ, V_PAD), F32),
                        pltpu.VMEM((2, MLA_HEADS * TQ, TK), F32),
                        pltpu.VMEM((MLA_HEADS * TQ, TK), BF16),
                        pltpu.VMEM((MLA_HEADS * TQ, LANES), F32)],
        compiler_params=pltpu.CompilerParams(dimension_semantics=("arbitrary", "arbitrary"),
                                             vmem_limit_bytes=56 * 1024 * 1024),
        name="attention",
    )(q, kt, v, wuv)


def _mix_kernel(attn_ref, ga_ref, p_ref, x_ref, woa_ref, wout_ref, gffn_ref, wr_ref, br_ref,
                x1_ref, h2_ref, mf_ref, slot_col_ref, slot_row_ref, run_chunks_ref, run_start_ref):
    a = _dot(attn_ref[...], woa_ref[...])
    mix = (ga_ref[...].astype(F32) * a + p_ref[...].astype(F32)).astype(BF16)
    x1 = x_ref[...] + _dot(mix, wout_ref[...])
    x1_ref[...] = x1
    h2 = _rms(x1, gffn_ref[...])
    h2_ref[...] = h2.astype(BF16)

    hi = h2.astype(BF16)
    lo = (h2 - hi.astype(F32)).astype(BF16)
    r1 = _dot(hi, wr_ref[...])
    r2 = _dot(lo, wr_ref[:, :LANES])
    logits_all = r1[:, :LANES] + r1[:, LANES:] + r2 + br_ref[...]

    for r in range(TMX // TM):
        mf, slots, run_chunks_rows, run_start_rows = _route_tile(logits_all[r * TM:(r + 1) * TM])
        mf_ref[r * TM:(r + 1) * TM, :] = mf
        slot_col_ref[r * TM:(r + 1) * TM, :] = slots.astype(jnp.int32)
        tile_rows = slice(SUBLANES * r, SUBLANES * (r + 1))
        slot_row_ref[tile_rows, :] = slots.T[:SUBLANES].astype(jnp.int32)
        run_chunks_ref[tile_rows, :] = run_chunks_rows.astype(jnp.int32)
        run_start_ref[tile_rows, :] = run_start_rows.astype(jnp.int32)


def _route_tile(logits):
    lane_i = lax.broadcasted_iota(jnp.int32, (TM, LANES), 1)
    lane = lane_i.astype(F32)
    lane_group = (lane_i >> 3).astype(F32)
    ninf = -jnp.inf
    is_group = (lane_i >= N_EXPERTS) & (lane_i < N_EXPERTS + N_GROUPS)
    lg = jnp.where(is_group, logits, ninf)
    gmax = jnp.max(lg, axis=-1, keepdims=True)
    gsum = jnp.sum(jnp.exp(lg - gmax), axis=-1, keepdims=True)
    p_top = 1.0 / gsum
    g_idx = jnp.min(jnp.where(lg == gmax, lane - N_EXPERTS, float(N_GROUPS)), axis=-1, keepdims=True)
    le = jnp.where((lane_i < N_EXPERTS) & (lane_group == g_idx), logits, ninf)
    t1 = jnp.max(le, axis=-1, keepdims=True)
    e1 = jnp.min(jnp.where(le == t1, lane, float(LANES)), axis=-1, keepdims=True)
    le2 = jnp.where(lane == e1, ninf, le)
    t2 = jnp.max(le2, axis=-1, keepdims=True)
    e2 = jnp.min(jnp.where(le2 == t2, lane, float(LANES)), axis=-1, keepdims=True)
    ex = jnp.exp(t2 - t1)
    w1 = p_top / (1.0 + ex)
    w2 = p_top * ex / (1.0 + ex)

    sel1 = lane == e1
    sel2 = lane == e2
    onehot = jnp.where(sel1 | sel2, 1.0, 0.0)
    rr = lax.broadcasted_iota(jnp.int32, (TM, TM), 0)
    cc = lax.broadcasted_iota(jnp.int32, (TM, TM), 1)
    ltri = jnp.where(cc < rr, 1.0, 0.0).astype(BF16)
    rank = _dot(ltri, onehot.astype(BF16))
    cnt = jnp.sum(onehot, axis=0, keepdims=True)
    run_chunks = jnp.floor((cnt + (CHUNK_ROWS - 1)) * (1.0 / CHUNK_ROWS))
    ur = lax.broadcasted_iota(jnp.int32, (LANES, LANES), 0)
    uc = lax.broadcasted_iota(jnp.int32, (LANES, LANES), 1)
    upper = jnp.where(ur < uc, 1.0, 0.0).astype(BF16)
    run_chunks_rows = jnp.broadcast_to(run_chunks, (SUBLANES, LANES))
    run_start_rows = _dot(run_chunks_rows.astype(BF16), upper)
    slot_all = CHUNK_ROWS * run_start_rows[0:1] + rank
    slot1 = jnp.sum(jnp.where(sel1, slot_all, 0.0), axis=-1, keepdims=True)
    slot2 = jnp.sum(jnp.where(sel2, slot_all, 0.0), axis=-1, keepdims=True)

    slots = jnp.where(lane_i == 0, slot1, jnp.where(lane_i == 1, slot2, 0.0))
    return jnp.where(lane_i == 0, w1, w2), slots, run_chunks_rows, run_start_rows


def _mix(attn, ga, p, x, woa, wout, gffn, wr, br):
    nt = N_TOK // TM
    sub = TMX // TM

    def const(shape):
        return pl.BlockSpec(shape, lambda i: (0,) * len(shape))

    def rows(width):
        return pl.BlockSpec((TMX, width), lambda i: (i, 0))

    return pl.pallas_call(
        _mix_kernel,
        grid=(N_TOK // TMX,),
        in_specs=[rows(MLA_WIDTH), rows(D_MODEL), rows(D_MODEL), rows(D_MODEL),
                  const((MLA_WIDTH, D_MODEL)), const((D_MODEL, D_MODEL)), const((1, D_MODEL)),
                  const((D_MODEL, 2 * LANES)), const((1, LANES))],
        out_specs=[rows(D_MODEL), rows(D_MODEL), rows(LANES), rows(LANES),
                   pl.BlockSpec((SUBLANES * sub, TM), lambda i: (i, 0)),
                   pl.BlockSpec((SUBLANES * sub, LANES), lambda i: (i, 0)),
                   pl.BlockSpec((SUBLANES * sub, LANES), lambda i: (i, 0))],
        out_shape=[jax.ShapeDtypeStruct((N_TOK, D_MODEL), F32),
                   jax.ShapeDtypeStruct((N_TOK, D_MODEL), BF16),
                   jax.ShapeDtypeStruct((N_TOK, LANES), F32),
                   jax.ShapeDtypeStruct((N_TOK, LANES), jnp.int32),
                   jax.ShapeDtypeStruct((nt * SUBLANES, TM), jnp.int32),
                   jax.ShapeDtypeStruct((nt * SUBLANES, LANES), jnp.int32),
                   jax.ShapeDtypeStruct((nt * SUBLANES, LANES), jnp.int32)],
        compiler_params=pltpu.CompilerParams(dimension_semantics=("arbitrary",),
                                             vmem_limit_bytes=40 * 1024 * 1024),
        name="mix",
    )(attn, ga, p, x, woa, wout, gffn, wr, br)


def _chunk_copy(src, dst, src_chunk, dst_chunk, sem, nchunks=1):
    rows = nchunks * CHUNK_ROWS
    return pltpu.make_async_copy(
        src.at[pl.ds(pl.multiple_of(src_chunk * CHUNK_ROWS, CHUNK_ROWS), rows)],
        dst.at[pl.ds(pl.multiple_of(dst_chunk * CHUNK_ROWS, CHUNK_ROWS), rows)], sem)


def _unrolled_loop(n, fn):
    groups = n >> CHUNK_UNROLL_SHIFT

    def group(g, carry):
        for u in range(CHUNK_UNROLL):
            fn(g * CHUNK_UNROLL + u)
        return carry

    lax.fori_loop(0, groups, group, 0)

    def single(k, carry):
        fn(k)
        return carry

    lax.fori_loop(groups << CHUNK_UNROLL_SHIFT, n, single, 0)


def _for_each_copy(t, copy_tables, fn):
    npair_ref, nsingle_ref, lpair_ref, gpair_ref, lsingle_ref, gsingle_ref = copy_tables
    pb = t * PAIR_SLOTS
    sb = t * N_EXPERTS
    _unrolled_loop(npair_ref[t], lambda k: fn(lpair_ref[pb + k], gpair_ref[pb + k], 2))
    _unrolled_loop(nsingle_ref[t], lambda k: fn(lsingle_ref[sb + k], gsingle_ref[sb + k], 1))


def _tile_chunks(t, copy_tables):
    return 2 * copy_tables[0][t] + copy_tables[1][t]


def _wait_chunks(n, src, dst, sem):
    groups = n >> CHUNK_UNROLL_SHIFT
    rows = CHUNK_UNROLL * CHUNK_ROWS

    def group(g, carry):
        pltpu.make_async_copy(src.at[pl.ds(0, rows)], dst.at[pl.ds(0, rows)], sem).wait()
        return carry

    lax.fori_loop(0, groups, group, 0)

    def single(k, carry):
        _chunk_copy(src, dst, 0, 0, sem).wait()
        return carry

    lax.fori_loop(groups << CHUNK_UNROLL_SHIFT, n, single, 0)


def _dispatch_kernel(npair_ref, nsingle_ref, lpair_ref, gpair_ref, lsingle_ref, gsingle_ref,
                     zero_start_ref, zero_count_ref, nvb_ref,
                     slot_ref, h2_ref, xs_hbm, sbuf, zbuf, sem, zsem):
    copy_tables = (npair_ref, nsingle_ref, lpair_ref, gpair_ref, lsingle_ref, gsingle_ref)
    t = pl.program_id(0)
    last = pl.num_programs(0) - 1
    cur = t % 2

    def zero_chunk_copy(dst_chunk):
        return _chunk_copy(zbuf, xs_hbm, 0, dst_chunk, zsem)

    def zero_block_copy(b):
        return pltpu.make_async_copy(
            zbuf, xs_hbm.at[pl.ds(pl.multiple_of(b * MOE_T, MOE_T), MOE_T)], zsem)

    def for_each_zero(chunk_fn, block_fn):
        def per_expert(e, carry):
            def per_chunk(j, carry2):
                chunk_fn(zero_start_ref[e] + j)
                return carry2

            lax.fori_loop(0, zero_count_ref[e], per_chunk, 0)
            return carry

        lax.fori_loop(0, N_EXPERTS, per_expert, 0)

        def per_block(b, carry):
            block_fn(b)
            return carry

        lax.fori_loop(nvb_ref[0], MOE_BLOCKS, per_block, 0)

    @pl.when(t == 0)
    def _():
        zbuf[...] = jnp.zeros(zbuf.shape, BF16)
        for_each_zero(lambda dst_chunk: zero_chunk_copy(dst_chunk).start(), lambda b: zero_block_copy(b).start())

    row = lax.broadcasted_iota(jnp.int32, (TILE_ROWS, TM), 0)
    slots = slot_ref[...]
    perm = jnp.where((row == slots[0:1, :]) | (row == slots[1:2, :]), 1.0, 0.0).astype(BF16)
    sbuf[cur] = _dot(perm, h2_ref[...]).astype(BF16)
    _for_each_copy(t, copy_tables,
                   lambda lc, gc, n: _chunk_copy(sbuf.at[cur], xs_hbm, lc, gc, sem.at[cur], n).start())

    def wait_tile(tile, slot):
        _wait_chunks(_tile_chunks(tile, copy_tables), sbuf.at[slot], xs_hbm, sem.at[slot])

    @pl.when(t > 0)
    def _():
        wait_tile(t - 1, 1 - cur)

    @pl.when(t == last)
    def _():
        wait_tile(t, cur)
        for_each_zero(lambda dst_chunk: zero_chunk_copy(dst_chunk).wait(), lambda b: zero_block_copy(b).wait())


def _dispatch(tables, slot_rows, h2):
    return pl.pallas_call(
        _dispatch_kernel,
        grid_spec=pltpu.PrefetchScalarGridSpec(
            num_scalar_prefetch=len(tables),
            grid=(N_TOK // TM,),
            in_specs=[pl.BlockSpec((SUBLANES, TM), lambda i, *_: (i, 0)),
                      pl.BlockSpec((TM, D_MODEL), lambda i, *_: (i, 0))],
            out_specs=pl.BlockSpec(memory_space=pl.ANY),
            scratch_shapes=[pltpu.VMEM((2, TILE_ROWS, D_MODEL), BF16), pltpu.VMEM((MOE_T, D_MODEL), BF16),
                            pltpu.SemaphoreType.DMA((2,)), pltpu.SemaphoreType.DMA(())]),
        out_shape=jax.ShapeDtypeStruct((MOE_ROWS, D_MODEL), BF16),
        compiler_params=pltpu.CompilerParams(dimension_semantics=("arbitrary",)),
        name="dispatch",
    )(*tables, slot_rows, h2)


def _expert_kernel(blk_e_ref, nvb_ref, xs_ref, *refs):
    w_refs, ys_ref, caches = refs[:3 * STEP_BLOCKS], refs[3 * STEP_BLOCKS], refs[3 * STEP_BLOCKS + 1:]
    i = pl.program_id(0)
    for h in range(STEP_BLOCKS):
        wg_ref, wu_ref, wd_ref = w_refs[3 * h:3 * h + 3]
        wgb, wub, wdb = caches[3 * h:3 * h + 3]
        rows = slice(h * MOE_T, (h + 1) * MOE_T)
        blk = i * STEP_BLOCKS + h
        live = blk < nvb_ref[0]
        same_slot_before = jnp.maximum(blk - STEP_BLOCKS, 0)

        @pl.when(live & ((i == 0) | (blk_e_ref[blk] != blk_e_ref[same_slot_before])))
        def _(wg_ref=wg_ref, wu_ref=wu_ref, wd_ref=wd_ref, wgb=wgb, wub=wub, wdb=wdb):
            wgb[...] = wg_ref[...].astype(BF16)
            wub[...] = wu_ref[...].astype(BF16)
            wdb[...] = wd_ref[...].astype(BF16)

        @pl.when(live)
        def _(rows=rows, wgb=wgb, wub=wub, wdb=wdb):
            xb = xs_ref[rows, :]
            g = _dot(xb, wgb[...])
            u = _dot(xb, wub[...])
            hid = (jax.nn.silu(g) * u).astype(BF16)
            ys_ref[rows, :] = _dot(hid, wdb[...]).astype(BF16)

        @pl.when(jnp.logical_not(live))
        def _(rows=rows):
            ys_ref[rows, :] = jnp.zeros((MOE_T, D_MODEL), ys_ref.dtype)


def _experts(blk_e, nvb, xs, wg, wu, wd):
    def row_block(i, be, nv):
        return (jnp.minimum(i, (nv[0] - 1) // STEP_BLOCKS), 0)

    def weight(h):
        return lambda i, be, nv: (be[jnp.minimum(i * STEP_BLOCKS + h, nv[0] - 1)], 0, 0)

    weight_specs = []
    for h in range(STEP_BLOCKS):
        weight_specs += [pl.BlockSpec((None, D_MODEL, D_EXPERT), weight(h)),
                         pl.BlockSpec((None, D_MODEL, D_EXPERT), weight(h)),
                         pl.BlockSpec((None, D_EXPERT, D_MODEL), weight(h))]
    caches = [pltpu.VMEM((D_MODEL, D_EXPERT), BF16), pltpu.VMEM((D_MODEL, D_EXPERT), BF16),
              pltpu.VMEM((D_EXPERT, D_MODEL), BF16)] * STEP_BLOCKS
    return pl.pallas_call(
        _expert_kernel,
        grid_spec=pltpu.PrefetchScalarGridSpec(
            num_scalar_prefetch=2,
            grid=(MOE_BLOCKS // STEP_BLOCKS,),
            in_specs=[pl.BlockSpec((STEP_BLOCKS * MOE_T, D_MODEL), row_block)] + weight_specs,
            out_specs=pl.BlockSpec((STEP_BLOCKS * MOE_T, D_MODEL), lambda i, be, nv: (i, 0)),
            scratch_shapes=caches),
        out_shape=jax.ShapeDtypeStruct((MOE_ROWS, D_MODEL), BF16),
        compiler_params=pltpu.CompilerParams(dimension_semantics=("arbitrary",),
                                             vmem_limit_bytes=48 * 1024 * 1024),
        name="experts",
    )(blk_e, nvb, xs, *([wg, wu, wd] * STEP_BLOCKS))


def _combine_kernel(npair_ref, nsingle_ref, lpair_ref, gpair_ref, lsingle_ref, gsingle_ref,
                    slot_ref, x1_ref, mf_ref, gfin_ref, ys_hbm, o_ref, ybuf, sem):
    copy_tables = (npair_ref, nsingle_ref, lpair_ref, gpair_ref, lsingle_ref, gsingle_ref)
    t = pl.program_id(0)
    cur = t % 2

    def fetch(tile, slot):
        _for_each_copy(tile, copy_tables,
                       lambda lc, gc, n: _chunk_copy(ys_hbm, ybuf.at[slot], gc, lc, sem.at[slot], n).start())

    @pl.when(t == 0)
    def _():
        ybuf[...] = jnp.zeros(ybuf.shape, BF16)
        fetch(0, 0)

    @pl.when(t + 1 < pl.num_programs(0))
    def _():
        fetch(t + 1, 1 - cur)

    used_chunks = _tile_chunks(t, copy_tables)
    _wait_chunks(used_chunks, ys_hbm, ybuf.at[cur], sem.at[cur])

    def finish(rows):
        yb = ybuf[cur, :rows, :]
        col = lax.broadcasted_iota(jnp.int32, (TM, rows), 1)
        slots = slot_ref[...]
        y1 = _dot(jnp.where(col == slots[:, 0:1], 1.0, 0.0).astype(BF16), yb)
        y2 = _dot(jnp.where(col == slots[:, 1:2], 1.0, 0.0).astype(BF16), yb)
        mf = mf_ref[...]
        x2 = x1_ref[...] + mf[:, 0:1] * y1 + mf[:, 1:2] * y2
        o_ref[...] = _rms(x2, gfin_ref[...])

    short = used_chunks * CHUNK_ROWS <= SHORT_TILE_ROWS

    @pl.when(short)
    def _():
        finish(SHORT_TILE_ROWS)

    @pl.when(jnp.logical_not(short))
    def _():
        finish(TILE_ROWS)


def _combine(tables, slot_cols, x1, mf, gfin, ys):
    return pl.pallas_call(
        _combine_kernel,
        grid_spec=pltpu.PrefetchScalarGridSpec(
            num_scalar_prefetch=len(tables),
            grid=(N_TOK // TM,),
            in_specs=[pl.BlockSpec((TM, LANES), lambda i, *_: (i, 0)),
                      pl.BlockSpec((TM, D_MODEL), lambda i, *_: (i, 0)),
                      pl.BlockSpec((TM, LANES), lambda i, *_: (i, 0)),
                      pl.BlockSpec((1, D_MODEL), lambda i, *_: (0, 0)),
                      pl.BlockSpec(memory_space=pl.ANY)],
            out_specs=pl.BlockSpec((TM, D_MODEL), lambda i, *_: (i, 0)),
            scratch_shapes=[pltpu.VMEM((2, TILE_ROWS, D_MODEL), BF16), pltpu.SemaphoreType.DMA((2,))]),
        out_shape=jax.ShapeDtypeStruct((N_TOK, D_MODEL), F32),
        compiler_params=pltpu.CompilerParams(dimension_semantics=("arbitrary",)),
        name="combine",
    )(*tables, slot_cols, x1, mf, gfin, ys)


def kernel(x, positions, g_attn_norm, w_in, g_cq, w_uq, g_ckv, w_uk, w_uv, w_o_attn, sgu_gain, sgu_bias, w_spatial, b_spatial, w_o_sgu, w_out, g_ffn_norm, w_router_group, b_router_group, w_router_expert, b_router_expert, w_exp_gate, w_exp_up, w_exp_down, g_final):
    assert x.shape == (BATCH, SEQ, D_MODEL) and w_in.shape[0] == 1
    half = QK_ROPE // 2
    swap = jnp.concatenate([jnp.arange(half, QK_ROPE), jnp.arange(0, half)])

    def pad_cols(w, width):
        return jnp.pad(w, ((0, 0), (0, width - w.shape[1])))

    wi = w_in[0]
    c0 = Q_RANK + KV_RANK
    kr = wi[:, c0:c0 + QK_ROPE]
    c1 = c0 + QK_ROPE
    win = jnp.concatenate([
        wi[:, :c0], jnp.tile(kr, (1, LANES // QK_ROPE)), jnp.tile(kr[:, swap], (1, LANES // QK_ROPE)),
        wi[:, c1:]], axis=1).astype(BF16)

    wq = w_uq[0].reshape(Q_RANK, MLA_HEADS, QK_NOPE + QK_ROPE)
    wq_nope = wq[:, :, :QK_NOPE].transpose(1, 0, 2)
    wq_rope = wq[:, :, QK_NOPE:]
    w_ukt = w_uk[0].reshape(KV_RANK, MLA_HEADS, QK_NOPE).transpose(1, 2, 0)
    wqlat = _fold(wq_nope, w_ukt).transpose(1, 0, 2).reshape(Q_RANK, MLA_HEADS * LANES).astype(BF16)

    wqr = wq_rope.reshape(Q_RANK, MLA_HEADS * QK_ROPE).astype(BF16)
    wqrs = wq_rope[:, :, swap].reshape(Q_RANK, MLA_HEADS * QK_ROPE).astype(BF16)

    per_row = LANES // half
    freqs = ROPE_THETA ** (-jnp.arange(0, QK_ROPE, 2, dtype=F32) / QK_ROPE)
    pos8 = jnp.repeat(positions.astype(F32).reshape(N_TOK // per_row, per_row), half, axis=1)
    cos8, sin8 = _rope_tables(pos8, jnp.tile(freqs, per_row)[None, :])
    cos16 = cos8.reshape(N_TOK, half)
    sin16 = sin8.reshape(N_TOK, half)
    cos_t = jnp.tile(jnp.concatenate([cos16, cos16], axis=1), (1, LANES // QK_ROPE))
    sin_t = jnp.tile(jnp.concatenate([-sin16, sin16], axis=1), (1, LANES // QK_ROPE))

    head_of_col = jnp.arange(MLA_WIDTH) // V_HEAD
    wuv = jnp.where(head_of_col[None, None, :] == jnp.arange(MLA_HEADS)[:, None, None],
                    w_uv[0][None], 0.0).astype(BF16)
    wuv = wuv.reshape(MLA_HEADS * KV_RANK, MLA_WIDTH)

    wsp = w_spatial[0].astype(BF16)
    bs = b_spatial[0]
    bsp = jnp.repeat(bs.reshape(SGU_GROUPS // 2, 2, SGU_LEN).transpose(0, 2, 1), SGU_GROUP_DIM, axis=2)

    wr32 = jnp.concatenate([w_router_expert[0].transpose(1, 0, 2).reshape(D_MODEL, N_EXPERTS),
                            w_router_group[0]], axis=1)
    wr32 = pad_cols(wr32, LANES)
    wr_hi = wr32.astype(BF16)
    wr_lo = (wr32 - wr_hi.astype(F32)).astype(BF16)
    wr = jnp.concatenate([wr_hi, wr_lo], axis=1)
    br = pad_cols(jnp.concatenate([b_router_expert[0].reshape(-1), b_router_group[0]])[None, :], LANES)

    xf = x.reshape(N_TOK, D_MODEL)
    q, kt, v, p, ga = _inproj(
        xf, cos_t, sin_t, g_attn_norm, win, g_cq, g_ckv, wqlat, wqr, wqrs,
        sgu_gain, sgu_bias, wsp, bsp, w_o_sgu[0].astype(BF16))
    attn = _attention(q, kt, v, wuv)
    x1, h2, mf, slot_cols, slot_rows, run_chunks_rows, run_start_rows = _mix(
        attn.reshape(N_TOK, MLA_WIDTH), ga, p, xf, w_o_attn[0].astype(BF16), w_out[0].astype(BF16),
        g_ffn_norm, wr, br)

    blk_chunks = MOE_T // CHUNK_ROWS
    run_chunks = run_chunks_rows[::SUBLANES, :N_EXPERTS]
    run_start = run_start_rows[::SUBLANES, :N_EXPERTS]
    seg_chunks = jnp.sum(run_chunks, axis=0)
    seg_padded = (seg_chunks + blk_chunks - 1) // blk_chunks * blk_chunks
    seg_end = jnp.cumsum(seg_padded)
    seg_start = seg_end - seg_padded
    run_dest = seg_start[None, :] + jnp.cumsum(run_chunks, axis=0) - run_chunks
    nvb = (seg_end[-1:] // blk_chunks).astype(jnp.int32)
    blk_first_chunk = jnp.arange(MOE_BLOCKS, dtype=jnp.int32) * blk_chunks
    blk_e = jnp.minimum(jnp.sum((seg_end[None, :] <= blk_first_chunk[:, None]).astype(jnp.int32), axis=1),
                        N_EXPERTS - 1)
    def flat_list(count, slots, first_local, first_global, step):
        off = jnp.cumsum(count, axis=1) - count
        k = jnp.arange(slots, dtype=jnp.int32)
        in_run = (off[:, None, :] <= k[None, :, None]) & (k[None, :, None] < (off + count)[:, None, :])

        def pick(first):
            return (jnp.sum(jnp.where(in_run, (first - step * off)[:, None, :], 0), axis=-1)
                    + step * k[None, :]).reshape(-1).astype(jnp.int32)

        return jnp.sum(count, axis=1).astype(jnp.int32), pick(first_local), pick(first_global)

    npair, lpair, gpair = flat_list(run_chunks // 2, PAIR_SLOTS, run_start, run_dest, 2)
    nsingle, lsingle, gsingle = flat_list(run_chunks % 2, N_EXPERTS, run_start + run_chunks - 1, run_dest + run_chunks - 1, 0)
    run_tables = (npair, nsingle, lpair, gpair, lsingle, gsingle)

    xs = _dispatch(run_tables + ((seg_start + seg_chunks).astype(jnp.int32), (seg_padded - seg_chunks).astype(jnp.int32), nvb),
                   slot_rows, h2)
    ys = _experts(blk_e, nvb, xs, w_exp_gate[0], w_exp_up[0], w_exp_down[0])
    out = _combine(run_tables, slot_cols, x1, mf, g_final.reshape(1, D_MODEL), ys)
    return out.reshape(BATCH, SEQ, D_MODEL)
```

```python
import jax
import jax.numpy as jnp
from jax import lax
from jax.experimental import pallas as pl
from jax.experimental.pallas import tpu as pltpu

D_MODEL = 1024
BATCH = 2
SEQ = 8192
N_TOK = BATCH * SEQ
CHUNK = 64
EPS = 1e-6
MLA_HEADS = 8
Q_RANK = 256
KV_RANK = 128
QK_NOPE = 64
QK_ROPE = 32
V_HEAD = 64
MLA_WIDTH = MLA_HEADS * V_HEAD
ROPE_THETA = 10000.0
SGU_GROUPS = 8
SGU_GROUP_DIM = 64
SGU_WIDTH = SGU_GROUPS * SGU_GROUP_DIM
SGU_LEN = 128
N_GROUPS = 4
EXPERTS_PER_GROUP = 8
N_EXPERTS = N_GROUPS * EXPERTS_PER_GROUP
TOP_K = 2
D_EXPERT = 256

LANES = 128
SUBLANES = 8
QK_PAD = 2 * LANES
V_PAD = 2 * LANES
SCALE = (QK_NOPE + QK_ROPE) ** -0.5
LOG2E = 1.4426950408889634
Q_SCALE = SCALE * LOG2E
NEG = -1e30

TM = 256
TMI = 512
TMX = 1024
TQ = 256
TK = 512
ATTN_UNROLL_SHIFT = 1
ATTN_UNROLL = 1 << ATTN_UNROLL_SHIFT
TK_SHIFT = TK.bit_length() - 1
CHUNK_SHIFT = CHUNK.bit_length() - 1
assert 1 << TK_SHIFT == TK and 1 << CHUNK_SHIFT == CHUNK
MOE_T = 512
CHUNK_ROWS = 16
TILE_ROWS = -(-(TOP_K * TM + N_EXPERTS * (CHUNK_ROWS - 1) + CHUNK_ROWS) // 256) * 256
SHORT_TILE_ROWS = TILE_ROWS - 256
TILE_CHUNKS = TILE_ROWS // CHUNK_ROWS
PAIR_SLOTS = TILE_CHUNKS // 2
CHUNK_UNROLL_SHIFT = 2
CHUNK_UNROLL = 1 << CHUNK_UNROLL_SHIFT
MOE_ROWS_MAX = (N_TOK * TOP_K + (N_TOK // TM) * N_EXPERTS * (CHUNK_ROWS - 1)
                + N_EXPERTS * (MOE_T - CHUNK_ROWS))
STEP_BLOCKS = 3
MOE_BLOCKS = -(-MOE_ROWS_MAX // (MOE_T * STEP_BLOCKS)) * STEP_BLOCKS
MOE_ROWS = MOE_BLOCKS * MOE_T

C_Q = 0
C_KV = C_Q + Q_RANK
C_KR = C_KV + KV_RANK
C_KRS = C_KR + LANES
C_U = C_KRS + LANES
C_V = C_U + SGU_WIDTH
C_GA = C_V + SGU_WIDTH
C_GB = C_GA + D_MODEL
C_END = C_GB + D_MODEL

F32 = jnp.float32
BF16 = jnp.bfloat16


def _dot(a, b):
    return jnp.dot(a, b, preferred_element_type=F32)


def _rms(x, g):
    return x * lax.rsqrt(jnp.mean(x * x, axis=-1, keepdims=True) + EPS) * g


def _fold_kernel(a_ref, b_ref, o_ref):
    o_ref[...] = Q_SCALE * jnp.dot(a_ref[...], b_ref[...], preferred_element_type=F32,
                                 precision=lax.Precision.HIGHEST)


def _fold(w_uq_nope, w_ukt):
    return pl.pallas_call(
        _fold_kernel,
        grid=(MLA_HEADS,),
        in_specs=[pl.BlockSpec((None, Q_RANK, QK_NOPE), lambda h: (h, 0, 0)),
                  pl.BlockSpec((None, QK_NOPE, KV_RANK), lambda h: (h, 0, 0))],
        out_specs=pl.BlockSpec((None, Q_RANK, KV_RANK), lambda h: (h, 0, 0)),
        out_shape=jax.ShapeDtypeStruct((MLA_HEADS, Q_RANK, KV_RANK), F32),
        name="fold",
    )(w_uq_nope, w_ukt)


def _rope_kernel(pos_ref, freq_ref, cos_ref, sin_ref):
    ang = pos_ref[...] * freq_ref[...]
    cos_ref[...] = jnp.cos(ang)
    sin_ref[...] = jnp.sin(ang)


def _rope_tables(pos8, freq8):
    rows = pos8.shape[0]
    blk = pl.BlockSpec((TM, LANES), lambda i: (i, 0))
    return pl.pallas_call(
        _rope_kernel,
        grid=(rows // TM,),
        in_specs=[blk, pl.BlockSpec((1, LANES), lambda i: (0, 0))],
        out_specs=[blk, blk],
        out_shape=[jax.ShapeDtypeStruct((rows, LANES), F32)] * 2,
        name="rope_tables",
    )(pos8, freq8)


def _inproj_kernel(x_ref, cos_ref, sin_ref, gattn_ref, win_ref, gcq_ref, gckv_ref, wqlat_ref, wqr_ref,
                   wqrs_ref, sgain_ref, sbias_ref, wsp_ref, bsp_ref, wosgu_ref,
                   q_ref, kt_ref, v_ref, p_ref, ga_ref):
    hb = _rms(x_ref[...], gattn_ref[...]).astype(BF16)

    def proj(a, b):
        return _dot(hb, win_ref[:, a:b])

    cos_t = cos_ref[...]
    sin_t = sin_ref[...]

    cqn = _rms(proj(C_Q, C_KV), gcq_ref[...]).astype(BF16)
    qlat = _dot(cqn, wqlat_ref[...])
    groups = MLA_HEADS * QK_ROPE // LANES
    rot = (_dot(cqn, wqr_ref[...]) * jnp.concatenate([cos_t * Q_SCALE] * groups, axis=1)
           + _dot(cqn, wqrs_ref[...]) * jnp.concatenate([sin_t * Q_SCALE] * groups, axis=1))
    lane = lax.broadcasted_iota(jnp.int32, (TQ, LANES), 1)
    for r in range(TMI // TQ):
        t0, t1 = r * TQ, (r + 1) * TQ
        for h in range(MLA_HEADS):
            a, b = h * LANES, (h + 1) * LANES
            q_ref[r, h * TQ:(h + 1) * TQ, :LANES] = qlat[t0:t1, a:b].astype(BF16)
            g0 = (h * QK_ROPE) // LANES * LANES
            off = (h * QK_ROPE) % LANES
            window = (lane >= off) & (lane < off + QK_ROPE)
            q_ref[r, h * TQ:(h + 1) * TQ, LANES:] = jnp.where(
                window, rot[t0:t1, g0:g0 + LANES], 0.0).astype(BF16)

    zk = proj(C_KV, C_U)
    ckvn = _rms(zk[:, :KV_RANK], gckv_ref[...])
    krope = zk[:, KV_RANK:KV_RANK + LANES] * cos_t + zk[:, KV_RANK + LANES:] * sin_t
    kt_ref[...] = jnp.concatenate([ckvn, krope], axis=1).T.astype(BF16)
    v_ref[...] = jnp.concatenate([ckvn, jnp.ones_like(ckvn)], axis=1).astype(BF16)

    u = jax.nn.gelu(proj(C_U, C_V))
    v = jax.nn.gelu(proj(C_V, C_GA))
    mu = jnp.mean(v, axis=-1, keepdims=True)
    vc = v - mu
    var = jnp.mean(vc * vc, axis=-1, keepdims=True)
    vb = (vc * lax.rsqrt(var + EPS) * sgain_ref[...] + sbias_ref[...]).astype(BF16)
    row = lax.broadcasted_iota(jnp.int32, (SGU_LEN, SGU_LEN), 0)
    col = lax.broadcasted_iota(jnp.int32, (SGU_LEN, SGU_LEN), 1)
    causal = (row >> CHUNK_SHIFT) >= (col >> CHUNK_SHIFT)
    low_half = col < SGU_GROUP_DIM
    zero_w = jnp.zeros((SGU_LEN, SGU_LEN), BF16)
    w_pairs = [jnp.concatenate([jnp.where(causal, wsp_ref[2 * p], zero_w),
                                jnp.where(causal, wsp_ref[2 * p + 1], zero_w)], axis=1)
               for p in range(SGU_WIDTH // LANES)]
    row_blocks = []
    for r in range(TMI // SGU_LEN):
        pieces = []
        for p in range(SGU_WIDTH // LANES):
            blk = vb[r * SGU_LEN:(r + 1) * SGU_LEN, p * LANES:(p + 1) * LANES]
            stacked = jnp.concatenate([jnp.where(low_half, blk, zero_w), jnp.where(low_half, zero_w, blk)], axis=0)
            sv = _dot(w_pairs[p], stacked) + bsp_ref[p]
            pieces.append((u[r * SGU_LEN:(r + 1) * SGU_LEN, p * LANES:(p + 1) * LANES] * sv).astype(BF16))
        row_blocks.append(jnp.concatenate(pieces, axis=1))
    sgu = _dot(jnp.concatenate(row_blocks, axis=0), wosgu_ref[...])

    ga_ref[...] = jax.nn.sigmoid(proj(C_GA, C_GB)).astype(BF16)
    p_ref[...] = (jax.nn.sigmoid(proj(C_GB, C_END)) * sgu).astype(BF16)


def _inproj(x, cos_t, sin_t, gattn, win, gcq, gckv, wqlat, wqr, wqrs, sgain, sbias, wsp, bsp, wosgu):
    nt = N_TOK // TMI
    per_b = SEQ // TMI
    per_k = TK // TMI

    def const(shape):
        return pl.BlockSpec(shape, lambda i: (0,) * len(shape))

    return pl.pallas_call(
        _inproj_kernel,
        grid=(nt,),
        in_specs=[pl.BlockSpec((TMI, D_MODEL), lambda i: (i, 0)),
                  pl.BlockSpec((TMI, LANES), lambda i: (i, 0)), pl.BlockSpec((TMI, LANES), lambda i: (i, 0)),
                  const((1, D_MODEL)), const((D_MODEL, C_END)), const((1, Q_RANK)), const((1, KV_RANK)),
                  const((Q_RANK, MLA_HEADS * LANES)), const((Q_RANK, MLA_HEADS * QK_ROPE)),
                  const((Q_RANK, MLA_HEADS * QK_ROPE)),
                  const((1, SGU_WIDTH)), const((1, SGU_WIDTH)),
                  const((SGU_GROUPS, SGU_LEN, SGU_LEN)), const((SGU_GROUPS // 2, SGU_LEN, LANES)),
                  const((SGU_WIDTH, D_MODEL))],
        out_specs=[pl.BlockSpec((TMI // TQ, MLA_HEADS * TQ, QK_PAD), lambda i: (i, 0, 0)),
                   pl.BlockSpec((None, None, QK_PAD, TMI),
                                lambda i: (i // per_b, (i % per_b) // per_k, 0, i % per_k)),
                   pl.BlockSpec((None, None, TMI, V_PAD),
                                lambda i: (i // per_b, (i % per_b) // per_k, i % per_k, 0)),
                   pl.BlockSpec((TMI, D_MODEL), lambda i: (i, 0)),
                   pl.BlockSpec((TMI, D_MODEL), lambda i: (i, 0))],
        out_shape=[jax.ShapeDtypeStruct((N_TOK // TQ, MLA_HEADS * TQ, QK_PAD), BF16),
                   jax.ShapeDtypeStruct((BATCH, SEQ // TK, QK_PAD, TK), BF16),
                   jax.ShapeDtypeStruct((BATCH, SEQ // TK, TK, V_PAD), BF16),
                   jax.ShapeDtypeStruct((N_TOK, D_MODEL), BF16),
                   jax.ShapeDtypeStruct((N_TOK, D_MODEL), BF16)],
        compiler_params=pltpu.CompilerParams(dimension_semantics=("arbitrary",),
                                             vmem_limit_bytes=56 * 1024 * 1024),
        name="inproj",
    )(x, cos_t, sin_t, gattn, win, gcq, gckv, wqlat, wqr, wqrs, sgain, sbias, wsp, bsp, wosgu)


def _attn_kernel(q_ref, kt_ref, v_ref, wuv_ref, o_ref, m_ref, acc_ref, s_ref, p_ref, a_ref):
    qi = pl.program_id(1)
    m_ref[...] = jnp.full(m_ref.shape, NEG, F32)
    acc_ref[...] = jnp.zeros(acc_ref.shape, F32)

    def scores(j, slot):
        s_ref[slot] = _dot(q_ref[...], kt_ref[j])

    def update(j, slot, width=None):
        masked = width is not None
        w = width if masked else TK
        if masked:
            q_chunk = ((qi * TQ - j * TK) >> CHUNK_SHIFT) + (
                lax.broadcasted_iota(jnp.int32, (TQ, w), 0) >> CHUNK_SHIFT)
            k_chunk = lax.broadcasted_iota(jnp.int32, (TQ, w), 1) >> CHUNK_SHIFT
            visible = k_chunk <= q_chunk
        for h in range(MLA_HEADS):
            rows = slice(h * TQ, (h + 1) * TQ)
            s = s_ref[slot, rows, :w]
            if masked:
                s = jnp.where(visible, s, NEG)
            m_prev = m_ref[rows, :]
            m_new = jnp.maximum(m_prev, jnp.max(s, axis=-1, keepdims=True))
            m_ref[rows, :] = m_new
            a_ref[rows, :] = jnp.exp2(m_prev - m_new)
            s = s_ref[slot, rows, :w]
            if masked:
                s = jnp.where(visible, s, NEG)
            p_ref[rows, :w] = jnp.exp2(s - jnp.concatenate([m_new] * (w // LANES), axis=1)).astype(BF16)
        alpha = a_ref[...]
        acc_ref[...] = jnp.concatenate([alpha, alpha], axis=1) * acc_ref[...] + _dot(
            p_ref[:, :w], v_ref[j, :w, :])

    diag = (qi * TQ) >> TK_SHIFT
    scores(0, 0)

    def run(first, count, last_masked):
        for u in range(count):
            if not (last_masked and u == count - 1):
                scores(first + u + 1, (u + 1) % 2)
                update(first + u, u % 2)
            else:
                for sub in range(TK // TQ):
                    @pl.when((qi & (TK // TQ - 1)) == sub)
                    def _(u=u, sub=sub):
                        update(first + u, u % 2, (sub + 1) * TQ)

    def body(t, carry):
        run(ATTN_UNROLL * t, ATTN_UNROLL, False)
        return carry

    trips = diag >> ATTN_UNROLL_SHIFT
    lax.fori_loop(0, trips, body, 0)
    done = trips << ATTN_UNROLL_SHIFT
    for r in range(ATTN_UNROLL):
        @pl.when(diag - done == r)
        def _(r=r):
            run(done, r + 1, True)

    o_lat = (acc_ref[:, :KV_RANK] / acc_ref[:, KV_RANK:]).astype(BF16)
    o_cat = jnp.concatenate([o_lat[h * TQ:(h + 1) * TQ] for h in range(MLA_HEADS)], axis=1)
    o_ref[...] = _dot(o_cat, wuv_ref[...]).astype(BF16)


def _attention(q, kt, v, wuv):
    nk = SEQ // TK
    return pl.pallas_call(
        _attn_kernel,
        grid=(BATCH, SEQ // TQ),
        in_specs=[pl.BlockSpec((None, MLA_HEADS * TQ, QK_PAD), lambda b, i: (b * (SEQ // TQ) + i, 0, 0)),
                  pl.BlockSpec((None, nk, QK_PAD, TK), lambda b, i: (b, 0, 0, 0)),
                  pl.BlockSpec((None, nk, TK, V_PAD), lambda b, i: (b, 0, 0, 0)),
                  pl.BlockSpec((MLA_HEADS * KV_RANK, MLA_WIDTH), lambda b, i: (0, 0))],
        out_specs=pl.BlockSpec((None, TQ, MLA_WIDTH), lambda b, i: (b, i, 0)),
        out_shape=jax.ShapeDtypeStruct((BATCH, SEQ, MLA_WIDTH), BF16),
        scratch_shapes=[pltpu.VMEM((MLA_HEADS * TQ, LANES), F32),
                        pltpu.VMEM((MLA_HEADS * TQ, V_PAD), F32),
                        pltpu.VMEM((2, MLA_HEADS * TQ, TK), F32),
                        pltpu.VMEM((MLA_HEADS * TQ, TK), BF16),
                        pltpu.VMEM((MLA_HEADS * TQ, LANES), F32)],
        compiler_params=pltpu.CompilerParams(dimension_semantics=("arbitrary", "arbitrary"),
                                             vmem_limit_bytes=56 * 1024 * 1024),
        name="attention",
    )(q, kt, v, wuv)


def _mix_kernel(attn_ref, ga_ref, p_ref, x_ref, woa_ref, wout_ref, gffn_ref, wr_ref, br_ref,
                x1_ref, h2_ref, mf_ref, slot_col_ref, slot_row_ref, run_chunks_ref, run_start_ref):
    a = _dot(attn_ref[...], woa_ref[...])
    mix = (ga_ref[...].astype(F32) * a + p_ref[...].astype(F32)).astype(BF16)
    x1 = x_ref[...] + _dot(mix, wout_ref[...])
    x1_ref[...] = x1
    h2 = _rms(x1, gffn_ref[...])
    h2_ref[...] = h2.astype(BF16)

    hi = h2.astype(BF16)
    lo = (h2 - hi.astype(F32)).astype(BF16)
    r1 = _dot(hi, wr_ref[...])
    r2 = _dot(lo, wr_ref[:, :LANES])
    logits_all = r1[:, :LANES] + r1[:, LANES:] + r2 + br_ref[...]

    for r in range(TMX // TM):
        mf, slots, run_chunks_rows, run_start_rows = _route_tile(logits_all[r * TM:(r + 1) * TM])
        mf_ref[r * TM:(r + 1) * TM, :] = mf
        slot_col_ref[r * TM:(r + 1) * TM, :] = slots.astype(jnp.int32)
        tile_rows = slice(SUBLANES * r, SUBLANES * (r + 1))
        slot_row_ref[tile_rows, :] = slots.T[:SUBLANES].astype(jnp.int32)
        run_chunks_ref[tile_rows, :] = run_chunks_rows.astype(jnp.int32)
        run_start_ref[tile_rows, :] = run_start_rows.astype(jnp.int32)


def _route_tile(logits):
    lane_i = lax.broadcasted_iota(jnp.int32, (TM, LANES), 1)
    lane = lane_i.astype(F32)
    lane_group = (lane_i >> 3).astype(F32)
    ninf = -jnp.inf
    is_group = (lane_i >= N_EXPERTS) & (lane_i < N_EXPERTS + N_GROUPS)
    lg = jnp.where(is_group, logits, ninf)
    gmax = jnp.max(lg, axis=-1, keepdims=True)
    gsum = jnp.sum(jnp.exp(lg - gmax), axis=-1, keepdims=True)
    p_top = 1.0 / gsum
    g_idx = jnp.min(jnp.where(lg == gmax, lane - N_EXPERTS, float(N_GROUPS)), axis=-1, keepdims=True)
    le = jnp.where((lane_i < N_EXPERTS) & (lane_group == g_idx), logits, ninf)
    t1 = jnp.max(le, axis=-1, keepdims=True)
    e1 = jnp.min(jnp.where(le == t1, lane, float(LANES)), axis=-1, keepdims=True)
    le2 = jnp.where(lane == e1, ninf, le)
    t2 = jnp.max(le2, axis=-1, keepdims=True)
    e2 = jnp.min(jnp.where(le2 == t2, lane, float(LANES)), axis=-1, keepdims=True)
    ex = jnp.exp(t2 - t1)
    w1 = p_top / (1.0 + ex)
    w2 = p_top * ex / (1.0 + ex)

    sel1 = lane == e1
    sel2 = lane == e2
    onehot = jnp.where(sel1 | sel2, 1.0, 0.0)
    rr = lax.broadcasted_iota(jnp.int32, (TM, TM), 0)
    cc = lax.broadcasted_iota(jnp.int32, (TM, TM), 1)
    ltri = jnp.where(cc < rr, 1.0, 0.0).astype(BF16)
    rank = _dot(ltri, onehot.astype(BF16))
    cnt = jnp.sum(onehot, axis=0, keepdims=True)
    run_chunks = jnp.floor((cnt + (CHUNK_ROWS - 1)) * (1.0 / CHUNK_ROWS))
    ur = lax.broadcasted_iota(jnp.int32, (LANES, LANES), 0)
    uc = lax.broadcasted_iota(jnp.int32, (LANES, LANES), 1)
    upper = jnp.where(ur < uc, 1.0, 0.0).astype(BF16)
    run_chunks_rows = jnp.broadcast_to(run_chunks, (SUBLANES, LANES))
    run_start_rows = _dot(run_chunks_rows.astype(BF16), upper)
    slot_all = CHUNK_ROWS * run_start_rows[0:1] + rank
    slot1 = jnp.sum(jnp.where(sel1, slot_all, 0.0), axis=-1, keepdims=True)
    slot2 = jnp.sum(jnp.where(sel2, slot_all, 0.0), axis=-1, keepdims=True)

    slots = jnp.where(lane_i == 0, slot1, jnp.where(lane_i == 1, slot2, 0.0))
    return jnp.where(lane_i == 0, w1, w2), slots, run_chunks_rows, run_start_rows


def _mix(attn, ga, p, x, woa, wout, gffn, wr, br):
    nt = N_TOK // TM
    sub = TMX // TM

    def const(shape):
        return pl.BlockSpec(shape, lambda i: (0,) * len(shape))

    def rows(width):
        return pl.BlockSpec((TMX, width), lambda i: (i, 0))

    return pl.pallas_call(
        _mix_kernel,
        grid=(N_TOK // TMX,),
        in_specs=[rows(MLA_WIDTH), rows(D_MODEL), rows(D_MODEL), rows(D_MODEL),
                  const((MLA_WIDTH, D_MODEL)), const((D_MODEL, D_MODEL)), const((1, D_MODEL)),
                  const((D_MODEL, 2 * LANES)), const((1, LANES))],
        out_specs=[rows(D_MODEL), rows(D_MODEL), rows(LANES), rows(LANES),
                   pl.BlockSpec((SUBLANES * sub, TM), lambda i: (i, 0)),
                   pl.BlockSpec((SUBLANES * sub, LANES), lambda i: (i, 0)),
                   pl.BlockSpec((SUBLANES * sub, LANES), lambda i: (i, 0))],
        out_shape=[jax.ShapeDtypeStruct((N_TOK, D_MODEL), F32),
                   jax.ShapeDtypeStruct((N_TOK, D_MODEL), BF16),
                   jax.ShapeDtypeStruct((N_TOK, LANES), F32),
                   jax.ShapeDtypeStruct((N_TOK, LANES), jnp.int32),
                   jax.ShapeDtypeStruct((nt * SUBLANES, TM), jnp.int32),
                   jax.ShapeDtypeStruct((nt * SUBLANES, LANES), jnp.int32),
                   jax.ShapeDtypeStruct((nt * SUBLANES, LANES), jnp.int32)],
        compiler_params=pltpu.CompilerParams(dimension_semantics=("arbitrary",),
                                             vmem_limit_bytes=40 * 1024 * 1024),
        name="mix",
    )(attn, ga, p, x, woa, wout, gffn, wr, br)


def _chunk_copy(src, dst, src_chunk, dst_chunk, sem, nchunks=1):
    rows = nchunks * CHUNK_ROWS
    return pltpu.make_async_copy(
        src.at[pl.ds(pl.multiple_of(src_chunk * CHUNK_ROWS, CHUNK_ROWS), rows)],
        dst.at[pl.ds(pl.multiple_of(dst_chunk * CHUNK_ROWS, CHUNK_ROWS), rows)], sem)


def _unrolled_loop(n, fn):
    groups = n >> CHUNK_UNROLL_SHIFT

    def group(g, carry):
        for u in range(CHUNK_UNROLL):
            fn(g * CHUNK_UNROLL + u)
        return carry

    lax.fori_loop(0, groups, group, 0)

    def single(k, carry):
        fn(k)
        return carry

    lax.fori_loop(groups << CHUNK_UNROLL_SHIFT, n, single, 0)


def _for_each_copy(t, copy_tables, fn):
    npair_ref, nsingle_ref, lpair_ref, gpair_ref, lsingle_ref, gsingle_ref = copy_tables
    pb = t * PAIR_SLOTS
    sb = t * N_EXPERTS
    _unrolled_loop(npair_ref[t], lambda k: fn(lpair_ref[pb + k], gpair_ref[pb + k], 2))
    _unrolled_loop(nsingle_ref[t], lambda k: fn(lsingle_ref[sb + k], gsingle_ref[sb + k], 1))


def _tile_chunks(t, copy_tables):
    return 2 * copy_tables[0][t] + copy_tables[1][t]


def _wait_chunks(n, src, dst, sem):
    groups = n >> CHUNK_UNROLL_SHIFT
    rows = CHUNK_UNROLL * CHUNK_ROWS

    def group(g, carry):
        pltpu.make_async_copy(src.at[pl.ds(0, rows)], dst.at[pl.ds(0, rows)], sem).wait()
        return carry

    lax.fori_loop(0, groups, group, 0)

    def single(k, carry):
        _chunk_copy(src, dst, 0, 0, sem).wait()
        return carry

    lax.fori_loop(groups << CHUNK_UNROLL_SHIFT, n, single, 0)


def _dispatch_kernel(npair_ref, nsingle_ref, lpair_ref, gpair_ref, lsingle_ref, gsingle_ref,
                     zero_start_ref, zero_count_ref, nvb_ref,
                     slot_ref, h2_ref, xs_hbm, sbuf, zbuf, sem, zsem):
    copy_tables = (npair_ref, nsingle_ref, lpair_ref, gpair_ref, lsingle_ref, gsingle_ref)
    t = pl.program_id(0)
    last = pl.num_programs(0) - 1
    cur = t % 2

    def zero_chunk_copy(dst_chunk):
        return _chunk_copy(zbuf, xs_hbm, 0, dst_chunk, zsem)

    def zero_block_copy(b):
        return pltpu.make_async_copy(
            zbuf, xs_hbm.at[pl.ds(pl.multiple_of(b * MOE_T, MOE_T), MOE_T)], zsem)

    def for_each_zero(chunk_fn, block_fn):
        def per_expert(e, carry):
            def per_chunk(j, carry2):
                chunk_fn(zero_start_ref[e] + j)
                return carry2

            lax.fori_loop(0, zero_count_ref[e], per_chunk, 0)
            return carry

        lax.fori_loop(0, N_EXPERTS, per_expert, 0)

        def per_block(b, carry):
            block_fn(b)
            return carry

        lax.fori_loop(nvb_ref[0], MOE_BLOCKS, per_block, 0)

    @pl.when(t == 0)
    def _():
        zbuf[...] = jnp.zeros(zbuf.shape, BF16)
        for_each_zero(lambda dst_chunk: zero_chunk_copy(dst_chunk).start(), lambda b: zero_block_copy(b).start())

    row = lax.broadcasted_iota(jnp.int32, (TILE_ROWS, TM), 0)
    slots = slot_ref[...]
    perm = jnp.where((row == slots[0:1, :]) | (row == slots[1:2, :]), 1.0, 0.0).astype(BF16)
    sbuf[cur] = _dot(perm, h2_ref[...]).astype(BF16)
    _for_each_copy(t, copy_tables,
                   lambda lc, gc, n: _chunk_copy(sbuf.at[cur], xs_hbm, lc, gc, sem.at[cur], n).start())

    def wait_tile(tile, slot):
        _wait_chunks(_tile_chunks(tile, copy_tables), sbuf.at[slot], xs_hbm, sem.at[slot])

    @pl.when(t > 0)
    def _():
        wait_tile(t - 1, 1 - cur)

    @pl.when(t == last)
    def _():
        wait_tile(t, cur)
        for_each_zero(lambda dst_chunk: zero_chunk_copy(dst_chunk).wait(), lambda b: zero_block_copy(b).wait())


def _dispatch(tables, slot_rows, h2):
    return pl.pallas_call(
        _dispatch_kernel,
        grid_spec=pltpu.PrefetchScalarGridSpec(
            num_scalar_prefetch=len(tables),
            grid=(N_TOK // TM,),
            in_specs=[pl.BlockSpec((SUBLANES, TM), lambda i, *_: (i, 0)),
                      pl.BlockSpec((TM, D_MODEL), lambda i, *_: (i, 0))],
            out_specs=pl.BlockSpec(memory_space=pl.ANY),
            scratch_shapes=[pltpu.VMEM((2, TILE_ROWS, D_MODEL), BF16), pltpu.VMEM((MOE_T, D_MODEL), BF16),
                            pltpu.SemaphoreType.DMA((2,)), pltpu.SemaphoreType.DMA(())]),
        out_shape=jax.ShapeDtypeStruct((MOE_ROWS, D_MODEL), BF16),
        compiler_params=pltpu.CompilerParams(dimension_semantics=("arbitrary",)),
        name="dispatch",
    )(*tables, slot_rows, h2)


def _expert_kernel(blk_e_ref, nvb_ref, xs_ref, *refs):
    w_refs, ys_ref, caches = refs[:3 * STEP_BLOCKS], refs[3 * STEP_BLOCKS], refs[3 * STEP_BLOCKS + 1:]
    i = pl.program_id(0)
    for h in range(STEP_BLOCKS):
        wg_ref, wu_ref, wd_ref = w_refs[3 * h:3 * h + 3]
        wgb, wub, wdb = caches[3 * h:3 * h + 3]
        rows = slice(h * MOE_T, (h + 1) * MOE_T)
        blk = i * STEP_BLOCKS + h
        live = blk < nvb_ref[0]
        same_slot_before = jnp.maximum(blk - STEP_BLOCKS, 0)

        @pl.when(live & ((i == 0) | (blk_e_ref[blk] != blk_e_ref[same_slot_before])))
        def _(wg_ref=wg_ref, wu_ref=wu_ref, wd_ref=wd_ref, wgb=wgb, wub=wub, wdb=wdb):
            wgb[...] = wg_ref[...].astype(BF16)
            wub[...] = wu_ref[...].astype(BF16)
            wdb[...] = wd_ref[...].astype(BF16)

        @pl.when(live)
        def _(rows=rows, wgb=wgb, wub=wub, wdb=wdb):
            xb = xs_ref[rows, :]
            g = _dot(xb, wgb[...])
            u = _dot(xb, wub[...])
            hid = (jax.nn.silu(g) * u).astype(BF16)
            ys_ref[rows, :] = _dot(hid, wdb[...]).astype(BF16)

        @pl.when(jnp.logical_not(live))
        def _(rows=rows):
            ys_ref[rows, :] = jnp.zeros((MOE_T, D_MODEL), ys_ref.dtype)


def _experts(blk_e, nvb, xs, wg, wu, wd):
    def row_block(i, be, nv):
        return (jnp.minimum(i, (nv[0] - 1) // STEP_BLOCKS), 0)

    def weight(h):
        return lambda i, be, nv: (be[jnp.minimum(i * STEP_BLOCKS + h, nv[0] - 1)], 0, 0)

    weight_specs = []
    for h in range(STEP_BLOCKS):
        weight_specs += [pl.BlockSpec((None, D_MODEL, D_EXPERT), weight(h)),
                         pl.BlockSpec((None, D_MODEL, D_EXPERT), weight(h)),
                         pl.BlockSpec((None, D_EXPERT, D_MODEL), weight(h))]
    caches = [pltpu.VMEM((D_MODEL, D_EXPERT), BF16), pltpu.VMEM((D_MODEL, D_EXPERT), BF16),
              pltpu.VMEM((D_EXPERT, D_MODEL), BF16)] * STEP_BLOCKS
    return pl.pallas_call(
        _expert_kernel,
        grid_spec=pltpu.PrefetchScalarGridSpec(
            num_scalar_prefetch=2,
            grid=(MOE_BLOCKS // STEP_BLOCKS,),
            in_specs=[pl.BlockSpec((STEP_BLOCKS * MOE_T, D_MODEL), row_block)] + weight_specs,
            out_specs=pl.BlockSpec((STEP_BLOCKS * MOE_T, D_MODEL), lambda i, be, nv: (i, 0)),
            scratch_shapes=caches),
        out_shape=jax.ShapeDtypeStruct((MOE_ROWS, D_MODEL), BF16),
        compiler_params=pltpu.CompilerParams(dimension_semantics=("arbitrary",),
                                             vmem_limit_bytes=48 * 1024 * 1024),
        name="experts",
    )(blk_e, nvb, xs, *([wg, wu, wd] * STEP_BLOCKS))


def _combine_kernel(npair_ref, nsingle_ref, lpair_ref, gpair_ref, lsingle_ref, gsingle_ref,
                    slot_ref, x1_ref, mf_ref, gfin_ref, ys_hbm, o_ref, ybuf, sem):
    copy_tables = (npair_ref, nsingle_ref, lpair_ref, gpair_ref, lsingle_ref, gsingle_ref)
    t = pl.program_id(0)
    cur = t % 2

    def fetch(tile, slot):
        _for_each_copy(tile, copy_tables,
                       lambda lc, gc, n: _chunk_copy(ys_hbm, ybuf.at[slot], gc, lc, sem.at[slot], n).start())

    @pl.when(t == 0)
    def _():
        ybuf[...] = jnp.zeros(ybuf.shape, BF16)
        fetch(0, 0)

    @pl.when(t + 1 < pl.num_programs(0))
    def _():
        fetch(t + 1, 1 - cur)

    used_chunks = _tile_chunks(t, copy_tables)
    _wait_chunks(used_chunks, ys_hbm, ybuf.at[cur], sem.at[cur])

    def finish(rows):
        yb = ybuf[cur, :rows, :]
        col = lax.broadcasted_iota(jnp.int32, (TM, rows), 1)
        slots = slot_ref[...]
        y1 = _dot(jnp.where(col == slots[:, 0:1], 1.0, 0.0).astype(BF16), yb)
        y2 = _dot(jnp.where(col == slots[:, 1:2], 1.0, 0.0).astype(BF16), yb)
        mf = mf_ref[...]
        x2 = x1_ref[...] + mf[:, 0:1] * y1 + mf[:, 1:2] * y2
        o_ref[...] = _rms(x2, gfin_ref[...])

    short = used_chunks * CHUNK_ROWS <= SHORT_TILE_ROWS

    @pl.when(short)
    def _():
        finish(SHORT_TILE_ROWS)

    @pl.when(jnp.logical_not(short))
    def _():
        finish(TILE_ROWS)


def _combine(tables, slot_cols, x1, mf, gfin, ys):
    return pl.pallas_call(
        _combine_kernel,
        grid_spec=pltpu.PrefetchScalarGridSpec(
            num_scalar_prefetch=len(tables),
            grid=(N_TOK // TM,),
            in_specs=[pl.BlockSpec((TM, LANES), lambda i, *_: (i, 0)),
                      pl.BlockSpec((TM, D_MODEL), lambda i, *_: (i, 0)),
                      pl.BlockSpec((TM, LANES), lambda i, *_: (i, 0)),
                      pl.BlockSpec((1, D_MODEL), lambda i, *_: (0, 0)),
                      pl.BlockSpec(memory_space=pl.ANY)],
            out_specs=pl.BlockSpec((TM, D_MODEL), lambda i, *_: (i, 0)),
            scratch_shapes=[pltpu.VMEM((2, TILE_ROWS, D_MODEL), BF16), pltpu.SemaphoreType.DMA((2,))]),
        out_shape=jax.ShapeDtypeStruct((N_TOK, D_MODEL), F32),
        compiler_params=pltpu.CompilerParams(dimension_semantics=("arbitrary",)),
        name="combine",
    )(*tables, slot_cols, x1, mf, gfin, ys)


def kernel(x, positions, g_attn_norm, w_in, g_cq, w_uq, g_ckv, w_uk, w_uv, w_o_attn, sgu_gain, sgu_bias, w_spatial, b_spatial, w_o_sgu, w_out, g_ffn_norm, w_router_group, b_router_group, w_router_expert, b_router_expert, w_exp_gate, w_exp_up, w_exp_down, g_final):
    assert x.shape == (BATCH, SEQ, D_MODEL) and w_in.shape[0] == 1
    half = QK_ROPE // 2
    swap = jnp.concatenate([jnp.arange(half, QK_ROPE), jnp.arange(0, half)])

    def pad_cols(w, width):
        return jnp.pad(w, ((0, 0), (0, width - w.shape[1])))

    wi = w_in[0]
    c0 = Q_RANK + KV_RANK
    kr = wi[:, c0:c0 + QK_ROPE]
    c1 = c0 + QK_ROPE
    win = jnp.concatenate([
        wi[:, :c0], jnp.tile(kr, (1, LANES // QK_ROPE)), jnp.tile(kr[:, swap], (1, LANES // QK_ROPE)),
        wi[:, c1:]], axis=1).astype(BF16)

    wq = w_uq[0].reshape(Q_RANK, MLA_HEADS, QK_NOPE + QK_ROPE)
    wq_nope = wq[:, :, :QK_NOPE].transpose(1, 0, 2)
    wq_rope = wq[:, :, QK_NOPE:]
    w_ukt = w_uk[0].reshape(KV_RANK, MLA_HEADS, QK_NOPE).transpose(1, 2, 0)
    wqlat = _fold(wq_nope, w_ukt).transpose(1, 0, 2).reshape(Q_RANK, MLA_HEADS * LANES).astype(BF16)

    wqr = wq_rope.reshape(Q_RANK, MLA_HEADS * QK_ROPE).astype(BF16)
    wqrs = wq_rope[:, :, swap].reshape(Q_RANK, MLA_HEADS * QK_ROPE).astype(BF16)

    per_row = LANES // half
    freqs = ROPE_THETA ** (-jnp.arange(0, QK_ROPE, 2, dtype=F32) / QK_ROPE)
    pos8 = jnp.repeat(positions.astype(F32).reshape(N_TOK // per_row, per_row), half, axis=1)
    cos8, sin8 = _rope_tables(pos8, jnp.tile(freqs, per_row)[None, :])
    cos16 = cos8.reshape(N_TOK, half)
    sin16 = sin8.reshape(N_TOK, half)
    cos_t = jnp.tile(jnp.concatenate([cos16, cos16], axis=1), (1, LANES // QK_ROPE))
    sin_t = jnp.tile(jnp.concatenate([-sin16, sin16], axis=1), (1, LANES // QK_ROPE))

    head_of_col = jnp.arange(MLA_WIDTH) // V_HEAD
    wuv = jnp.where(head_of_col[None, None, :] == jnp.arange(MLA_HEADS)[:, None, None],
                    w_uv[0][None], 0.0).astype(BF16)
    wuv = wuv.reshape(MLA_HEADS * KV_RANK, MLA_WIDTH)

    wsp = w_spatial[0].astype(BF16)
    bs = b_spatial[0]
    bsp = jnp.repeat(bs.reshape(SGU_GROUPS // 2, 2, SGU_LEN).transpose(0, 2, 1), SGU_GROUP_DIM, axis=2)

    wr32 = jnp.concatenate([w_router_expert[0].transpose(1, 0, 2).reshape(D_MODEL, N_EXPERTS),
                            w_router_group[0]], axis=1)
    wr32 = pad_cols(wr32, LANES)
    wr_hi = wr32.astype(BF16)
    wr_lo = (wr32 - wr_hi.astype(F32)).astype(BF16)
    wr = jnp.concatenate([wr_hi, wr_lo], axis=1)
    br = pad_cols(jnp.concatenate([b_router_expert[0].reshape(-1), b_router_group[0]])[None, :], LANES)

    xf = x.reshape(N_TOK, D_MODEL)
    q, kt, v, p, ga = _inproj(
        xf, cos_t, sin_t, g_attn_norm, win, g_cq, g_ckv, wqlat, wqr, wqrs,
        sgu_gain, sgu_bias, wsp, bsp, w_o_sgu[0].astype(BF16))
    attn = _attention(q, kt, v, wuv)
    x1, h2, mf, slot_cols, slot_rows, run_chunks_rows, run_start_rows = _mix(
        attn.reshape(N_TOK, MLA_WIDTH), ga, p, xf, w_o_attn[0].astype(BF16), w_out[0].astype(BF16),
        g_ffn_norm, wr, br)

    blk_chunks = MOE_T // CHUNK_ROWS
    run_chunks = run_chunks_rows[::SUBLANES, :N_EXPERTS]
    run_start = run_start_rows[::SUBLANES, :N_EXPERTS]
    seg_chunks = jnp.sum(run_chunks, axis=0)
    seg_padded = (seg_chunks + blk_chunks - 1) // blk_chunks * blk_chunks
    seg_end = jnp.cumsum(seg_padded)
    seg_start = seg_end - seg_padded
    run_dest = seg_start[None, :] + jnp.cumsum(run_chunks, axis=0) - run_chunks
    nvb = (seg_end[-1:] // blk_chunks).astype(jnp.int32)
    blk_first_chunk = jnp.arange(MOE_BLOCKS, dtype=jnp.int32) * blk_chunks
    blk_e = jnp.minimum(jnp.sum((seg_end[None, :] <= blk_first_chunk[:, None]).astype(jnp.int32), axis=1),
                        N_EXPERTS - 1)
    def flat_list(count, slots, first_local, first_global, step):
        off = jnp.cumsum(count, axis=1) - count
        k = jnp.arange(slots, dtype=jnp.int32)
        in_run = (off[:, None, :] <= k[None, :, None]) & (k[None, :, None] < (off + count)[:, None, :])

        def pick(first):
            return (jnp.sum(jnp.where(in_run, (first - step * off)[:, None, :], 0), axis=-1)
                    + step * k[None, :]).reshape(-1).astype(jnp.int32)

        return jnp.sum(count, axis=1).astype(jnp.int32), pick(first_local), pick(first_global)

    npair, lpair, gpair = flat_list(run_chunks // 2, PAIR_SLOTS, run_start, run_dest, 2)
    nsingle, lsingle, gsingle = flat_list(run_chunks % 2, N_EXPERTS, run_start + run_chunks - 1, run_dest + run_chunks - 1, 0)
    run_tables = (npair, nsingle, lpair, gpair, lsingle, gsingle)

    xs = _dispatch(run_tables + ((seg_start + seg_chunks).astype(jnp.int32), (seg_padded - seg_chunks).astype(jnp.int32), nvb),
                   slot_rows, h2)
    ys = _experts(blk_e, nvb, xs, w_exp_gate[0], w_exp_up[0], w_exp_down[0])
    out = _combine(run_tables, slot_cols, x1, mf, g_final.reshape(1, D_MODEL), ys)
    return out.reshape(BATCH, SEQ, D_MODEL)
```

```python
import jax
import jax.numpy as jnp
from jax import lax
from jax.experimental import pallas as pl
from jax.experimental.pallas import tpu as pltpu

D_MODEL = 1024
BATCH = 2
SEQ = 8192
N_TOK = BATCH * SEQ
CHUNK = 64
EPS = 1e-6
MLA_HEADS = 8
Q_RANK = 256
KV_RANK = 128
QK_NOPE = 64
QK_ROPE = 32
V_HEAD = 64
MLA_WIDTH = MLA_HEADS * V_HEAD
ROPE_THETA = 10000.0
SGU_GROUPS = 8
SGU_GROUP_DIM = 64
SGU_WIDTH = SGU_GROUPS * SGU_GROUP_DIM
SGU_LEN = 128
N_GROUPS = 4
EXPERTS_PER_GROUP = 8
N_EXPERTS = N_GROUPS * EXPERTS_PER_GROUP
TOP_K = 2
D_EXPERT = 256

LANES = 128
SUBLANES = 8
QK_PAD = 2 * LANES
V_PAD = 2 * LANES
SCALE = (QK_NOPE + QK_ROPE) ** -0.5
LOG2E = 1.4426950408889634
Q_SCALE = SCALE * LOG2E
NEG = -1e30

TM = 256
TMI = 512
TMX = 1024
TQ = 256
TK = 512
ATTN_UNROLL_SHIFT = 1
ATTN_UNROLL = 1 << ATTN_UNROLL_SHIFT
TK_SHIFT = TK.bit_length() - 1
CHUNK_SHIFT = CHUNK.bit_length() - 1
assert 1 << TK_SHIFT == TK and 1 << CHUNK_SHIFT == CHUNK
MOE_T = 512
CHUNK_ROWS = 16
TILE_ROWS = -(-(TOP_K * TM + N_EXPERTS * (CHUNK_ROWS - 1) + CHUNK_ROWS) // 256) * 256
SHORT_TILE_ROWS = TILE_ROWS - 256
TILE_CHUNKS = TILE_ROWS // CHUNK_ROWS
PAIR_SLOTS = TILE_CHUNKS // 2
CHUNK_UNROLL_SHIFT = 2
CHUNK_UNROLL = 1 << CHUNK_UNROLL_SHIFT
MOE_ROWS_MAX = (N_TOK * TOP_K + (N_TOK // TM) * N_EXPERTS * (CHUNK_ROWS - 1)
                + N_EXPERTS * (MOE_T - CHUNK_ROWS))
STEP_BLOCKS = 2
MOE_BLOCKS = -(-MOE_ROWS_MAX // (MOE_T * STEP_BLOCKS)) * STEP_BLOCKS
MOE_ROWS = MOE_BLOCKS * MOE_T

C_Q = 0
C_KV = C_Q + Q_RANK
C_KR = C_KV + KV_RANK
C_KRS = C_KR + LANES
C_U = C_KRS + LANES
C_V = C_U + SGU_WIDTH
C_GA = C_V + SGU_WIDTH
C_GB = C_GA + D_MODEL
C_END = C_GB + D_MODEL

F32 = jnp.float32
BF16 = jnp.bfloat16


def _dot(a, b):
    return jnp.dot(a, b, preferred_element_type=F32)


def _rms(x, g):
    return x * lax.rsqrt(jnp.mean(x * x, axis=-1, keepdims=True) + EPS) * g


def _fold_kernel(a_ref, b_ref, o_ref):
    o_ref[...] = Q_SCALE * jnp.dot(a_ref[...], b_ref[...], preferred_element_type=F32,
                                 precision=lax.Precision.HIGHEST)


def _fold(w_uq_nope, w_ukt):
    return pl.pallas_call(
        _fold_kernel,
        grid=(MLA_HEADS,),
        in_specs=[pl.BlockSpec((None, Q_RANK, QK_NOPE), lambda h: (h, 0, 0)),
                  pl.BlockSpec((None, QK_NOPE, KV_RANK), lambda h: (h, 0, 0))],
        out_specs=pl.BlockSpec((None, Q_RANK, KV_RANK), lambda h: (h, 0, 0)),
        out_shape=jax.ShapeDtypeStruct((MLA_HEADS, Q_RANK, KV_RANK), F32),
        name="fold",
    )(w_uq_nope, w_ukt)


def _rope_kernel(pos_ref, freq_ref, cos_ref, sin_ref):
    ang = pos_ref[...] * freq_ref[...]
    cos_ref[...] = jnp.cos(ang)
    sin_ref[...] = jnp.sin(ang)


def _rope_tables(pos8, freq8):
    rows = pos8.shape[0]
    blk = pl.BlockSpec((TM, LANES), lambda i: (i, 0))
    return pl.pallas_call(
        _rope_kernel,
        grid=(rows // TM,),
        in_specs=[blk, pl.BlockSpec((1, LANES), lambda i: (0, 0))],
        out_specs=[blk, blk],
        out_shape=[jax.ShapeDtypeStruct((rows, LANES), F32)] * 2,
        name="rope_tables",
    )(pos8, freq8)


def _inproj_kernel(x_ref, cos_ref, sin_ref, gattn_ref, win_ref, gcq_ref, gckv_ref, wqlat_ref, wqr_ref,
                   wqrs_ref, sgain_ref, sbias_ref, wsp_ref, bsp_ref, wosgu_ref,
                   q_ref, kt_ref, v_ref, p_ref, ga_ref):
    hb = _rms(x_ref[...], gattn_ref[...]).astype(BF16)

    def proj(a, b):
        return _dot(hb, win_ref[:, a:b])

    cos_t = cos_ref[...]
    sin_t = sin_ref[...]

    cqn = _rms(proj(C_Q, C_KV), gcq_ref[...]).astype(BF16)
    qlat = _dot(cqn, wqlat_ref[...])
    groups = MLA_HEADS * QK_ROPE // LANES
    rot = (_dot(cqn, wqr_ref[...]) * jnp.concatenate([cos_t * Q_SCALE] * groups, axis=1)
           + _dot(cqn, wqrs_ref[...]) * jnp.concatenate([sin_t * Q_SCALE] * groups, axis=1))
    lane = lax.broadcasted_iota(jnp.int32, (TQ, LANES), 1)
    for r in range(TMI // TQ):
        t0, t1 = r * TQ, (r + 1) * TQ
        for h in range(MLA_HEADS):
            a, b = h * LANES, (h + 1) * LANES
            q_ref[r, h * TQ:(h + 1) * TQ, :LANES] = qlat[t0:t1, a:b].astype(BF16)
            g0 = (h * QK_ROPE) // LANES * LANES
            off = (h * QK_ROPE) % LANES
            window = (lane >= off) & (lane < off + QK_ROPE)
            q_ref[r, h * TQ:(h + 1) * TQ, LANES:] = jnp.where(
                window, rot[t0:t1, g0:g0 + LANES], 0.0).astype(BF16)

    zk = proj(C_KV, C_U)
    ckvn = _rms(zk[:, :KV_RANK], gckv_ref[...])
    krope = zk[:, KV_RANK:KV_RANK + LANES] * cos_t + zk[:, KV_RANK + LANES:] * sin_t
    kt_ref[...] = jnp.concatenate([ckvn, krope], axis=1).T.astype(BF16)
    v_ref[...] = jnp.concatenate([ckvn, jnp.ones_like(ckvn)], axis=1).astype(BF16)

    u = jax.nn.gelu(proj(C_U, C_V))
    v = jax.nn.gelu(proj(C_V, C_GA))
    mu = jnp.mean(v, axis=-1, keepdims=True)
    vc = v - mu
    var = jnp.mean(vc * vc, axis=-1, keepdims=True)
    vb = (vc * lax.rsqrt(var + EPS) * sgain_ref[...] + sbias_ref[...]).astype(BF16)
    row = lax.broadcasted_iota(jnp.int32, (SGU_LEN, SGU_LEN), 0)
    col = lax.broadcasted_iota(jnp.int32, (SGU_LEN, SGU_LEN), 1)
    causal = (row >> CHUNK_SHIFT) >= (col >> CHUNK_SHIFT)
    low_half = col < SGU_GROUP_DIM
    zero_w = jnp.zeros((SGU_LEN, SGU_LEN), BF16)
    w_pairs = [jnp.concatenate([jnp.where(causal, wsp_ref[2 * p], zero_w),
                                jnp.where(causal, wsp_ref[2 * p + 1], zero_w)], axis=1)
               for p in range(SGU_WIDTH // LANES)]
    row_blocks = []
    for r in range(TMI // SGU_LEN):
        pieces = []
        for p in range(SGU_WIDTH // LANES):
            blk = vb[r * SGU_LEN:(r + 1) * SGU_LEN, p * LANES:(p + 1) * LANES]
            stacked = jnp.concatenate([jnp.where(low_half, blk, zero_w), jnp.where(low_half, zero_w, blk)], axis=0)
            sv = _dot(w_pairs[p], stacked) + bsp_ref[p]
            pieces.append((u[r * SGU_LEN:(r + 1) * SGU_LEN, p * LANES:(p + 1) * LANES] * sv).astype(BF16))
        row_blocks.append(jnp.concatenate(pieces, axis=1))
    sgu = _dot(jnp.concatenate(row_blocks, axis=0), wosgu_ref[...])

    ga_ref[...] = jax.nn.sigmoid(proj(C_GA, C_GB)).astype(BF16)
    p_ref[...] = (jax.nn.sigmoid(proj(C_GB, C_END)) * sgu).astype(BF16)


def _inproj(x, cos_t, sin_t, gattn, win, gcq, gckv, wqlat, wqr, wqrs, sgain, sbias, wsp, bsp, wosgu):
    nt = N_TOK // TMI
    per_b = SEQ // TMI
    per_k = TK // TMI

    def const(shape):
        return pl.BlockSpec(shape, lambda i: (0,) * len(shape))

    return pl.pallas_call(
        _inproj_kernel,
        grid=(nt,),
        in_specs=[pl.BlockSpec((TMI, D_MODEL), lambda i: (i, 0)),
                  pl.BlockSpec((TMI, LANES), lambda i: (i, 0)), pl.BlockSpec((TMI, LANES), lambda i: (i, 0)),
                  const((1, D_MODEL)), const((D_MODEL, C_END)), const((1, Q_RANK)), const((1, KV_RANK)),
                  const((Q_RANK, MLA_HEADS * LANES)), const((Q_RANK, MLA_HEADS * QK_ROPE)),
                  const((Q_RANK, MLA_HEADS * QK_ROPE)),
                  const((1, SGU_WIDTH)), const((1, SGU_WIDTH)),
                  const((SGU_GROUPS, SGU_LEN, SGU_LEN)), const((SGU_GROUPS // 2, SGU_LEN, LANES)),
                  const((SGU_WIDTH, D_MODEL))],
        out_specs=[pl.BlockSpec((TMI // TQ, MLA_HEADS * TQ, QK_PAD), lambda i: (i, 0, 0)),
                   pl.BlockSpec((None, None, QK_PAD, TMI),
                                lambda i: (i // per_b, (i % per_b) // per_k, 0, i % per_k)),
                   pl.BlockSpec((None, None, TMI, V_PAD),
                                lambda i: (i // per_b, (i % per_b) // per_k, i % per_k, 0)),
                   pl.BlockSpec((TMI, D_MODEL), lambda i: (i, 0)),
                   pl.BlockSpec((TMI, D_MODEL), lambda i: (i, 0))],
        out_shape=[jax.ShapeDtypeStruct((N_TOK // TQ, MLA_HEADS * TQ, QK_PAD), BF16),
                   jax.ShapeDtypeStruct((BATCH, SEQ // TK, QK_PAD, TK), BF16),
                   jax.ShapeDtypeStruct((BATCH, SEQ // TK, TK, V_PAD), BF16),
                   jax.ShapeDtypeStruct((N_TOK, D_MODEL), BF16),
                   jax.ShapeDtypeStruct((N_TOK, D_MODEL), BF16)],
        compiler_params=pltpu.CompilerParams(dimension_semantics=("arbitrary",),
                                             vmem_limit_bytes=56 * 1024 * 1024,
                                             allow_input_fusion=[True] * 15),
        name="inproj",
    )(x, cos_t, sin_t, gattn, win, gcq, gckv, wqlat, wqr, wqrs, sgain, sbias, wsp, bsp, wosgu)


def _attn_kernel(q_ref, kt_ref, v_ref, wuv_ref, o_ref, m_ref, acc_ref, s_ref, p_ref, a_ref):
    qi = pl.program_id(1)
    m_ref[...] = jnp.full(m_ref.shape, NEG, F32)
    acc_ref[...] = jnp.zeros(acc_ref.shape, F32)

    def scores(j, slot):
        s_ref[slot] = _dot(q_ref[...], kt_ref[j])

    def update(j, slot, width=None):
        masked = width is not None
        w = width if masked else TK
        if masked:
            q_chunk = ((qi * TQ - j * TK) >> CHUNK_SHIFT) + (
                lax.broadcasted_iota(jnp.int32, (TQ, w), 0) >> CHUNK_SHIFT)
            k_chunk = lax.broadcasted_iota(jnp.int32, (TQ, w), 1) >> CHUNK_SHIFT
            visible = k_chunk <= q_chunk
        for h in range(MLA_HEADS):
            rows = slice(h * TQ, (h + 1) * TQ)
            s = s_ref[slot, rows, :w]
            if masked:
                s = jnp.where(visible, s, NEG)
            m_prev = m_ref[rows, :]
            m_new = jnp.maximum(m_prev, jnp.max(s, axis=-1, keepdims=True))
            m_ref[rows, :] = m_new
            a_ref[rows, :] = jnp.exp2(m_prev - m_new)
            s = s_ref[slot, rows, :w]
            if masked:
                s = jnp.where(visible, s, NEG)
            p_ref[rows, :w] = jnp.exp2(s - jnp.concatenate([m_new] * (w // LANES), axis=1)).astype(BF16)
        alpha = a_ref[...]
        acc_ref[...] = jnp.concatenate([alpha, alpha], axis=1) * acc_ref[...] + _dot(
            p_ref[:, :w], v_ref[j, :w, :])

    diag = (qi * TQ) >> TK_SHIFT
    scores(0, 0)

    def run(first, count, last_masked):
        for u in range(count):
            if not (last_masked and u == count - 1):
                scores(first + u + 1, (u + 1) % 2)
                update(first + u, u % 2)
            else:
                for sub in range(TK // TQ):
                    @pl.when((qi & (TK // TQ - 1)) == sub)
                    def _(u=u, sub=sub):
                        update(first + u, u % 2, (sub + 1) * TQ)

    def body(t, carry):
        run(ATTN_UNROLL * t, ATTN_UNROLL, False)
        return carry

    trips = diag >> ATTN_UNROLL_SHIFT
    lax.fori_loop(0, trips, body, 0)
    done = trips << ATTN_UNROLL_SHIFT
    for r in range(ATTN_UNROLL):
        @pl.when(diag - done == r)
        def _(r=r):
            run(done, r + 1, True)

    o_lat = (acc_ref[:, :KV_RANK] / acc_ref[:, KV_RANK:]).astype(BF16)
    o_cat = jnp.concatenate([o_lat[h * TQ:(h + 1) * TQ] for h in range(MLA_HEADS)], axis=1)
    o_ref[...] = _dot(o_cat, wuv_ref[...]).astype(BF16)


def _attention(q, kt, v, wuv):
    nk = SEQ // TK
    return pl.pallas_call(
        _attn_kernel,
        grid=(BATCH, SEQ // TQ),
        in_specs=[pl.BlockSpec((None, MLA_HEADS * TQ, QK_PAD), lambda b, i: (b * (SEQ // TQ) + i, 0, 0)),
                  pl.BlockSpec((None, nk, QK_PAD, TK), lambda b, i: (b, 0, 0, 0)),
                  pl.BlockSpec((None, nk, TK, V_PAD), lambda b, i: (b, 0, 0, 0)),
                  pl.BlockSpec((MLA_HEADS * KV_RANK, MLA_WIDTH), lambda b, i: (0, 0))],
        out_specs=pl.BlockSpec((None, TQ, MLA_WIDTH), lambda b, i: (b, i, 0)),
        out_shape=jax.ShapeDtypeStruct((BATCH, SEQ, MLA_WIDTH), BF16),
        scratch_shapes=[pltpu.VMEM((MLA_HEADS * TQ, LANES), F32),
                        pltpu.VMEM((MLA_HEADS * TQ, V_PAD), F32),
                        pltpu.VMEM((2, MLA_HEADS * TQ, TK), F32),
                        pltpu.VMEM((MLA_HEADS * TQ, TK), BF16),
                        pltpu.VMEM((MLA_HEADS * TQ, LANES), F32)],
        compiler_params=pltpu.CompilerParams(dimension_semantics=("arbitrary", "arbitrary"),
                                             vmem_limit_bytes=56 * 1024 * 1024),
        name="attention",
    )(q, kt, v, wuv)


def _mix_kernel(attn_ref, ga_ref, p_ref, x_ref, woa_ref, wout_ref, gffn_ref, wr_ref, br_ref,
                x1_ref, h2_ref, mf_ref, slot_col_ref, slot_row_ref, run_chunks_ref, run_start_ref):
    a = _dot(attn_ref[...], woa_ref[...])
    mix = (ga_ref[...].astype(F32) * a + p_ref[...].astype(F32)).astype(BF16)
    x1 = x_ref[...] + _dot(mix, wout_ref[...])
    x1_ref[...] = x1
    h2 = _rms(x1, gffn_ref[...])
    h2_ref[...] = h2.astype(BF16)

    hi = h2.astype(BF16)
    lo = (h2 - hi.astype(F32)).astype(BF16)
    r1 = _dot(hi, wr_ref[...])
    r2 = _dot(lo, wr_ref[:, :LANES])
    logits_all = r1[:, :LANES] + r1[:, LANES:] + r2 + br_ref[...]

    for r in range(TMX // TM):
        mf, slots, run_chunks_rows, run_start_rows = _route_tile(logits_all[r * TM:(r + 1) * TM])
        mf_ref[r * TM:(r + 1) * TM, :] = mf
        slot_col_ref[r * TM:(r + 1) * TM, :] = slots.astype(jnp.int32)
        tile_rows = slice(SUBLANES * r, SUBLANES * (r + 1))
        slot_row_ref[tile_rows, :] = slots.T[:SUBLANES].astype(jnp.int32)
        run_chunks_ref[tile_rows, :] = run_chunks_rows.astype(jnp.int32)
        run_start_ref[tile_rows, :] = run_start_rows.astype(jnp.int32)


def _route_tile(logits):
    lane_i = lax.broadcasted_iota(jnp.int32, (TM, LANES), 1)
    lane = lane_i.astype(F32)
    lane_group = (lane_i >> 3).astype(F32)
    ninf = -jnp.inf
    is_group = (lane_i >= N_EXPERTS) & (lane_i < N_EXPERTS + N_GROUPS)
    lg = jnp.where(is_group, logits, ninf)
    gmax = jnp.max(lg, axis=-1, keepdims=True)
    gsum = jnp.sum(jnp.exp(lg - gmax), axis=-1, keepdims=True)
    p_top = 1.0 / gsum
    g_idx = jnp.min(jnp.where(lg == gmax, lane - N_EXPERTS, float(N_GROUPS)), axis=-1, keepdims=True)
    le = jnp.where((lane_i < N_EXPERTS) & (lane_group == g_idx), logits, ninf)
    t1 = jnp.max(le, axis=-1, keepdims=True)
    e1 = jnp.min(jnp.where(le == t1, lane, float(LANES)), axis=-1, keepdims=True)
    le2 = jnp.where(lane == e1, ninf, le)
    t2 = jnp.max(le2, axis=-1, keepdims=True)
    e2 = jnp.min(jnp.where(le2 == t2, lane, float(LANES)), axis=-1, keepdims=True)
    ex = jnp.exp(t2 - t1)
    w1 = p_top / (1.0 + ex)
    w2 = p_top * ex / (1.0 + ex)

    sel1 = lane == e1
    sel2 = lane == e2
    onehot = jnp.where(sel1 | sel2, 1.0, 0.0)
    rr = lax.broadcasted_iota(jnp.int32, (TM, TM), 0)
    cc = lax.broadcasted_iota(jnp.int32, (TM, TM), 1)
    ltri = jnp.where(cc < rr, 1.0, 0.0).astype(BF16)
    rank = _dot(ltri, onehot.astype(BF16))
    cnt = jnp.sum(onehot, axis=0, keepdims=True)
    run_chunks = jnp.floor((cnt + (CHUNK_ROWS - 1)) * (1.0 / CHUNK_ROWS))
    ur = lax.broadcasted_iota(jnp.int32, (LANES, LANES), 0)
    uc = lax.broadcasted_iota(jnp.int32, (LANES, LANES), 1)
    upper = jnp.where(ur < uc, 1.0, 0.0).astype(BF16)
    run_chunks_rows = jnp.broadcast_to(run_chunks, (SUBLANES, LANES))
    run_start_rows = _dot(run_chunks_rows.astype(BF16), upper)
    slot_all = CHUNK_ROWS * run_start_rows[0:1] + rank
    slot1 = jnp.sum(jnp.where(sel1, slot_all, 0.0), axis=-1, keepdims=True)
    slot2 = jnp.sum(jnp.where(sel2, slot_all, 0.0), axis=-1, keepdims=True)

    slots = jnp.where(lane_i == 0, slot1, jnp.where(lane_i == 1, slot2, 0.0))
    return jnp.where(lane_i == 0, w1, w2), slots, run_chunks_rows, run_start_rows


def _mix(attn, ga, p, x, woa, wout, gffn, wr, br):
    nt = N_TOK // TM
    sub = TMX // TM

    def const(shape):
        return pl.BlockSpec(shape, lambda i: (0,) * len(shape))

    def rows(width):
        return pl.BlockSpec((TMX, width), lambda i: (i, 0))

    return pl.pallas_call(
        _mix_kernel,
        grid=(N_TOK // TMX,),
        in_specs=[rows(MLA_WIDTH), rows(D_MODEL), rows(D_MODEL), rows(D_MODEL),
                  const((MLA_WIDTH, D_MODEL)), const((D_MODEL, D_MODEL)), const((1, D_MODEL)),
                  const((D_MODEL, 2 * LANES)), const((1, LANES))],
        out_specs=[rows(D_MODEL), rows(D_MODEL), rows(LANES), rows(LANES),
                   pl.BlockSpec((SUBLANES * sub, TM), lambda i: (i, 0)),
                   pl.BlockSpec((SUBLANES * sub, LANES), lambda i: (i, 0)),
                   pl.BlockSpec((SUBLANES * sub, LANES), lambda i: (i, 0))],
        out_shape=[jax.ShapeDtypeStruct((N_TOK, D_MODEL), F32),
                   jax.ShapeDtypeStruct((N_TOK, D_MODEL), BF16),
                   jax.ShapeDtypeStruct((N_TOK, LANES), F32),
                   jax.ShapeDtypeStruct((N_TOK, LANES), jnp.int32),
                   jax.ShapeDtypeStruct((nt * SUBLANES, TM), jnp.int32),
                   jax.ShapeDtypeStruct((nt * SUBLANES, LANES), jnp.int32),
                   jax.ShapeDtypeStruct((nt * SUBLANES, LANES), jnp.int32)],
        compiler_params=pltpu.CompilerParams(dimension_semantics=("arbitrary",),
                                             vmem_limit_bytes=40 * 1024 * 1024,
                                             allow_input_fusion=[True] * 9),
        name="mix",
    )(attn, ga, p, x, woa, wout, gffn, wr, br)


def _chunk_copy(src, dst, src_chunk, dst_chunk, sem, nchunks=1):
    rows = nchunks * CHUNK_ROWS
    return pltpu.make_async_copy(
        src.at[pl.ds(pl.multiple_of(src_chunk * CHUNK_ROWS, CHUNK_ROWS), rows)],
        dst.at[pl.ds(pl.multiple_of(dst_chunk * CHUNK_ROWS, CHUNK_ROWS), rows)], sem)


def _unrolled_loop(n, fn):
    groups = n >> CHUNK_UNROLL_SHIFT

    def group(g, carry):
        for u in range(CHUNK_UNROLL):
            fn(g * CHUNK_UNROLL + u)
        return carry

    lax.fori_loop(0, groups, group, 0)

    def single(k, carry):
        fn(k)
        return carry

    lax.fori_loop(groups << CHUNK_UNROLL_SHIFT, n, single, 0)


def _for_each_copy(t, copy_tables, fn):
    npair_ref, nsingle_ref, lpair_ref, gpair_ref, lsingle_ref, gsingle_ref = copy_tables
    pb = t * PAIR_SLOTS
    sb = t * N_EXPERTS
    _unrolled_loop(npair_ref[t], lambda k: fn(lpair_ref[pb + k], gpair_ref[pb + k], 2))
    _unrolled_loop(nsingle_ref[t], lambda k: fn(lsingle_ref[sb + k], gsingle_ref[sb + k], 1))


def _tile_chunks(t, copy_tables):
    return 2 * copy_tables[0][t] + copy_tables[1][t]


def _wait_chunks(n, src, dst, sem):
    groups = n >> CHUNK_UNROLL_SHIFT
    rows = CHUNK_UNROLL * CHUNK_ROWS

    def group(g, carry):
        pltpu.make_async_copy(src.at[pl.ds(0, rows)], dst.at[pl.ds(0, rows)], sem).wait()
        return carry

    lax.fori_loop(0, groups, group, 0)

    def single(k, carry):
        _chunk_copy(src, dst, 0, 0, sem).wait()
        return carry

    lax.fori_loop(groups << CHUNK_UNROLL_SHIFT, n, single, 0)


def _dispatch_kernel(npair_ref, nsingle_ref, lpair_ref, gpair_ref, lsingle_ref, gsingle_ref,
                     zero_start_ref, zero_count_ref, nvb_ref,
                     slot_ref, h2_ref, xs_hbm, sbuf, zbuf, sem, zsem):
    copy_tables = (npair_ref, nsingle_ref, lpair_ref, gpair_ref, lsingle_ref, gsingle_ref)
    t = pl.program_id(0)
    last = pl.num_programs(0) - 1
    cur = t % 2

    def zero_chunk_copy(dst_chunk):
        return _chunk_copy(zbuf, xs_hbm, 0, dst_chunk, zsem)

    def zero_block_copy(b):
        return pltpu.make_async_copy(
            zbuf, xs_hbm.at[pl.ds(pl.multiple_of(b * MOE_T, MOE_T), MOE_T)], zsem)

    def for_each_zero(chunk_fn, block_fn):
        def per_expert(e, carry):
            def per_chunk(j, carry2):
                chunk_fn(zero_start_ref[e] + j)
                return carry2

            lax.fori_loop(0, zero_count_ref[e], per_chunk, 0)
            return carry

        lax.fori_loop(0, N_EXPERTS, per_expert, 0)

        def per_block(b, carry):
            block_fn(b)
            return carry

        lax.fori_loop(nvb_ref[0], MOE_BLOCKS, per_block, 0)

    @pl.when(t == 0)
    def _():
        zbuf[...] = jnp.zeros(zbuf.shape, BF16)
        for_each_zero(lambda dst_chunk: zero_chunk_copy(dst_chunk).start(), lambda b: zero_block_copy(b).start())

    row = lax.broadcasted_iota(jnp.int32, (TILE_ROWS, TM), 0)
    slots = slot_ref[...]
    perm = jnp.where((row == slots[0:1, :]) | (row == slots[1:2, :]), 1.0, 0.0).astype(BF16)
    sbuf[cur] = _dot(perm, h2_ref[...]).astype(BF16)
    _for_each_copy(t, copy_tables,
                   lambda lc, gc, n: _chunk_copy(sbuf.at[cur], xs_hbm, lc, gc, sem.at[cur], n).start())

    def wait_tile(tile, slot):
        _wait_chunks(_tile_chunks(tile, copy_tables), sbuf.at[slot], xs_hbm, sem.at[slot])

    @pl.when(t > 0)
    def _():
        wait_tile(t - 1, 1 - cur)

    @pl.when(t == last)
    def _():
        wait_tile(t, cur)
        for_each_zero(lambda dst_chunk: zero_chunk_copy(dst_chunk).wait(), lambda b: zero_block_copy(b).wait())


def _dispatch(tables, slot_rows, h2):
    return pl.pallas_call(
        _dispatch_kernel,
        grid_spec=pltpu.PrefetchScalarGridSpec(
            num_scalar_prefetch=len(tables),
            grid=(N_TOK // TM,),
            in_specs=[pl.BlockSpec((SUBLANES, TM), lambda i, *_: (i, 0)),
                      pl.BlockSpec((TM, D_MODEL), lambda i, *_: (i, 0))],
            out_specs=pl.BlockSpec(memory_space=pl.ANY),
            scratch_shapes=[pltpu.VMEM((2, TILE_ROWS, D_MODEL), BF16), pltpu.VMEM((MOE_T, D_MODEL), BF16),
                            pltpu.SemaphoreType.DMA((2,)), pltpu.SemaphoreType.DMA(())]),
        out_shape=jax.ShapeDtypeStruct((MOE_ROWS, D_MODEL), BF16),
        compiler_params=pltpu.CompilerParams(dimension_semantics=("arbitrary",)),
        name="dispatch",
    )(*tables, slot_rows, h2)


def _expert_kernel(blk_e_ref, nvb_ref, xs_ref, *refs):
    w_refs, ys_ref, caches = refs[:3 * STEP_BLOCKS], refs[3 * STEP_BLOCKS], refs[3 * STEP_BLOCKS + 1:]
    i = pl.program_id(0)
    for h in range(STEP_BLOCKS):
        wg_ref, wu_ref, wd_ref = w_refs[3 * h:3 * h + 3]
        wgb, wub, wdb = caches[3 * h:3 * h + 3]
        rows = slice(h * MOE_T, (h + 1) * MOE_T)
        blk = i * STEP_BLOCKS + h
        live = blk < nvb_ref[0]
        same_slot_before = jnp.maximum(blk - STEP_BLOCKS, 0)

        @pl.when(live & ((i == 0) | (blk_e_ref[blk] != blk_e_ref[same_slot_before])))
        def _(wg_ref=wg_ref, wu_ref=wu_ref, wd_ref=wd_ref, wgb=wgb, wub=wub, wdb=wdb):
            wgb[...] = wg_ref[...].astype(BF16)
            wub[...] = wu_ref[...].astype(BF16)
            wdb[...] = wd_ref[...].astype(BF16)

        @pl.when(live)
        def _(rows=rows, wgb=wgb, wub=wub, wdb=wdb):
            xb = xs_ref[rows, :]
            g = _dot(xb, wgb[...])
            u = _dot(xb, wub[...])
            hid = (jax.nn.silu(g) * u).astype(BF16)
            ys_ref[rows, :] = _dot(hid, wdb[...]).astype(BF16)

        @pl.when(jnp.logical_not(live))
        def _(rows=rows):
            ys_ref[rows, :] = jnp.zeros((MOE_T, D_MODEL), ys_ref.dtype)


def _experts(blk_e, nvb, xs, wg, wu, wd):
    def row_block(i, be, nv):
        return (jnp.minimum(i, (nv[0] - 1) // STEP_BLOCKS), 0)

    def weight(h):
        return lambda i, be, nv: (be[jnp.minimum(i * STEP_BLOCKS + h, nv[0] - 1)], 0, 0)

    weight_specs = []
    for h in range(STEP_BLOCKS):
        weight_specs += [pl.BlockSpec((None, D_MODEL, D_EXPERT), weight(h)),
                         pl.BlockSpec((None, D_MODEL, D_EXPERT), weight(h)),
                         pl.BlockSpec((None, D_EXPERT, D_MODEL), weight(h))]
    caches = [pltpu.VMEM((D_MODEL, D_EXPERT), BF16), pltpu.VMEM((D_MODEL, D_EXPERT), BF16),
              pltpu.VMEM((D_EXPERT, D_MODEL), BF16)] * STEP_BLOCKS
    return pl.pallas_call(
        _expert_kernel,
        grid_spec=pltpu.PrefetchScalarGridSpec(
            num_scalar_prefetch=2,
            grid=(MOE_BLOCKS // STEP_BLOCKS,),
            in_specs=[pl.BlockSpec((STEP_BLOCKS * MOE_T, D_MODEL), row_block)] + weight_specs,
            out_specs=pl.BlockSpec((STEP_BLOCKS * MOE_T, D_MODEL), lambda i, be, nv: (i, 0)),
            scratch_shapes=caches),
        out_shape=jax.ShapeDtypeStruct((MOE_ROWS, D_MODEL), BF16),
        compiler_params=pltpu.CompilerParams(dimension_semantics=("arbitrary",),
                                             vmem_limit_bytes=48 * 1024 * 1024),
        name="experts",
    )(blk_e, nvb, xs, *([wg, wu, wd] * STEP_BLOCKS))


def _combine_kernel(npair_ref, nsingle_ref, lpair_ref, gpair_ref, lsingle_ref, gsingle_ref,
                    slot_ref, x1_ref, mf_ref, gfin_ref, ys_hbm, o_ref, ybuf, sem):
    copy_tables = (npair_ref, nsingle_ref, lpair_ref, gpair_ref, lsingle_ref, gsingle_ref)
    t = pl.program_id(0)
    cur = t % 2

    def fetch(tile, slot):
        _for_each_copy(tile, copy_tables,
                       lambda lc, gc, n: _chunk_copy(ys_hbm, ybuf.at[slot], gc, lc, sem.at[slot], n).start())

    @pl.when(t == 0)
    def _():
        ybuf[...] = jnp.zeros(ybuf.shape, BF16)
        fetch(0, 0)

    @pl.when(t + 1 < pl.num_programs(0))
    def _():
        fetch(t + 1, 1 - cur)

    used_chunks = _tile_chunks(t, copy_tables)
    _wait_chunks(used_chunks, ys_hbm, ybuf.at[cur], sem.at[cur])

    def finish(rows):
        yb = ybuf[cur, :rows, :]
        col = lax.broadcasted_iota(jnp.int32, (TM, rows), 1)
        slots = slot_ref[...]
        y1 = _dot(jnp.where(col == slots[:, 0:1], 1.0, 0.0).astype(BF16), yb)
        y2 = _dot(jnp.where(col == slots[:, 1:2], 1.0, 0.0).astype(BF16), yb)
        mf = mf_ref[...]
        x2 = x1_ref[...] + mf[:, 0:1] * y1 + mf[:, 1:2] * y2
        o_ref[...] = _rms(x2, gfin_ref[...])

    short = used_chunks * CHUNK_ROWS <= SHORT_TILE_ROWS

    @pl.when(short)
    def _():
        finish(SHORT_TILE_ROWS)

    @pl.when(jnp.logical_not(short))
    def _():
        finish(TILE_ROWS)


def _combine(tables, slot_cols, x1, mf, gfin, ys):
    return pl.pallas_call(
        _combine_kernel,
        grid_spec=pltpu.PrefetchScalarGridSpec(
            num_scalar_prefetch=len(tables),
            grid=(N_TOK // TM,),
            in_specs=[pl.BlockSpec((TM, LANES), lambda i, *_: (i, 0)),
                      pl.BlockSpec((TM, D_MODEL), lambda i, *_: (i, 0)),
                      pl.BlockSpec((TM, LANES), lambda i, *_: (i, 0)),
                      pl.BlockSpec((1, D_MODEL), lambda i, *_: (0, 0)),
                      pl.BlockSpec(memory_space=pl.ANY)],
            out_specs=pl.BlockSpec((TM, D_MODEL), lambda i, *_: (i, 0)),
            scratch_shapes=[pltpu.VMEM((2, TILE_ROWS, D_MODEL), BF16), pltpu.SemaphoreType.DMA((2,))]),
        out_shape=jax.ShapeDtypeStruct((N_TOK, D_MODEL), F32),
        compiler_params=pltpu.CompilerParams(dimension_semantics=("arbitrary",)),
        name="combine",
    )(*tables, slot_cols, x1, mf, gfin, ys)


def kernel(x, positions, g_attn_norm, w_in, g_cq, w_uq, g_ckv, w_uk, w_uv, w_o_attn, sgu_gain, sgu_bias, w_spatial, b_spatial, w_o_sgu, w_out, g_ffn_norm, w_router_group, b_router_group, w_router_expert, b_router_expert, w_exp_gate, w_exp_up, w_exp_down, g_final):
    assert x.shape == (BATCH, SEQ, D_MODEL) and w_in.shape[0] == 1
    half = QK_ROPE // 2
    swap = jnp.concatenate([jnp.arange(half, QK_ROPE), jnp.arange(0, half)])

    def pad_cols(w, width):
        return jnp.pad(w, ((0, 0), (0, width - w.shape[1])))

    wi = w_in[0]
    c0 = Q_RANK + KV_RANK
    kr = wi[:, c0:c0 + QK_ROPE]
    c1 = c0 + QK_ROPE
    win = jnp.concatenate([
        wi[:, :c0], jnp.tile(kr, (1, LANES // QK_ROPE)), jnp.tile(kr[:, swap], (1, LANES // QK_ROPE)),
        wi[:, c1:]], axis=1).astype(BF16)

    wq = w_uq[0].reshape(Q_RANK, MLA_HEADS, QK_NOPE + QK_ROPE)
    wq_nope = wq[:, :, :QK_NOPE].transpose(1, 0, 2)
    wq_rope = wq[:, :, QK_NOPE:]
    w_ukt = w_uk[0].reshape(KV_RANK, MLA_HEADS, QK_NOPE).transpose(1, 2, 0)
    wqlat = _fold(wq_nope, w_ukt).transpose(1, 0, 2).reshape(Q_RANK, MLA_HEADS * LANES).astype(BF16)

    wqr = wq_rope.reshape(Q_RANK, MLA_HEADS * QK_ROPE).astype(BF16)
    wqrs = wq_rope[:, :, swap].reshape(Q_RANK, MLA_HEADS * QK_ROPE).astype(BF16)

    per_row = LANES // half
    freqs = ROPE_THETA ** (-jnp.arange(0, QK_ROPE, 2, dtype=F32) / QK_ROPE)
    pos8 = jnp.repeat(positions.astype(F32).reshape(N_TOK // per_row, per_row), half, axis=1)
    cos8, sin8 = _rope_tables(pos8, jnp.tile(freqs, per_row)[None, :])
    cos16 = cos8.reshape(N_TOK, half)
    sin16 = sin8.reshape(N_TOK, half)
    cos_t = jnp.tile(jnp.concatenate([cos16, cos16], axis=1), (1, LANES // QK_ROPE))
    sin_t = jnp.tile(jnp.concatenate([-sin16, sin16], axis=1), (1, LANES // QK_ROPE))

    head_of_col = jnp.arange(MLA_WIDTH) // V_HEAD
    wuv = jnp.where(head_of_col[None, None, :] == jnp.arange(MLA_HEADS)[:, None, None],
                    w_uv[0][None], 0.0).astype(BF16)
    wuv = wuv.reshape(MLA_HEADS * KV_RANK, MLA_WIDTH)

    wsp = w_spatial[0].astype(BF16)
    bs = b_spatial[0]
    bsp = jnp.repeat(bs.reshape(SGU_GROUPS // 2, 2, SGU_LEN).transpose(0, 2, 1), SGU_GROUP_DIM, axis=2)

    wr32 = jnp.concatenate([w_router_expert[0].transpose(1, 0, 2).reshape(D_MODEL, N_EXPERTS),
                            w_router_group[0]], axis=1)
    wr32 = pad_cols(wr32, LANES)
    wr_hi = wr32.astype(BF16)
    wr_lo = (wr32 - wr_hi.astype(F32)).astype(BF16)
    wr = jnp.concatenate([wr_hi, wr_lo], axis=1)
    br = pad_cols(jnp.concatenate([b_router_expert[0].reshape(-1), b_router_group[0]])[None, :], LANES)

    xf = x.reshape(N_TOK, D_MODEL)
    q, kt, v, p, ga = _inproj(
        xf, cos_t, sin_t, g_attn_norm, win, g_cq, g_ckv, wqlat, wqr, wqrs,
        sgu_gain, sgu_bias, wsp, bsp, w_o_sgu[0].astype(BF16))
    attn = _attention(q, kt, v, wuv)
    x1, h2, mf, slot_cols, slot_rows, run_chunks_rows, run_start_rows = _mix(
        attn.reshape(N_TOK, MLA_WIDTH), ga, p, xf, w_o_attn[0].astype(BF16), w_out[0].astype(BF16),
        g_ffn_norm, wr, br)

    blk_chunks = MOE_T // CHUNK_ROWS
    run_chunks = run_chunks_rows[::SUBLANES, :N_EXPERTS]
    run_start = run_start_rows[::SUBLANES, :N_EXPERTS]
    seg_chunks = jnp.sum(run_chunks, axis=0)
    seg_padded = (seg_chunks + blk_chunks - 1) // blk_chunks * blk_chunks
    seg_end = jnp.cumsum(seg_padded)
    seg_start = seg_end - seg_padded
    run_dest = seg_start[None, :] + jnp.cumsum(run_chunks, axis=0) - run_chunks
    nvb = (seg_end[-1:] // blk_chunks).astype(jnp.int32)
    blk_first_chunk = jnp.arange(MOE_BLOCKS, dtype=jnp.int32) * blk_chunks
    blk_e = jnp.minimum(jnp.sum((seg_end[None, :] <= blk_first_chunk[:, None]).astype(jnp.int32), axis=1),
                        N_EXPERTS - 1)
    def flat_list(count, slots, first_local, first_global, step):
        off = jnp.cumsum(count, axis=1) - count
        k = jnp.arange(slots, dtype=jnp.int32)
        in_run = (off[:, None, :] <= k[None, :, None]) & (k[None, :, None] < (off + count)[:, None, :])

        def pick(first):
            return (jnp.sum(jnp.where(in_run, (first - step * off)[:, None, :], 0), axis=-1)
                    + step * k[None, :]).reshape(-1).astype(jnp.int32)

        return jnp.sum(count, axis=1).astype(jnp.int32), pick(first_local), pick(first_global)

    npair, lpair, gpair = flat_list(run_chunks // 2, PAIR_SLOTS, run_start, run_dest, 2)
    nsingle, lsingle, gsingle = flat_list(run_chunks % 2, N_EXPERTS, run_start + run_chunks - 1, run_dest + run_chunks - 1, 0)
    run_tables = (npair, nsingle, lpair, gpair, lsingle, gsingle)

    xs = _dispatch(run_tables + ((seg_start + seg_chunks).astype(jnp.int32), (seg_padded - seg_chunks).astype(jnp.int32), nvb),
                   slot_rows, h2)
    ys = _experts(blk_e, nvb, xs, w_exp_gate[0], w_exp_up[0], w_exp_down[0])
    out = _combine(run_tables, slot_cols, x1, mf, g_final.reshape(1, D_MODEL), ys)
    return out.reshape(BATCH, SEQ, D_MODEL)
```
